```python
import math
import jax, jax.numpy as jnp
from jax import lax
import numpy as np

D_MODEL = 1024
BATCH = 2
SEQ = 8192
DEPTH = 1
DEC_BATCH = 128
DEC_SEQ = 4
PAST_LEN = 8192
PAGE_SIZE = 128

HEAD_DIM = 64
MIX_A = D_MODEL // 2
N_HEADS_A = MIX_A // HEAD_DIM
MIX_B = D_MODEL - MIX_A
SSM_GROUP = 16
N_SSM_GROUPS = MIX_B // SSM_GROUP
SSM_STATE = 64
MIX = MIX_A + MIX_B
PROJ_COLS = 3 * MIX_A + MIX_B
DILATED_CONFIGS = ((128, 1), (512, 4), (2048, 16))
N_CONFIGS = len(DILATED_CONFIGS)
N_TAPS = max(w // d for w, d in DILATED_CONFIGS) + 1
MAX_WINDOW = max(w for w, _ in DILATED_CONFIGS)
Q_BLOCK = 128
N_BUCKETS = 32
MAX_DISTANCE = MAX_WINDOW
N_EXPERT_GROUPS = 4
EXPERTS_PER_GROUP = 8
N_EXPERTS = N_EXPERT_GROUPS * EXPERTS_PER_GROUP
EXPERT_TOP_K = 2
D_EXPERT = 256
MOE_CHUNK = 1024
EPS = 1e-6
NEG_INF = -1e30

kernel_name = 'hymba_dilated_s5_hmoe_step'


def rmsnorm(x, g):
    xf = x.astype(jnp.float32)
    y = xf * lax.rsqrt(jnp.mean(xf * xf, axis=-1, keepdims=True) + EPS)
    return (y * g.astype(jnp.float32)).astype(x.dtype)


def tap_offsets():
    j = jnp.arange(N_TAPS, dtype=jnp.int32)
    dil = jnp.array([d for _, d in DILATED_CONFIGS], jnp.int32)
    span = jnp.array([w // d for w, d in DILATED_CONFIGS], jnp.int32)
    return dil[:, None] * j[None, :], j[None, :] <= span[:, None]


def t5_bucket(distance):
    max_exact = N_BUCKETS // 2
    nf = jnp.maximum(distance, 1).astype(jnp.float32)
    large = max_exact + (jnp.log(nf / max_exact) / math.log(MAX_DISTANCE / max_exact)
                         * (N_BUCKETS - max_exact)).astype(jnp.int32)
    large = jnp.minimum(large, N_BUCKETS - 1)
    return jnp.where(distance < max_exact, distance, large)


def dilated_attention(q, k_all, v_all, n_buf, pos0, rel_bias):
    b, t, h, hd = q.shape
    n_rows = k_all.shape[1]
    offsets, in_window = tap_offsets()
    bias = jnp.transpose(rel_bias[t5_bucket(offsets)].astype(jnp.float32), (2, 0, 1))
    qb = Q_BLOCK if t % Q_BLOCK == 0 else t
    scale = HEAD_DIM ** -0.5

    def block(n):
        start = n * qb
        q_blk = lax.dynamic_slice_in_dim(q, start, qb, axis=1)
        i = start + jnp.arange(qb, dtype=jnp.int32)
        key_local = n_buf + i[:, None, None] - offsets[None]
        valid = in_window[None] & (pos0 + i[:, None, None] - offsets[None] >= 0)
        idx = jnp.clip(key_local, 0, n_rows - 1)
        k_g = jnp.take(k_all, idx, axis=1)
        v_g = jnp.take(v_all, idx, axis=1)
        logits = jnp.einsum('bqhd,bqcjhd->bhqcj', q_blk, k_g).astype(jnp.float32) * scale
        logits = jnp.where(valid[None, None], logits + bias[None, :, None], NEG_INF)
        m = jnp.max(logits, axis=(-2, -1), keepdims=True)
        e = jnp.where(valid[None, None], jnp.exp(logits - m), 0.0)
        num = jnp.einsum('bhqcj,bqcjhd->bqhd', e, v_g.astype(jnp.float32))
        den = jnp.transpose(jnp.sum(e, axis=(-2, -1)), (0, 2, 1))
        return (num / den[..., None]).astype(q.dtype)

    out = lax.map(block, jnp.arange(t // qb))
    return jnp.transpose(out, (1, 0, 2, 3, 4)).reshape(b, t, h * hd)


def s5_scan(u, s0_re, s0_im, a_re, a_im, log_dt, b_re, b_im, c_re, c_im, d_skip):
    f32 = jnp.float32
    u = u.astype(f32)
    a_re, a_im = a_re.astype(f32), a_im.astype(f32)
    b_re, b_im = b_re.astype(f32), b_im.astype(f32)
    dt = jnp.exp(log_dt.astype(f32))[:, None]
    decay = jnp.exp(a_re * dt)
    abar_re = decay * jnp.cos(a_im * dt)
    abar_im = decay * jnp.sin(a_im * dt)
    inv = 1.0 / (a_re * a_re + a_im * a_im)
    coef_re = ((abar_re - 1.0) * a_re + abar_im * a_im) * inv
    coef_im = (abar_im * a_re - (abar_re - 1.0) * a_im) * inv
    bbar_re = coef_re[..., None] * b_re - coef_im[..., None] * b_im
    bbar_im = coef_re[..., None] * b_im + coef_im[..., None] * b_re
    bu_re = jnp.einsum('btgc,gpc->btgp', u, bbar_re)
    bu_im = jnp.einsum('btgc,gpc->btgp', u, bbar_im)
    s0_re, s0_im = s0_re.astype(f32), s0_im.astype(f32)
    bu_re = bu_re.at[:, 0].add(abar_re * s0_re - abar_im * s0_im)
    bu_im = bu_im.at[:, 0].add(abar_re * s0_im + abar_im * s0_re)
    ar = jnp.broadcast_to(abar_re, bu_re.shape)
    ai = jnp.broadcast_to(abar_im, bu_im.shape)

    def combine(e1, e2):
        a1r, a1i, b1r, b1i = e1
        a2r, a2i, b2r, b2i = e2
        return (a1r * a2r - a1i * a2i, a1r * a2i + a1i * a2r,
                a2r * b1r - a2i * b1i + b2r, a2r * b1i + a2i * b1r + b2i)

    _, _, s_re, s_im = lax.associative_scan(combine, (ar, ai, bu_re, bu_im), axis=1)
    y = (jnp.einsum('gcp,btgp->btgc', c_re.astype(f32), s_re)
         - jnp.einsum('gcp,btgp->btgc', c_im.astype(f32), s_im)
         + d_skip.astype(f32) * u)
    return y, s_re[:, -1], s_im[:, -1]


def ssm_mixer(u, s0_re, s0_im, p):
    b, t, _ = u.shape
    ug = u.reshape(b, t, N_SSM_GROUPS, SSM_GROUP)
    d_skip = p['ssm_d'].reshape(N_SSM_GROUPS, SSM_GROUP)
    y, s_re, s_im = s5_scan(ug, s0_re, s0_im, p['ssm_a_re'], p['ssm_a_im'], p['ssm_log_dt'],
                            p['ssm_b_re'], p['ssm_b_im'], p['ssm_c_re'], p['ssm_c_im'], d_skip)
    z = jax.nn.gelu(y.reshape(b, t, MIX_B))
    gate = jax.nn.sigmoid(z @ p['w_glu'].astype(jnp.float32) + p['b_glu'].astype(jnp.float32))
    return (z * gate).astype(u.dtype), s_re, s_im


def hier_moe(h, p):
    b, t, d = h.shape
    n = b * t
    chunk = math.gcd(n, MOE_CHUNK)
    f32 = jnp.float32

    def run(xc):
        xf = xc.astype(f32)
        p_group = jax.nn.softmax(xf @ p['w_router_group'].astype(f32), axis=-1)
        g_sel = jnp.argmax(p_group, axis=-1)
        g_prob = jnp.max(p_group, axis=-1)
        e_logits = jnp.einsum('cd,gde->cge', xf, p['w_router_expert'].astype(f32))
        e_logits = jnp.take_along_axis(e_logits, g_sel[:, None, None], axis=1)[:, 0]
        top_p, top_i = lax.top_k(jax.nn.softmax(e_logits, axis=-1), EXPERT_TOP_K)
        top_p = top_p / jnp.sum(top_p, axis=-1, keepdims=True)
        expert_id = g_sel[:, None] * EXPERTS_PER_GROUP + top_i
        gates = jnp.sum(jax.nn.one_hot(expert_id, N_EXPERTS, dtype=f32)
                        * (g_prob[:, None] * top_p)[..., None], axis=1)
        a = jnp.einsum('cd,edf->cef', xc, p['w_expert_gate'])
        g = jnp.einsum('cd,edf->cef', xc, p['w_expert_up'])
        act = jax.nn.silu(a) * g * gates[..., None].astype(xc.dtype)
        return jnp.einsum('cef,efd->cd', act, p['w_expert_down'])

    return lax.map(run, h.reshape(n // chunk, chunk, d)).reshape(b, t, d)


def layer(x, k_buf, v_buf, s_re, s_im, pos0, rel_bias, p):
    b, t, _ = x.shape
    h = rmsnorm(x, p['g_mix'])
    proj = h @ p['w_in']
    q, k, v, u = jnp.split(proj, [MIX_A, 2 * MIX_A, 3 * MIX_A], axis=-1)
    q = q.reshape(b, t, N_HEADS_A, HEAD_DIM)
    k = k.reshape(b, t, N_HEADS_A, HEAD_DIM)
    v = v.reshape(b, t, N_HEADS_A, HEAD_DIM)
    k_all = jnp.concatenate([k_buf.astype(k.dtype), k], axis=1)
    v_all = jnp.concatenate([v_buf.astype(v.dtype), v], axis=1)
    o_a = dilated_attention(q, k_all, v_all, k_buf.shape[1], pos0, rel_bias)
    o_b, new_re, new_im = ssm_mixer(u, s_re, s_im, p)
    mixed = jnp.concatenate([rmsnorm(o_a, p['g_out_a']), rmsnorm(o_b, p['g_out_b'])], axis=-1)
    x = x + mixed @ p['w_out']
    x = x + hier_moe(rmsnorm(x, p['g_ffn']), p)
    return x, k, v, new_re, new_im


def setup_inputs(seed: int = 0) -> dict:
    key = jax.random.key(seed)
    ks = iter(jax.random.split(key, 40))
    f32 = jnp.float32

    def nrm(shape, scale):
        return jax.random.normal(next(ks), shape, f32) * scale

    def gain(shape):
        return 1.0 + nrm(shape, 0.01)

    win_buf = min(MAX_WINDOW, PAST_LEN)
    state_n = jnp.arange(SSM_STATE, dtype=f32)
    return {
        'x_prompt': nrm((BATCH, SEQ, D_MODEL), 1.0),
        'x_sample': nrm((DEC_BATCH, DEC_SEQ, D_MODEL), 1.0),
        'cache_k': nrm((DEPTH, DEC_BATCH, win_buf, N_HEADS_A, HEAD_DIM), 1.0),
        'cache_v': nrm((DEPTH, DEC_BATCH, win_buf, N_HEADS_A, HEAD_DIM), 1.0),
        'state_ssm_re': nrm((DEPTH, DEC_BATCH, N_SSM_GROUPS, SSM_STATE), 0.1),
        'state_ssm_im': nrm((DEPTH, DEC_BATCH, N_SSM_GROUPS, SSM_STATE), 0.1),
        'rel_bias': nrm((N_BUCKETS, N_HEADS_A), 0.5),
        'g_mix': gain((DEPTH, D_MODEL)),
        'w_in': nrm((DEPTH, D_MODEL, PROJ_COLS), D_MODEL ** -0.5),
        'g_out_a': gain((DEPTH, MIX_A)),
        'g_out_b': gain((DEPTH, MIX_B)),
        'w_out': nrm((DEPTH, MIX, D_MODEL), MIX ** -0.5),
        'ssm_a_re': -0.5 + nrm((DEPTH, N_SSM_GROUPS, SSM_STATE), 0.01),
        'ssm_a_im': math.pi * state_n + nrm((DEPTH, N_SSM_GROUPS, SSM_STATE), 0.01),
        'ssm_log_dt': jax.random.uniform(next(ks), (DEPTH, N_SSM_GROUPS), f32,
                                         math.log(1e-3), math.log(1e-1)),
        'ssm_b_re': nrm((DEPTH, N_SSM_GROUPS, SSM_STATE, SSM_GROUP), (2 * SSM_GROUP) ** -0.5),
        'ssm_b_im': nrm((DEPTH, N_SSM_GROUPS, SSM_STATE, SSM_GROUP), (2 * SSM_GROUP) ** -0.5),
        'ssm_c_re': nrm((DEPTH, N_SSM_GROUPS, SSM_GROUP, SSM_STATE), (2 * SSM_STATE) ** -0.5),
        'ssm_c_im': nrm((DEPTH, N_SSM_GROUPS, SSM_GROUP, SSM_STATE), (2 * SSM_STATE) ** -0.5),
        'ssm_d': nrm((DEPTH, MIX_B), 1.0),
        'w_glu': nrm((DEPTH, MIX_B, MIX_B), MIX_B ** -0.5),
        'b_glu': nrm((DEPTH, MIX_B), 0.01),
        'g_ffn': gain((DEPTH, D_MODEL)),
        'w_router_group': nrm((DEPTH, D_MODEL, N_EXPERT_GROUPS), D_MODEL ** -0.5),
        'w_router_expert': nrm((DEPTH, N_EXPERT_GROUPS, D_MODEL, EXPERTS_PER_GROUP), D_MODEL ** -0.5),
        'w_expert_gate': nrm((DEPTH, N_EXPERTS, D_MODEL, D_EXPERT), D_MODEL ** -0.5),
        'w_expert_up': nrm((DEPTH, N_EXPERTS, D_MODEL, D_EXPERT), D_MODEL ** -0.5),
        'w_expert_down': nrm((DEPTH, N_EXPERTS, D_EXPERT, D_MODEL), D_EXPERT ** -0.5),
        'g_final': gain((D_MODEL,)),
    }


def reference(x_prompt, x_sample, cache_k, cache_v, state_ssm_re, state_ssm_im, rel_bias,
              g_mix, w_in, g_out_a, g_out_b, w_out, ssm_a_re, ssm_a_im, ssm_log_dt,
              ssm_b_re, ssm_b_im, ssm_c_re, ssm_c_im, ssm_d, w_glu, b_glu, g_ffn,
              w_router_group, w_router_expert, w_expert_gate, w_expert_up, w_expert_down,
              g_final):
    b, t, _ = x_prompt.shape
    keep = min(MAX_WINDOW, t)
    empty = jnp.zeros((b, 0, N_HEADS_A, HEAD_DIM), x_prompt.dtype)
    zero_state = jnp.zeros((b, N_SSM_GROUPS, SSM_STATE), jnp.float32)
    xp, xs = x_prompt, x_sample
    kp_l, vp_l, ks_l, vs_l, rp_l, ip_l, rs_l, is_l = [], [], [], [], [], [], [], []
    for l in range(DEPTH):
        p = {
            'g_mix': g_mix[l], 'w_in': w_in[l], 'g_out_a': g_out_a[l], 'g_out_b': g_out_b[l],
            'w_out': w_out[l], 'ssm_a_re': ssm_a_re[l], 'ssm_a_im': ssm_a_im[l],
            'ssm_log_dt': ssm_log_dt[l], 'ssm_b_re': ssm_b_re[l], 'ssm_b_im': ssm_b_im[l],
            'ssm_c_re': ssm_c_re[l], 'ssm_c_im': ssm_c_im[l], 'ssm_d': ssm_d[l],
            'w_glu': w_glu[l], 'b_glu': b_glu[l], 'g_ffn': g_ffn[l],
            'w_router_group': w_router_group[l], 'w_router_expert': w_router_expert[l],
            'w_expert_gate': w_expert_gate[l], 'w_expert_up': w_expert_up[l],
            'w_expert_down': w_expert_down[l],
        }
        xp, kp, vp, rp, ip = layer(xp, empty, empty, zero_state, zero_state, 0, rel_bias, p)
        xs, kn, vn, rs, is_ = layer(xs, cache_k[l], cache_v[l], state_ssm_re[l], state_ssm_im[l],
                                    PAST_LEN, rel_bias, p)
        kp_l.append(kp[:, t - keep:]); vp_l.append(vp[:, t - keep:])
        ks_l.append(kn); vs_l.append(vn)
        rp_l.append(rp); ip_l.append(ip); rs_l.append(rs); is_l.append(is_)
    y_prompt = rmsnorm(xp, g_final)
    y_sample = rmsnorm(xs, g_final)
    k_win_prompt = jnp.stack(kp_l)
    v_win_prompt = jnp.stack(vp_l)
    k_new_sample = jnp.stack(ks_l)
    v_new_sample = jnp.stack(vs_l)
    ssm_re_prompt = jnp.stack(rp_l)
    ssm_im_prompt = jnp.stack(ip_l)
    ssm_re_sample = jnp.stack(rs_l)
    ssm_im_sample = jnp.stack(is_l)
    return (y_prompt, y_sample, k_win_prompt, v_win_prompt, k_new_sample, v_new_sample,
            ssm_re_prompt, ssm_im_prompt, ssm_re_sample, ssm_im_sample)
```

```python
import functools
import math

import jax
import jax.numpy as jnp
import numpy as np
from jax import lax
from jax.experimental import pallas as pl
from jax.experimental.pallas import tpu as pltpu

F32 = jnp.float32
BF16 = jnp.bfloat16

D_MODEL = 1024
HEAD_DIM = 64
MIX_A = 512
N_HEADS = 8
MIX_B = 512
SSM_GROUP = 16
N_GROUPS = 32
SSM_STATE = 64
PROJ_COLS = 3 * MIX_A + MIX_B
DILATIONS = (1, 4, 16)
TAPS = 128
MAX_WINDOW = 2048
N_BUCKETS = 32
N_EXPERT_GROUPS = 4
EXPERTS_PER_GROUP = 8
N_EXPERTS = 32
D_EXPERT = 256
EPS = 1e-6
NEG_INF = -1e30
SCALE = HEAD_DIM ** -0.5

LANES = 128
ROUTER_LANES = 128
CHUNK = 16
ATT_BLOCK = 2048
VMEM_LIMIT = 56 * 1024 * 1024


def _cparams(n_axes):
    return pltpu.CompilerParams(dimension_semantics=("arbitrary",) * n_axes,
                                vmem_limit_bytes=VMEM_LIMIT)


def _t5_bucket(distance):
    max_exact = N_BUCKETS // 2
    nf = jnp.maximum(distance, 1).astype(F32)
    large = max_exact + (jnp.log(nf / max_exact) / math.log(MAX_WINDOW / max_exact)
                         * (N_BUCKETS - max_exact)).astype(jnp.int32)
    large = jnp.minimum(large, N_BUCKETS - 1)
    return jnp.where(distance < max_exact, distance, large)


def _rms(x, g):
    return x * lax.rsqrt(jnp.mean(x * x, axis=-1, keepdims=True) + EPS) * g


def _inproj_pair_kernel(x_ref, g_ref, w_ref, q_ref, k_ref, v_ref, ks_ref, vs_ref, u_ref):
    h = _rms(x_ref[...], g_ref[...])
    p = jnp.dot(h.astype(BF16), w_ref[...], preferred_element_type=F32)
    for j in range(MIX_A // LANES):
        q_ref[j] = p[:, LANES * j:LANES * (j + 1)] * SCALE
        k_ref[j] = p[:, MIX_A + LANES * j:MIX_A + LANES * (j + 1)]
        v_ref[j] = p[:, 2 * MIX_A + LANES * j:2 * MIX_A + LANES * (j + 1)]
    ks_ref[...] = p[:, MIX_A:2 * MIX_A]
    vs_ref[...] = p[:, 2 * MIX_A:3 * MIX_A]
    u_ref[...] = p[:, 3 * MIX_A:]


def _inproj_pair(x, g, w_bf16, tm):
    b, t, d = x.shape
    npair = MIX_A // LANES
    pair = jax.ShapeDtypeStruct((b, npair, t, LANES), F32)
    tok = jax.ShapeDtypeStruct((b, t, MIX_A), F32)
    pair_spec = pl.BlockSpec((None, npair, tm, LANES), lambda bi, i: (bi, 0, i, 0))
    tok_spec = pl.BlockSpec((None, tm, MIX_A), lambda bi, i: (bi, i, 0))
    return pl.pallas_call(
        _inproj_pair_kernel,
        grid=(b, t // tm),
        in_specs=[pl.BlockSpec((None, tm, d), lambda bi, i: (bi, i, 0)),
                  pl.BlockSpec((1, d), lambda bi, i: (0, 0)),
                  pl.BlockSpec((d, PROJ_COLS), lambda bi, i: (0, 0))],
        out_specs=[pair_spec, pair_spec, pair_spec, tok_spec, tok_spec, tok_spec],
        out_shape=[pair, pair, pair, tok, tok, tok],
        compiler_params=_cparams(2),
        name="inproj_prompt",
    )(x, g, w_bf16)


def _inproj_tok_kernel(x_ref, g_ref, w_ref, q_ref, k_ref, v_ref, u_ref):
    h = _rms(x_ref[...], g_ref[...])
    p = jnp.dot(h.astype(BF16), w_ref[...], preferred_element_type=F32)
    q_ref[...] = p[:, :MIX_A] * SCALE
    k_ref[...] = p[:, MIX_A:2 * MIX_A]
    v_ref[...] = p[:, 2 * MIX_A:3 * MIX_A]
    u_ref[...] = p[:, 3 * MIX_A:]


def _inproj_tok(x, g, w_bf16):
    n, d = x.shape
    out = jax.ShapeDtypeStruct((n, MIX_A), F32)
    return pl.pallas_call(
        _inproj_tok_kernel,
        out_shape=[out, out, out, out],
        compiler_params=pltpu.CompilerParams(vmem_limit_bytes=VMEM_LIMIT),
        name="inproj_sample",
    )(x, g, w_bf16)


def _prompt_bias_ids():
    r = jnp.arange(TAPS, dtype=jnp.int32)[:, None]
    kpos = jnp.arange(2 * TAPS, dtype=jnp.int32)[None, :] - TAPS
    rel = r - kpos
    valid = (rel >= 0) & (rel <= TAPS)
    ids = [jnp.where(valid, _t5_bucket(d * rel), -1) for d in DILATIONS]
    return jnp.stack(ids).astype(jnp.int32)


def _attn_prompt_kernel(rb_ref, ids_ref, q_ref, kc_ref, kp_ref, vc_ref, vp_ref, o_ref,
                        bias_scr, fm_scr, kcat, vcat, m_scr, l_scr, acc_scr):
    bi = pl.program_id(0)
    p = pl.program_id(1)
    i = pl.program_id(2)
    blk = ATT_BLOCK

    @pl.when((bi == 0) & (p == 0) & (i == 0))
    def _build_bias():
        col = lax.broadcasted_iota(jnp.int32, (TAPS, 2 * TAPS), 1)
        fm_scr[0] = jnp.zeros((TAPS, 2 * TAPS), F32)
        fm_scr[1] = jnp.where(col < TAPS, NEG_INF, 0.0).astype(F32)
        for c in range(len(DILATIONS)):
            ids = ids_ref[c]

            def head_body(h, carry, ids=ids, c=c):
                tile = jnp.full((TAPS, 2 * TAPS), NEG_INF, F32)
                for bkt in range(N_BUCKETS):
                    tile = jnp.where(ids == bkt, rb_ref[bkt, h], tile)
                bias_scr[c * N_HEADS + h] = tile
                return carry

            lax.fori_loop(0, N_HEADS, head_body, 0)

    kcat[pl.ds(0, blk), :] = kp_ref[...]
    kcat[pl.ds(blk, blk), :] = kc_ref[...]
    vcat[pl.ds(0, blk), :] = vp_ref[...]
    vcat[pl.ds(blk, blk), :] = vc_ref[...]

    lane = lax.broadcasted_iota(jnp.int32, (1, LANES), 1)
    low = lane < HEAD_DIM

    def rows(ref, start, d):
        if d == 1:
            return ref[pl.ds(start, TAPS), :]
        return ref[pl.ds(start, TAPS, stride=d), :]

    def put(ref, start, d, val):
        if d == 1:
            ref[pl.ds(start, TAPS), :] = val
        else:
            ref[pl.ds(start, TAPS, stride=d), :] = val

    for c, d in enumerate(DILATIONS):
        n_res = d
        n_grp = blk // (d * TAPS)

        def sub_body(s, carry, c=c, d=d, n_grp=n_grp):
            r = s // n_grp
            g = s - r * n_grp
            qs = r + d * TAPS * g
            first = jnp.where((i == 0) & (g == 0), 1, 0)
            q = rows(q_ref, qs, d)
            k = jnp.concatenate([rows(kcat, blk + qs - d * TAPS, d), rows(kcat, blk + qs, d)],
                                axis=0).astype(BF16)
            v = jnp.concatenate([rows(vcat, blk + qs - d * TAPS, d), rows(vcat, blk + qs, d)],
                                axis=0).astype(BF16)
            fmask = fm_scr[first]
            res = []
            for e in range(2):
                sel = low if e == 0 else jnp.logical_not(low)
                qe = jnp.where(sel, q, 0.0).astype(BF16)
                s_ = lax.dot_general(qe, k, (((1,), (1,)), ((), ())), preferred_element_type=F32)
                s_ = s_ + bias_scr[c * N_HEADS + 2 * p + e] + fmask
                m = jnp.max(s_, axis=1, keepdims=True)
                pe = jnp.exp(s_ - m)
                l = jnp.sum(pe, axis=1, keepdims=True)
                o = jnp.dot(pe.astype(BF16), v, preferred_element_type=F32)
                res.append((m, l, o))
            (m0, l0, o0), (m1, l1, o1) = res
            m_new = jnp.where(low, m0, m1)
            l_new = jnp.where(low, l0, l1)
            o_new = jnp.where(low, o0, o1)
            if c == 0:
                put(m_scr, qs, d, m_new)
                put(l_scr, qs, d, l_new)
                put(acc_scr, qs, d, o_new)
            else:
                m_old = rows(m_scr, qs, d)
                m_tot = jnp.maximum(m_old, m_new)
                a_old = jnp.exp(m_old - m_tot)
                a_new = jnp.exp(m_new - m_tot)
                put(m_scr, qs, d, m_tot)
                put(l_scr, qs, d, rows(l_scr, qs, d) * a_old + l_new * a_new)
                put(acc_scr, qs, d, rows(acc_scr, qs, d) * a_old + o_new * a_new)
            return carry

        lax.fori_loop(0, n_res * n_grp, sub_body, 0)

    o_ref[...] = acc_scr[...] / l_scr[...]


def _attn_prompt(q, k, v, rel_bias):
    b, npair, t, _ = q.shape
    blk = ATT_BLOCK
    cur = pl.BlockSpec((None, None, blk, LANES), lambda bi, p, i: (bi, p, i, 0))
    prev = pl.BlockSpec((None, None, blk, LANES), lambda bi, p, i: (bi, p, jnp.maximum(i - 1, 0), 0))
    n_tiles = len(DILATIONS) * N_HEADS
    return pl.pallas_call(
        _attn_prompt_kernel,
        grid=(b, npair, t // blk),
        in_specs=[pl.BlockSpec(memory_space=pltpu.SMEM),
                  pl.BlockSpec((len(DILATIONS), TAPS, 2 * TAPS), lambda bi, p, i: (0, 0, 0)),
                  cur, cur, prev, cur, prev],
        out_specs=cur,
        out_shape=jax.ShapeDtypeStruct(q.shape, F32),
        scratch_shapes=[pltpu.VMEM((n_tiles, TAPS, 2 * TAPS), F32),
                        pltpu.VMEM((2, TAPS, 2 * TAPS), F32),
                        pltpu.VMEM((2 * blk, LANES), F32),
                        pltpu.VMEM((2 * blk, LANES), F32),
                        pltpu.VMEM((blk, LANES), F32),
                        pltpu.VMEM((blk, LANES), F32),
                        pltpu.VMEM((blk, LANES), F32)],
        compiler_params=_cparams(3),
        name="attn_prompt",
    )(rel_bias, _prompt_bias_ids(), q, k, k, v, v)


DEC_T = 4
DEC_DIL_ROWS = 96
DEC_TAIL = 512
DEC_KEYS = 1024


def _decode_tables():
    qi = jnp.arange(DEC_T, dtype=jnp.int32)[:, None]
    i2 = jnp.repeat(jnp.arange(DEC_T, dtype=jnp.int32), DEC_DIL_ROWS)[None, :]
    m = jnp.tile(jnp.arange(DEC_DIL_ROWS, dtype=jnp.int32), DEC_T)[None, :]
    dist_d = MAX_WINDOW + qi - (16 * m + i2)
    mult_d = ((dist_d % 16) == 0).astype(jnp.int32)
    tt = jnp.arange(DEC_TAIL, dtype=jnp.int32)[None, :]
    dist_t = DEC_TAIL + qi - tt
    mult_t = ((dist_t <= 128).astype(jnp.int32)
              + (((dist_t % 4) == 0) & (dist_t <= 512)).astype(jnp.int32)
              + ((dist_t % 16) == 0).astype(jnp.int32))
    jj = jnp.arange(DEC_T, dtype=jnp.int32)[None, :]
    dist_n = qi - jj
    mult_n = jnp.where(dist_n == 0, 3, jnp.where(dist_n > 0, 1, 0)).astype(jnp.int32)
    pad = DEC_KEYS - (DEC_T * DEC_DIL_ROWS + DEC_TAIL + DEC_T)
    dist = jnp.concatenate([dist_d, dist_t, jnp.maximum(dist_n, 0), jnp.zeros((DEC_T, pad), jnp.int32)], axis=1)
    mult = jnp.concatenate([mult_d, mult_t, mult_n, jnp.zeros((DEC_T, pad), jnp.int32)], axis=1)
    ids = jnp.where(mult > 0, _t5_bucket(dist), -1).astype(jnp.int32)
    ids = jnp.repeat(ids, N_HEADS, axis=0)
    mult = jnp.repeat(mult, N_HEADS, axis=0).astype(F32)
    return ids, mult


def _attn_decode_kernel(rbt_ref, ids_ref, mult_ref, q_ref, kn_ref, vn_ref, kt_ref, vt_ref,
                        kd_ref, vd_ref, o_ref, bias_scr):
    n_rows = DEC_T * N_HEADS

    @pl.when(pl.program_id(0) == 0)
    def _build_bias():
        ids = ids_ref[...]
        tile = jnp.full((n_rows, DEC_KEYS), NEG_INF, F32)
        for bkt in range(N_BUCKETS):
            tile = jnp.where(ids == bkt, rbt_ref[:, bkt:bkt + 1], tile)
        bias_scr[...] = tile

    lane = lax.broadcasted_iota(jnp.int32, (N_HEADS, MIX_A), 1)
    head = lax.broadcasted_iota(jnp.int32, (N_HEADS, MIX_A), 0)
    hmask = (lane // HEAD_DIM) == head

    q = q_ref[...]
    qm = jnp.where(hmask[None], q[:, None, :], 0.0).reshape(n_rows, MIX_A).astype(BF16)
    pad = DEC_KEYS - (DEC_T * DEC_DIL_ROWS + DEC_TAIL + DEC_T)

    def keys(d_ref, t_ref, n_ref):
        parts = [d_ref[:, MIX_A * j:MIX_A * (j + 1)] for j in range(DEC_T)]
        parts += [t_ref[...], n_ref[...], jnp.zeros((pad, MIX_A), F32)]
        return jnp.concatenate(parts, axis=0).astype(BF16)

    k = keys(kd_ref, kt_ref, kn_ref)
    v = keys(vd_ref, vt_ref, vn_ref)
    s = lax.dot_general(qm, k, (((1,), (1,)), ((), ())), preferred_element_type=F32)
    s = s + bias_scr[...]
    m = jnp.max(s, axis=1, keepdims=True)
    pe = jnp.exp(s - m) * mult_ref[...]
    l = jnp.sum(pe, axis=1, keepdims=True)
    o = jnp.dot(pe.astype(BF16), v, preferred_element_type=F32) / l
    o3 = o.reshape(DEC_T, N_HEADS, MIX_A)
    o_ref[...] = jnp.sum(jnp.where(hmask[None], o3, 0.0), axis=1)


def _attn_decode(q, k_new, v_new, cache_k, cache_v, rel_bias):
    bd = q.shape[0]
    ids, mult = _decode_tables()
    rbt = jnp.tile(rel_bias.T, (DEC_T, 1))
    ck2 = cache_k.reshape(bd, MAX_WINDOW // 16, 16 * MIX_A)
    cv2 = cache_v.reshape(bd, MAX_WINDOW // 16, 16 * MIX_A)
    new_spec = pl.BlockSpec((None, DEC_T, MIX_A), lambda b: (b, 0, 0))
    tail_spec = pl.BlockSpec((None, DEC_TAIL, MIX_A), lambda b: (b, MAX_WINDOW // DEC_TAIL - 1, 0))
    dil_spec = pl.BlockSpec((None, DEC_DIL_ROWS, DEC_T * MIX_A), lambda b: (b, 0, 0))
    n_rows = DEC_T * N_HEADS
    const = lambda shape: pl.BlockSpec(shape, lambda b: (0, 0))
    return pl.pallas_call(
        _attn_decode_kernel,
        grid=(bd,),
        in_specs=[const((n_rows, N_BUCKETS)), const((n_rows, DEC_KEYS)), const((n_rows, DEC_KEYS)),
                  new_spec, new_spec, new_spec, tail_spec, tail_spec, dil_spec, dil_spec],
        out_specs=new_spec,
        out_shape=jax.ShapeDtypeStruct((bd, DEC_T, MIX_A), F32),
        scratch_shapes=[pltpu.VMEM((n_rows, DEC_KEYS), F32)],
        compiler_params=_cparams(1),
        name="attn_decode",
    )(rbt, ids, mult, q, k_new, v_new, cache_k, cache_v, ck2, cv2)


def _ssm_matrices(a_re, a_im, log_dt, b_re, b_im, c_re, c_im, d_skip):
    hi = lax.Precision.HIGHEST
    g, p = a_re.shape
    dt = jnp.exp(log_dt)[:, None]
    decay = jnp.exp(a_re * dt)
    abar_re = decay * jnp.cos(a_im * dt)
    abar_im = decay * jnp.sin(a_im * dt)
    inv = 1.0 / (a_re * a_re + a_im * a_im)
    coef_re = ((abar_re - 1.0) * a_re + abar_im * a_im) * inv
    coef_im = (abar_im * a_re - (abar_re - 1.0) * a_im) * inv
    bbar_re = coef_re[..., None] * b_re - coef_im[..., None] * b_im
    bbar_im = coef_re[..., None] * b_im + coef_im[..., None] * b_re
    pw_re = [jnp.ones_like(abar_re)]
    pw_im = [jnp.zeros_like(abar_im)]
    for _ in range(CHUNK):
        pr, pi = pw_re[-1], pw_im[-1]
        pw_re.append(pr * abar_re - pi * abar_im)
        pw_im.append(pr * abar_im + pi * abar_re)
    pw_re = jnp.stack(pw_re)
    pw_im = jnp.stack(pw_im)
    cp_re = c_re[None] * pw_re[:, :, None, :] - c_im[None] * pw_im[:, :, None, :]
    cp_im = c_re[None] * pw_im[:, :, None, :] + c_im[None] * pw_re[:, :, None, :]
    kern = (jnp.einsum('mgcp,gpd->mgcd', cp_re[:CHUNK], bbar_re, precision=hi)
            - jnp.einsum('mgcp,gpd->mgcd', cp_im[:CHUNK], bbar_im, precision=hi))
    ti = jnp.arange(CHUNK)[:, None]
    to = jnp.arange(CHUNK)[None, :]
    lag = to - ti
    blocks = kern[jnp.clip(lag, 0, CHUNK - 1)]
    blocks = jnp.where((lag >= 0)[:, :, None, None, None], blocks, 0.0)
    t_intra = jnp.transpose(blocks, (2, 0, 4, 1, 3)).reshape(g, CHUNK * SSM_GROUP, CHUNK * SSM_GROUP)
    rev_re = pw_re[CHUNK - 1 - jnp.arange(CHUNK)]
    rev_im = pw_im[CHUNK - 1 - jnp.arange(CHUNK)]
    in_re = rev_re[..., None] * bbar_re[None] - rev_im[..., None] * bbar_im[None]
    in_im = rev_re[..., None] * bbar_im[None] + rev_im[..., None] * bbar_re[None]
    t_in_re = jnp.transpose(in_re, (1, 0, 3, 2)).reshape(g, CHUNK * SSM_GROUP, p)
    t_in_im = jnp.transpose(in_im, (1, 0, 3, 2)).reshape(g, CHUNK * SSM_GROUP, p)
    t_out_re = jnp.transpose(cp_re[1:], (1, 3, 0, 2)).reshape(g, p, CHUNK * SSM_GROUP)
    t_out_im = -jnp.transpose(cp_im[1:], (1, 3, 0, 2)).reshape(g, p, CHUNK * SSM_GROUP)
    d_tile = jnp.tile(d_skip.reshape(g, 1, SSM_GROUP), (1, 1, CHUNK))
    return dict(t_intra=t_intra, t_in_re=t_in_re, t_in_im=t_in_im, t_out_re=t_out_re,
                t_out_im=t_out_im, d_tile=d_tile, pw_re=pw_re, pw_im=pw_im)


def _ssm_prompt_kernel(u_ref, ti_ref, tinr_ref, tini_ref, toutr_ref, touti_ref, d_ref, ar_ref, ai_ref,
                       y_ref, sre_ref, sim_ref, pr_scr, pi_scr, sr_scr, si_scr):
    nc = u_ref.shape[0]
    u = u_ref[...]
    ub = u.astype(BF16)
    pr_scr[...] = jnp.dot(ub, tinr_ref[...], preferred_element_type=F32)
    pi_scr[...] = jnp.dot(ub, tini_ref[...], preferred_element_type=F32)
    ar = ar_ref[...]
    ai = ai_ref[...]

    def step(c, carry):
        sr, si = carry
        sr_scr[pl.ds(c, 1), :] = sr
        si_scr[pl.ds(c, 1), :] = si
        nr = ar * sr - ai * si + pr_scr[pl.ds(c, 1), :]
        ni = ar * si + ai * sr + pi_scr[pl.ds(c, 1), :]
        return nr, ni

    zero = jnp.zeros((1, SSM_STATE), F32)
    sr, si = lax.fori_loop(0, nc, step, (zero, zero), unroll=8)
    sre_ref[...] = sr
    sim_ref[...] = si
    y = jnp.dot(ub, ti_ref[...], preferred_element_type=F32)
    y = y + jnp.dot(sr_scr[...].astype(BF16), toutr_ref[...], preferred_element_type=F32)
    y = y + jnp.dot(si_scr[...].astype(BF16), touti_ref[...], preferred_element_type=F32)
    y_ref[...] = y + d_ref[...] * u


def _ssm_prompt(ug, mats):
    b, g, nc, w = ug.shape
    p = SSM_STATE
    gspec = lambda shape: pl.BlockSpec((None,) + shape, lambda bi, gi: (gi, 0, 0))
    io = pl.BlockSpec((None, None, nc, w), lambda bi, gi: (bi, gi, 0, 0))
    st = pl.BlockSpec((None, None, 1, p), lambda bi, gi: (bi, gi, 0, 0))
    return pl.pallas_call(
        _ssm_prompt_kernel,
        grid=(b, g),
        in_specs=[io, gspec((w, w)), gspec((w, p)), gspec((w, p)), gspec((p, w)), gspec((p, w)),
                  gspec((1, w)), gspec((1, p)), gspec((1, p))],
        out_specs=[io, st, st],
        out_shape=[jax.ShapeDtypeStruct(ug.shape, F32),
                   jax.ShapeDtypeStruct((b, g, 1, p), F32), jax.ShapeDtypeStruct((b, g, 1, p), F32)],
        scratch_shapes=[pltpu.VMEM((nc, p), F32)] * 4,
        compiler_params=_cparams(2),
        name="ssm_prompt",
    )(ug, mats['t_intra'].astype(BF16), mats['t_in_re'].astype(BF16), mats['t_in_im'].astype(BF16),
      mats['t_out_re'].astype(BF16), mats['t_out_im'].astype(BF16), mats['d_tile'],
      mats['pw_re'][CHUNK][:, None, :], mats['pw_im'][CHUNK][:, None, :])


def _ssm_decode_kernel(u_ref, s0r_ref, s0i_ref, ti_ref, tinr_ref, tini_ref, toutr_ref, touti_ref, d_ref,
                       ar_ref, ai_ref, y_ref, sre_ref, sim_ref):
    u = u_ref[...]
    ub = u.astype(BF16)
    s0r = s0r_ref[...]
    s0i = s0i_ref[...]
    y = jnp.dot(ub, ti_ref[...], preferred_element_type=F32)
    y = y + jnp.dot(s0r.astype(BF16), toutr_ref[...], preferred_element_type=F32)
    y = y + jnp.dot(s0i.astype(BF16), touti_ref[...], preferred_element_type=F32)
    y_ref[...] = y + d_ref[...] * u
    ar = ar_ref[...]
    ai = ai_ref[...]
    sre_ref[...] = ar * s0r - ai * s0i + jnp.dot(ub, tinr_ref[...], preferred_element_type=F32)
    sim_ref[...] = ar * s0i + ai * s0r + jnp.dot(ub, tini_ref[...], preferred_element_type=F32)


def _ssm_decode(ug, s0_re, s0_im, mats):
    g, bd, w = ug.shape
    p = SSM_STATE
    lo = (CHUNK - DEC_T) * SSM_GROUP
    gspec = lambda shape: pl.BlockSpec((None,) + shape, lambda gi: (gi, 0, 0))
    return pl.pallas_call(
        _ssm_decode_kernel,
        grid=(g,),
        in_specs=[gspec((bd, w)), gspec((bd, p)), gspec((bd, p)), gspec((w, w)), gspec((w, p)), gspec((w, p)),
                  gspec((p, w)), gspec((p, w)), gspec((1, w)), gspec((1, p)), gspec((1, p))],
        out_specs=[gspec((bd, w)), gspec((bd, p)), gspec((bd, p))],
        out_shape=[jax.ShapeDtypeStruct((g, bd, w), F32), jax.ShapeDtypeStruct((g, bd, p), F32),
                   jax.ShapeDtypeStruct((g, bd, p), F32)],
        compiler_params=_cparams(1),
        name="ssm_decode",
    )(ug, s0_re, s0_im, mats['t_intra'][:, :w, :w].astype(BF16),
      mats['t_in_re'][:, lo:, :].astype(BF16), mats['t_in_im'][:, lo:, :].astype(BF16),
      mats['t_out_re'][:, :, :w].astype(BF16), mats['t_out_im'][:, :, :w].astype(BF16),
      mats['d_tile'][:, :, :w], mats['pw_re'][DEC_T][:, None, :], mats['pw_im'][DEC_T][:, None, :])


def _gelu_tanh(x):
    return 0.5 * x * (1.0 + jnp.tanh(math.sqrt(2.0 / math.pi) * (x + 0.044715 * (x * x * x))))


def _sigmoid(x):
    return 1.0 / (1.0 + jnp.exp(-x))


def _route(hn, wr_ref):
    n = hn.shape[0]
    logits = jnp.dot(hn, wr_ref[...], preferred_element_type=F32, precision=lax.Precision.HIGHEST)
    lidx = lax.broadcasted_iota(jnp.int32, (n, ROUTER_LANES), 1)
    is_e = lidx < N_EXPERTS
    is_g = jnp.logical_and(lidx >= N_EXPERTS, lidx < N_EXPERTS + N_EXPERT_GROUPS)
    gmax = jnp.max(jnp.where(is_g, logits, -jnp.inf), axis=1, keepdims=True)
    g_prob = 1.0 / jnp.sum(jnp.where(is_g, jnp.exp(logits - gmax), 0.0), axis=1, keepdims=True)
    g_sel = jnp.min(jnp.where(jnp.logical_and(is_g, logits == gmax), lidx - N_EXPERTS, N_EXPERT_GROUPS),
                    axis=1, keepdims=True)
    in_grp = jnp.logical_and(is_e, (lidx // EXPERTS_PER_GROUP) == g_sel)
    l1 = jnp.max(jnp.where(in_grp, logits, -jnp.inf), axis=1, keepdims=True)
    i1 = jnp.min(jnp.where(jnp.logical_and(in_grp, logits == l1), lidx, ROUTER_LANES), axis=1, keepdims=True)
    rest = jnp.logical_and(in_grp, lidx != i1)
    l2 = jnp.max(jnp.where(rest, logits, -jnp.inf), axis=1, keepdims=True)
    i2 = jnp.min(jnp.where(jnp.logical_and(rest, logits == l2), lidx, ROUTER_LANES), axis=1, keepdims=True)
    e2 = jnp.exp(l2 - l1)
    w1 = g_prob / (1.0 + e2)
    w2 = g_prob * e2 / (1.0 + e2)
    return jnp.where(lidx == i1, w1, 0.0) + jnp.where(lidx == i2, w2, 0.0)


def _post_body(oa, ys, x, ga_ref, gb_ref, wglu_ref, bglu_ref, wout_ref, gf_ref, wr_ref,
               x1_ref, hn_ref, gates_ref):
    z = _gelu_tanh(ys)
    gate = _sigmoid(jnp.dot(z.astype(BF16), wglu_ref[...], preferred_element_type=F32) + bglu_ref[...])
    ob = z * gate
    mixed = jnp.concatenate([_rms(oa, ga_ref[...]), _rms(ob, gb_ref[...])], axis=1).astype(BF16)
    x1 = x + jnp.dot(mixed, wout_ref[...], preferred_element_type=F32)
    x1_ref[...] = x1
    hn = _rms(x1, gf_ref[...])
    hn_ref[...] = hn.astype(BF16)
    gates_ref[...] = _route(hn, wr_ref)


def _post_pair_kernel(oa_ref, ys_ref, x_ref, *rest):
    oa = jnp.concatenate([oa_ref[j] for j in range(MIX_A // LANES)], axis=1)
    _post_body(oa, ys_ref[...], x_ref[...], *rest)


def _post_tok_kernel(oa_ref, ys_ref, x_ref, *rest):
    _post_body(oa_ref[...], ys_ref[...], x_ref[...], *rest)


def _post_mix(oa, ys, x, w, tm, pair):
    b, t, d = x.shape
    row = lambda width: pl.BlockSpec((None, tm, width), lambda bi, i: (bi, i, 0))
    const = lambda a: pl.BlockSpec(a.shape, lambda bi, i: (0,) * a.ndim)
    oa_spec = (pl.BlockSpec((None, MIX_A // LANES, tm, LANES), lambda bi, i: (bi, 0, i, 0)) if pair
               else row(MIX_A))
    weights = [w['g_out_a'], w['g_out_b'], w['w_glu'], w['b_glu'], w['w_out'], w['g_ffn'], w['w_router']]
    return pl.pallas_call(
        _post_pair_kernel if pair else _post_tok_kernel,
        grid=(b, t // tm),
        in_specs=[oa_spec, row(MIX_B), row(d)] + [const(a) for a in weights],
        out_specs=[row(d), row(d), row(ROUTER_LANES)],
        out_shape=[jax.ShapeDtypeStruct((b, t, d), F32), jax.ShapeDtypeStruct((b, t, d), BF16),
                   jax.ShapeDtypeStruct((b, t, ROUTER_LANES), F32)],
        compiler_params=_cparams(2),
        name="post_mix_prompt" if pair else "post_mix_sample",
    )(oa, ys, x, *weights)


def _moe_kernel(hn_ref, gates_ref, x1_ref, wg_ref, wu_ref, wd_ref, gfin_ref, y_ref, acc_scr):
    e = pl.program_id(1)

    @pl.when(e == 0)
    def _init():
        acc_scr[...] = jnp.zeros_like(acc_scr)

    hn = hn_ref[...]
    a = jnp.dot(hn, wg_ref[...], preferred_element_type=F32)
    u = jnp.dot(hn, wu_ref[...], preferred_element_type=F32)
    eidx = lax.broadcasted_iota(jnp.int32, gates_ref.shape, 1)
    gate = jnp.sum(jnp.where(eidx == e, gates_ref[...], 0.0), axis=1, keepdims=True)
    act = (a * _sigmoid(a)) * u * gate
    acc_scr[...] += jnp.dot(act.astype(BF16), wd_ref[...], preferred_element_type=F32)

    @pl.when(e == pl.num_programs(1) - 1)
    def _fin():
        y_ref[...] = _rms(x1_ref[...] + acc_scr[...], gfin_ref[...])


def _moe(hn, gates, x1, wg, wu, wd, g_final, tm):
    n, d = x1.shape
    row = lambda width: pl.BlockSpec((tm, width), lambda i, e: (i, 0))
    return pl.pallas_call(
        _moe_kernel,
        grid=(n // tm, N_EXPERTS),
        in_specs=[row(d), row(ROUTER_LANES), row(d),
                  pl.BlockSpec((None, d, D_EXPERT), lambda i, e: (e, 0, 0)),
                  pl.BlockSpec((None, d, D_EXPERT), lambda i, e: (e, 0, 0)),
                  pl.BlockSpec((None, D_EXPERT, d), lambda i, e: (e, 0, 0)),
                  pl.BlockSpec((1, d), lambda i, e: (0, 0))],
        out_specs=row(d),
        out_shape=jax.ShapeDtypeStruct((n, d), F32),
        scratch_shapes=[pltpu.VMEM((tm, d), F32)],
        compiler_params=_cparams(2),
        name="moe",
    )(hn, gates, x1, wg, wu, wd, g_final)


def kernel(x_prompt, x_sample, cache_k, cache_v, state_ssm_re, state_ssm_im, rel_bias, g_mix, w_in, g_out_a, g_out_b, w_out, ssm_a_re, ssm_a_im, ssm_log_dt, ssm_b_re, ssm_b_im, ssm_c_re, ssm_c_im, ssm_d, w_glu, b_glu, g_ffn, w_router_group, w_router_expert, w_expert_gate, w_expert_up, w_expert_down, g_final):
    depth = g_mix.shape[0]
    assert depth == 1, "kernel is written for the single-layer configuration of the problem"
    l = 0
    b, t, d = x_prompt.shape
    bd, ts, _ = x_sample.shape
    assert ts == DEC_T and t % ATT_BLOCK == 0 and cache_k.shape[2] == MAX_WINDOW
    keep = min(MAX_WINDOW, t)

    w_in_b = w_in[l].astype(BF16)
    gm = g_mix[l][None, :]
    post_w = dict(
        g_out_a=g_out_a[l][None, :], g_out_b=g_out_b[l][None, :], w_glu=w_glu[l].astype(BF16),
        b_glu=b_glu[l][None, :], w_out=w_out[l].astype(BF16), g_ffn=g_ffn[l][None, :],
        w_router=jnp.concatenate(
            [jnp.transpose(w_router_expert[l], (1, 0, 2)).reshape(d, N_EXPERTS), w_router_group[l],
             jnp.zeros((d, ROUTER_LANES - N_EXPERTS - N_EXPERT_GROUPS), F32)], axis=1))
    wg = w_expert_gate[l].astype(BF16)
    wu = w_expert_up[l].astype(BF16)
    wd = w_expert_down[l].astype(BF16)
    gfin = g_final[None, :]
    mats = _ssm_matrices(ssm_a_re[l], ssm_a_im[l], ssm_log_dt[l], ssm_b_re[l], ssm_b_im[l],
                         ssm_c_re[l], ssm_c_im[l], ssm_d[l])

    qp, kp, vp, k_tok, v_tok, u_tok = _inproj_pair(x_prompt, gm, w_in_b, tm=512)
    oa_p = _attn_prompt(qp, kp, vp, rel_bias)
    ug = jnp.transpose(u_tok.reshape(b, t, N_GROUPS, SSM_GROUP), (0, 2, 1, 3))
    ug = ug.reshape(b, N_GROUPS, t // CHUNK, CHUNK * SSM_GROUP)
    yg, rp, ip = _ssm_prompt(ug, mats)
    ys_p = jnp.transpose(yg.reshape(b, N_GROUPS, t, SSM_GROUP), (0, 2, 1, 3)).reshape(b, t, MIX_B)
    x1_p, hn_p, gates_p = _post_mix(oa_p, ys_p, x_prompt, post_w, tm=512, pair=True)
    y_p = _moe(hn_p.reshape(b * t, d), gates_p.reshape(b * t, ROUTER_LANES), x1_p.reshape(b * t, d),
               wg, wu, wd, gfin, tm=1024)

    n_s = bd * ts
    qs, ks, vs, us = _inproj_tok(x_sample.reshape(n_s, d), gm, w_in_b)
    oa_s = _attn_decode(qs.reshape(bd, ts, MIX_A), ks.reshape(bd, ts, MIX_A), vs.reshape(bd, ts, MIX_A),
                        cache_k[l].reshape(bd, MAX_WINDOW, MIX_A), cache_v[l].reshape(bd, MAX_WINDOW, MIX_A),
                        rel_bias)
    ugs = jnp.transpose(us.reshape(bd, ts, N_GROUPS, SSM_GROUP), (2, 0, 1, 3)).reshape(N_GROUPS, bd, ts * SSM_GROUP)
    s0r = jnp.transpose(state_ssm_re[l], (1, 0, 2))
    s0i = jnp.transpose(state_ssm_im[l], (1, 0, 2))
    ygs, rs, is_ = _ssm_decode(ugs, s0r, s0i, mats)
    ys_s = jnp.transpose(ygs.reshape(N_GROUPS, bd, ts, SSM_GROUP), (1, 2, 0, 3)).reshape(n_s, MIX_B)
    x1_s, hn_s, gates_s = _post_mix(oa_s.reshape(1, n_s, MIX_A), ys_s.reshape(1, n_s, MIX_B),
                                    x_sample.reshape(1, n_s, d), post_w, tm=n_s, pair=False)
    y_s = _moe(hn_s.reshape(n_s, d), gates_s.reshape(n_s, ROUTER_LANES), x1_s.reshape(n_s, d),
               wg, wu, wd, gfin, tm=n_s)

    y_prompt = y_p.reshape(b, t, d)
    y_sample = y_s.reshape(bd, ts, d)
    k_win = k_tok[:, t - keep:].reshape(1, b, keep, N_HEADS, HEAD_DIM)
    v_win = v_tok[:, t - keep:].reshape(1, b, keep, N_HEADS, HEAD_DIM)
    k_new = ks.reshape(1, bd, ts, N_HEADS, HEAD_DIM)
    v_new = vs.reshape(1, bd, ts, N_HEADS, HEAD_DIM)
    return (y_prompt, y_sample, k_win, v_win, k_new, v_new,
            rp.reshape(1, b, N_GROUPS, SSM_STATE), ip.reshape(1, b, N_GROUPS, SSM_STATE),
            jnp.transpose(rs, (1, 0, 2))[None], jnp.transpose(is_, (1, 0, 2))[None])
```

```python
import functools
import math

import jax
import jax.numpy as jnp
import numpy as np
from jax import lax
from jax.experimental import pallas as pl
from jax.experimental.pallas import tpu as pltpu

F32 = jnp.float32
BF16 = jnp.bfloat16

D_MODEL = 1024
HEAD_DIM = 64
MIX_A = 512
N_HEADS = 8
MIX_B = 512
SSM_GROUP = 16
N_GROUPS = 32
SSM_STATE = 64
PROJ_COLS = 3 * MIX_A + MIX_B
DILATIONS = (1, 4, 16)
TAPS = 128
MAX_WINDOW = 2048
N_BUCKETS = 32
N_EXPERT_GROUPS = 4
EXPERTS_PER_GROUP = 8
N_EXPERTS = 32
D_EXPERT = 256
EPS = 1e-6
NEG_INF = -1e30
SCALE = HEAD_DIM ** -0.5

LANES = 128
ROUTER_LANES = 128
CHUNK = 16
ATT_BLOCK = 2048
VMEM_LIMIT = 56 * 1024 * 1024


def _cparams(n_axes):
    return pltpu.CompilerParams(dimension_semantics=("arbitrary",) * n_axes,
                                vmem_limit_bytes=VMEM_LIMIT)


def _t5_bucket(distance):
    max_exact = N_BUCKETS // 2
    nf = jnp.maximum(distance, 1).astype(F32)
    large = max_exact + (jnp.log(nf / max_exact) / math.log(MAX_WINDOW / max_exact)
                         * (N_BUCKETS - max_exact)).astype(jnp.int32)
    large = jnp.minimum(large, N_BUCKETS - 1)
    return jnp.where(distance < max_exact, distance, large)


def _rms(x, g):
    return x * lax.rsqrt(jnp.mean(x * x, axis=-1, keepdims=True) + EPS) * g


def _inproj_pair_kernel(x_ref, g_ref, w_ref, q_ref, k_ref, v_ref, ks_ref, vs_ref, u_ref):
    h = _rms(x_ref[...], g_ref[...])
    p = jnp.dot(h.astype(BF16), w_ref[...], preferred_element_type=F32)
    for j in range(MIX_A // LANES):
        q_ref[j] = p[:, LANES * j:LANES * (j + 1)] * SCALE
        k_ref[j] = p[:, MIX_A + LANES * j:MIX_A + LANES * (j + 1)]
        v_ref[j] = p[:, 2 * MIX_A + LANES * j:2 * MIX_A + LANES * (j + 1)]
    ks_ref[...] = p[:, MIX_A:2 * MIX_A]
    vs_ref[...] = p[:, 2 * MIX_A:3 * MIX_A]
    u_ref[...] = p[:, 3 * MIX_A:]


def _inproj_pair(x, g, w_bf16, tm):
    b, t, d = x.shape
    npair = MIX_A // LANES
    pair = jax.ShapeDtypeStruct((b, npair, t, LANES), F32)
    tok = jax.ShapeDtypeStruct((b, t, MIX_A), F32)
    pair_spec = pl.BlockSpec((None, npair, tm, LANES), lambda bi, i: (bi, 0, i, 0))
    tok_spec = pl.BlockSpec((None, tm, MIX_A), lambda bi, i: (bi, i, 0))
    return pl.pallas_call(
        _inproj_pair_kernel,
        grid=(b, t // tm),
        in_specs=[pl.BlockSpec((None, tm, d), lambda bi, i: (bi, i, 0)),
                  pl.BlockSpec((1, d), lambda bi, i: (0, 0)),
                  pl.BlockSpec((d, PROJ_COLS), lambda bi, i: (0, 0))],
        out_specs=[pair_spec, pair_spec, pair_spec, tok_spec, tok_spec, tok_spec],
        out_shape=[pair, pair, pair, tok, tok, tok],
        compiler_params=_cparams(2),
        name="inproj_prompt",
    )(x, g, w_bf16)


def _inproj_tok_kernel(x_ref, g_ref, w_ref, q_ref, k_ref, v_ref, u_ref):
    h = _rms(x_ref[...], g_ref[...])
    p = jnp.dot(h.astype(BF16), w_ref[...], preferred_element_type=F32)
    q_ref[...] = p[:, :MIX_A] * SCALE
    k_ref[...] = p[:, MIX_A:2 * MIX_A]
    v_ref[...] = p[:, 2 * MIX_A:3 * MIX_A]
    u_ref[...] = p[:, 3 * MIX_A:]


def _inproj_tok(x, g, w_bf16):
    n, d = x.shape
    out = jax.ShapeDtypeStruct((n, MIX_A), F32)
    return pl.pallas_call(
        _inproj_tok_kernel,
        out_shape=[out, out, out, out],
        compiler_params=pltpu.CompilerParams(vmem_limit_bytes=VMEM_LIMIT),
        name="inproj_sample",
    )(x, g, w_bf16)


def _prompt_bias_ids():
    r = jnp.arange(TAPS, dtype=jnp.int32)[:, None]
    kpos = jnp.arange(2 * TAPS, dtype=jnp.int32)[None, :] - TAPS
    rel = r - kpos
    valid = (rel >= 0) & (rel <= TAPS)
    ids = [jnp.where(valid, _t5_bucket(d * rel), -1) for d in DILATIONS]
    return jnp.stack(ids).astype(jnp.int32)


def _attn_prompt_kernel(rb_ref, ids_ref, q_ref, kc_ref, kp_ref, vc_ref, vp_ref, o_ref,
                        bias_scr, fm_scr, kcat, vcat, m_scr, l_scr, acc_scr):
    bi = pl.program_id(0)
    p = pl.program_id(1)
    i = pl.program_id(2)
    blk = ATT_BLOCK

    @pl.when((bi == 0) & (p == 0) & (i == 0))
    def _build_bias():
        col = lax.broadcasted_iota(jnp.int32, (TAPS, 2 * TAPS), 1)
        fm_scr[0] = jnp.zeros((TAPS, 2 * TAPS), F32)
        fm_scr[1] = jnp.where(col < TAPS, NEG_INF, 0.0).astype(F32)
        for c in range(len(DILATIONS)):
            ids = ids_ref[c]

            def head_body(h, carry, ids=ids, c=c):
                tile = jnp.full((TAPS, 2 * TAPS), NEG_INF, F32)
                for bkt in range(N_BUCKETS):
                    tile = jnp.where(ids == bkt, rb_ref[bkt, h], tile)
                bias_scr[c * N_HEADS + h] = tile
                return carry

            lax.fori_loop(0, N_HEADS, head_body, 0)

    kcat[pl.ds(0, blk), :] = kp_ref[...]
    kcat[pl.ds(blk, blk), :] = kc_ref[...]
    vcat[pl.ds(0, blk), :] = vp_ref[...]
    vcat[pl.ds(blk, blk), :] = vc_ref[...]

    lane = lax.broadcasted_iota(jnp.int32, (1, LANES), 1)
    low = lane < HEAD_DIM

    def rows(ref, start, d):
        if d == 1:
            return ref[pl.ds(start, TAPS), :]
        return ref[pl.ds(start, TAPS, stride=d), :]

    def put(ref, start, d, val):
        if d == 1:
            ref[pl.ds(start, TAPS), :] = val
        else:
            ref[pl.ds(start, TAPS, stride=d), :] = val

    for c, d in enumerate(DILATIONS):
        n_res = d
        n_grp = blk // (d * TAPS)

        def sub_body(s, carry, c=c, d=d, n_grp=n_grp):
            r = s // n_grp
            g = s - r * n_grp
            qs = r + d * TAPS * g
            first = jnp.where((i == 0) & (g == 0), 1, 0)
            q = rows(q_ref, qs, d)
            k = jnp.concatenate([rows(kcat, blk + qs - d * TAPS, d), rows(kcat, blk + qs, d)],
                                axis=0).astype(BF16)
            v = jnp.concatenate([rows(vcat, blk + qs - d * TAPS, d), rows(vcat, blk + qs, d)],
                                axis=0).astype(BF16)
            fmask = fm_scr[first]
            res = []
            for e in range(2):
                sel = low if e == 0 else jnp.logical_not(low)
                qe = jnp.where(sel, q, 0.0).astype(BF16)
                s_ = lax.dot_general(qe, k, (((1,), (1,)), ((), ())), preferred_element_type=F32)
                s_ = s_ + bias_scr[c * N_HEADS + 2 * p + e] + fmask
                m = jnp.max(s_, axis=1, keepdims=True)
                pe = jnp.exp(s_ - m)
                l = jnp.sum(pe, axis=1, keepdims=True)
                o = jnp.dot(pe.astype(BF16), v, preferred_element_type=F32)
                res.append((m, l, o))
            (m0, l0, o0), (m1, l1, o1) = res
            m_new = jnp.where(low, m0, m1)
            l_new = jnp.where(low, l0, l1)
            o_new = jnp.where(low, o0, o1)
            if c == 0:
                put(m_scr, qs, d, m_new)
                put(l_scr, qs, d, l_new)
                put(acc_scr, qs, d, o_new)
            else:
                m_old = rows(m_scr, qs, d)
                m_tot = jnp.maximum(m_old, m_new)
                a_old = jnp.exp(m_old - m_tot)
                a_new = jnp.exp(m_new - m_tot)
                put(m_scr, qs, d, m_tot)
                put(l_scr, qs, d, rows(l_scr, qs, d) * a_old + l_new * a_new)
                put(acc_scr, qs, d, rows(acc_scr, qs, d) * a_old + o_new * a_new)
            return carry

        lax.fori_loop(0, n_res * n_grp, sub_body, 0)

    o_ref[...] = acc_scr[...] / l_scr[...]


def _attn_prompt(q, k, v, rel_bias):
    b, npair, t, _ = q.shape
    blk = ATT_BLOCK
    cur = pl.BlockSpec((None, None, blk, LANES), lambda bi, p, i: (bi, p, i, 0))
    prev = pl.BlockSpec((None, None, blk, LANES), lambda bi, p, i: (bi, p, jnp.maximum(i - 1, 0), 0))
    n_tiles = len(DILATIONS) * N_HEADS
    return pl.pallas_call(
        _attn_prompt_kernel,
        grid=(b, npair, t // blk),
        in_specs=[pl.BlockSpec(memory_space=pltpu.SMEM),
                  pl.BlockSpec((len(DILATIONS), TAPS, 2 * TAPS), lambda bi, p, i: (0, 0, 0)),
                  cur, cur, prev, cur, prev],
        out_specs=cur,
        out_shape=jax.ShapeDtypeStruct(q.shape, F32),
        scratch_shapes=[pltpu.VMEM((n_tiles, TAPS, 2 * TAPS), F32),
                        pltpu.VMEM((2, TAPS, 2 * TAPS), F32),
                        pltpu.VMEM((2 * blk, LANES), F32),
                        pltpu.VMEM((2 * blk, LANES), F32),
                        pltpu.VMEM((blk, LANES), F32),
                        pltpu.VMEM((blk, LANES), F32),
                        pltpu.VMEM((blk, LANES), F32)],
        compiler_params=_cparams(3),
        name="attn_prompt",
    )(rel_bias, _prompt_bias_ids(), q, k, k, v, v)


DEC_T = 4
DEC_NEW_PAD = 128
DEC_KEYS = MAX_WINDOW + DEC_NEW_PAD


def _decode_tables():
    qi = jnp.arange(DEC_T, dtype=jnp.int32)[:, None]
    rho = jnp.arange(MAX_WINDOW, dtype=jnp.int32)[None, :]
    dist_b = MAX_WINDOW + qi - rho
    mult_b = jnp.zeros_like(dist_b)
    for d in DILATIONS:
        mult_b = mult_b + (((dist_b % d) == 0) & (dist_b <= d * TAPS)).astype(jnp.int32)
    jj = jnp.arange(DEC_NEW_PAD, dtype=jnp.int32)[None, :]
    dist_n = qi - jj
    mult_n = jnp.where(dist_n == 0, len(DILATIONS), jnp.where((dist_n > 0) & (jj < DEC_T), 1, 0))
    dist = jnp.concatenate([dist_b, jnp.maximum(dist_n, 0)], axis=1)
    mult = jnp.concatenate([mult_b, mult_n.astype(jnp.int32)], axis=1)
    ids = jnp.where(mult > 0, _t5_bucket(dist), -1).astype(jnp.int32)
    ids = jnp.repeat(ids, N_HEADS, axis=0)
    mult = jnp.repeat(mult, N_HEADS, axis=0).astype(F32)
    return ids, mult


def _attn_decode_kernel(rbt_ref, ids_ref, mult_ref, q_ref, kn_ref, vn_ref, kt_ref, vt_ref, o_ref, bias_scr):
    n_rows = DEC_T * N_HEADS

    @pl.when(pl.program_id(0) == 0)
    def _build_bias():
        ids = ids_ref[...]
        tile = jnp.full((n_rows, DEC_KEYS), NEG_INF, F32)
        for bkt in range(N_BUCKETS):
            tile = jnp.where(ids == bkt, rbt_ref[:, bkt:bkt + 1], tile)
        bias_scr[...] = tile

    lane = lax.broadcasted_iota(jnp.int32, (N_HEADS, MIX_A), 1)
    head = lax.broadcasted_iota(jnp.int32, (N_HEADS, MIX_A), 0)
    hmask = (lane // HEAD_DIM) == head

    q = q_ref[...]
    qm = jnp.where(hmask[None], q[:, None, :], 0.0).reshape(n_rows, MIX_A).astype(BF16)
    zpad = jnp.zeros((DEC_NEW_PAD - DEC_T, MIX_A), F32)
    kn = jnp.concatenate([kn_ref[...], zpad], axis=0).astype(BF16)
    vn = jnp.concatenate([vn_ref[...], zpad], axis=0).astype(BF16)
    kt = kt_ref[...].astype(BF16)
    vt = vt_ref[...].astype(BF16)
    nt = (((1,), (1,)), ((), ()))
    s = jnp.concatenate([jnp.dot(qm, kt, preferred_element_type=F32),
                         lax.dot_general(qm, kn, nt, preferred_element_type=F32)], axis=1)
    s = s + bias_scr[...]
    m = jnp.max(s, axis=1, keepdims=True)
    pe = jnp.exp(s - m) * mult_ref[...]
    l = jnp.sum(pe, axis=1, keepdims=True)
    pb = pe.astype(BF16)
    o = (lax.dot_general(pb[:, :MAX_WINDOW], vt, nt, preferred_element_type=F32)
         + jnp.dot(pb[:, MAX_WINDOW:], vn, preferred_element_type=F32)) / l
    o3 = o.reshape(DEC_T, N_HEADS, MIX_A)
    o_ref[...] = jnp.sum(jnp.where(hmask[None], o3, 0.0), axis=1)


def _attn_decode(q, k_new, v_new, cache_kt, cache_vt, rel_bias):
    bd = q.shape[0]
    ids, mult = _decode_tables()
    rbt = jnp.tile(rel_bias.T, (DEC_T, 1))
    new_spec = pl.BlockSpec((None, DEC_T, MIX_A), lambda b: (b, 0, 0))
    cache_spec = pl.BlockSpec((None, MIX_A, MAX_WINDOW), lambda b: (b, 0, 0))
    n_rows = DEC_T * N_HEADS
    const = lambda shape: pl.BlockSpec(shape, lambda b: (0, 0))
    return pl.pallas_call(
        _attn_decode_kernel,
        grid=(bd,),
        in_specs=[const((n_rows, N_BUCKETS)), const((n_rows, DEC_KEYS)), const((n_rows, DEC_KEYS)),
                  new_spec, new_spec, new_spec, cache_spec, cache_spec],
        out_specs=new_spec,
        out_shape=jax.ShapeDtypeStruct((bd, DEC_T, MIX_A), F32),
        scratch_shapes=[pltpu.VMEM((n_rows, DEC_KEYS), F32)],
        compiler_params=_cparams(1),
        name="attn_decode",
    )(rbt, ids, mult, q, k_new, v_new, cache_kt, cache_vt)


def _ssm_matrices(a_re, a_im, log_dt, b_re, b_im, c_re, c_im, d_skip):
    hi = lax.Precision.HIGHEST
    g, p = a_re.shape
    dt = jnp.exp(log_dt)[:, None]
    decay = jnp.exp(a_re * dt)
    abar_re = decay * jnp.cos(a_im * dt)
    abar_im = decay * jnp.sin(a_im * dt)
    inv = 1.0 / (a_re * a_re + a_im * a_im)
    coef_re = ((abar_re - 1.0) * a_re + abar_im * a_im) * inv
    coef_im = (abar_im * a_re - (abar_re - 1.0) * a_im) * inv
    bbar_re = coef_re[..., None] * b_re - coef_im[..., None] * b_im
    bbar_im = coef_re[..., None] * b_im + coef_im[..., None] * b_re
    pw_re = [jnp.ones_like(abar_re)]
    pw_im = [jnp.zeros_like(abar_im)]
    for _ in range(CHUNK):
        pr, pi = pw_re[-1], pw_im[-1]
        pw_re.append(pr * abar_re - pi * abar_im)
        pw_im.append(pr * abar_im + pi * abar_re)
    pw_re = jnp.stack(pw_re)
    pw_im = jnp.stack(pw_im)
    cp_re = c_re[None] * pw_re[:, :, None, :] - c_im[None] * pw_im[:, :, None, :]
    cp_im = c_re[None] * pw_im[:, :, None, :] + c_im[None] * pw_re[:, :, None, :]
    kern = (jnp.einsum('mgcp,gpd->mgcd', cp_re[:CHUNK], bbar_re, precision=hi)
            - jnp.einsum('mgcp,gpd->mgcd', cp_im[:CHUNK], bbar_im, precision=hi))
    ti = jnp.arange(CHUNK)[:, None]
    to = jnp.arange(CHUNK)[None, :]
    lag = to - ti
    blocks = kern[jnp.clip(lag, 0, CHUNK - 1)]
    blocks = jnp.where((lag >= 0)[:, :, None, None, None], blocks, 0.0)
    t_intra = jnp.transpose(blocks, (2, 0, 4, 1, 3)).reshape(g, CHUNK * SSM_GROUP, CHUNK * SSM_GROUP)
    rev_re = pw_re[CHUNK - 1 - jnp.arange(CHUNK)]
    rev_im = pw_im[CHUNK - 1 - jnp.arange(CHUNK)]
    in_re = rev_re[..., None] * bbar_re[None] - rev_im[..., None] * bbar_im[None]
    in_im = rev_re[..., None] * bbar_im[None] + rev_im[..., None] * bbar_re[None]
    t_in_re = jnp.transpose(in_re, (1, 0, 3, 2)).reshape(g, CHUNK * SSM_GROUP, p)
    t_in_im = jnp.transpose(in_im, (1, 0, 3, 2)).reshape(g, CHUNK * SSM_GROUP, p)
    t_out_re = jnp.transpose(cp_re[1:], (1, 3, 0, 2)).reshape(g, p, CHUNK * SSM_GROUP)
    t_out_im = -jnp.transpose(cp_im[1:], (1, 3, 0, 2)).reshape(g, p, CHUNK * SSM_GROUP)
    d_tile = jnp.tile(d_skip.reshape(g, 1, SSM_GROUP), (1, 1, CHUNK))
    return dict(t_intra=t_intra, t_in_re=t_in_re, t_in_im=t_in_im, t_out_re=t_out_re,
                t_out_im=t_out_im, d_tile=d_tile, pw_re=pw_re, pw_im=pw_im)


def _ssm_prompt_kernel(u_ref, ti_ref, tinr_ref, tini_ref, toutr_ref, touti_ref, d_ref, ar_ref, ai_ref,
                       y_ref, sre_ref, sim_ref, pr_scr, pi_scr, sr_scr, si_scr):
    nc = u_ref.shape[0]
    u = u_ref[...]
    ub = u.astype(BF16)
    pr_scr[...] = jnp.dot(ub, tinr_ref[...], preferred_element_type=F32)
    pi_scr[...] = jnp.dot(ub, tini_ref[...], preferred_element_type=F32)
    ar = ar_ref[...]
    ai = ai_ref[...]

    def step(c, carry):
        sr, si = carry
        sr_scr[pl.ds(c, 1), :] = sr
        si_scr[pl.ds(c, 1), :] = si
        nr = ar * sr - ai * si + pr_scr[pl.ds(c, 1), :]
        ni = ar * si + ai * sr + pi_scr[pl.ds(c, 1), :]
        return nr, ni

    zero = jnp.zeros((1, SSM_STATE), F32)
    sr, si = lax.fori_loop(0, nc, step, (zero, zero), unroll=8)
    sre_ref[...] = sr
    sim_ref[...] = si
    y = jnp.dot(ub, ti_ref[...], preferred_element_type=F32)
    y = y + jnp.dot(sr_scr[...].astype(BF16), toutr_ref[...], preferred_element_type=F32)
    y = y + jnp.dot(si_scr[...].astype(BF16), touti_ref[...], preferred_element_type=F32)
    y_ref[...] = y + d_ref[...] * u


def _ssm_prompt(ug, mats):
    b, g, nc, w = ug.shape
    p = SSM_STATE
    gspec = lambda shape: pl.BlockSpec((None,) + shape, lambda bi, gi: (gi, 0, 0))
    io = pl.BlockSpec((None, None, nc, w), lambda bi, gi: (bi, gi, 0, 0))
    st = pl.BlockSpec((None, None, 1, p), lambda bi, gi: (bi, gi, 0, 0))
    return pl.pallas_call(
        _ssm_prompt_kernel,
        grid=(b, g),
        in_specs=[io, gspec((w, w)), gspec((w, p)), gspec((w, p)), gspec((p, w)), gspec((p, w)),
                  gspec((1, w)), gspec((1, p)), gspec((1, p))],
        out_specs=[io, st, st],
        out_shape=[jax.ShapeDtypeStruct(ug.shape, F32),
                   jax.ShapeDtypeStruct((b, g, 1, p), F32), jax.ShapeDtypeStruct((b, g, 1, p), F32)],
        scratch_shapes=[pltpu.VMEM((nc, p), F32)] * 4,
        compiler_params=_cparams(2),
        name="ssm_prompt",
    )(ug, mats['t_intra'].astype(BF16), mats['t_in_re'].astype(BF16), mats['t_in_im'].astype(BF16),
      mats['t_out_re'].astype(BF16), mats['t_out_im'].astype(BF16), mats['d_tile'],
      mats['pw_re'][CHUNK][:, None, :], mats['pw_im'][CHUNK][:, None, :])


def _ssm_decode_kernel(u_ref, s0r_ref, s0i_ref, ti_ref, tinr_ref, tini_ref, toutr_ref, touti_ref, d_ref,
                       ar_ref, ai_ref, y_ref, sre_ref, sim_ref):
    u = u_ref[...]
    ub = u.astype(BF16)
    s0r = s0r_ref[...]
    s0i = s0i_ref[...]
    y = jnp.dot(ub, ti_ref[...], preferred_element_type=F32)
    y = y + jnp.dot(s0r.astype(BF16), toutr_ref[...], preferred_element_type=F32)
    y = y + jnp.dot(s0i.astype(BF16), touti_ref[...], preferred_element_type=F32)
    y_ref[...] = y + d_ref[...] * u
    ar = ar_ref[...]
    ai = ai_ref[...]
    sre_ref[...] = ar * s0r - ai * s0i + jnp.dot(ub, tinr_ref[...], preferred_element_type=F32)
    sim_ref[...] = ar * s0i + ai * s0r + jnp.dot(ub, tini_ref[...], preferred_element_type=F32)


def _ssm_decode(ug, s0_re, s0_im, mats):
    g, bd, w = ug.shape
    p = SSM_STATE
    lo = (CHUNK - DEC_T) * SSM_GROUP
    gspec = lambda shape: pl.BlockSpec((None,) + shape, lambda gi: (gi, 0, 0))
    return pl.pallas_call(
        _ssm_decode_kernel,
        grid=(g,),
        in_specs=[gspec((bd, w)), gspec((bd, p)), gspec((bd, p)), gspec((w, w)), gspec((w, p)), gspec((w, p)),
                  gspec((p, w)), gspec((p, w)), gspec((1, w)), gspec((1, p)), gspec((1, p))],
        out_specs=[gspec((bd, w)), gspec((bd, p)), gspec((bd, p))],
        out_shape=[jax.ShapeDtypeStruct((g, bd, w), F32), jax.ShapeDtypeStruct((g, bd, p), F32),
                   jax.ShapeDtypeStruct((g, bd, p), F32)],
        compiler_params=_cparams(1),
        name="ssm_decode",
    )(ug, s0_re, s0_im, mats['t_intra'][:, :w, :w].astype(BF16),
      mats['t_in_re'][:, lo:, :].astype(BF16), mats['t_in_im'][:, lo:, :].astype(BF16),
      mats['t_out_re'][:, :, :w].astype(BF16), mats['t_out_im'][:, :, :w].astype(BF16),
      mats['d_tile'][:, :, :w], mats['pw_re'][DEC_T][:, None, :], mats['pw_im'][DEC_T][:, None, :])


def _gelu_tanh(x):
    return 0.5 * x * (1.0 + jnp.tanh(math.sqrt(2.0 / math.pi) * (x + 0.044715 * (x * x * x))))


def _sigmoid(x):
    return 1.0 / (1.0 + jnp.exp(-x))


def _route(hn, wr_ref):
    n = hn.shape[0]
    logits = jnp.dot(hn, wr_ref[...], preferred_element_type=F32, precision=lax.Precision.HIGHEST)
    lidx = lax.broadcasted_iota(jnp.int32, (n, ROUTER_LANES), 1)
    is_e = lidx < N_EXPERTS
    is_g = jnp.logical_and(lidx >= N_EXPERTS, lidx < N_EXPERTS + N_EXPERT_GROUPS)
    gmax = jnp.max(jnp.where(is_g, logits, -jnp.inf), axis=1, keepdims=True)
    g_prob = 1.0 / jnp.sum(jnp.where(is_g, jnp.exp(logits - gmax), 0.0), axis=1, keepdims=True)
    g_sel = jnp.min(jnp.where(jnp.logical_and(is_g, logits == gmax), lidx - N_EXPERTS, N_EXPERT_GROUPS),
                    axis=1, keepdims=True)
    in_grp = jnp.logical_and(is_e, (lidx // EXPERTS_PER_GROUP) == g_sel)
    l1 = jnp.max(jnp.where(in_grp, logits, -jnp.inf), axis=1, keepdims=True)
    i1 = jnp.min(jnp.where(jnp.logical_and(in_grp, logits == l1), lidx, ROUTER_LANES), axis=1, keepdims=True)
    rest = jnp.logical_and(in_grp, lidx != i1)
    l2 = jnp.max(jnp.where(rest, logits, -jnp.inf), axis=1, keepdims=True)
    i2 = jnp.min(jnp.where(jnp.logical_and(rest, logits == l2), lidx, ROUTER_LANES), axis=1, keepdims=True)
    e2 = jnp.exp(l2 - l1)
    w1 = g_prob / (1.0 + e2)
    w2 = g_prob * e2 / (1.0 + e2)
    return jnp.where(lidx == i1, w1, 0.0) + jnp.where(lidx == i2, w2, 0.0)


def _post_body(oa, ys, x, ga_ref, gb_ref, wglu_ref, bglu_ref, wout_ref, gf_ref, wr_ref,
               x1_ref, hn_ref, gates_ref):
    z = _gelu_tanh(ys)
    gate = _sigmoid(jnp.dot(z.astype(BF16), wglu_ref[...], preferred_element_type=F32) + bglu_ref[...])
    ob = z * gate
    mixed = jnp.concatenate([_rms(oa, ga_ref[...]), _rms(ob, gb_ref[...])], axis=1).astype(BF16)
    x1 = x + jnp.dot(mixed, wout_ref[...], preferred_element_type=F32)
    x1_ref[...] = x1
    hn = _rms(x1, gf_ref[...])
    hn_ref[...] = hn.astype(BF16)
    gates_ref[...] = _route(hn, wr_ref)


def _post_pair_kernel(oa_ref, ys_ref, x_ref, *rest):
    oa = jnp.concatenate([oa_ref[j] for j in range(MIX_A // LANES)], axis=1)
    _post_body(oa, ys_ref[...], x_ref[...], *rest)


def _post_tok_kernel(oa_ref, ys_ref, x_ref, *rest):
    _post_body(oa_ref[...], ys_ref[...], x_ref[...], *rest)


def _post_mix(oa, ys, x, w, tm, pair):
    b, t, d = x.shape
    row = lambda width: pl.BlockSpec((None, tm, width), lambda bi, i: (bi, i, 0))
    const = lambda a: pl.BlockSpec(a.shape, lambda bi, i: (0,) * a.ndim)
    oa_spec = (pl.BlockSpec((None, MIX_A // LANES, tm, LANES), lambda bi, i: (bi, 0, i, 0)) if pair
               else row(MIX_A))
    weights = [w['g_out_a'], w['g_out_b'], w['w_glu'], w['b_glu'], w['w_out'], w['g_ffn'], w['w_router']]
    return pl.pallas_call(
        _post_pair_kernel if pair else _post_tok_kernel,
        grid=(b, t // tm),
        in_specs=[oa_spec, row(MIX_B), row(d)] + [const(a) for a in weights],
        out_specs=[row(d), row(d), row(ROUTER_LANES)],
        out_shape=[jax.ShapeDtypeStruct((b, t, d), F32), jax.ShapeDtypeStruct((b, t, d), BF16),
                   jax.ShapeDtypeStruct((b, t, ROUTER_LANES), F32)],
        compiler_params=_cparams(2),
        name="post_mix_prompt" if pair else "post_mix_sample",
    )(oa, ys, x, *weights)


def _moe_kernel(hn_ref, gates_ref, x1_ref, wg_ref, wu_ref, wd_ref, gfin_ref, y_ref, acc_scr):
    e = pl.program_id(1)

    @pl.when(e == 0)
    def _init():
        acc_scr[...] = jnp.zeros_like(acc_scr)

    hn = hn_ref[...]
    a = jnp.dot(hn, wg_ref[...], preferred_element_type=F32)
    u = jnp.dot(hn, wu_ref[...], preferred_element_type=F32)
    eidx = lax.broadcasted_iota(jnp.int32, gates_ref.shape, 1)
    gate = jnp.sum(jnp.where(eidx == e, gates_ref[...], 0.0), axis=1, keepdims=True)
    act = (a * _sigmoid(a)) * u * gate
    acc_scr[...] += jnp.dot(act.astype(BF16), wd_ref[...], preferred_element_type=F32)

    @pl.when(e == pl.num_programs(1) - 1)
    def _fin():
        y_ref[...] = _rms(x1_ref[...] + acc_scr[...], gfin_ref[...])


def _moe(hn, gates, x1, wg, wu, wd, g_final, tm):
    n, d = x1.shape
    row = lambda width: pl.BlockSpec((tm, width), lambda i, e: (i, 0))
    return pl.pallas_call(
        _moe_kernel,
        grid=(n // tm, N_EXPERTS),
        in_specs=[row(d), row(ROUTER_LANES), row(d),
                  pl.BlockSpec((None, d, D_EXPERT), lambda i, e: (e, 0, 0)),
                  pl.BlockSpec((None, d, D_EXPERT), lambda i, e: (e, 0, 0)),
                  pl.BlockSpec((None, D_EXPERT, d), lambda i, e: (e, 0, 0)),
                  pl.BlockSpec((1, d), lambda i, e: (0, 0))],
        out_specs=row(d),
        out_shape=jax.ShapeDtypeStruct((n, d), F32),
        scratch_shapes=[pltpu.VMEM((tm, d), F32)],
        compiler_params=_cparams(2),
        name="moe",
    )(hn, gates, x1, wg, wu, wd, g_final)


def kernel(x_prompt, x_sample, cache_k, cache_v, state_ssm_re, state_ssm_im, rel_bias, g_mix, w_in, g_out_a, g_out_b, w_out, ssm_a_re, ssm_a_im, ssm_log_dt, ssm_b_re, ssm_b_im, ssm_c_re, ssm_c_im, ssm_d, w_glu, b_glu, g_ffn, w_router_group, w_router_expert, w_expert_gate, w_expert_up, w_expert_down, g_final):
    depth = g_mix.shape[0]
    assert depth == 1, "kernel is written for the single-layer configuration of the problem"
    l = 0
    b, t, d = x_prompt.shape
    bd, ts, _ = x_sample.shape
    assert ts == DEC_T and t % ATT_BLOCK == 0 and cache_k.shape[2] == MAX_WINDOW
    keep = min(MAX_WINDOW, t)

    w_in_b = w_in[l].astype(BF16)
    gm = g_mix[l][None, :]
    post_w = dict(
        g_out_a=g_out_a[l][None, :], g_out_b=g_out_b[l][None, :], w_glu=w_glu[l].astype(BF16),
        b_glu=b_glu[l][None, :], w_out=w_out[l].astype(BF16), g_ffn=g_ffn[l][None, :],
        w_router=jnp.concatenate(
            [jnp.transpose(w_router_expert[l], (1, 0, 2)).reshape(d, N_EXPERTS), w_router_group[l],
             jnp.zeros((d, ROUTER_LANES - N_EXPERTS - N_EXPERT_GROUPS), F32)], axis=1))
    wg = w_expert_gate[l].astype(BF16)
    wu = w_expert_up[l].astype(BF16)
    wd = w_expert_down[l].astype(BF16)
    gfin = g_final[None, :]
    mats = _ssm_matrices(ssm_a_re[l], ssm_a_im[l], ssm_log_dt[l], ssm_b_re[l], ssm_b_im[l],
                         ssm_c_re[l], ssm_c_im[l], ssm_d[l])

    qp, kp, vp, k_tok, v_tok, u_tok = _inproj_pair(x_prompt, gm, w_in_b, tm=512)
    oa_p = _attn_prompt(qp, kp, vp, rel_bias)
    ug = jnp.transpose(u_tok.reshape(b, t, N_GROUPS, SSM_GROUP), (0, 2, 1, 3))
    ug = ug.reshape(b, N_GROUPS, t // CHUNK, CHUNK * SSM_GROUP)
    yg, rp, ip = _ssm_prompt(ug, mats)
    ys_p = jnp.transpose(yg.reshape(b, N_GROUPS, t, SSM_GROUP), (0, 2, 1, 3)).reshape(b, t, MIX_B)
    x1_p, hn_p, gates_p = _post_mix(oa_p, ys_p, x_prompt, post_w, tm=512, pair=True)
    y_p = _moe(hn_p.reshape(b * t, d), gates_p.reshape(b * t, ROUTER_LANES), x1_p.reshape(b * t, d),
               wg, wu, wd, gfin, tm=1024)

    n_s = bd * ts
    qs, ks, vs, us = _inproj_tok(x_sample.reshape(n_s, d), gm, w_in_b)
    ckt = jnp.transpose(cache_k[l], (0, 2, 3, 1)).reshape(bd, MIX_A, MAX_WINDOW)
    cvt = jnp.transpose(cache_v[l], (0, 2, 3, 1)).reshape(bd, MIX_A, MAX_WINDOW)
    oa_s = _attn_decode(qs.reshape(bd, ts, MIX_A), ks.reshape(bd, ts, MIX_A), vs.reshape(bd, ts, MIX_A),
                        ckt, cvt, rel_bias)
    ugs = jnp.transpose(us.reshape(bd, ts, N_GROUPS, SSM_GROUP), (2, 0, 1, 3)).reshape(N_GROUPS, bd, ts * SSM_GROUP)
    s0r = jnp.transpose(state_ssm_re[l], (1, 0, 2))
    s0i = jnp.transpose(state_ssm_im[l], (1, 0, 2))
    ygs, rs, is_ = _ssm_decode(ugs, s0r, s0i, mats)
    ys_s = jnp.transpose(ygs.reshape(N_GROUPS, bd, ts, SSM_GROUP), (1, 2, 0, 3)).reshape(n_s, MIX_B)
    x1_s, hn_s, gates_s = _post_mix(oa_s.reshape(1, n_s, MIX_A), ys_s.reshape(1, n_s, MIX_B),
                                    x_sample.reshape(1, n_s, d), post_w, tm=n_s, pair=False)
    y_s = _moe(hn_s.reshape(n_s, d), gates_s.reshape(n_s, ROUTER_LANES), x1_s.reshape(n_s, d),
               wg, wu, wd, gfin, tm=n_s)

    y_prompt = y_p.reshape(b, t, d)
    y_sample = y_s.reshape(bd, ts, d)
    k_win = k_tok[:, t - keep:].reshape(1, b, keep, N_HEADS, HEAD_DIM)
    v_win = v_tok[:, t - keep:].reshape(1, b, keep, N_HEADS, HEAD_DIM)
    k_new = ks.reshape(1, bd, ts, N_HEADS, HEAD_DIM)
    v_new = vs.reshape(1, bd, ts, N_HEADS, HEAD_DIM)
    return (y_prompt, y_sample, k_win, v_win, k_new, v_new,
            rp.reshape(1, b, N_GROUPS, SSM_STATE), ip.reshape(1, b, N_GROUPS, SSM_STATE),
            jnp.transpose(rs, (1, 0, 2))[None], jnp.transpose(is_, (1, 0, 2))[None])
```

```python
import functools
import math

import jax
import jax.numpy as jnp
import numpy as np
from jax import lax
from jax.experimental import pallas as pl
from jax.experimental.pallas import tpu as pltpu

F32 = jnp.float32
BF16 = jnp.bfloat16

D_MODEL = 1024
HEAD_DIM = 64
MIX_A = 512
N_HEADS = 8
MIX_B = 512
SSM_GROUP = 16
N_GROUPS = 32
SSM_STATE = 64
PROJ_COLS = 3 * MIX_A + MIX_B
DILATIONS = (1, 4, 16)
TAPS = 128
MAX_WINDOW = 2048
N_BUCKETS = 32
N_EXPERT_GROUPS = 4
EXPERTS_PER_GROUP = 8
N_EXPERTS = 32
D_EXPERT = 256
EPS = 1e-6
NEG_INF = -1e30
SCALE = HEAD_DIM ** -0.5

LANES = 128
ROUTER_LANES = 128
CHUNK = 16
ATT_BLOCK = 2048
ATT_UNROLL = 8
ATT_PITCH = 136
VMEM_LIMIT = 56 * 1024 * 1024


def _cparams(n_axes):
    return pltpu.CompilerParams(dimension_semantics=("arbitrary",) * n_axes,
                                vmem_limit_bytes=VMEM_LIMIT)


def _t5_bucket(distance):
    max_exact = N_BUCKETS // 2
    nf = jnp.maximum(distance, 1).astype(F32)
    large = max_exact + (jnp.log(nf / max_exact) / math.log(MAX_WINDOW / max_exact)
                         * (N_BUCKETS - max_exact)).astype(jnp.int32)
    large = jnp.minimum(large, N_BUCKETS - 1)
    return jnp.where(distance < max_exact, distance, large)


def _rms(x, g):
    return x * lax.rsqrt(jnp.mean(x * x, axis=-1, keepdims=True) + EPS) * g


def _inproj_pair_kernel(x_ref, g_ref, w_ref, q_ref, k_ref, v_ref, ks_ref, vs_ref, u_ref):
    h = _rms(x_ref[...], g_ref[...])
    p = jnp.dot(h.astype(BF16), w_ref[...], preferred_element_type=F32)
    for j in range(MIX_A // LANES):
        q_ref[j] = p[:, LANES * j:LANES * (j + 1)] * SCALE
        k_ref[j] = p[:, MIX_A + LANES * j:MIX_A + LANES * (j + 1)]
        v_ref[j] = p[:, 2 * MIX_A + LANES * j:2 * MIX_A + LANES * (j + 1)]
    ks_ref[...] = p[:, MIX_A:2 * MIX_A]
    vs_ref[...] = p[:, 2 * MIX_A:3 * MIX_A]
    u_ref[...] = p[:, 3 * MIX_A:]


def _inproj_pair(x, g, w_bf16, tm):
    b, t, d = x.shape
    npair = MIX_A // LANES
    pair = jax.ShapeDtypeStruct((b, npair, t, LANES), F32)
    tok = jax.ShapeDtypeStruct((b, t, MIX_A), F32)
    pair_spec = pl.BlockSpec((None, npair, tm, LANES), lambda bi, i: (bi, 0, i, 0))
    tok_spec = pl.BlockSpec((None, tm, MIX_A), lambda bi, i: (bi, i, 0))
    return pl.pallas_call(
        _inproj_pair_kernel,
        grid=(b, t // tm),
        in_specs=[pl.BlockSpec((None, tm, d), lambda bi, i: (bi, i, 0)),
                  pl.BlockSpec((1, d), lambda bi, i: (0, 0)),
                  pl.BlockSpec((d, PROJ_COLS), lambda bi, i: (0, 0))],
        out_specs=[pair_spec, pair_spec, pair_spec, tok_spec, tok_spec, tok_spec],
        out_shape=[pair, pair, pair, tok, tok, tok],
        compiler_params=_cparams(2),
        name="inproj_prompt",
    )(x, g, w_bf16)


def _inproj_tok_kernel(x_ref, g_ref, w_ref, q_ref, k_ref, v_ref, u_ref):
    h = _rms(x_ref[...], g_ref[...])
    p = jnp.dot(h.astype(BF16), w_ref[...], preferred_element_type=F32)
    q_ref[...] = p[:, :MIX_A] * SCALE
    k_ref[...] = p[:, MIX_A:2 * MIX_A]
    v_ref[...] = p[:, 2 * MIX_A:3 * MIX_A]
    u_ref[...] = p[:, 3 * MIX_A:]


def _inproj_tok(x, g, w_bf16):
    n, d = x.shape
    out = jax.ShapeDtypeStruct((n, MIX_A), F32)
    return pl.pallas_call(
        _inproj_tok_kernel,
        out_shape=[out, out, out, out],
        compiler_params=pltpu.CompilerParams(vmem_limit_bytes=VMEM_LIMIT),
        name="inproj_sample",
    )(x, g, w_bf16)


def _prompt_bias_ids():
    r = jnp.arange(TAPS, dtype=jnp.int32)[:, None]
    kpos = jnp.arange(2 * TAPS, dtype=jnp.int32)[None, :] - TAPS
    rel = r - kpos
    valid = (rel >= 0) & (rel <= TAPS)
    ids = [jnp.where(valid, _t5_bucket(d * rel), -1) for d in DILATIONS]
    return jnp.stack(ids).astype(jnp.int32)


def _attn_prompt_kernel(rb_ref, ids_ref, q_ref, kc_ref, kp_ref, vc_ref, vp_ref, o_ref,
                        bias_scr, fm_scr, kcat, vcat, m0_scr, l0_scr, o0_scr, m1_scr, l1_scr, o1_scr,
                        m2_scr, l2_scr, o2_scr):
    bi = pl.program_id(0)
    p = pl.program_id(1)
    i = pl.program_id(2)
    blk = ATT_BLOCK
    npair = N_HEADS // 2

    @pl.when((bi == 0) & (p == 0) & (i == 0))
    def _build_bias():
        col = lax.broadcasted_iota(jnp.int32, (2 * TAPS, 2 * TAPS), 1)
        fm_scr[0] = jnp.zeros((2 * TAPS, 2 * TAPS), F32)
        fm_scr[1] = jnp.where(col < TAPS, NEG_INF, 0.0).astype(F32)
        for c in range(len(DILATIONS)):
            ids = ids_ref[c]

            def head_body(h, carry, ids=ids, c=c):
                tile = jnp.full((TAPS, 2 * TAPS), NEG_INF, F32)
                for bkt in range(N_BUCKETS):
                    tile = jnp.where(ids == bkt, rb_ref[bkt, h], tile)
                row0 = pl.multiple_of((h % 2) * TAPS, TAPS)
                bias_scr[c * npair + h // 2, pl.ds(row0, TAPS), :] = tile
                return carry

            lax.fori_loop(0, N_HEADS, head_body, 0)

    kcat[pl.ds(0, blk), :] = kp_ref[...]
    kcat[pl.ds(blk, blk), :] = kc_ref[...]
    vcat[pl.ds(0, blk), :] = vp_ref[...]
    vcat[pl.ds(blk, blk), :] = vc_ref[...]

    lane = lax.broadcasted_iota(jnp.int32, (1, LANES), 1)
    low = lane < HEAD_DIM
    nt = (((1,), (1,)), ((), ()))
    ones = jnp.ones((2 * TAPS, LANES), BF16)
    stats = ((m0_scr, l0_scr, o0_scr), (m1_scr, l1_scr, o1_scr), (m2_scr, l2_scr, o2_scr))

    def rows(ref, start, d):
        if d == 1:
            return ref[pl.ds(start, TAPS), :]
        return ref[pl.ds(start, TAPS, stride=d), :]

    def sub_block(c, d, qs, first, dst_rows):
        q = rows(q_ref, qs, d)
        q2 = jnp.concatenate([jnp.where(low, q, 0.0), jnp.where(low, 0.0, q)], axis=0).astype(BF16)
        k = jnp.concatenate([rows(kcat, blk + qs - d * TAPS, d), rows(kcat, blk + qs, d)],
                            axis=0).astype(BF16)
        v = jnp.concatenate([rows(vcat, blk + qs - d * TAPS, d), rows(vcat, blk + qs, d)],
                            axis=0).astype(BF16)
        s_ = lax.dot_general(q2, k, nt, preferred_element_type=F32) + bias_scr[c * npair + p]
        if first is not None:
            s_ = s_ + fm_scr[first]
        m = jnp.max(s_, axis=1, keepdims=True)
        pe = jnp.exp(s_ - m).astype(BF16)
        ol = jnp.dot(pe, jnp.concatenate([v, ones], axis=1), preferred_element_type=F32)
        o, l = ol[:, :LANES], ol[:, LANES:]
        m_ref, l_ref, a_ref = stats[c]
        m_ref[dst_rows, :] = jnp.where(low, m[:TAPS], m[TAPS:])
        l_ref[dst_rows, :] = jnp.where(low, l[:TAPS], l[TAPS:])
        a_ref[dst_rows, :] = jnp.where(low, o[:TAPS], o[TAPS:])

    at_start = jnp.where(i == 0, 1, 0)
    n_sub = blk // TAPS
    for c, d in enumerate(DILATIONS):
        n_grp = blk // (d * TAPS)

        def group_body(s8, carry, c=c, d=d, n_grp=n_grp):
            for u in range(ATT_UNROLL):
                s = s8 * ATT_UNROLL + u
                if n_grp >= ATT_UNROLL:
                    qs = pl.multiple_of(s * TAPS, TAPS)
                    first = jnp.where(s8 == 0, at_start, 0) if u == 0 else None
                    dst_rows = pl.ds(qs, TAPS)
                else:
                    r = s8 * (ATT_UNROLL // n_grp) + u // n_grp
                    g = u % n_grp
                    qs = r + d * TAPS * g
                    first = at_start if g == 0 else None
                    if n_grp > 1:
                        dst_rows = pl.ds(qs, TAPS, stride=d)
                    else:
                        dst_rows = pl.ds(pl.multiple_of(r * ATT_PITCH, 8), TAPS)
                sub_block(c, d, qs, first, dst_rows)
            return carry

        lax.fori_loop(0, n_sub // ATT_UNROLL, group_body, 0)

    d_last = DILATIONS[-1]

    def merge_body(j, carry):
        tok = pl.ds(pl.multiple_of(j * d_last, d_last), d_last)
        dil = pl.ds(j, d_last, stride=ATT_PITCH)
        m0, m1, m2 = m0_scr[tok, :], m1_scr[tok, :], m2_scr[dil, :]
        m = jnp.maximum(jnp.maximum(m0, m1), m2)
        a0, a1, a2 = jnp.exp(m0 - m), jnp.exp(m1 - m), jnp.exp(m2 - m)
        num = o0_scr[tok, :] * a0 + o1_scr[tok, :] * a1 + o2_scr[dil, :] * a2
        den = l0_scr[tok, :] * a0 + l1_scr[tok, :] * a1 + l2_scr[dil, :] * a2
        o_ref[tok, :] = num / den
        return carry

    lax.fori_loop(0, blk // d_last, merge_body, 0, unroll=4)


def _attn_prompt(q, k, v, rel_bias):
    b, npair, t, _ = q.shape
    blk = ATT_BLOCK
    cur = pl.BlockSpec((None, None, blk, LANES), lambda bi, p, i: (bi, p, i, 0))
    prev = pl.BlockSpec((None, None, blk, LANES), lambda bi, p, i: (bi, p, jnp.maximum(i - 1, 0), 0))
    n_tiles = len(DILATIONS) * npair
    stats = ([pltpu.VMEM((blk, LANES), F32)] * 6
             + [pltpu.VMEM((DILATIONS[-1] * ATT_PITCH, LANES), F32)] * 3)
    return pl.pallas_call(
        _attn_prompt_kernel,
        grid=(b, npair, t // blk),
        in_specs=[pl.BlockSpec(memory_space=pltpu.SMEM),
                  pl.BlockSpec((len(DILATIONS), TAPS, 2 * TAPS), lambda bi, p, i: (0, 0, 0)),
                  cur, cur, prev, cur, prev],
        out_specs=cur,
        out_shape=jax.ShapeDtypeStruct(q.shape, F32),
        scratch_shapes=[pltpu.VMEM((n_tiles, 2 * TAPS, 2 * TAPS), F32),
                        pltpu.VMEM((2, 2 * TAPS, 2 * TAPS), F32),
                        pltpu.VMEM((2 * blk, LANES), F32),
                        pltpu.VMEM((2 * blk, LANES), F32)] + stats,
        compiler_params=_cparams(3),
        name="attn_prompt",
    )(rel_bias, _prompt_bias_ids(), q, k, k, v, v)


DEC_T = 4
DEC_NEW_PAD = 128
DEC_KEYS = MAX_WINDOW + DEC_NEW_PAD


def _decode_tables():
    qi = jnp.arange(DEC_T, dtype=jnp.int32)[:, None]
    rho = jnp.arange(MAX_WINDOW, dtype=jnp.int32)[None, :]
    dist_b = MAX_WINDOW + qi - rho
    mult_b = jnp.zeros_like(dist_b)
    for d in DILATIONS:
        mult_b = mult_b + (((dist_b % d) == 0) & (dist_b <= d * TAPS)).astype(jnp.int32)
    jj = jnp.arange(DEC_NEW_PAD, dtype=jnp.int32)[None, :]
    dist_n = qi - jj
    mult_n = jnp.where(dist_n == 0, len(DILATIONS), jnp.where((dist_n > 0) & (jj < DEC_T), 1, 0))
    dist = jnp.concatenate([dist_b, jnp.maximum(dist_n, 0)], axis=1)
    mult = jnp.concatenate([mult_b, mult_n.astype(jnp.int32)], axis=1)
    ids = jnp.where(mult > 0, _t5_bucket(dist), -1).astype(jnp.int32)
    ids = jnp.repeat(ids, N_HEADS, axis=0)
    mult = jnp.repeat(mult, N_HEADS, axis=0).astype(F32)
    return ids, mult


def _attn_decode_kernel(rbt_ref, ids_ref, mult_ref, q_ref, kn_ref, vn_ref, kt_ref, vt_ref, o_ref, bias_scr):
    n_rows = DEC_T * N_HEADS

    @pl.when(pl.program_id(0) == 0)
    def _build_bias():
        ids = ids_ref[...]
        tile = jnp.full((n_rows, DEC_KEYS), NEG_INF, F32)
        for bkt in range(N_BUCKETS):
            tile = jnp.where(ids == bkt, rbt_ref[:, bkt:bkt + 1], tile)
        bias_scr[...] = tile

    lane = lax.broadcasted_iota(jnp.int32, (N_HEADS, MIX_A), 1)
    head = lax.broadcasted_iota(jnp.int32, (N_HEADS, MIX_A), 0)
    hmask = (lane // HEAD_DIM) == head

    q = q_ref[...]
    qm = jnp.where(hmask[None], q[:, None, :], 0.0).reshape(n_rows, MIX_A).astype(BF16)
    zpad = jnp.zeros((DEC_NEW_PAD - DEC_T, MIX_A), F32)
    kn = jnp.concatenate([kn_ref[...], zpad], axis=0).astype(BF16)
    vn = jnp.concatenate([vn_ref[...], zpad], axis=0).astype(BF16)
    kt = kt_ref[...].astype(BF16)
    vt = vt_ref[...].astype(BF16)
    nt = (((1,), (1,)), ((), ()))
    s = jnp.concatenate([jnp.dot(qm, kt, preferred_element_type=F32),
                         lax.dot_general(qm, kn, nt, preferred_element_type=F32)], axis=1)
    s = s + bias_scr[...]
    m = jnp.max(s, axis=1, keepdims=True)
    pe = jnp.exp(s - m) * mult_ref[...]
    l = jnp.sum(pe, axis=1, keepdims=True)
    pb = pe.astype(BF16)
    o = (lax.dot_general(pb[:, :MAX_WINDOW], vt, nt, preferred_element_type=F32)
         + jnp.dot(pb[:, MAX_WINDOW:], vn, preferred_element_type=F32)) / l
    o3 = o.reshape(DEC_T, N_HEADS, MIX_A)
    o_ref[...] = jnp.sum(jnp.where(hmask[None], o3, 0.0), axis=1)


def _attn_decode(q, k_new, v_new, cache_kt, cache_vt, rel_bias):
    bd = q.shape[0]
    ids, mult = _decode_tables()
    rbt = jnp.tile(rel_bias.T, (DEC_T, 1))
    new_spec = pl.BlockSpec((None, DEC_T, MIX_A), lambda b: (b, 0, 0))
    cache_spec = pl.BlockSpec((None, MIX_A, MAX_WINDOW), lambda b: (b, 0, 0))
    n_rows = DEC_T * N_HEADS
    const = lambda shape: pl.BlockSpec(shape, lambda b: (0, 0))
    return pl.pallas_call(
        _attn_decode_kernel,
        grid=(bd,),
        in_specs=[const((n_rows, N_BUCKETS)), const((n_rows, DEC_KEYS)), const((n_rows, DEC_KEYS)),
                  new_spec, new_spec, new_spec, cache_spec, cache_spec],
        out_specs=new_spec,
        out_shape=jax.ShapeDtypeStruct((bd, DEC_T, MIX_A), F32),
        scratch_shapes=[pltpu.VMEM((n_rows, DEC_KEYS), F32)],
        compiler_params=_cparams(1),
        name="attn_decode",
    )(rbt, ids, mult, q, k_new, v_new, cache_kt, cache_vt)


def _ssm_matrices(a_re, a_im, log_dt, b_re, b_im, c_re, c_im, d_skip):
    hi = lax.Precision.HIGHEST
    g, p = a_re.shape
    dt = jnp.exp(log_dt)[:, None]
    decay = jnp.exp(a_re * dt)
    abar_re = decay * jnp.cos(a_im * dt)
    abar_im = decay * jnp.sin(a_im * dt)
    inv = 1.0 / (a_re * a_re + a_im * a_im)
    coef_re = ((abar_re - 1.0) * a_re + abar_im * a_im) * inv
    coef_im = (abar_im * a_re - (abar_re - 1.0) * a_im) * inv
    bbar_re = coef_re[..., None] * b_re - coef_im[..., None] * b_im
    bbar_im = coef_re[..., None] * b_im + coef_im[..., None] * b_re
    pw_re = [jnp.ones_like(abar_re)]
    pw_im = [jnp.zeros_like(abar_im)]
    for _ in range(CHUNK):
        pr, pi = pw_re[-1], pw_im[-1]
        pw_re.append(pr * abar_re - pi * abar_im)
        pw_im.append(pr * abar_im + pi * abar_re)
    pw_re = jnp.stack(pw_re)
    pw_im = jnp.stack(pw_im)
    cp_re = c_re[None] * pw_re[:, :, None, :] - c_im[None] * pw_im[:, :, None, :]
    cp_im = c_re[None] * pw_im[:, :, None, :] + c_im[None] * pw_re[:, :, None, :]
    kern = (jnp.einsum('mgcp,gpd->mgcd', cp_re[:CHUNK], bbar_re, precision=hi)
            - jnp.einsum('mgcp,gpd->mgcd', cp_im[:CHUNK], bbar_im, precision=hi))
    ti = jnp.arange(CHUNK)[:, None]
    to = jnp.arange(CHUNK)[None, :]
    lag = to - ti
    blocks = kern[jnp.clip(lag, 0, CHUNK - 1)]
    blocks = jnp.where((lag >= 0)[:, :, None, None, None], blocks, 0.0)
    t_intra = jnp.transpose(blocks, (2, 0, 4, 1, 3)).reshape(g, CHUNK * SSM_GROUP, CHUNK * SSM_GROUP)
    rev_re = pw_re[CHUNK - 1 - jnp.arange(CHUNK)]
    rev_im = pw_im[CHUNK - 1 - jnp.arange(CHUNK)]
    in_re = rev_re[..., None] * bbar_re[None] - rev_im[..., None] * bbar_im[None]
    in_im = rev_re[..., None] * bbar_im[None] + rev_im[..., None] * bbar_re[None]
    t_in_re = jnp.transpose(in_re, (1, 0, 3, 2)).reshape(g, CHUNK * SSM_GROUP, p)
    t_in_im = jnp.transpose(in_im, (1, 0, 3, 2)).reshape(g, CHUNK * SSM_GROUP, p)
    t_out_re = jnp.transpose(cp_re[1:], (1, 3, 0, 2)).reshape(g, p, CHUNK * SSM_GROUP)
    t_out_im = -jnp.transpose(cp_im[1:], (1, 3, 0, 2)).reshape(g, p, CHUNK * SSM_GROUP)
    d_tile = jnp.tile(d_skip.reshape(g, 1, SSM_GROUP), (1, 1, CHUNK))
    return dict(t_intra=t_intra, t_in_re=t_in_re, t_in_im=t_in_im, t_out_re=t_out_re,
                t_out_im=t_out_im, d_tile=d_tile, pw_re=pw_re, pw_im=pw_im)


def _ssm_prompt_kernel(u_ref, ti_ref, tinr_ref, tini_ref, toutr_ref, touti_ref, d_ref, ar_ref, ai_ref,
                       y_ref, sre_ref, sim_ref, pr_scr, pi_scr, sr_scr, si_scr):
    nc = u_ref.shape[0]
    u = u_ref[...]
    ub = u.astype(BF16)
    pr_scr[...] = jnp.dot(ub, tinr_ref[...], preferred_element_type=F32)
    pi_scr[...] = jnp.dot(ub, tini_ref[...], preferred_element_type=F32)
    ar = ar_ref[...]
    ai = ai_ref[...]

    def step(c, carry):
        sr, si = carry
        sr_scr[pl.ds(c, 1), :] = sr
        si_scr[pl.ds(c, 1), :] = si
        nr = ar * sr - ai * si + pr_scr[pl.ds(c, 1), :]
        ni = ar * si + ai * sr + pi_scr[pl.ds(c, 1), :]
        return nr, ni

    zero = jnp.zeros((1, SSM_STATE), F32)
    sr, si = lax.fori_loop(0, nc, step, (zero, zero), unroll=8)
    sre_ref[...] = sr
    sim_ref[...] = si
    y = jnp.dot(ub, ti_ref[...], preferred_element_type=F32)
    y = y + jnp.dot(sr_scr[...].astype(BF16), toutr_ref[...], preferred_element_type=F32)
    y = y + jnp.dot(si_scr[...].astype(BF16), touti_ref[...], preferred_element_type=F32)
    y_ref[...] = y + d_ref[...] * u


def _ssm_prompt(ug, mats):
    b, g, nc, w = ug.shape
    p = SSM_STATE
    gspec = lambda shape: pl.BlockSpec((None,) + shape, lambda bi, gi: (gi, 0, 0))
    io = pl.BlockSpec((None, None, nc, w), lambda bi, gi: (bi, gi, 0, 0))
    st = pl.BlockSpec((None, None, 1, p), lambda bi, gi: (bi, gi, 0, 0))
    return pl.pallas_call(
        _ssm_prompt_kernel,
        grid=(b, g),
        in_specs=[io, gspec((w, w)), gspec((w, p)), gspec((w, p)), gspec((p, w)), gspec((p, w)),
                  gspec((1, w)), gspec((1, p)), gspec((1, p))],
        out_specs=[io, st, st],
        out_shape=[jax.ShapeDtypeStruct(ug.shape, F32),
                   jax.ShapeDtypeStruct((b, g, 1, p), F32), jax.ShapeDtypeStruct((b, g, 1, p), F32)],
        scratch_shapes=[pltpu.VMEM((nc, p), F32)] * 4,
        compiler_params=_cparams(2),
        name="ssm_prompt",
    )(ug, mats['t_intra'].astype(BF16), mats['t_in_re'].astype(BF16), mats['t_in_im'].astype(BF16),
      mats['t_out_re'].astype(BF16), mats['t_out_im'].astype(BF16), mats['d_tile'],
      mats['pw_re'][CHUNK][:, None, :], mats['pw_im'][CHUNK][:, None, :])


def _ssm_decode_kernel(u_ref, s0r_ref, s0i_ref, ti_ref, tinr_ref, tini_ref, toutr_ref, touti_ref, d_ref,
                       ar_ref, ai_ref, y_ref, sre_ref, sim_ref):
    u = u_ref[...]
    ub = u.astype(BF16)
    s0r = s0r_ref[...]
    s0i = s0i_ref[...]
    y = jnp.dot(ub, ti_ref[...], preferred_element_type=F32)
    y = y + jnp.dot(s0r.astype(BF16), toutr_ref[...], preferred_element_type=F32)
    y = y + jnp.dot(s0i.astype(BF16), touti_ref[...], preferred_element_type=F32)
    y_ref[...] = y + d_ref[...] * u
    ar = ar_ref[...]
    ai = ai_ref[...]
    sre_ref[...] = ar * s0r - ai * s0i + jnp.dot(ub, tinr_ref[...], preferred_element_type=F32)
    sim_ref[...] = ar * s0i + ai * s0r + jnp.dot(ub, tini_ref[...], preferred_element_type=F32)


def _ssm_decode(ug, s0_re, s0_im, mats):
    g, bd, w = ug.shape
    p = SSM_STATE
    lo = (CHUNK - DEC_T) * SSM_GROUP
    gspec = lambda shape: pl.BlockSpec((None,) + shape, lambda gi: (gi, 0, 0))
    return pl.pallas_call(
        _ssm_decode_kernel,
        grid=(g,),
        in_specs=[gspec((bd, w)), gspec((bd, p)), gspec((bd, p)), gspec((w, w)), gspec((w, p)), gspec((w, p)),
                  gspec((p, w)), gspec((p, w)), gspec((1, w)), gspec((1, p)), gspec((1, p))],
        out_specs=[gspec((bd, w)), gspec((bd, p)), gspec((bd, p))],
        out_shape=[jax.ShapeDtypeStruct((g, bd, w), F32), jax.ShapeDtypeStruct((g, bd, p), F32),
                   jax.ShapeDtypeStruct((g, bd, p), F32)],
        compiler_params=_cparams(1),
        name="ssm_decode",
    )(ug, s0_re, s0_im, mats['t_intra'][:, :w, :w].astype(BF16),
      mats['t_in_re'][:, lo:, :].astype(BF16), mats['t_in_im'][:, lo:, :].astype(BF16),
      mats['t_out_re'][:, :, :w].astype(BF16), mats['t_out_im'][:, :, :w].astype(BF16),
      mats['d_tile'][:, :, :w], mats['pw_re'][DEC_T][:, None, :], mats['pw_im'][DEC_T][:, None, :])


def _gelu_tanh(x):
    return 0.5 * x * (1.0 + jnp.tanh(math.sqrt(2.0 / math.pi) * (x + 0.044715 * (x * x * x))))


def _sigmoid(x):
    return 1.0 / (1.0 + jnp.exp(-x))


def _route(hn, wr_ref):
    n = hn.shape[0]
    logits = jnp.dot(hn, wr_ref[...], preferred_element_type=F32, precision=lax.Precision.HIGHEST)
    lidx = lax.broadcasted_iota(jnp.int32, (n, ROUTER_LANES), 1)
    is_e = lidx < N_EXPERTS
    is_g = jnp.logical_and(lidx >= N_EXPERTS, lidx < N_EXPERTS + N_EXPERT_GROUPS)
    gmax = jnp.max(jnp.where(is_g, logits, -jnp.inf), axis=1, keepdims=True)
    g_prob = 1.0 / jnp.sum(jnp.where(is_g, jnp.exp(logits - gmax), 0.0), axis=1, keepdims=True)
    g_sel = jnp.min(jnp.where(jnp.logical_and(is_g, logits == gmax), lidx - N_EXPERTS, N_EXPERT_GROUPS),
                    axis=1, keepdims=True)
    in_grp = jnp.logical_and(is_e, (lidx // EXPERTS_PER_GROUP) == g_sel)
    l1 = jnp.max(jnp.where(in_grp, logits, -jnp.inf), axis=1, keepdims=True)
    i1 = jnp.min(jnp.where(jnp.logical_and(in_grp, logits == l1), lidx, ROUTER_LANES), axis=1, keepdims=True)
    rest = jnp.logical_and(in_grp, lidx != i1)
    l2 = jnp.max(jnp.where(rest, logits, -jnp.inf), axis=1, keepdims=True)
    i2 = jnp.min(jnp.where(jnp.logical_and(rest, logits == l2), lidx, ROUTER_LANES), axis=1, keepdims=True)
    e2 = jnp.exp(l2 - l1)
    w1 = g_prob / (1.0 + e2)
    w2 = g_prob * e2 / (1.0 + e2)
    return jnp.where(lidx == i1, w1, 0.0) + jnp.where(lidx == i2, w2, 0.0)


def _post_body(oa, ys, x, ga_ref, gb_ref, wglu_ref, bglu_ref, wout_ref, gf_ref, wr_ref,
               x1_ref, hn_ref, gates_ref):
    z = _gelu_tanh(ys)
    gate = _sigmoid(jnp.dot(z.astype(BF16), wglu_ref[...], preferred_element_type=F32) + bglu_ref[...])
    ob = z * gate
    mixed = jnp.concatenate([_rms(oa, ga_ref[...]), _rms(ob, gb_ref[...])], axis=1).astype(BF16)
    x1 = x + jnp.dot(mixed, wout_ref[...], preferred_element_type=F32)
    x1_ref[...] = x1
    hn = _rms(x1, gf_ref[...])
    hn_ref[...] = hn.astype(BF16)
    gates_ref[...] = _route(hn, wr_ref)


def _post_pair_kernel(oa_ref, ys_ref, x_ref, *rest):
    oa = jnp.concatenate([oa_ref[j] for j in range(MIX_A // LANES)], axis=1)
    _post_body(oa, ys_ref[...], x_ref[...], *rest)


def _post_tok_kernel(oa_ref, ys_ref, x_ref, *rest):
    _post_body(oa_ref[...], ys_ref[...], x_ref[...], *rest)


def _post_mix(oa, ys, x, w, tm, pair):
    b, t, d = x.shape
    row = lambda width: pl.BlockSpec((None, tm, width), lambda bi, i: (bi, i, 0))
    const = lambda a: pl.BlockSpec(a.shape, lambda bi, i: (0,) * a.ndim)
    oa_spec = (pl.BlockSpec((None, MIX_A // LANES, tm, LANES), lambda bi, i: (bi, 0, i, 0)) if pair
               else row(MIX_A))
    weights = [w['g_out_a'], w['g_out_b'], w['w_glu'], w['b_glu'], w['w_out'], w['g_ffn'], w['w_router']]
    return pl.pallas_call(
        _post_pair_kernel if pair else _post_tok_kernel,
        grid=(b, t // tm),
        in_specs=[oa_spec, row(MIX_B), row(d)] + [const(a) for a in weights],
        out_specs=[row(d), row(d), row(ROUTER_LANES)],
        out_shape=[jax.ShapeDtypeStruct((b, t, d), F32), jax.ShapeDtypeStruct((b, t, d), BF16),
                   jax.ShapeDtypeStruct((b, t, ROUTER_LANES), F32)],
        compiler_params=_cparams(2),
        name="post_mix_prompt" if pair else "post_mix_sample",
    )(oa, ys, x, *weights)


def _moe_kernel(hn_ref, gates_ref, x1_ref, wg_ref, wu_ref, wd_ref, gfin_ref, y_ref, acc_scr):
    e = pl.program_id(1)

    @pl.when(e == 0)
    def _init():
        acc_scr[...] = jnp.zeros_like(acc_scr)

    hn = hn_ref[...]
    a = jnp.dot(hn, wg_ref[...], preferred_element_type=F32)
    u = jnp.dot(hn, wu_ref[...], preferred_element_type=F32)
    eidx = lax.broadcasted_iota(jnp.int32, gates_ref.shape, 1)
    gate = jnp.sum(jnp.where(eidx == e, gates_ref[...], 0.0), axis=1, keepdims=True)
    act = (a * _sigmoid(a)) * u * gate
    acc_scr[...] += jnp.dot(act.astype(BF16), wd_ref[...], preferred_element_type=F32)

    @pl.when(e == pl.num_programs(1) - 1)
    def _fin():
        y_ref[...] = _rms(x1_ref[...] + acc_scr[...], gfin_ref[...])


def _moe(hn, gates, x1, wg, wu, wd, g_final, tm):
    n, d = x1.shape
    row = lambda width: pl.BlockSpec((tm, width), lambda i, e: (i, 0))
    return pl.pallas_call(
        _moe_kernel,
        grid=(n // tm, N_EXPERTS),
        in_specs=[row(d), row(ROUTER_LANES), row(d),
                  pl.BlockSpec((None, d, D_EXPERT), lambda i, e: (e, 0, 0)),
                  pl.BlockSpec((None, d, D_EXPERT), lambda i, e: (e, 0, 0)),
                  pl.BlockSpec((None, D_EXPERT, d), lambda i, e: (e, 0, 0)),
                  pl.BlockSpec((1, d), lambda i, e: (0, 0))],
        out_specs=row(d),
        out_shape=jax.ShapeDtypeStruct((n, d), F32),
        scratch_shapes=[pltpu.VMEM((tm, d), F32)],
        compiler_params=_cparams(2),
        name="moe",
    )(hn, gates, x1, wg, wu, wd, g_final)


def kernel(x_prompt, x_sample, cache_k, cache_v, state_ssm_re, state_ssm_im, rel_bias, g_mix, w_in, g_out_a, g_out_b, w_out, ssm_a_re, ssm_a_im, ssm_log_dt, ssm_b_re, ssm_b_im, ssm_c_re, ssm_c_im, ssm_d, w_glu, b_glu, g_ffn, w_router_group, w_router_expert, w_expert_gate, w_expert_up, w_expert_down, g_final):
    depth = g_mix.shape[0]
    assert depth == 1, "kernel is written for the single-layer configuration of the problem"
    l = 0
    b, t, d = x_prompt.shape
    bd, ts, _ = x_sample.shape
    assert ts == DEC_T and t % ATT_BLOCK == 0 and cache_k.shape[2] == MAX_WINDOW
    keep = min(MAX_WINDOW, t)

    w_in_b = w_in[l].astype(BF16)
    gm = g_mix[l][None, :]
    post_w = dict(
        g_out_a=g_out_a[l][None, :], g_out_b=g_out_b[l][None, :], w_glu=w_glu[l].astype(BF16),
        b_glu=b_glu[l][None, :], w_out=w_out[l].astype(BF16), g_ffn=g_ffn[l][None, :],
        w_router=jnp.concatenate(
            [jnp.transpose(w_router_expert[l], (1, 0, 2)).reshape(d, N_EXPERTS), w_router_group[l],
             jnp.zeros((d, ROUTER_LANES - N_EXPERTS - N_EXPERT_GROUPS), F32)], axis=1))
    wg = w_expert_gate[l].astype(BF16)
    wu = w_expert_up[l].astype(BF16)
    wd = w_expert_down[l].astype(BF16)
    gfin = g_final[None, :]
    mats = _ssm_matrices(ssm_a_re[l], ssm_a_im[l], ssm_log_dt[l], ssm_b_re[l], ssm_b_im[l],
                         ssm_c_re[l], ssm_c_im[l], ssm_d[l])

    qp, kp, vp, k_tok, v_tok, u_tok = _inproj_pair(x_prompt, gm, w_in_b, tm=512)
    oa_p = _attn_prompt(qp, kp, vp, rel_bias)
    ug = jnp.transpose(u_tok.reshape(b, t, N_GROUPS, SSM_GROUP), (0, 2, 1, 3))
    ug = ug.reshape(b, N_GROUPS, t // CHUNK, CHUNK * SSM_GROUP)
    yg, rp, ip = _ssm_prompt(ug, mats)
    ys_p = jnp.transpose(yg.reshape(b, N_GROUPS, t, SSM_GROUP), (0, 2, 1, 3)).reshape(b, t, MIX_B)
    x1_p, hn_p, gates_p = _post_mix(oa_p, ys_p, x_prompt, post_w, tm=512, pair=True)
    y_p = _moe(hn_p.reshape(b * t, d), gates_p.reshape(b * t, ROUTER_LANES), x1_p.reshape(b * t, d),
               wg, wu, wd, gfin, tm=1024)

    n_s = bd * ts
    qs, ks, vs, us = _inproj_tok(x_sample.reshape(n_s, d), gm, w_in_b)
    ckt = jnp.transpose(cache_k[l], (0, 2, 3, 1)).reshape(bd, MIX_A, MAX_WINDOW)
    cvt = jnp.transpose(cache_v[l], (0, 2, 3, 1)).reshape(bd, MIX_A, MAX_WINDOW)
    oa_s = _attn_decode(qs.reshape(bd, ts, MIX_A), ks.reshape(bd, ts, MIX_A), vs.reshape(bd, ts, MIX_A),
                        ckt, cvt, rel_bias)
    ugs = jnp.transpose(us.reshape(bd, ts, N_GROUPS, SSM_GROUP), (2, 0, 1, 3)).reshape(N_GROUPS, bd, ts * SSM_GROUP)
    s0r = jnp.transpose(state_ssm_re[l], (1, 0, 2))
    s0i = jnp.transpose(state_ssm_im[l], (1, 0, 2))
    ygs, rs, is_ = _ssm_decode(ugs, s0r, s0i, mats)
    ys_s = jnp.transpose(ygs.reshape(N_GROUPS, bd, ts, SSM_GROUP), (1, 2, 0, 3)).reshape(n_s, MIX_B)
    x1_s, hn_s, gates_s = _post_mix(oa_s.reshape(1, n_s, MIX_A), ys_s.reshape(1, n_s, MIX_B),
                                    x_sample.reshape(1, n_s, d), post_w, tm=n_s, pair=False)
    y_s = _moe(hn_s.reshape(n_s, d), gates_s.reshape(n_s, ROUTER_LANES), x1_s.reshape(n_s, d),
               wg, wu, wd, gfin, tm=n_s)

    y_prompt = y_p.reshape(b, t, d)
    y_sample = y_s.reshape(bd, ts, d)
    k_win = k_tok[:, t - keep:].reshape(1, b, keep, N_HEADS, HEAD_DIM)
    v_win = v_tok[:, t - keep:].reshape(1, b, keep, N_HEADS, HEAD_DIM)
    k_new = ks.reshape(1, bd, ts, N_HEADS, HEAD_DIM)
    v_new = vs.reshape(1, bd, ts, N_HEADS, HEAD_DIM)
    return (y_prompt, y_sample, k_win, v_win, k_new, v_new,
            rp.reshape(1, b, N_GROUPS, SSM_STATE), ip.reshape(1, b, N_GROUPS, SSM_STATE),
            jnp.transpose(rs, (1, 0, 2))[None], jnp.transpose(is_, (1, 0, 2))[None])
```

```python
import functools
import math

import jax
import jax.numpy as jnp
import numpy as np
from jax import lax
from jax.experimental import pallas as pl
from jax.experimental.pallas import tpu as pltpu

F32 = jnp.float32
BF16 = jnp.bfloat16

D_MODEL = 1024
HEAD_DIM = 64
MIX_A = 512
N_HEADS = 8
MIX_B = 512
SSM_GROUP = 16
N_GROUPS = 32
SSM_STATE = 64
PROJ_COLS = 3 * MIX_A + MIX_B
DILATIONS = (1, 4, 16)
TAPS = 128
MAX_WINDOW = 2048
N_BUCKETS = 32
N_EXPERT_GROUPS = 4
EXPERTS_PER_GROUP = 8
N_EXPERTS = 32
D_EXPERT = 256
EPS = 1e-6
NEG_INF = -1e30
SCALE = HEAD_DIM ** -0.5

LANES = 128
ROUTER_LANES = 128
CHUNK = 16
SSM_GROUPS_PER_STEP = 4
ATT_BLOCK = 2048
ATT_UNROLL = 8
ATT_PITCH = 136
VMEM_LIMIT = 56 * 1024 * 1024


def _cparams(n_axes):
    return pltpu.CompilerParams(dimension_semantics=("arbitrary",) * n_axes,
                                vmem_limit_bytes=VMEM_LIMIT)


def _t5_bucket(distance):
    max_exact = N_BUCKETS // 2
    nf = jnp.maximum(distance, 1).astype(F32)
    large = max_exact + (jnp.log(nf / max_exact) / math.log(MAX_WINDOW / max_exact)
                         * (N_BUCKETS - max_exact)).astype(jnp.int32)
    large = jnp.minimum(large, N_BUCKETS - 1)
    return jnp.where(distance < max_exact, distance, large)


def _rms(x, g):
    return x * lax.rsqrt(jnp.mean(x * x, axis=-1, keepdims=True) + EPS) * g


def _inproj_pair_kernel(first_win_tile, x_ref, g_ref, w_ref, q_ref, k_ref, v_ref, kt_ref, vt_ref, u_ref):
    h = _rms(x_ref[...], g_ref[...])
    p = jnp.dot(h.astype(BF16), w_ref[...], preferred_element_type=F32)
    for j in range(MIX_A // LANES):
        q_ref[j] = p[:, LANES * j:LANES * (j + 1)] * SCALE
        k_ref[j] = p[:, MIX_A + LANES * j:MIX_A + LANES * (j + 1)]
        v_ref[j] = p[:, 2 * MIX_A + LANES * j:2 * MIX_A + LANES * (j + 1)]
    u_ref[...] = p[:, 3 * MIX_A:]

    @pl.when(pl.program_id(1) >= first_win_tile)
    def _window():
        kt_ref[...] = p[:, MIX_A:2 * MIX_A].T
        vt_ref[...] = p[:, 2 * MIX_A:3 * MIX_A].T


def _inproj_pair(x, g, w_bf16, tm, keep):
    b, t, d = x.shape
    npair = MIX_A // LANES
    first = (t - keep) // tm
    pair = jax.ShapeDtypeStruct((b, npair, t, LANES), F32)
    win = jax.ShapeDtypeStruct((b, MIX_A, keep), F32)
    tok = jax.ShapeDtypeStruct((b, t, MIX_A), F32)
    pair_spec = pl.BlockSpec((None, npair, tm, LANES), lambda bi, i: (bi, 0, i, 0))
    win_spec = pl.BlockSpec((None, MIX_A, tm), lambda bi, i: (bi, 0, jnp.maximum(i - first, 0)))
    tok_spec = pl.BlockSpec((None, tm, MIX_A), lambda bi, i: (bi, i, 0))
    return pl.pallas_call(
        functools.partial(_inproj_pair_kernel, first),
        grid=(b, t // tm),
        in_specs=[pl.BlockSpec((None, tm, d), lambda bi, i: (bi, i, 0)),
                  pl.BlockSpec((1, d), lambda bi, i: (0, 0)),
                  pl.BlockSpec((d, PROJ_COLS), lambda bi, i: (0, 0))],
        out_specs=[pair_spec, pair_spec, pair_spec, win_spec, win_spec, tok_spec],
        out_shape=[pair, pair, pair, win, win, tok],
        compiler_params=_cparams(2),
        name="inproj_prompt",
    )(x, g, w_bf16)


def _inproj_tok_kernel(x_ref, g_ref, w_ref, q_ref, k_ref, v_ref, u_ref):
    h = _rms(x_ref[...], g_ref[...])
    p = jnp.dot(h.astype(BF16), w_ref[...], preferred_element_type=F32)
    q_ref[...] = p[:, :MIX_A] * SCALE
    k_ref[...] = p[:, MIX_A:2 * MIX_A]
    v_ref[...] = p[:, 2 * MIX_A:3 * MIX_A]
    u_ref[...] = p[:, 3 * MIX_A:]


def _inproj_tok(x, g, w_bf16):
    n, d = x.shape
    out = jax.ShapeDtypeStruct((n, MIX_A), F32)
    return pl.pallas_call(
        _inproj_tok_kernel,
        out_shape=[out, out, out, out],
        compiler_params=pltpu.CompilerParams(vmem_limit_bytes=VMEM_LIMIT),
        name="inproj_sample",
    )(x, g, w_bf16)


def _prompt_bias_ids():
    r = jnp.arange(TAPS, dtype=jnp.int32)[:, None]
    kpos = jnp.arange(2 * TAPS, dtype=jnp.int32)[None, :] - TAPS
    rel = r - kpos
    valid = (rel >= 0) & (rel <= TAPS)
    ids = [jnp.where(valid, _t5_bucket(d * rel), -1) for d in DILATIONS]
    return jnp.stack(ids).astype(jnp.int32)


def _attn_prompt_kernel(rb_ref, ids_ref, q_ref, kc_ref, kp_ref, vc_ref, vp_ref, o_ref,
                        bias_scr, fm_scr, kcat, vcat, m0_scr, l0_scr, o0_scr, m1_scr, l1_scr, o1_scr,
                        m2_scr, l2_scr, o2_scr):
    bi = pl.program_id(0)
    p = pl.program_id(1)
    i = pl.program_id(2)
    blk = ATT_BLOCK
    npair = N_HEADS // 2

    @pl.when((bi == 0) & (p == 0) & (i == 0))
    def _build_bias():
        col = lax.broadcasted_iota(jnp.int32, (2 * TAPS, 2 * TAPS), 1)
        fm_scr[0] = jnp.zeros((2 * TAPS, 2 * TAPS), F32)
        fm_scr[1] = jnp.where(col < TAPS, NEG_INF, 0.0).astype(F32)
        for c in range(len(DILATIONS)):
            ids = ids_ref[c]

            def head_body(h, carry, ids=ids, c=c):
                tile = jnp.full((TAPS, 2 * TAPS), NEG_INF, F32)
                for bkt in range(N_BUCKETS):
                    tile = jnp.where(ids == bkt, rb_ref[bkt, h], tile)
                row0 = pl.multiple_of((h % 2) * TAPS, TAPS)
                bias_scr[c * npair + h // 2, pl.ds(row0, TAPS), :] = tile
                return carry

            lax.fori_loop(0, N_HEADS, head_body, 0)

    kcat[pl.ds(0, blk), :] = kp_ref[...]
    kcat[pl.ds(blk, blk), :] = kc_ref[...]
    vcat[pl.ds(0, blk), :] = vp_ref[...]
    vcat[pl.ds(blk, blk), :] = vc_ref[...]

    lane = lax.broadcasted_iota(jnp.int32, (1, LANES), 1)
    low = lane < HEAD_DIM
    nt = (((1,), (1,)), ((), ()))
    ones = jnp.ones((2 * TAPS, LANES), BF16)
    stats = ((m0_scr, l0_scr, o0_scr), (m1_scr, l1_scr, o1_scr), (m2_scr, l2_scr, o2_scr))

    def rows(ref, start, d):
        if d == 1:
            return ref[pl.ds(start, TAPS), :]
        return ref[pl.ds(start, TAPS, stride=d), :]

    def sub_block(c, d, qs, first, dst_rows):
        q = rows(q_ref, qs, d)
        q2 = jnp.concatenate([jnp.where(low, q, 0.0), jnp.where(low, 0.0, q)], axis=0).astype(BF16)
        k = jnp.concatenate([rows(kcat, blk + qs - d * TAPS, d), rows(kcat, blk + qs, d)],
                            axis=0).astype(BF16)
        v = jnp.concatenate([rows(vcat, blk + qs - d * TAPS, d), rows(vcat, blk + qs, d)],
                            axis=0).astype(BF16)
        s_ = lax.dot_general(q2, k, nt, preferred_element_type=F32) + bias_scr[c * npair + p]
        if first is not None:
            s_ = s_ + fm_scr[first]
        m = jnp.max(s_, axis=1, keepdims=True)
        pe = jnp.exp(s_ - m).astype(BF16)
        ol = jnp.dot(pe, jnp.concatenate([v, ones], axis=1), preferred_element_type=F32)
        o, l = ol[:, :LANES], ol[:, LANES:]
        m_ref, l_ref, a_ref = stats[c]
        m_ref[dst_rows, :] = jnp.where(low, m[:TAPS], m[TAPS:])
        l_ref[dst_rows, :] = jnp.where(low, l[:TAPS], l[TAPS:])
        a_ref[dst_rows, :] = jnp.where(low, o[:TAPS], o[TAPS:])

    at_start = jnp.where(i == 0, 1, 0)
    n_sub = blk // TAPS
    for c, d in enumerate(DILATIONS):
        n_grp = blk // (d * TAPS)

        def group_body(s8, carry, c=c, d=d, n_grp=n_grp):
            for u in range(ATT_UNROLL):
                s = s8 * ATT_UNROLL + u
                if n_grp >= ATT_UNROLL:
                    qs = pl.multiple_of(s * TAPS, TAPS)
                    first = jnp.where(s8 == 0, at_start, 0) if u == 0 else None
                    dst_rows = pl.ds(qs, TAPS)
                else:
                    r = s8 * (ATT_UNROLL // n_grp) + u // n_grp
                    g = u % n_grp
                    qs = r + d * TAPS * g
                    first = at_start if g == 0 else None
                    if n_grp > 1:
                        dst_rows = pl.ds(qs, TAPS, stride=d)
                    else:
                        dst_rows = pl.ds(pl.multiple_of(r * ATT_PITCH, 8), TAPS)
                sub_block(c, d, qs, first, dst_rows)
            return carry

        lax.fori_loop(0, n_sub // ATT_UNROLL, group_body, 0)

    d_last = DILATIONS[-1]

    def merge_body(j, carry):
        tok = pl.ds(pl.multiple_of(j * d_last, d_last), d_last)
        dil = pl.ds(j, d_last, stride=ATT_PITCH)
        m0, m1, m2 = m0_scr[tok, :], m1_scr[tok, :], m2_scr[dil, :]
        m = jnp.maximum(jnp.maximum(m0, m1), m2)
        a0, a1, a2 = jnp.exp(m0 - m), jnp.exp(m1 - m), jnp.exp(m2 - m)
        num = o0_scr[tok, :] * a0 + o1_scr[tok, :] * a1 + o2_scr[dil, :] * a2
        den = l0_scr[tok, :] * a0 + l1_scr[tok, :] * a1 + l2_scr[dil, :] * a2
        o_ref[tok, :] = num / den
        return carry

    lax.fori_loop(0, blk // d_last, merge_body, 0, unroll=4)


def _attn_prompt(q, k, v, rel_bias):
    b, npair, t, _ = q.shape
    blk = ATT_BLOCK
    cur = pl.BlockSpec((None, None, blk, LANES), lambda bi, p, i: (bi, p, i, 0))
    prev = pl.BlockSpec((None, None, blk, LANES), lambda bi, p, i: (bi, p, jnp.maximum(i - 1, 0), 0))
    n_tiles = len(DILATIONS) * npair
    stats = ([pltpu.VMEM((blk, LANES), F32)] * 6
             + [pltpu.VMEM((DILATIONS[-1] * ATT_PITCH, LANES), F32)] * 3)
    return pl.pallas_call(
        _attn_prompt_kernel,
        grid=(b, npair, t // blk),
        in_specs=[pl.BlockSpec(memory_space=pltpu.SMEM),
                  pl.BlockSpec((len(DILATIONS), TAPS, 2 * TAPS), lambda bi, p, i: (0, 0, 0)),
                  cur, cur, prev, cur, prev],
        out_specs=cur,
        out_shape=jax.ShapeDtypeStruct(q.shape, F32),
        scratch_shapes=[pltpu.VMEM((n_tiles, 2 * TAPS, 2 * TAPS), F32),
                        pltpu.VMEM((2, 2 * TAPS, 2 * TAPS), F32),
                        pltpu.VMEM((2 * blk, LANES), F32),
                        pltpu.VMEM((2 * blk, LANES), F32)] + stats,
        compiler_params=_cparams(3),
        name="attn_prompt",
    )(rel_bias, _prompt_bias_ids(), q, k, k, v, v)


DEC_T = 4
DEC_NEW_PAD = 128
DEC_KEYS = MAX_WINDOW + DEC_NEW_PAD


def _decode_tables():
    qi = jnp.arange(DEC_T, dtype=jnp.int32)[:, None]
    rho = jnp.arange(MAX_WINDOW, dtype=jnp.int32)[None, :]
    dist_b = MAX_WINDOW + qi - rho
    mult_b = jnp.zeros_like(dist_b)
    for d in DILATIONS:
        mult_b = mult_b + (((dist_b % d) == 0) & (dist_b <= d * TAPS)).astype(jnp.int32)
    jj = jnp.arange(DEC_NEW_PAD, dtype=jnp.int32)[None, :]
    dist_n = qi - jj
    mult_n = jnp.where(dist_n == 0, len(DILATIONS), jnp.where((dist_n > 0) & (jj < DEC_T), 1, 0))
    dist = jnp.concatenate([dist_b, jnp.maximum(dist_n, 0)], axis=1)
    mult = jnp.concatenate([mult_b, mult_n.astype(jnp.int32)], axis=1)
    ids = jnp.where(mult > 0, _t5_bucket(dist), -1).astype(jnp.int32)
    ids = jnp.repeat(ids, N_HEADS, axis=0)
    mult = jnp.repeat(mult, N_HEADS, axis=0).astype(F32)
    return ids, mult


def _attn_decode_kernel(rbt_ref, ids_ref, mult_ref, q_ref, kn_ref, vn_ref, kt_ref, vt_ref, o_ref, bias_scr):
    n_rows = DEC_T * N_HEADS

    @pl.when(pl.program_id(0) == 0)
    def _build_bias():
        ids = ids_ref[...]
        tile = jnp.full((n_rows, DEC_KEYS), NEG_INF, F32)
        for bkt in range(N_BUCKETS):
            tile = jnp.where(ids == bkt, rbt_ref[:, bkt:bkt + 1], tile)
        bias_scr[...] = tile

    lane = lax.broadcasted_iota(jnp.int32, (N_HEADS, MIX_A), 1)
    head = lax.broadcasted_iota(jnp.int32, (N_HEADS, MIX_A), 0)
    hmask = (lane // HEAD_DIM) == head

    q = q_ref[...]
    qm = jnp.where(hmask[None], q[:, None, :], 0.0).reshape(n_rows, MIX_A).astype(BF16)
    zpad = jnp.zeros((DEC_NEW_PAD - DEC_T, MIX_A), F32)
    kn = jnp.concatenate([kn_ref[...], zpad], axis=0).astype(BF16)
    vn = jnp.concatenate([vn_ref[...], zpad], axis=0).astype(BF16)
    kt = kt_ref[...].astype(BF16)
    vt = vt_ref[...].astype(BF16)
    nt = (((1,), (1,)), ((), ()))
    s = jnp.concatenate([jnp.dot(qm, kt, preferred_element_type=F32),
                         lax.dot_general(qm, kn, nt, preferred_element_type=F32)], axis=1)
    s = s + bias_scr[...]
    m = jnp.max(s, axis=1, keepdims=True)
    pe = jnp.exp(s - m) * mult_ref[...]
    l = jnp.sum(pe, axis=1, keepdims=True)
    pb = pe.astype(BF16)
    o = (lax.dot_general(pb[:, :MAX_WINDOW], vt, nt, preferred_element_type=F32)
         + jnp.dot(pb[:, MAX_WINDOW:], vn, preferred_element_type=F32)) / l
    o3 = o.reshape(DEC_T, N_HEADS, MIX_A)
    o_ref[...] = jnp.sum(jnp.where(hmask[None], o3, 0.0), axis=1)


def _attn_decode(q, k_new, v_new, cache_kt, cache_vt, rel_bias):
    bd = q.shape[0]
    ids, mult = _decode_tables()
    rbt = jnp.tile(rel_bias.T, (DEC_T, 1))
    new_spec = pl.BlockSpec((None, DEC_T, MIX_A), lambda b: (b, 0, 0))
    cache_spec = pl.BlockSpec((None, MIX_A, MAX_WINDOW), lambda b: (b, 0, 0))
    n_rows = DEC_T * N_HEADS
    const = lambda shape: pl.BlockSpec(shape, lambda b: (0, 0))
    return pl.pallas_call(
        _attn_decode_kernel,
        grid=(bd,),
        in_specs=[const((n_rows, N_BUCKETS)), const((n_rows, DEC_KEYS)), const((n_rows, DEC_KEYS)),
                  new_spec, new_spec, new_spec, cache_spec, cache_spec],
        out_specs=new_spec,
        out_shape=jax.ShapeDtypeStruct((bd, DEC_T, MIX_A), F32),
        scratch_shapes=[pltpu.VMEM((n_rows, DEC_KEYS), F32)],
        compiler_params=_cparams(1),
        name="attn_decode",
    )(rbt, ids, mult, q, k_new, v_new, cache_kt, cache_vt)


def _ssm_prep_kernel(are_ref, aim_ref, ldt_ref, btr_ref, bti_ref, cre_ref, cim_ref,
                     tit_ref, tinr_ref, tini_ref, toutr_ref, touti_ref, pw_ref):
    hi = lax.Precision.HIGHEST
    nt = (((1,), (1,)), ((), ()))
    a_re, a_im = are_ref[...], aim_ref[...]
    dt = jnp.exp(ldt_ref[...])
    decay = jnp.exp(a_re * dt)
    ab_re = decay * jnp.cos(a_im * dt)
    ab_im = decay * jnp.sin(a_im * dt)
    inv = 1.0 / (a_re * a_re + a_im * a_im)
    coef_re = ((ab_re - 1.0) * a_re + ab_im * a_im) * inv
    coef_im = (ab_im * a_re - (ab_re - 1.0) * a_im) * inv
    bt_re, bt_im = btr_ref[...], bti_ref[...]
    bb_re = coef_re * bt_re - coef_im * bt_im
    bb_im = coef_re * bt_im + coef_im * bt_re
    pw = [(jnp.ones_like(ab_re), jnp.zeros_like(ab_im))]
    for _ in range(CHUNK):
        pr, pi = pw[-1]
        pw.append((pr * ab_re - pi * ab_im, pr * ab_im + pi * ab_re))
    tinr_ref[...] = jnp.concatenate(
        [pw[CHUNK - 1 - ti][0] * bb_re - pw[CHUNK - 1 - ti][1] * bb_im for ti in range(CHUNK)], axis=0).astype(BF16)
    tini_ref[...] = jnp.concatenate(
        [pw[CHUNK - 1 - ti][0] * bb_im + pw[CHUNK - 1 - ti][1] * bb_re for ti in range(CHUNK)], axis=0).astype(BF16)
    c_re, c_im = cre_ref[...], cim_ref[...]
    cp_re = [c_re * pr - c_im * pi for pr, pi in pw]
    cp_im = [c_re * pi + c_im * pr for pr, pi in pw]
    toutr_ref[...] = jnp.concatenate(cp_re[1:], axis=0).astype(BF16)
    touti_ref[...] = (-jnp.concatenate(cp_im[1:], axis=0)).astype(BF16)
    kall = (lax.dot_general(jnp.concatenate(cp_re[:CHUNK], axis=0), bb_re, nt, precision=hi,
                            preferred_element_type=F32)
            - lax.dot_general(jnp.concatenate(cp_im[:CHUNK], axis=0), bb_im, nt, precision=hi,
                              preferred_element_type=F32))
    w = CHUNK * SSM_GROUP
    kb = kall.astype(BF16)
    ci_idx = lax.broadcasted_iota(jnp.int32, (SSM_GROUP, w), 0)
    col_idx = lax.broadcasted_iota(jnp.int32, (SSM_GROUP, w), 1)
    acc = jnp.zeros((w, w), F32)
    for ti in range(CHUNK):
        rows = ti * SSM_GROUP
        shifted = kb if ti == 0 else jnp.concatenate(
            [jnp.zeros((rows, SSM_GROUP), BF16), kb[:w - rows]], axis=0)
        place = jnp.where(col_idx == ci_idx + rows, 1.0, 0.0).astype(BF16)
        acc = acc + jnp.dot(shifted, place, preferred_element_type=F32)
    tit_ref[...] = acc.astype(BF16)
    zero = jnp.zeros_like(ab_re)
    pw_ref[...] = jnp.concatenate([pw[CHUNK][0], pw[CHUNK][1], pw[DEC_T][0], pw[DEC_T][1],
                                   zero, zero, zero, zero], axis=0)


def _ssm_prep(a_re, a_im, log_dt, b_re, b_im, c_re, c_im):
    g, p = a_re.shape
    w = CHUNK * SSM_GROUP
    row = lambda a: a[:, None, :]
    bt_re = jnp.transpose(b_re, (0, 2, 1))
    bt_im = jnp.transpose(b_im, (0, 2, 1))
    ldt = jnp.broadcast_to(log_dt[:, None, None], (g, 1, p))
    gspec = lambda shape: pl.BlockSpec((None,) + shape, lambda gi: (gi, 0, 0))
    return pl.pallas_call(
        _ssm_prep_kernel,
        grid=(g,),
        in_specs=[gspec((1, p)), gspec((1, p)), gspec((1, p)), gspec((SSM_GROUP, p)), gspec((SSM_GROUP, p)),
                  gspec((SSM_GROUP, p)), gspec((SSM_GROUP, p))],
        out_specs=[gspec((w, w)), gspec((w, p)), gspec((w, p)), gspec((w, p)), gspec((w, p)), gspec((8, p))],
        out_shape=[jax.ShapeDtypeStruct((g, w, w), BF16)] + [jax.ShapeDtypeStruct((g, w, p), BF16)] * 4
                  + [jax.ShapeDtypeStruct((g, 8, p), F32)],
        compiler_params=_cparams(1),
        name="ssm_prep",
    )(row(a_re), row(a_im), ldt, bt_re, bt_im, c_re, c_im)


def _ssm_prompt_kernel(u_ref, tit_ref, tinr_ref, tini_ref, toutr_ref, touti_ref, d_ref, pw_ref,
                       y_ref, sre_ref, sim_ref, pr_scr, pi_scr, sr_scr, si_scr):
    ngrp, nc = u_ref.shape[0], u_ref.shape[1]
    nt = (((1,), (1,)), ((), ()))
    for j in range(ngrp):
        ub = u_ref[j].astype(BF16)
        pr_scr[j] = jnp.dot(ub, tinr_ref[j], preferred_element_type=F32)
        pi_scr[j] = jnp.dot(ub, tini_ref[j], preferred_element_type=F32)
    ar = [pw_ref[j, 0:1, :] for j in range(ngrp)]
    ai = [pw_ref[j, 1:2, :] for j in range(ngrp)]

    def step(c, carry):
        out = []
        for j in range(ngrp):
            sr, si = carry[j]
            sr_scr[j, pl.ds(c, 1), :] = sr
            si_scr[j, pl.ds(c, 1), :] = si
            out.append((ar[j] * sr - ai[j] * si + pr_scr[j, pl.ds(c, 1), :],
                        ar[j] * si + ai[j] * sr + pi_scr[j, pl.ds(c, 1), :]))
        return tuple(out)

    zero = jnp.zeros((1, SSM_STATE), F32)
    final = lax.fori_loop(0, nc, step, ((zero, zero),) * ngrp, unroll=8)
    for j in range(ngrp):
        sre_ref[j] = final[j][0]
        sim_ref[j] = final[j][1]
        u = u_ref[j]
        y = lax.dot_general(u.astype(BF16), tit_ref[j], nt, preferred_element_type=F32)
        y = y + lax.dot_general(sr_scr[j].astype(BF16), toutr_ref[j], nt, preferred_element_type=F32)
        y = y + lax.dot_general(si_scr[j].astype(BF16), touti_ref[j], nt, preferred_element_type=F32)
        y_ref[j] = y + d_ref[j] * u


def _ssm_prompt(ug, ops, d_tile):
    b, g, nc, w = ug.shape
    p = SSM_STATE
    ngrp = SSM_GROUPS_PER_STEP
    gspec = lambda shape: pl.BlockSpec((ngrp,) + shape, lambda bi, gi: (gi, 0, 0))
    io = pl.BlockSpec((None, ngrp, nc, w), lambda bi, gi: (bi, gi, 0, 0))
    st = pl.BlockSpec((None, ngrp, 1, p), lambda bi, gi: (bi, gi, 0, 0))
    tit, tinr, tini, toutr, touti, pw = ops
    return pl.pallas_call(
        _ssm_prompt_kernel,
        grid=(b, g // ngrp),
        in_specs=[io, gspec((w, w)), gspec((w, p)), gspec((w, p)), gspec((w, p)), gspec((w, p)),
                  gspec((1, w)), gspec((8, p))],
        out_specs=[io, st, st],
        out_shape=[jax.ShapeDtypeStruct(ug.shape, F32),
                   jax.ShapeDtypeStruct((b, g, 1, p), F32), jax.ShapeDtypeStruct((b, g, 1, p), F32)],
        scratch_shapes=[pltpu.VMEM((ngrp, nc, p), F32)] * 4,
        compiler_params=_cparams(2),
        name="ssm_prompt",
    )(ug, tit, tinr, tini, toutr, touti, d_tile, pw)


def _ssm_decode_kernel(u_ref, s0r_ref, s0i_ref, tit_ref, tinr_ref, tini_ref, toutr_ref, touti_ref, d_ref,
                       pw_ref, y_ref, sre_ref, sim_ref):
    nt = (((1,), (1,)), ((), ()))
    w = DEC_T * SSM_GROUP
    lo = (CHUNK - DEC_T) * SSM_GROUP
    u = u_ref[...]
    ub = u.astype(BF16)
    s0r = s0r_ref[...]
    s0i = s0i_ref[...]
    y = lax.dot_general(ub, tit_ref[0:w, 0:w], nt, preferred_element_type=F32)
    y = y + lax.dot_general(s0r.astype(BF16), toutr_ref[0:w, :], nt, preferred_element_type=F32)
    y = y + lax.dot_general(s0i.astype(BF16), touti_ref[0:w, :], nt, preferred_element_type=F32)
    y_ref[...] = y + d_ref[:, 0:w] * u
    ar = pw_ref[2:3, :]
    ai = pw_ref[3:4, :]
    sre_ref[...] = ar * s0r - ai * s0i + jnp.dot(ub, tinr_ref[lo:, :], preferred_element_type=F32)
    sim_ref[...] = ar * s0i + ai * s0r + jnp.dot(ub, tini_ref[lo:, :], preferred_element_type=F32)


def _ssm_decode(ug, s0_re, s0_im, ops, d_tile):
    g, bd, w = ug.shape
    p = SSM_STATE
    wc = CHUNK * SSM_GROUP
    gspec = lambda shape: pl.BlockSpec((None,) + shape, lambda gi: (gi, 0, 0))
    tit, tinr, tini, toutr, touti, pw = ops
    return pl.pallas_call(
        _ssm_decode_kernel,
        grid=(g,),
        in_specs=[gspec((bd, w)), gspec((bd, p)), gspec((bd, p)), gspec((wc, wc)), gspec((wc, p)),
                  gspec((wc, p)), gspec((wc, p)), gspec((wc, p)), gspec((1, wc)), gspec((8, p))],
        out_specs=[gspec((bd, w)), gspec((bd, p)), gspec((bd, p))],
        out_shape=[jax.ShapeDtypeStruct((g, bd, w), F32), jax.ShapeDtypeStruct((g, bd, p), F32),
                   jax.ShapeDtypeStruct((g, bd, p), F32)],
        compiler_params=_cparams(1),
        name="ssm_decode",
    )(ug, s0_re, s0_im, tit, tinr, tini, toutr, touti, d_tile, pw)


def _gelu_tanh(x):
    return 0.5 * x * (1.0 + jnp.tanh(math.sqrt(2.0 / math.pi) * (x + 0.044715 * (x * x * x))))


def _sigmoid(x):
    return 1.0 / (1.0 + jnp.exp(-x))


def _route(hn, wr_ref):
    n = hn.shape[0]
    h_hi = hn.astype(BF16)
    h_lo = (hn - h_hi.astype(F32)).astype(BF16)
    prod = jnp.dot(jnp.concatenate([h_hi, h_lo], axis=0), wr_ref[...], preferred_element_type=F32)
    logits = (prod[:n, :ROUTER_LANES] + prod[:n, ROUTER_LANES:]
              + prod[n:, :ROUTER_LANES] + prod[n:, ROUTER_LANES:])
    lidx = lax.broadcasted_iota(jnp.int32, (n, ROUTER_LANES), 1)
    is_e = lidx < N_EXPERTS
    is_g = jnp.logical_and(lidx >= N_EXPERTS, lidx < N_EXPERTS + N_EXPERT_GROUPS)
    gmax = jnp.max(jnp.where(is_g, logits, -jnp.inf), axis=1, keepdims=True)
    g_prob = 1.0 / jnp.sum(jnp.where(is_g, jnp.exp(logits - gmax), 0.0), axis=1, keepdims=True)
    g_sel = jnp.min(jnp.where(jnp.logical_and(is_g, logits == gmax), lidx - N_EXPERTS, N_EXPERT_GROUPS),
                    axis=1, keepdims=True)
    in_grp = jnp.logical_and(is_e, (lidx // EXPERTS_PER_GROUP) == g_sel)
    l1 = jnp.max(jnp.where(in_grp, logits, -jnp.inf), axis=1, keepdims=True)
    i1 = jnp.min(jnp.where(jnp.logical_and(in_grp, logits == l1), lidx, ROUTER_LANES), axis=1, keepdims=True)
    rest = jnp.logical_and(in_grp, lidx != i1)
    l2 = jnp.max(jnp.where(rest, logits, -jnp.inf), axis=1, keepdims=True)
    i2 = jnp.min(jnp.where(jnp.logical_and(rest, logits == l2), lidx, ROUTER_LANES), axis=1, keepdims=True)
    e2 = jnp.exp(l2 - l1)
    w1 = g_prob / (1.0 + e2)
    w2 = g_prob * e2 / (1.0 + e2)
    return jnp.where(lidx == i1, w1, 0.0) + jnp.where(lidx == i2, w2, 0.0)


def _post_body(oa, ys, x, ga_ref, gb_ref, wglu_ref, bglu_ref, wout_ref, gf_ref, wr_ref,
               x1_ref, hn_ref, gates_ref):
    z = _gelu_tanh(ys)
    gate = _sigmoid(jnp.dot(z.astype(BF16), wglu_ref[...], preferred_element_type=F32) + bglu_ref[...])
    ob = z * gate
    mixed = jnp.concatenate([_rms(oa, ga_ref[...]), _rms(ob, gb_ref[...])], axis=1).astype(BF16)
    x1 = x + jnp.dot(mixed, wout_ref[...], preferred_element_type=F32)
    x1_ref[...] = x1
    hn = _rms(x1, gf_ref[...])
    hn_ref[...] = hn.astype(BF16)
    gates_ref[...] = _route(hn, wr_ref)


def _post_pair_kernel(oa_ref, ys_ref, x_ref, *rest):
    oa = jnp.concatenate([oa_ref[j] for j in range(MIX_A // LANES)], axis=1)
    _post_body(oa, ys_ref[...], x_ref[...], *rest)


def _post_tok_kernel(oa_ref, ys_ref, x_ref, *rest):
    _post_body(oa_ref[...], ys_ref[...], x_ref[...], *rest)


def _post_mix(oa, ys, x, w, tm, pair):
    b, t, d = x.shape
    row = lambda width: pl.BlockSpec((None, tm, width), lambda bi, i: (bi, i, 0))
    const = lambda a: pl.BlockSpec(a.shape, lambda bi, i: (0,) * a.ndim)
    oa_spec = (pl.BlockSpec((None, MIX_A // LANES, tm, LANES), lambda bi, i: (bi, 0, i, 0)) if pair
               else row(MIX_A))
    weights = [w['g_out_a'], w['g_out_b'], w['w_glu'], w['b_glu'], w['w_out'], w['g_ffn'], w['w_router']]
    return pl.pallas_call(
        _post_pair_kernel if pair else _post_tok_kernel,
        grid=(b, t // tm),
        in_specs=[oa_spec, row(MIX_B), row(d)] + [const(a) for a in weights],
        out_specs=[row(d), row(d), row(ROUTER_LANES)],
        out_shape=[jax.ShapeDtypeStruct((b, t, d), F32), jax.ShapeDtypeStruct((b, t, d), BF16),
                   jax.ShapeDtypeStruct((b, t, ROUTER_LANES), F32)],
        compiler_params=_cparams(2),
        name="post_mix_prompt" if pair else "post_mix_sample",
    )(oa, ys, x, *weights)


def _moe_kernel(hn_ref, gates_ref, x1_ref, wg_ref, wu_ref, wd_ref, gfin_ref, y_ref, acc_scr):
    e = pl.program_id(1)

    @pl.when(e == 0)
    def _init():
        acc_scr[...] = jnp.zeros_like(acc_scr)

    hn = hn_ref[...]
    a = jnp.dot(hn, wg_ref[...].astype(BF16), preferred_element_type=F32)
    u = jnp.dot(hn, wu_ref[...].astype(BF16), preferred_element_type=F32)
    eidx = lax.broadcasted_iota(jnp.int32, gates_ref.shape, 1)
    gate = jnp.sum(jnp.where(eidx == e, gates_ref[...], 0.0), axis=1, keepdims=True)
    act = (a * _sigmoid(a)) * u * gate
    acc_scr[...] += jnp.dot(act.astype(BF16), wd_ref[...].astype(BF16), preferred_element_type=F32)

    @pl.when(e == pl.num_programs(1) - 1)
    def _fin():
        y_ref[...] = _rms(x1_ref[...] + acc_scr[...], gfin_ref[...])


def _moe(hn, gates, x1, wg, wu, wd, g_final, tm):
    n, d = x1.shape
    row = lambda width: pl.BlockSpec((tm, width), lambda i, e: (i, 0))
    return pl.pallas_call(
        _moe_kernel,
        grid=(n // tm, N_EXPERTS),
        in_specs=[row(d), row(ROUTER_LANES), row(d),
                  pl.BlockSpec((None, d, D_EXPERT), lambda i, e: (e, 0, 0)),
                  pl.BlockSpec((None, d, D_EXPERT), lambda i, e: (e, 0, 0)),
                  pl.BlockSpec((None, D_EXPERT, d), lambda i, e: (e, 0, 0)),
                  pl.BlockSpec((1, d), lambda i, e: (0, 0))],
        out_specs=row(d),
        out_shape=jax.ShapeDtypeStruct((n, d), F32),
        scratch_shapes=[pltpu.VMEM((tm, d), F32)],
        compiler_params=_cparams(2),
        name="moe",
    )(hn, gates, x1, wg, wu, wd, g_final)


def kernel(x_prompt, x_sample, cache_k, cache_v, state_ssm_re, state_ssm_im, rel_bias, g_mix, w_in, g_out_a, g_out_b, w_out, ssm_a_re, ssm_a_im, ssm_log_dt, ssm_b_re, ssm_b_im, ssm_c_re, ssm_c_im, ssm_d, w_glu, b_glu, g_ffn, w_router_group, w_router_expert, w_expert_gate, w_expert_up, w_expert_down, g_final):
    depth = g_mix.shape[0]
    assert depth == 1, "kernel is written for the single-layer configuration of the problem"
    l = 0
    b, t, d = x_prompt.shape
    bd, ts, _ = x_sample.shape
    assert ts == DEC_T and t % ATT_BLOCK == 0 and cache_k.shape[2] == MAX_WINDOW
    keep = min(MAX_WINDOW, t)

    w_in_b = w_in[l].astype(BF16)
    gm = g_mix[l][None, :]
    w_router = jnp.concatenate(
        [jnp.transpose(w_router_expert[l], (1, 0, 2)).reshape(d, N_EXPERTS), w_router_group[l],
         jnp.zeros((d, ROUTER_LANES - N_EXPERTS - N_EXPERT_GROUPS), F32)], axis=1)
    w_router_hi = w_router.astype(BF16)
    w_router_lo = (w_router - w_router_hi.astype(F32)).astype(BF16)
    post_w = dict(
        g_out_a=g_out_a[l][None, :], g_out_b=g_out_b[l][None, :], w_glu=w_glu[l].astype(BF16),
        b_glu=b_glu[l][None, :], w_out=w_out[l].astype(BF16), g_ffn=g_ffn[l][None, :],
        w_router=jnp.concatenate([w_router_hi, w_router_lo], axis=1))
    wg, wu, wd = w_expert_gate[l], w_expert_up[l], w_expert_down[l]
    gfin = g_final[None, :]
    ssm_ops = _ssm_prep(ssm_a_re[l], ssm_a_im[l], ssm_log_dt[l], ssm_b_re[l], ssm_b_im[l],
                        ssm_c_re[l], ssm_c_im[l])
    d_tile = jnp.tile(ssm_d[l].reshape(N_GROUPS, 1, SSM_GROUP), (1, 1, CHUNK))

    qp, kp, vp, kt_win, vt_win, u_tok = _inproj_pair(x_prompt, gm, w_in_b, tm=512, keep=keep)
    oa_p = _attn_prompt(qp, kp, vp, rel_bias)
    ug = jnp.transpose(u_tok.reshape(b, t, N_GROUPS, SSM_GROUP), (0, 2, 1, 3))
    ug = ug.reshape(b, N_GROUPS, t // CHUNK, CHUNK * SSM_GROUP)
    yg, rp, ip = _ssm_prompt(ug, ssm_ops, d_tile)
    ys_p = jnp.transpose(yg.reshape(b, N_GROUPS, t, SSM_GROUP), (0, 2, 1, 3)).reshape(b, t, MIX_B)
    x1_p, hn_p, gates_p = _post_mix(oa_p, ys_p, x_prompt, post_w, tm=512, pair=True)
    y_p = _moe(hn_p.reshape(b * t, d), gates_p.reshape(b * t, ROUTER_LANES), x1_p.reshape(b * t, d),
               wg, wu, wd, gfin, tm=1024)

    n_s = bd * ts
    qs, ks, vs, us = _inproj_tok(x_sample.reshape(n_s, d), gm, w_in_b)
    ckt = jnp.transpose(cache_k[l], (0, 2, 3, 1)).reshape(bd, MIX_A, MAX_WINDOW)
    cvt = jnp.transpose(cache_v[l], (0, 2, 3, 1)).reshape(bd, MIX_A, MAX_WINDOW)
    oa_s = _attn_decode(qs.reshape(bd, ts, MIX_A), ks.reshape(bd, ts, MIX_A), vs.reshape(bd, ts, MIX_A),
                        ckt, cvt, rel_bias)
    ugs = jnp.transpose(us.reshape(bd, ts, N_GROUPS, SSM_GROUP), (2, 0, 1, 3)).reshape(N_GROUPS, bd, ts * SSM_GROUP)
    s0r = jnp.transpose(state_ssm_re[l], (1, 0, 2))
    s0i = jnp.transpose(state_ssm_im[l], (1, 0, 2))
    ygs, rs, is_ = _ssm_decode(ugs, s0r, s0i, ssm_ops, d_tile)
    ys_s = jnp.transpose(ygs.reshape(N_GROUPS, bd, ts, SSM_GROUP), (1, 2, 0, 3)).reshape(n_s, MIX_B)
    x1_s, hn_s, gates_s = _post_mix(oa_s.reshape(1, n_s, MIX_A), ys_s.reshape(1, n_s, MIX_B),
                                    x_sample.reshape(1, n_s, d), post_w, tm=n_s, pair=False)
    y_s = _moe(hn_s.reshape(n_s, d), gates_s.reshape(n_s, ROUTER_LANES), x1_s.reshape(n_s, d),
               wg, wu, wd, gfin, tm=n_s)

    y_prompt = y_p.reshape(b, t, d)
    y_sample = y_s.reshape(bd, ts, d)
    k_win = jnp.transpose(kt_win.reshape(b, N_HEADS, HEAD_DIM, keep), (0, 3, 1, 2))[None]
    v_win = jnp.transpose(vt_win.reshape(b, N_HEADS, HEAD_DIM, keep), (0, 3, 1, 2))[None]
    k_new = ks.reshape(1, bd, ts, N_HEADS, HEAD_DIM)
    v_new = vs.reshape(1, bd, ts, N_HEADS, HEAD_DIM)
    return (y_prompt, y_sample, k_win, v_win, k_new, v_new,
            rp.reshape(1, b, N_GROUPS, SSM_STATE), ip.reshape(1, b, N_GROUPS, SSM_STATE),
            jnp.transpose(rs, (1, 0, 2))[None], jnp.transpose(is_, (1, 0, 2))[None])
```

```python
import functools
import math

import jax
import jax.numpy as jnp
import numpy as np
from jax import lax
from jax.experimental import pallas as pl
from jax.experimental.pallas import tpu as pltpu

F32 = jnp.float32
BF16 = jnp.bfloat16

D_MODEL = 1024
HEAD_DIM = 64
MIX_A = 512
N_HEADS = 8
MIX_B = 512
SSM_GROUP = 16
N_GROUPS = 32
SSM_STATE = 64
PROJ_COLS = 3 * MIX_A + MIX_B
DILATIONS = (1, 4, 16)
TAPS = 128
MAX_WINDOW = 2048
N_BUCKETS = 32
N_EXPERT_GROUPS = 4
EXPERTS_PER_GROUP = 8
N_EXPERTS = 32
D_EXPERT = 256
EPS = 1e-6
NEG_INF = -1e30
SCALE = HEAD_DIM ** -0.5

LANES = 128
ROUTER_LANES = 128
CHUNK = 16
SSM_GROUPS_PER_STEP = 4
ATT_BLOCK = 2048
ATT_UNROLL = 8
ATT_PITCH = 136
VMEM_LIMIT = 56 * 1024 * 1024


def _cparams(n_axes):
    return pltpu.CompilerParams(dimension_semantics=("arbitrary",) * n_axes,
                                vmem_limit_bytes=VMEM_LIMIT)


def _t5_bucket(distance):
    max_exact = N_BUCKETS // 2
    nf = jnp.maximum(distance, 1).astype(F32)
    large = max_exact + (jnp.log(nf / max_exact) / math.log(MAX_WINDOW / max_exact)
                         * (N_BUCKETS - max_exact)).astype(jnp.int32)
    large = jnp.minimum(large, N_BUCKETS - 1)
    return jnp.where(distance < max_exact, distance, large)


def _rms(x, g):
    return x * lax.rsqrt(jnp.mean(x * x, axis=-1, keepdims=True) + EPS) * g


def _inproj_pair_kernel(first_win_tile, x_ref, g_ref, w_ref, q_ref, k_ref, v_ref, kt_ref, vt_ref, u_ref):
    h = _rms(x_ref[...], g_ref[...])
    p = jnp.dot(h.astype(BF16), w_ref[...], preferred_element_type=F32)
    for j in range(MIX_A // LANES):
        q_ref[j] = p[:, LANES * j:LANES * (j + 1)] * SCALE
        k_ref[j] = p[:, MIX_A + LANES * j:MIX_A + LANES * (j + 1)]
        v_ref[j] = p[:, 2 * MIX_A + LANES * j:2 * MIX_A + LANES * (j + 1)]
    u_ref[...] = p[:, 3 * MIX_A:]

    @pl.when(pl.program_id(1) >= first_win_tile)
    def _window():
        kt_ref[...] = p[:, MIX_A:2 * MIX_A].T
        vt_ref[...] = p[:, 2 * MIX_A:3 * MIX_A].T


def _inproj_pair(x, g, w_bf16, tm, keep):
    b, t, d = x.shape
    npair = MIX_A // LANES
    first = (t - keep) // tm
    pair = jax.ShapeDtypeStruct((b, npair, t, LANES), F32)
    win = jax.ShapeDtypeStruct((b, MIX_A, keep), F32)
    tok = jax.ShapeDtypeStruct((b, t, MIX_A), F32)
    pair_spec = pl.BlockSpec((None, npair, tm, LANES), lambda bi, i: (bi, 0, i, 0))
    win_spec = pl.BlockSpec((None, MIX_A, tm), lambda bi, i: (bi, 0, jnp.maximum(i - first, 0)))
    tok_spec = pl.BlockSpec((None, tm, MIX_A), lambda bi, i: (bi, i, 0))
    return pl.pallas_call(
        functools.partial(_inproj_pair_kernel, first),
        grid=(b, t // tm),
        in_specs=[pl.BlockSpec((None, tm, d), lambda bi, i: (bi, i, 0)),
                  pl.BlockSpec((1, d), lambda bi, i: (0, 0)),
                  pl.BlockSpec((d, PROJ_COLS), lambda bi, i: (0, 0))],
        out_specs=[pair_spec, pair_spec, pair_spec, win_spec, win_spec, tok_spec],
        out_shape=[pair, pair, pair, win, win, tok],
        compiler_params=_cparams(2),
        name="inproj_prompt",
    )(x, g, w_bf16)


def _inproj_tok_kernel(x_ref, g_ref, w_ref, q_ref, k_ref, v_ref, u_ref):
    h = _rms(x_ref[...], g_ref[...])
    p = jnp.dot(h.astype(BF16), w_ref[...], preferred_element_type=F32)
    q_ref[...] = p[:, :MIX_A] * SCALE
    k_ref[...] = p[:, MIX_A:2 * MIX_A]
    v_ref[...] = p[:, 2 * MIX_A:3 * MIX_A]
    u_ref[...] = p[:, 3 * MIX_A:]


def _inproj_tok(x, g, w_bf16):
    n, d = x.shape
    out = jax.ShapeDtypeStruct((n, MIX_A), F32)
    return pl.pallas_call(
        _inproj_tok_kernel,
        out_shape=[out, out, out, out],
        compiler_params=pltpu.CompilerParams(vmem_limit_bytes=VMEM_LIMIT),
        name="inproj_sample",
    )(x, g, w_bf16)


def _prompt_bias_ids():
    r = jnp.arange(TAPS, dtype=jnp.int32)[:, None]
    kpos = jnp.arange(2 * TAPS, dtype=jnp.int32)[None, :] - TAPS
    rel = r - kpos
    valid = (rel >= 0) & (rel <= TAPS)
    ids = [jnp.where(valid, _t5_bucket(d * rel), -1) for d in DILATIONS]
    return jnp.stack(ids).astype(jnp.int32)


def _attn_prompt_kernel(rb_ref, ids_ref, q_ref, kc_ref, kp_ref, vc_ref, vp_ref, o_ref,
                        bias_scr, fm_scr, kcat, vcat, m0_scr, l0_scr, o0_scr, m1_scr, l1_scr, o1_scr,
                        m2_scr, l2_scr, o2_scr):
    bi = pl.program_id(0)
    p = pl.program_id(1)
    i = pl.program_id(2)
    blk = ATT_BLOCK
    npair = N_HEADS // 2

    @pl.when((bi == 0) & (p == 0) & (i == 0))
    def _build_bias():
        col = lax.broadcasted_iota(jnp.int32, (2 * TAPS, 2 * TAPS), 1)
        fm_scr[0] = jnp.zeros((2 * TAPS, 2 * TAPS), F32)
        fm_scr[1] = jnp.where(col < TAPS, NEG_INF, 0.0).astype(F32)
        for c in range(len(DILATIONS)):
            ids = ids_ref[c]

            def head_body(h, carry, ids=ids, c=c):
                tile = jnp.full((TAPS, 2 * TAPS), NEG_INF, F32)
                for bkt in range(N_BUCKETS):
                    tile = jnp.where(ids == bkt, rb_ref[bkt, h], tile)
                row0 = pl.multiple_of((h % 2) * TAPS, TAPS)
                bias_scr[c * npair + h // 2, pl.ds(row0, TAPS), :] = tile
                return carry

            lax.fori_loop(0, N_HEADS, head_body, 0)

    kcat[pl.ds(0, blk), :] = kp_ref[...]
    kcat[pl.ds(blk, blk), :] = kc_ref[...]
    vcat[pl.ds(0, blk), :] = vp_ref[...]
    vcat[pl.ds(blk, blk), :] = vc_ref[...]

    lane = lax.broadcasted_iota(jnp.int32, (1, LANES), 1)
    low = lane < HEAD_DIM
    nt = (((1,), (1,)), ((), ()))
    ones = jnp.ones((2 * TAPS, LANES), BF16)
    stats = ((m0_scr, l0_scr, o0_scr), (m1_scr, l1_scr, o1_scr), (m2_scr, l2_scr, o2_scr))

    def rows(ref, start, d):
        if d == 1:
            return ref[pl.ds(start, TAPS), :]
        return ref[pl.ds(start, TAPS, stride=d), :]

    def sub_block(c, d, qs, first, dst_rows):
        q = rows(q_ref, qs, d)
        q2 = jnp.concatenate([jnp.where(low, q, 0.0), jnp.where(low, 0.0, q)], axis=0).astype(BF16)
        k = jnp.concatenate([rows(kcat, blk + qs - d * TAPS, d), rows(kcat, blk + qs, d)],
                            axis=0).astype(BF16)
        v = jnp.concatenate([rows(vcat, blk + qs - d * TAPS, d), rows(vcat, blk + qs, d)],
                            axis=0).astype(BF16)
        s_ = lax.dot_general(q2, k, nt, preferred_element_type=F32) + bias_scr[c * npair + p]
        if first is not None:
            s_ = s_ + fm_scr[first]
        m = jnp.max(s_, axis=1, keepdims=True)
        pe = jnp.exp(s_ - m).astype(BF16)
        ol = jnp.dot(pe, jnp.concatenate([v, ones], axis=1), preferred_element_type=F32)
        o, l = ol[:, :LANES], ol[:, LANES:]
        m_ref, l_ref, a_ref = stats[c]
        m_ref[dst_rows, :] = jnp.where(low, m[:TAPS], m[TAPS:])
        l_ref[dst_rows, :] = jnp.where(low, l[:TAPS], l[TAPS:])
        a_ref[dst_rows, :] = jnp.where(low, o[:TAPS], o[TAPS:])

    at_start = jnp.where(i == 0, 1, 0)
    n_sub = blk // TAPS
    for c, d in enumerate(DILATIONS):
        n_grp = blk // (d * TAPS)

        def group_body(s8, carry, c=c, d=d, n_grp=n_grp):
            for u in range(ATT_UNROLL):
                s = s8 * ATT_UNROLL + u
                if n_grp >= ATT_UNROLL:
                    qs = pl.multiple_of(s * TAPS, TAPS)
                    first = jnp.where(s8 == 0, at_start, 0) if u == 0 else None
                    dst_rows = pl.ds(qs, TAPS)
                else:
                    r = s8 * (ATT_UNROLL // n_grp) + u // n_grp
                    g = u % n_grp
                    qs = r + d * TAPS * g
                    first = at_start if g == 0 else None
                    if n_grp > 1:
                        dst_rows = pl.ds(qs, TAPS, stride=d)
                    else:
                        dst_rows = pl.ds(pl.multiple_of(r * ATT_PITCH, 8), TAPS)
                sub_block(c, d, qs, first, dst_rows)
            return carry

        lax.fori_loop(0, n_sub // ATT_UNROLL, group_body, 0)

    d_last = DILATIONS[-1]

    def merge_body(j, carry):
        tok = pl.ds(pl.multiple_of(j * d_last, d_last), d_last)
        dil = pl.ds(j, d_last, stride=ATT_PITCH)
        m0, m1, m2 = m0_scr[tok, :], m1_scr[tok, :], m2_scr[dil, :]
        m = jnp.maximum(jnp.maximum(m0, m1), m2)
        a0, a1, a2 = jnp.exp(m0 - m), jnp.exp(m1 - m), jnp.exp(m2 - m)
        num = o0_scr[tok, :] * a0 + o1_scr[tok, :] * a1 + o2_scr[dil, :] * a2
        den = l0_scr[tok, :] * a0 + l1_scr[tok, :] * a1 + l2_scr[dil, :] * a2
        o_ref[tok, :] = num / den
        return carry

    lax.fori_loop(0, blk // d_last, merge_body, 0, unroll=4)


def _attn_prompt(q, k, v, rel_bias):
    b, npair, t, _ = q.shape
    blk = ATT_BLOCK
    cur = pl.BlockSpec((None, None, blk, LANES), lambda bi, p, i: (bi, p, i, 0))
    prev = pl.BlockSpec((None, None, blk, LANES), lambda bi, p, i: (bi, p, jnp.maximum(i - 1, 0), 0))
    n_tiles = len(DILATIONS) * npair
    stats = ([pltpu.VMEM((blk, LANES), F32)] * 6
             + [pltpu.VMEM((DILATIONS[-1] * ATT_PITCH, LANES), F32)] * 3)
    return pl.pallas_call(
        _attn_prompt_kernel,
        grid=(b, npair, t // blk),
        in_specs=[pl.BlockSpec(memory_space=pltpu.SMEM),
                  pl.BlockSpec((len(DILATIONS), TAPS, 2 * TAPS), lambda bi, p, i: (0, 0, 0)),
                  cur, cur, prev, cur, prev],
        out_specs=cur,
        out_shape=jax.ShapeDtypeStruct(q.shape, F32),
        scratch_shapes=[pltpu.VMEM((n_tiles, 2 * TAPS, 2 * TAPS), F32),
                        pltpu.VMEM((2, 2 * TAPS, 2 * TAPS), F32),
                        pltpu.VMEM((2 * blk, LANES), F32),
                        pltpu.VMEM((2 * blk, LANES), F32)] + stats,
        compiler_params=_cparams(3),
        name="attn_prompt",
    )(rel_bias, _prompt_bias_ids(), q, k, k, v, v)


DEC_T = 4
DEC_NEW_PAD = 128
DEC_KEYS = MAX_WINDOW + DEC_NEW_PAD


def _decode_tables():
    qi = jnp.arange(DEC_T, dtype=jnp.int32)[:, None]
    rho = jnp.arange(MAX_WINDOW, dtype=jnp.int32)[None, :]
    dist_b = MAX_WINDOW + qi - rho
    mult_b = jnp.zeros_like(dist_b)
    for d in DILATIONS:
        mult_b = mult_b + (((dist_b % d) == 0) & (dist_b <= d * TAPS)).astype(jnp.int32)
    jj = jnp.arange(DEC_NEW_PAD, dtype=jnp.int32)[None, :]
    dist_n = qi - jj
    mult_n = jnp.where(dist_n == 0, len(DILATIONS), jnp.where((dist_n > 0) & (jj < DEC_T), 1, 0))
    dist = jnp.concatenate([dist_b, jnp.maximum(dist_n, 0)], axis=1)
    mult = jnp.concatenate([mult_b, mult_n.astype(jnp.int32)], axis=1)
    ids = jnp.where(mult > 0, _t5_bucket(dist), -1).astype(jnp.int32)
    ids = jnp.repeat(ids, N_HEADS, axis=0)
    mult = jnp.repeat(mult, N_HEADS, axis=0).astype(F32)
    return ids, mult


def _attn_decode_kernel(rbt_ref, ids_ref, mult_ref, q_ref, kn_ref, vn_ref, kt_ref, vt_ref, o_ref, bias_scr):
    n_rows = DEC_T * N_HEADS

    @pl.when(pl.program_id(0) == 0)
    def _build_bias():
        ids = ids_ref[...]
        tile = jnp.full((n_rows, DEC_KEYS), NEG_INF, F32)
        for bkt in range(N_BUCKETS):
            tile = jnp.where(ids == bkt, rbt_ref[:, bkt:bkt + 1], tile)
        bias_scr[...] = tile

    lane = lax.broadcasted_iota(jnp.int32, (N_HEADS, MIX_A), 1)
    head = lax.broadcasted_iota(jnp.int32, (N_HEADS, MIX_A), 0)
    hmask = (lane // HEAD_DIM) == head

    q = q_ref[...]
    qm = jnp.where(hmask[None], q[:, None, :], 0.0).reshape(n_rows, MIX_A).astype(BF16)
    zpad = jnp.zeros((DEC_NEW_PAD - DEC_T, MIX_A), F32)
    kn = jnp.concatenate([kn_ref[...], zpad], axis=0).astype(BF16)
    vn = jnp.concatenate([vn_ref[...], zpad], axis=0).astype(BF16)
    kt = kt_ref[...].astype(BF16)
    vt = vt_ref[...].astype(BF16)
    nt = (((1,), (1,)), ((), ()))
    s = jnp.concatenate([jnp.dot(qm, kt, preferred_element_type=F32),
                         lax.dot_general(qm, kn, nt, preferred_element_type=F32)], axis=1)
    s = s + bias_scr[...]
    m = jnp.max(s, axis=1, keepdims=True)
    pe = jnp.exp(s - m) * mult_ref[...]
    l = jnp.sum(pe, axis=1, keepdims=True)
    pb = pe.astype(BF16)
    o = (lax.dot_general(pb[:, :MAX_WINDOW], vt, nt, preferred_element_type=F32)
         + jnp.dot(pb[:, MAX_WINDOW:], vn, preferred_element_type=F32)) / l
    o3 = o.reshape(DEC_T, N_HEADS, MIX_A)
    o_ref[...] = jnp.sum(jnp.where(hmask[None], o3, 0.0), axis=1)


def _attn_decode(q, k_new, v_new, cache_kt, cache_vt, rel_bias):
    bd = q.shape[0]
    ids, mult = _decode_tables()
    rbt = jnp.tile(rel_bias.T, (DEC_T, 1))
    new_spec = pl.BlockSpec((None, DEC_T, MIX_A), lambda b: (b, 0, 0))
    cache_spec = pl.BlockSpec((None, MIX_A, MAX_WINDOW), lambda b: (b, 0, 0))
    n_rows = DEC_T * N_HEADS
    const = lambda shape: pl.BlockSpec(shape, lambda b: (0, 0))
    return pl.pallas_call(
        _attn_decode_kernel,
        grid=(bd,),
        in_specs=[const((n_rows, N_BUCKETS)), const((n_rows, DEC_KEYS)), const((n_rows, DEC_KEYS)),
                  new_spec, new_spec, new_spec, cache_spec, cache_spec],
        out_specs=new_spec,
        out_shape=jax.ShapeDtypeStruct((bd, DEC_T, MIX_A), F32),
        scratch_shapes=[pltpu.VMEM((n_rows, DEC_KEYS), F32)],
        compiler_params=_cparams(1),
        name="attn_decode",
    )(rbt, ids, mult, q, k_new, v_new, cache_kt, cache_vt)


def _ssm_prep_kernel(are_ref, aim_ref, ldt_ref, btr_ref, bti_ref, cre_ref, cim_ref,
                     tit_ref, tinr_ref, tini_ref, toutr_ref, touti_ref, pw_ref):
    hi = lax.Precision.HIGHEST
    nt = (((1,), (1,)), ((), ()))
    a_re, a_im = are_ref[...], aim_ref[...]
    dt = jnp.exp(ldt_ref[...])
    decay = jnp.exp(a_re * dt)
    ab_re = decay * jnp.cos(a_im * dt)
    ab_im = decay * jnp.sin(a_im * dt)
    inv = 1.0 / (a_re * a_re + a_im * a_im)
    coef_re = ((ab_re - 1.0) * a_re + ab_im * a_im) * inv
    coef_im = (ab_im * a_re - (ab_re - 1.0) * a_im) * inv
    bt_re, bt_im = btr_ref[...], bti_ref[...]
    bb_re = coef_re * bt_re - coef_im * bt_im
    bb_im = coef_re * bt_im + coef_im * bt_re
    pw = [(jnp.ones_like(ab_re), jnp.zeros_like(ab_im))]
    for _ in range(CHUNK):
        pr, pi = pw[-1]
        pw.append((pr * ab_re - pi * ab_im, pr * ab_im + pi * ab_re))
    tinr_ref[...] = jnp.concatenate(
        [pw[CHUNK - 1 - ti][0] * bb_re - pw[CHUNK - 1 - ti][1] * bb_im for ti in range(CHUNK)], axis=0).astype(BF16)
    tini_ref[...] = jnp.concatenate(
        [pw[CHUNK - 1 - ti][0] * bb_im + pw[CHUNK - 1 - ti][1] * bb_re for ti in range(CHUNK)], axis=0).astype(BF16)
    c_re, c_im = cre_ref[...], cim_ref[...]
    cp_re = [c_re * pr - c_im * pi for pr, pi in pw]
    cp_im = [c_re * pi + c_im * pr for pr, pi in pw]
    toutr_ref[...] = jnp.concatenate(cp_re[1:], axis=0).astype(BF16)
    touti_ref[...] = (-jnp.concatenate(cp_im[1:], axis=0)).astype(BF16)
    kall = (lax.dot_general(jnp.concatenate(cp_re[:CHUNK], axis=0), bb_re, nt, precision=hi,
                            preferred_element_type=F32)
            - lax.dot_general(jnp.concatenate(cp_im[:CHUNK], axis=0), bb_im, nt, precision=hi,
                              preferred_element_type=F32))
    w = CHUNK * SSM_GROUP
    kb = kall.astype(BF16)
    ci_idx = lax.broadcasted_iota(jnp.int32, (SSM_GROUP, w), 0)
    col_idx = lax.broadcasted_iota(jnp.int32, (SSM_GROUP, w), 1)
    acc = jnp.zeros((w, w), F32)
    for ti in range(CHUNK):
        rows = ti * SSM_GROUP
        shifted = kb if ti == 0 else jnp.concatenate(
            [jnp.zeros((rows, SSM_GROUP), BF16), kb[:w - rows]], axis=0)
        place = jnp.where(col_idx == ci_idx + rows, 1.0, 0.0).astype(BF16)
        acc = acc + jnp.dot(shifted, place, preferred_element_type=F32)
    tit_ref[...] = acc.astype(BF16)
    zero = jnp.zeros_like(ab_re)
    pw_ref[...] = jnp.concatenate([pw[CHUNK][0], pw[CHUNK][1], pw[DEC_T][0], pw[DEC_T][1],
                                   zero, zero, zero, zero], axis=0)


def _ssm_prep(a_re, a_im, log_dt, b_re, b_im, c_re, c_im):
    g, p = a_re.shape
    w = CHUNK * SSM_GROUP
    row = lambda a: a[:, None, :]
    bt_re = jnp.transpose(b_re, (0, 2, 1))
    bt_im = jnp.transpose(b_im, (0, 2, 1))
    ldt = jnp.broadcast_to(log_dt[:, None, None], (g, 1, p))
    gspec = lambda shape: pl.BlockSpec((None,) + shape, lambda gi: (gi, 0, 0))
    return pl.pallas_call(
        _ssm_prep_kernel,
        grid=(g,),
        in_specs=[gspec((1, p)), gspec((1, p)), gspec((1, p)), gspec((SSM_GROUP, p)), gspec((SSM_GROUP, p)),
                  gspec((SSM_GROUP, p)), gspec((SSM_GROUP, p))],
        out_specs=[gspec((w, w)), gspec((w, p)), gspec((w, p)), gspec((w, p)), gspec((w, p)), gspec((8, p))],
        out_shape=[jax.ShapeDtypeStruct((g, w, w), BF16)] + [jax.ShapeDtypeStruct((g, w, p), BF16)] * 4
                  + [jax.ShapeDtypeStruct((g, 8, p), F32)],
        compiler_params=_cparams(1),
        name="ssm_prep",
    )(row(a_re), row(a_im), ldt, bt_re, bt_im, c_re, c_im)


def _ssm_prompt_kernel(u_ref, tit_ref, tinr_ref, tini_ref, toutr_ref, touti_ref, d_ref, pw_ref,
                       y_ref, sre_ref, sim_ref, pr_scr, pi_scr, sr_scr, si_scr):
    ngrp, nc = u_ref.shape[0], u_ref.shape[1]
    nt = (((1,), (1,)), ((), ()))
    for j in range(ngrp):
        ub = u_ref[j].astype(BF16)
        pr_scr[j] = jnp.dot(ub, tinr_ref[j], preferred_element_type=F32)
        pi_scr[j] = jnp.dot(ub, tini_ref[j], preferred_element_type=F32)
    ar = [pw_ref[j, 0:1, :] for j in range(ngrp)]
    ai = [pw_ref[j, 1:2, :] for j in range(ngrp)]

    def step(c, carry):
        out = []
        for j in range(ngrp):
            sr, si = carry[j]
            sr_scr[j, pl.ds(c, 1), :] = sr
            si_scr[j, pl.ds(c, 1), :] = si
            out.append((ar[j] * sr - ai[j] * si + pr_scr[j, pl.ds(c, 1), :],
                        ar[j] * si + ai[j] * sr + pi_scr[j, pl.ds(c, 1), :]))
        return tuple(out)

    zero = jnp.zeros((1, SSM_STATE), F32)
    final = lax.fori_loop(0, nc, step, ((zero, zero),) * ngrp, unroll=8)
    for j in range(ngrp):
        sre_ref[j] = final[j][0]
        sim_ref[j] = final[j][1]
        u = u_ref[j]
        y = lax.dot_general(u.astype(BF16), tit_ref[j], nt, preferred_element_type=F32)
        y = y + lax.dot_general(sr_scr[j].astype(BF16), toutr_ref[j], nt, preferred_element_type=F32)
        y = y + lax.dot_general(si_scr[j].astype(BF16), touti_ref[j], nt, preferred_element_type=F32)
        y_ref[j] = y + d_ref[j] * u


def _ssm_prompt(ug, ops, d_tile):
    b, g, nc, w = ug.shape
    p = SSM_STATE
    ngrp = SSM_GROUPS_PER_STEP
    gspec = lambda shape: pl.BlockSpec((ngrp,) + shape, lambda bi, gi: (gi, 0, 0))
    io = pl.BlockSpec((None, ngrp, nc, w), lambda bi, gi: (bi, gi, 0, 0))
    st = pl.BlockSpec((None, ngrp, 1, p), lambda bi, gi: (bi, gi, 0, 0))
    tit, tinr, tini, toutr, touti, pw = ops
    return pl.pallas_call(
        _ssm_prompt_kernel,
        grid=(b, g // ngrp),
        in_specs=[io, gspec((w, w)), gspec((w, p)), gspec((w, p)), gspec((w, p)), gspec((w, p)),
                  gspec((1, w)), gspec((8, p))],
        out_specs=[io, st, st],
        out_shape=[jax.ShapeDtypeStruct(ug.shape, F32),
                   jax.ShapeDtypeStruct((b, g, 1, p), F32), jax.ShapeDtypeStruct((b, g, 1, p), F32)],
        scratch_shapes=[pltpu.VMEM((ngrp, nc, p), F32)] * 4,
        compiler_params=_cparams(2),
        name="ssm_prompt",
    )(ug, tit, tinr, tini, toutr, touti, d_tile, pw)


def _ssm_decode_kernel(u_ref, s0r_ref, s0i_ref, tit_ref, tinr_ref, tini_ref, toutr_ref, touti_ref, d_ref,
                       pw_ref, y_ref, sre_ref, sim_ref):
    nt = (((1,), (1,)), ((), ()))
    w = DEC_T * SSM_GROUP
    lo = (CHUNK - DEC_T) * SSM_GROUP
    u = u_ref[...]
    ub = u.astype(BF16)
    s0r = s0r_ref[...]
    s0i = s0i_ref[...]
    y = lax.dot_general(ub, tit_ref[0:w, 0:w], nt, preferred_element_type=F32)
    y = y + lax.dot_general(s0r.astype(BF16), toutr_ref[0:w, :], nt, preferred_element_type=F32)
    y = y + lax.dot_general(s0i.astype(BF16), touti_ref[0:w, :], nt, preferred_element_type=F32)
    y_ref[...] = y + d_ref[:, 0:w] * u
    ar = pw_ref[2:3, :]
    ai = pw_ref[3:4, :]
    sre_ref[...] = ar * s0r - ai * s0i + jnp.dot(ub, tinr_ref[lo:, :], preferred_element_type=F32)
    sim_ref[...] = ar * s0i + ai * s0r + jnp.dot(ub, tini_ref[lo:, :], preferred_element_type=F32)


def _ssm_decode(ug, s0_re, s0_im, ops, d_tile):
    g, bd, w = ug.shape
    p = SSM_STATE
    wc = CHUNK * SSM_GROUP
    gspec = lambda shape: pl.BlockSpec((None,) + shape, lambda gi: (gi, 0, 0))
    tit, tinr, tini, toutr, touti, pw = ops
    return pl.pallas_call(
        _ssm_decode_kernel,
        grid=(g,),
        in_specs=[gspec((bd, w)), gspec((bd, p)), gspec((bd, p)), gspec((wc, wc)), gspec((wc, p)),
                  gspec((wc, p)), gspec((wc, p)), gspec((wc, p)), gspec((1, wc)), gspec((8, p))],
        out_specs=[gspec((bd, w)), gspec((bd, p)), gspec((bd, p))],
        out_shape=[jax.ShapeDtypeStruct((g, bd, w), F32), jax.ShapeDtypeStruct((g, bd, p), F32),
                   jax.ShapeDtypeStruct((g, bd, p), F32)],
        compiler_params=_cparams(1),
        name="ssm_decode",
    )(ug, s0_re, s0_im, tit, tinr, tini, toutr, touti, d_tile, pw)


def _gelu_tanh(x):
    return 0.5 * x * (1.0 + jnp.tanh(math.sqrt(2.0 / math.pi) * (x + 0.044715 * (x * x * x))))


def _sigmoid(x):
    return 1.0 / (1.0 + jnp.exp(-x))


def _route(hn, wr_ref):
    n = hn.shape[0]
    h_hi = hn.astype(BF16)
    h_lo = (hn - h_hi.astype(F32)).astype(BF16)
    prod = jnp.dot(jnp.concatenate([h_hi, h_lo], axis=0), wr_ref[...], preferred_element_type=F32)
    logits = (prod[:n, :ROUTER_LANES] + prod[:n, ROUTER_LANES:]
              + prod[n:, :ROUTER_LANES] + prod[n:, ROUTER_LANES:])
    lidx = lax.broadcasted_iota(jnp.int32, (n, ROUTER_LANES), 1)
    is_e = lidx < N_EXPERTS
    is_g = jnp.logical_and(lidx >= N_EXPERTS, lidx < N_EXPERTS + N_EXPERT_GROUPS)
    gmax = jnp.max(jnp.where(is_g, logits, -jnp.inf), axis=1, keepdims=True)
    g_prob = 1.0 / jnp.sum(jnp.where(is_g, jnp.exp(logits - gmax), 0.0), axis=1, keepdims=True)
    g_sel = jnp.min(jnp.where(jnp.logical_and(is_g, logits == gmax), lidx - N_EXPERTS, N_EXPERT_GROUPS),
                    axis=1, keepdims=True)
    in_grp = jnp.logical_and(is_e, (lidx // EXPERTS_PER_GROUP) == g_sel)
    l1 = jnp.max(jnp.where(in_grp, logits, -jnp.inf), axis=1, keepdims=True)
    i1 = jnp.min(jnp.where(jnp.logical_and(in_grp, logits == l1), lidx, ROUTER_LANES), axis=1, keepdims=True)
    rest = jnp.logical_and(in_grp, lidx != i1)
    l2 = jnp.max(jnp.where(rest, logits, -jnp.inf), axis=1, keepdims=True)
    i2 = jnp.min(jnp.where(jnp.logical_and(rest, logits == l2), lidx, ROUTER_LANES), axis=1, keepdims=True)
    e2 = jnp.exp(l2 - l1)
    w1 = g_prob / (1.0 + e2)
    w2 = g_prob * e2 / (1.0 + e2)
    return jnp.where(lidx == i1, w1, 0.0) + jnp.where(lidx == i2, w2, 0.0)


def _post_body(oa, ys, x, ga_ref, gb_ref, wglu_ref, bglu_ref, wout_ref, gf_ref, wr_ref,
               x1_ref, hn_ref, gates_ref):
    z = _gelu_tanh(ys)
    gate = _sigmoid(jnp.dot(z.astype(BF16), wglu_ref[...], preferred_element_type=F32) + bglu_ref[...])
    ob = z * gate
    mixed = jnp.concatenate([_rms(oa, ga_ref[...]), _rms(ob, gb_ref[...])], axis=1).astype(BF16)
    x1 = x + jnp.dot(mixed, wout_ref[...], preferred_element_type=F32)
    x1_ref[...] = x1
    hn = _rms(x1, gf_ref[...])
    bits = lax.bitcast_convert_type(hn.astype(BF16).astype(F32), jnp.int32)
    half = hn.shape[1] // 2
    hn_ref[...] = (lax.shift_right_logical(bits[:, :half], 16) | (bits[:, half:] & jnp.int32(-65536)))
    gates_ref[...] = _route(hn, wr_ref)


def _post_pair_kernel(oa_ref, ys_ref, x_ref, *rest):
    oa = jnp.concatenate([oa_ref[j] for j in range(MIX_A // LANES)], axis=1)
    _post_body(oa, ys_ref[...], x_ref[...], *rest)


def _post_tok_kernel(oa_ref, ys_ref, x_ref, *rest):
    _post_body(oa_ref[...], ys_ref[...], x_ref[...], *rest)


def _post_mix(oa, ys, x, w, tm, pair):
    b, t, d = x.shape
    row = lambda width: pl.BlockSpec((None, tm, width), lambda bi, i: (bi, i, 0))
    const = lambda a: pl.BlockSpec(a.shape, lambda bi, i: (0,) * a.ndim)
    oa_spec = (pl.BlockSpec((None, MIX_A // LANES, tm, LANES), lambda bi, i: (bi, 0, i, 0)) if pair
               else row(MIX_A))
    weights = [w['g_out_a'], w['g_out_b'], w['w_glu'], w['b_glu'], w['w_out'], w['g_ffn'], w['w_router']]
    return pl.pallas_call(
        _post_pair_kernel if pair else _post_tok_kernel,
        grid=(b, t // tm),
        in_specs=[oa_spec, row(MIX_B), row(d)] + [const(a) for a in weights],
        out_specs=[row(d), row(d // 2), row(ROUTER_LANES)],
        out_shape=[jax.ShapeDtypeStruct((b, t, d), F32), jax.ShapeDtypeStruct((b, t, d // 2), jnp.int32),
                   jax.ShapeDtypeStruct((b, t, ROUTER_LANES), F32)],
        compiler_params=_cparams(2),
        name="post_mix_prompt" if pair else "post_mix_sample",
    )(oa, ys, x, *weights)


MOE_BLOCK = 2048
MOE_ROWS = 128
MOE_ALIGN = 8
PLAN_CHUNK = 256


def _moe_rows(tb):
    rows = 2 * tb + N_EXPERTS * (MOE_ALIGN - 1) + MOE_ROWS
    return -(-rows // MOE_ALIGN) * MOE_ALIGN


def _moe_plan_kernel(gates_ref, plan_ref, meta_ref):
    tb = gates_ref.shape[0]
    gates = gates_ref[...]
    hot = gates > 0.0
    onehot = jnp.where(hot, 1.0, 0.0).astype(BF16)
    ch = min(PLAN_CHUNK, tb)
    r_i = lax.broadcasted_iota(jnp.int32, (ch, ch), 0)
    c_i = lax.broadcasted_iota(jnp.int32, (ch, ch), 1)
    earlier = jnp.where(c_i < r_i, 1.0, 0.0).astype(BF16)
    carry = jnp.zeros((1, ROUTER_LANES), F32)
    ranks = []
    for k in range(tb // ch):
        oh = onehot[k * ch:(k + 1) * ch]
        ranks.append(jnp.dot(earlier, oh, preferred_element_type=F32) + carry)
        carry = carry + jnp.sum(oh.astype(F32), axis=0, keepdims=True)
    rank = jnp.concatenate(ranks, axis=0)
    seg = jnp.floor((carry + (MOE_ALIGN - 1.0)) * (1.0 / MOE_ALIGN)) * MOE_ALIGN
    l_i = lax.broadcasted_iota(jnp.int32, (ROUTER_LANES, ROUTER_LANES), 0)
    l_j = lax.broadcasted_iota(jnp.int32, (ROUTER_LANES, ROUTER_LANES), 1)
    before = jnp.where(l_i < l_j, 1.0, 0.0).astype(F32)
    offs = jnp.dot(jnp.broadcast_to(seg, (8, ROUTER_LANES)), before, precision=lax.Precision.HIGHEST,
                   preferred_element_type=F32)[0:1]
    pos = rank + offs
    lane = lax.broadcasted_iota(jnp.int32, (tb, ROUTER_LANES), 1)
    lane_a = jnp.min(jnp.where(hot, lane, ROUTER_LANES), axis=1, keepdims=True)
    lane_b = jnp.max(jnp.where(hot, lane, -1), axis=1, keepdims=True)
    pick = lambda sel, val: jnp.sum(jnp.where(sel, val, 0.0), axis=1, keepdims=True)
    sel_a, sel_b = lane == lane_a, lane == lane_b
    z = (jnp.where(lane == 0, pick(sel_a, pos), 0.0) + jnp.where(lane == 1, pick(sel_b, pos), 0.0)
         + jnp.where(lane == 2, pick(sel_a, gates), 0.0) + jnp.where(lane == 3, pick(sel_b, gates), 0.0))
    plan_ref[...] = z.T[0:8, :]
    meta_ref[...] = jnp.concatenate([offs, carry, jnp.zeros((6, ROUTER_LANES), F32)], axis=0)


def _moe_plan(gates, tb):
    n = gates.shape[0]
    nb = n // tb
    return pl.pallas_call(
        _moe_plan_kernel,
        grid=(nb,),
        in_specs=[pl.BlockSpec((tb, ROUTER_LANES), lambda i: (i, 0))],
        out_specs=[pl.BlockSpec((None, 8, tb), lambda i: (i, 0, 0)),
                   pl.BlockSpec((None, 8, ROUTER_LANES), lambda i: (i, 0, 0))],
        out_shape=[jax.ShapeDtypeStruct((nb, 8, tb), F32), jax.ShapeDtypeStruct((nb, 8, ROUTER_LANES), F32)],
        compiler_params=_cparams(1),
        name="moe_plan",
    )(gates)


def _moe_kernel(offs_ref, cnts_ref, pos_ref, w_ref, hn_ref, x1_ref, wg_ref, wu_ref, wd_ref, gfin_ref,
                y_ref, xs_scr, gcol_scr, otile_scr, tok_scr):
    blk = pl.program_id(0)
    e = pl.program_id(1)
    tb = hn_ref.shape[0]

    @pl.when(e == 0)
    def _group_rows():
        xs_scr[...] = jnp.zeros_like(xs_scr)
        gcol_scr[...] = jnp.zeros_like(gcol_scr)
        y_ref[...] = x1_ref[...]

        def clear_pad(ex, c):
            end = offs_ref[blk, ex] + cnts_ref[blk, ex]
            for k in range(MOE_ALIGN - 1):
                tok_scr[end + k] = 0
            return c

        lax.fori_loop(0, N_EXPERTS, clear_pad, 0)

        def place(t8, c):
            for u in range(8):
                t = t8 * 8 + u
                row = hn_ref[pl.ds(t, 1), :]
                for s in range(2):
                    p = pos_ref[s, t]
                    xs_scr[pl.ds(p, 1), :] = row
                    gcol_scr[pl.ds(p, 1), :] = jnp.full((1, LANES), w_ref[s, t], F32)
                    tok_scr[p] = t
            return c

        lax.fori_loop(0, tb // 8, place, 0)

    off = offs_ref[blk, e]
    cnt = cnts_ref[blk, e]

    @pl.when(cnt > 0)
    def _expert():
        wg = wg_ref[...]
        wu = wu_ref[...]
        wd = wd_ref[...]

        def tile(c, carry):
            r0 = pl.multiple_of(off + c * MOE_ROWS, MOE_ALIGN)
            xp = xs_scr[pl.ds(r0, MOE_ROWS), :]
            x = jnp.concatenate(
                [lax.bitcast_convert_type(xp << 16, F32).astype(BF16),
                 lax.bitcast_convert_type(xp & jnp.int32(-65536), F32).astype(BF16)], axis=1)
            a = jnp.dot(x, wg, preferred_element_type=F32)
            u = jnp.dot(x, wu, preferred_element_type=F32)
            g = gcol_scr[pl.ds(r0, MOE_ROWS), :]
            act = (a * _sigmoid(a)) * u * jnp.concatenate([g] * (D_EXPERT // LANES), axis=1)
            otile_scr[...] = jnp.dot(act.astype(BF16), wd, preferred_element_type=F32)
            valid = jnp.minimum(MOE_ROWS, cnt - c * MOE_ROWS)

            def add8(j8, cc):
                base = pl.multiple_of(j8 * 8, 8)
                toks = [tok_scr[r0 + base + u] for u in range(8)]
                rows = [y_ref[pl.ds(toks[u], 1), :] + otile_scr[pl.ds(base + u, 1), :] for u in range(8)]
                for u in range(8):
                    y_ref[pl.ds(toks[u], 1), :] = rows[u]
                return cc

            lax.fori_loop(0, (valid + 7) // 8, add8, 0)
            return carry

        lax.fori_loop(0, (cnt + MOE_ROWS - 1) // MOE_ROWS, tile, 0)

    @pl.when(e == pl.num_programs(1) - 1)
    def _fin():
        y_ref[...] = _rms(y_ref[...], gfin_ref[...])


def _moe(hn_packed, gates, x1, wg, wu, wd, g_final, tb):
    n, d = x1.shape
    nb = n // tb
    plan, meta = _moe_plan(gates, tb)
    pos = plan[:, 0:2, :].astype(jnp.int32)
    wts = plan[:, 2:4, :]
    offs = meta[:, 0, :N_EXPERTS].astype(jnp.int32)
    cnts = meta[:, 1, :N_EXPERTS].astype(jnp.int32)
    p_rows = _moe_rows(tb)
    smem = lambda: pl.BlockSpec((None, 2, tb), lambda i, e, *_: (i, 0, 0), memory_space=pltpu.SMEM)
    once = dict(pipeline_mode=pl.Buffered(1))
    grid_spec = pltpu.PrefetchScalarGridSpec(
        num_scalar_prefetch=2,
        grid=(nb, N_EXPERTS),
        in_specs=[smem(), smem(),
                  pl.BlockSpec((tb, d // 2), lambda i, e, *_: (i, 0), **once),
                  pl.BlockSpec((tb, d), lambda i, e, *_: (i, 0), **once),
                  pl.BlockSpec((None, d, D_EXPERT), lambda i, e, *_: (e, 0, 0)),
                  pl.BlockSpec((None, d, D_EXPERT), lambda i, e, *_: (e, 0, 0)),
                  pl.BlockSpec((None, D_EXPERT, d), lambda i, e, *_: (e, 0, 0)),
                  pl.BlockSpec((1, d), lambda i, e, *_: (0, 0))],
        out_specs=pl.BlockSpec((tb, d), lambda i, e, *_: (i, 0)),
        scratch_shapes=[pltpu.VMEM((p_rows, d // 2), jnp.int32),
                        pltpu.VMEM((p_rows, LANES), F32),
                        pltpu.VMEM((MOE_ROWS, d), F32),
                        pltpu.SMEM((p_rows,), jnp.int32)])
    return pl.pallas_call(
        _moe_kernel,
        grid_spec=grid_spec,
        out_shape=jax.ShapeDtypeStruct((n, d), F32),
        compiler_params=_cparams(2),
        name="moe",
    )(offs, cnts, pos, wts, hn_packed, x1, wg, wu, wd, g_final)


def kernel(x_prompt, x_sample, cache_k, cache_v, state_ssm_re, state_ssm_im, rel_bias, g_mix, w_in, g_out_a, g_out_b, w_out, ssm_a_re, ssm_a_im, ssm_log_dt, ssm_b_re, ssm_b_im, ssm_c_re, ssm_c_im, ssm_d, w_glu, b_glu, g_ffn, w_router_group, w_router_expert, w_expert_gate, w_expert_up, w_expert_down, g_final):
    depth = g_mix.shape[0]
    assert depth == 1, "kernel is written for the single-layer configuration of the problem"
    l = 0
    b, t, d = x_prompt.shape
    bd, ts, _ = x_sample.shape
    assert ts == DEC_T and t % ATT_BLOCK == 0 and cache_k.shape[2] == MAX_WINDOW
    keep = min(MAX_WINDOW, t)

    w_in_b = w_in[l].astype(BF16)
    gm = g_mix[l][None, :]
    w_router = jnp.concatenate(
        [jnp.transpose(w_router_expert[l], (1, 0, 2)).reshape(d, N_EXPERTS), w_router_group[l],
         jnp.zeros((d, ROUTER_LANES - N_EXPERTS - N_EXPERT_GROUPS), F32)], axis=1)
    w_router_hi = w_router.astype(BF16)
    w_router_lo = (w_router - w_router_hi.astype(F32)).astype(BF16)
    post_w = dict(
        g_out_a=g_out_a[l][None, :], g_out_b=g_out_b[l][None, :], w_glu=w_glu[l].astype(BF16),
        b_glu=b_glu[l][None, :], w_out=w_out[l].astype(BF16), g_ffn=g_ffn[l][None, :],
        w_router=jnp.concatenate([w_router_hi, w_router_lo], axis=1))
    wg = w_expert_gate[l].astype(BF16)
    wu = w_expert_up[l].astype(BF16)
    wd = w_expert_down[l].astype(BF16)
    gfin = g_final[None, :]
    ssm_ops = _ssm_prep(ssm_a_re[l], ssm_a_im[l], ssm_log_dt[l], ssm_b_re[l], ssm_b_im[l],
                        ssm_c_re[l], ssm_c_im[l])
    d_tile = jnp.tile(ssm_d[l].reshape(N_GROUPS, 1, SSM_GROUP), (1, 1, CHUNK))

    qp, kp, vp, kt_win, vt_win, u_tok = _inproj_pair(x_prompt, gm, w_in_b, tm=512, keep=keep)
    oa_p = _attn_prompt(qp, kp, vp, rel_bias)
    ug = jnp.transpose(u_tok.reshape(b, t, N_GROUPS, SSM_GROUP), (0, 2, 1, 3))
    ug = ug.reshape(b, N_GROUPS, t // CHUNK, CHUNK * SSM_GROUP)
    yg, rp, ip = _ssm_prompt(ug, ssm_ops, d_tile)
    ys_p = jnp.transpose(yg.reshape(b, N_GROUPS, t, SSM_GROUP), (0, 2, 1, 3)).reshape(b, t, MIX_B)
    x1_p, hn_p, gates_p = _post_mix(oa_p, ys_p, x_prompt, post_w, tm=512, pair=True)
    y_p = _moe(hn_p.reshape(b * t, d // 2), gates_p.reshape(b * t, ROUTER_LANES), x1_p.reshape(b * t, d),
               wg, wu, wd, gfin, tb=min(MOE_BLOCK, b * t))

    n_s = bd * ts
    qs, ks, vs, us = _inproj_tok(x_sample.reshape(n_s, d), gm, w_in_b)
    ckt = jnp.transpose(cache_k[l], (0, 2, 3, 1)).reshape(bd, MIX_A, MAX_WINDOW)
    cvt = jnp.transpose(cache_v[l], (0, 2, 3, 1)).reshape(bd, MIX_A, MAX_WINDOW)
    oa_s = _attn_decode(qs.reshape(bd, ts, MIX_A), ks.reshape(bd, ts, MIX_A), vs.reshape(bd, ts, MIX_A),
                        ckt, cvt, rel_bias)
    ugs = jnp.transpose(us.reshape(bd, ts, N_GROUPS, SSM_GROUP), (2, 0, 1, 3)).reshape(N_GROUPS, bd, ts * SSM_GROUP)
    s0r = jnp.transpose(state_ssm_re[l], (1, 0, 2))
    s0i = jnp.transpose(state_ssm_im[l], (1, 0, 2))
    ygs, rs, is_ = _ssm_decode(ugs, s0r, s0i, ssm_ops, d_tile)
    ys_s = jnp.transpose(ygs.reshape(N_GROUPS, bd, ts, SSM_GROUP), (1, 2, 0, 3)).reshape(n_s, MIX_B)
    x1_s, hn_s, gates_s = _post_mix(oa_s.reshape(1, n_s, MIX_A), ys_s.reshape(1, n_s, MIX_B),
                                    x_sample.reshape(1, n_s, d), post_w, tm=n_s, pair=False)
    y_s = _moe(hn_s.reshape(n_s, d // 2), gates_s.reshape(n_s, ROUTER_LANES), x1_s.reshape(n_s, d),
               wg, wu, wd, gfin, tb=n_s)

    y_prompt = y_p.reshape(b, t, d)
    y_sample = y_s.reshape(bd, ts, d)
    k_win = jnp.transpose(kt_win.reshape(b, N_HEADS, HEAD_DIM, keep), (0, 3, 1, 2))[None]
    v_win = jnp.transpose(vt_win.reshape(b, N_HEADS, HEAD_DIM, keep), (0, 3, 1, 2))[None]
    k_new = ks.reshape(1, bd, ts, N_HEADS, HEAD_DIM)
    v_new = vs.reshape(1, bd, ts, N_HEADS, HEAD_DIM)
    return (y_prompt, y_sample, k_win, v_win, k_new, v_new,
            rp.reshape(1, b, N_GROUPS, SSM_STATE), ip.reshape(1, b, N_GROUPS, SSM_STATE),
            jnp.transpose(rs, (1, 0, 2))[None], jnp.transpose(is_, (1, 0, 2))[None])
```

```python
import functools
import math

import jax
import jax.numpy as jnp
import numpy as np
from jax import lax
from jax.experimental import pallas as pl
from jax.experimental.pallas import tpu as pltpu

F32 = jnp.float32
BF16 = jnp.bfloat16

D_MODEL = 1024
HEAD_DIM = 64
MIX_A = 512
N_HEADS = 8
MIX_B = 512
SSM_GROUP = 16
N_GROUPS = 32
SSM_STATE = 64
PROJ_COLS = 3 * MIX_A + MIX_B
DILATIONS = (1, 4, 16)
TAPS = 128
MAX_WINDOW = 2048
N_BUCKETS = 32
N_EXPERT_GROUPS = 4
EXPERTS_PER_GROUP = 8
N_EXPERTS = 32
D_EXPERT = 256
EPS = 1e-6
NEG_INF = -1e30
SCALE = HEAD_DIM ** -0.5

LANES = 128
ROUTER_LANES = 128
CHUNK = 16
ATT_BLOCK = 2048
ATT_UNROLL = 8
ATT_PITCH = 136
VMEM_LIMIT = 56 * 1024 * 1024


def _cparams(n_axes):
    return pltpu.CompilerParams(dimension_semantics=("arbitrary",) * n_axes,
                                vmem_limit_bytes=VMEM_LIMIT)


def _t5_bucket(distance):
    max_exact = N_BUCKETS // 2
    nf = jnp.maximum(distance, 1).astype(F32)
    large = max_exact + (jnp.log(nf / max_exact) / math.log(MAX_WINDOW / max_exact)
                         * (N_BUCKETS - max_exact)).astype(jnp.int32)
    large = jnp.minimum(large, N_BUCKETS - 1)
    return jnp.where(distance < max_exact, distance, large)


def _rms(x, g):
    return x * lax.rsqrt(jnp.mean(x * x, axis=-1, keepdims=True) + EPS) * g


def _inproj_pair_kernel(first_win_tile, x_ref, g_ref, w_ref, q_ref, k_ref, v_ref, kt_ref, vt_ref, u_ref):
    h = _rms(x_ref[...], g_ref[...])
    p = jnp.dot(h.astype(BF16), w_ref[...], preferred_element_type=F32)
    for j in range(MIX_A // LANES):
        q_ref[j] = p[:, LANES * j:LANES * (j + 1)] * SCALE
        k_ref[j] = p[:, MIX_A + LANES * j:MIX_A + LANES * (j + 1)]
        v_ref[j] = p[:, 2 * MIX_A + LANES * j:2 * MIX_A + LANES * (j + 1)]
    u_ref[...] = p[:, 3 * MIX_A:]

    @pl.when(pl.program_id(1) >= first_win_tile)
    def _window():
        kt_ref[...] = p[:, MIX_A:2 * MIX_A].T
        vt_ref[...] = p[:, 2 * MIX_A:3 * MIX_A].T


def _inproj_pair(x, g, w_bf16, tm, keep):
    b, t, d = x.shape
    npair = MIX_A // LANES
    first = (t - keep) // tm
    pair = jax.ShapeDtypeStruct((b, npair, t, LANES), F32)
    win = jax.ShapeDtypeStruct((b, MIX_A, keep), F32)
    tok = jax.ShapeDtypeStruct((b, t, MIX_A), F32)
    pair_spec = pl.BlockSpec((None, npair, tm, LANES), lambda bi, i: (bi, 0, i, 0))
    win_spec = pl.BlockSpec((None, MIX_A, tm), lambda bi, i: (bi, 0, jnp.maximum(i - first, 0)))
    tok_spec = pl.BlockSpec((None, tm, MIX_A), lambda bi, i: (bi, i, 0))
    return pl.pallas_call(
        functools.partial(_inproj_pair_kernel, first),
        grid=(b, t // tm),
        in_specs=[pl.BlockSpec((None, tm, d), lambda bi, i: (bi, i, 0)),
                  pl.BlockSpec((1, d), lambda bi, i: (0, 0)),
                  pl.BlockSpec((d, PROJ_COLS), lambda bi, i: (0, 0))],
        out_specs=[pair_spec, pair_spec, pair_spec, win_spec, win_spec, tok_spec],
        out_shape=[pair, pair, pair, win, win, tok],
        compiler_params=_cparams(2),
        name="inproj_prompt",
    )(x, g, w_bf16)


def _inproj_tok_kernel(x_ref, g_ref, w_ref, q_ref, k_ref, v_ref, u_ref):
    h = _rms(x_ref[...], g_ref[...])
    p = jnp.dot(h.astype(BF16), w_ref[...], preferred_element_type=F32)
    q_ref[...] = p[:, :MIX_A] * SCALE
    k_ref[...] = p[:, MIX_A:2 * MIX_A]
    v_ref[...] = p[:, 2 * MIX_A:3 * MIX_A]
    u_ref[...] = p[:, 3 * MIX_A:]


def _inproj_tok(x, g, w_bf16):
    n, d = x.shape
    out = jax.ShapeDtypeStruct((n, MIX_A), F32)
    return pl.pallas_call(
        _inproj_tok_kernel,
        out_shape=[out, out, out, out],
        compiler_params=pltpu.CompilerParams(vmem_limit_bytes=VMEM_LIMIT),
        name="inproj_sample",
    )(x, g, w_bf16)


def _prompt_bias_ids():
    r = jnp.arange(TAPS, dtype=jnp.int32)[:, None]
    kpos = jnp.arange(2 * TAPS, dtype=jnp.int32)[None, :] - TAPS
    rel = r - kpos
    valid = (rel >= 0) & (rel <= TAPS)
    ids = [jnp.where(valid, _t5_bucket(d * rel), -1) for d in DILATIONS]
    return jnp.stack(ids).astype(jnp.int32)


def _attn_prompt_kernel(rb_ref, ids_ref, q_ref, kc_ref, kp_ref, vc_ref, vp_ref, o_ref,
                        bias_scr, fm_scr, kcat, vcat, m0_scr, l0_scr, o0_scr, m1_scr, l1_scr, o1_scr,
                        m2_scr, l2_scr, o2_scr):
    bi = pl.program_id(0)
    p = pl.program_id(1)
    i = pl.program_id(2)
    blk = ATT_BLOCK
    npair = N_HEADS // 2

    @pl.when((bi == 0) & (p == 0) & (i == 0))
    def _build_bias():
        col = lax.broadcasted_iota(jnp.int32, (2 * TAPS, 2 * TAPS), 1)
        fm_scr[0] = jnp.zeros((2 * TAPS, 2 * TAPS), F32)
        fm_scr[1] = jnp.where(col < TAPS, NEG_INF, 0.0).astype(F32)
        for c in range(len(DILATIONS)):
            ids = ids_ref[c]

            def head_body(h, carry, ids=ids, c=c):
                tile = jnp.full((TAPS, 2 * TAPS), NEG_INF, F32)
                for bkt in range(N_BUCKETS):
                    tile = jnp.where(ids == bkt, rb_ref[bkt, h], tile)
                row0 = pl.multiple_of((h % 2) * TAPS, TAPS)
                bias_scr[c * npair + h // 2, pl.ds(row0, TAPS), :] = tile
                return carry

            lax.fori_loop(0, N_HEADS, head_body, 0)

    kcat[pl.ds(0, blk), :] = kp_ref[...]
    kcat[pl.ds(blk, blk), :] = kc_ref[...]
    vcat[pl.ds(0, blk), :] = vp_ref[...]
    vcat[pl.ds(blk, blk), :] = vc_ref[...]

    lane = lax.broadcasted_iota(jnp.int32, (1, LANES), 1)
    low = lane < HEAD_DIM
    nt = (((1,), (1,)), ((), ()))
    ones = jnp.ones((2 * TAPS, LANES), BF16)
    stats = ((m0_scr, l0_scr, o0_scr), (m1_scr, l1_scr, o1_scr), (m2_scr, l2_scr, o2_scr))

    def rows(ref, start, d):
        if d == 1:
            return ref[pl.ds(start, TAPS), :]
        return ref[pl.ds(start, TAPS, stride=d), :]

    def sub_block(c, d, qs, first, dst_rows):
        q = rows(q_ref, qs, d)
        q2 = jnp.concatenate([jnp.where(low, q, 0.0), jnp.where(low, 0.0, q)], axis=0).astype(BF16)
        k = jnp.concatenate([rows(kcat, blk + qs - d * TAPS, d), rows(kcat, blk + qs, d)],
                            axis=0).astype(BF16)
        v = jnp.concatenate([rows(vcat, blk + qs - d * TAPS, d), rows(vcat, blk + qs, d)],
                            axis=0).astype(BF16)
        s_ = lax.dot_general(q2, k, nt, preferred_element_type=F32) + bias_scr[c * npair + p]
        if first is not None:
            s_ = s_ + fm_scr[first]
        m = jnp.max(s_, axis=1, keepdims=True)
        pe = jnp.exp(s_ - m).astype(BF16)
        ol = jnp.dot(pe, jnp.concatenate([v, ones], axis=1), preferred_element_type=F32)
        o, l = ol[:, :LANES], ol[:, LANES:]
        m_ref, l_ref, a_ref = stats[c]
        m_ref[dst_rows, :] = jnp.where(low, m[:TAPS], m[TAPS:])
        l_ref[dst_rows, :] = jnp.where(low, l[:TAPS], l[TAPS:])
        a_ref[dst_rows, :] = jnp.where(low, o[:TAPS], o[TAPS:])

    at_start = jnp.where(i == 0, 1, 0)
    n_sub = blk // TAPS
    for c, d in enumerate(DILATIONS):
        n_grp = blk // (d * TAPS)

        def group_body(s8, carry, c=c, d=d, n_grp=n_grp):
            for u in range(ATT_UNROLL):
                s = s8 * ATT_UNROLL + u
                if n_grp >= ATT_UNROLL:
                    qs = pl.multiple_of(s * TAPS, TAPS)
                    first = jnp.where(s8 == 0, at_start, 0) if u == 0 else None
                    dst_rows = pl.ds(qs, TAPS)
                else:
                    r = s8 * (ATT_UNROLL // n_grp) + u // n_grp
                    g = u % n_grp
                    qs = r + d * TAPS * g
                    first = at_start if g == 0 else None
                    if n_grp > 1:
                        dst_rows = pl.ds(qs, TAPS, stride=d)
                    else:
                        dst_rows = pl.ds(pl.multiple_of(r * ATT_PITCH, 8), TAPS)
                sub_block(c, d, qs, first, dst_rows)
            return carry

        lax.fori_loop(0, n_sub // ATT_UNROLL, group_body, 0)

    d_last = DILATIONS[-1]

    def merge_body(j, carry):
        tok = pl.ds(pl.multiple_of(j * d_last, d_last), d_last)
        dil = pl.ds(j, d_last, stride=ATT_PITCH)
        m0, m1, m2 = m0_scr[tok, :], m1_scr[tok, :], m2_scr[dil, :]
        m = jnp.maximum(jnp.maximum(m0, m1), m2)
        a0, a1, a2 = jnp.exp(m0 - m), jnp.exp(m1 - m), jnp.exp(m2 - m)
        num = o0_scr[tok, :] * a0 + o1_scr[tok, :] * a1 + o2_scr[dil, :] * a2
        den = l0_scr[tok, :] * a0 + l1_scr[tok, :] * a1 + l2_scr[dil, :] * a2
        o_ref[tok, :] = num / den
        return carry

    lax.fori_loop(0, blk // d_last, merge_body, 0, unroll=4)


def _attn_prompt(q, k, v, rel_bias):
    b, npair, t, _ = q.shape
    blk = ATT_BLOCK
    cur = pl.BlockSpec((None, None, blk, LANES), lambda bi, p, i: (bi, p, i, 0))
    prev = pl.BlockSpec((None, None, blk, LANES), lambda bi, p, i: (bi, p, jnp.maximum(i - 1, 0), 0))
    n_tiles = len(DILATIONS) * npair
    stats = ([pltpu.VMEM((blk, LANES), F32)] * 6
             + [pltpu.VMEM((DILATIONS[-1] * ATT_PITCH, LANES), F32)] * 3)
    return pl.pallas_call(
        _attn_prompt_kernel,
        grid=(b, npair, t // blk),
        in_specs=[pl.BlockSpec(memory_space=pltpu.SMEM),
                  pl.BlockSpec((len(DILATIONS), TAPS, 2 * TAPS), lambda bi, p, i: (0, 0, 0)),
                  cur, cur, prev, cur, prev],
        out_specs=cur,
        out_shape=jax.ShapeDtypeStruct(q.shape, F32),
        scratch_shapes=[pltpu.VMEM((n_tiles, 2 * TAPS, 2 * TAPS), F32),
                        pltpu.VMEM((2, 2 * TAPS, 2 * TAPS), F32),
                        pltpu.VMEM((2 * blk, LANES), F32),
                        pltpu.VMEM((2 * blk, LANES), F32)] + stats,
        compiler_params=_cparams(3),
        name="attn_prompt",
    )(rel_bias, _prompt_bias_ids(), q, k, k, v, v)


DEC_T = 4
DEC_NEW_PAD = 128
DEC_SEQ_PER_STEP = 2
DEC_KEYS = MAX_WINDOW + DEC_NEW_PAD


def _decode_tables():
    qi = jnp.arange(DEC_T, dtype=jnp.int32)[:, None]
    rho = jnp.arange(MAX_WINDOW, dtype=jnp.int32)[None, :]
    dist_b = MAX_WINDOW + qi - rho
    mult_b = jnp.zeros_like(dist_b)
    for d in DILATIONS:
        mult_b = mult_b + (((dist_b % d) == 0) & (dist_b <= d * TAPS)).astype(jnp.int32)
    jj = jnp.arange(DEC_NEW_PAD, dtype=jnp.int32)[None, :]
    dist_n = qi - jj
    mult_n = jnp.where(dist_n == 0, len(DILATIONS), jnp.where((dist_n > 0) & (jj < DEC_T), 1, 0))
    dist = jnp.concatenate([dist_b, jnp.maximum(dist_n, 0)], axis=1)
    mult = jnp.concatenate([mult_b, mult_n.astype(jnp.int32)], axis=1)
    ids = jnp.where(mult > 0, _t5_bucket(dist), -1).astype(jnp.int32)
    ids = jnp.repeat(ids, N_HEADS, axis=0)
    mult = jnp.repeat(mult, N_HEADS, axis=0).astype(F32)
    return ids, mult


def _attn_decode_kernel(rbt_ref, ids_ref, mult_ref, q_ref, kn_ref, vn_ref, kt_ref, vt_ref, o_ref, bias_scr):
    n_rows = DEC_T * N_HEADS

    @pl.when(pl.program_id(0) == 0)
    def _build_bias():
        ids = ids_ref[...]
        tile = jnp.full((n_rows, DEC_KEYS), NEG_INF, F32)
        for bkt in range(N_BUCKETS):
            tile = jnp.where(ids == bkt, rbt_ref[:, bkt:bkt + 1], tile)
        bias_scr[...] = tile

    lane = lax.broadcasted_iota(jnp.int32, (N_HEADS, MIX_A), 1)
    head = lax.broadcasted_iota(jnp.int32, (N_HEADS, MIX_A), 0)
    hmask = (lane // HEAD_DIM) == head

    zpad = jnp.zeros((DEC_NEW_PAD - DEC_T, MIX_A), F32)
    nt = (((1,), (1,)), ((), ()))
    for j in range(q_ref.shape[0]):
        q = q_ref[j]
        qm = jnp.where(hmask[None], q[:, None, :], 0.0).reshape(n_rows, MIX_A).astype(BF16)
        kn = jnp.concatenate([kn_ref[j], zpad], axis=0).astype(BF16)
        vn = jnp.concatenate([vn_ref[j], zpad], axis=0).astype(BF16)
        kt = kt_ref[j].astype(BF16)
        vt = vt_ref[j].astype(BF16)
        s = jnp.concatenate([jnp.dot(qm, kt, preferred_element_type=F32),
                             lax.dot_general(qm, kn, nt, preferred_element_type=F32)], axis=1)
        s = s + bias_scr[...]
        m = jnp.max(s, axis=1, keepdims=True)
        pe = jnp.exp(s - m) * mult_ref[...]
        l = jnp.sum(pe, axis=1, keepdims=True)
        pb = pe.astype(BF16)
        o = (lax.dot_general(pb[:, :MAX_WINDOW], vt, nt, preferred_element_type=F32)
             + jnp.dot(pb[:, MAX_WINDOW:], vn, preferred_element_type=F32)) / l
        o3 = o.reshape(DEC_T, N_HEADS, MIX_A)
        o_ref[j] = jnp.sum(jnp.where(hmask[None], o3, 0.0), axis=1)


def _attn_decode(q, k_new, v_new, cache_kt, cache_vt, rel_bias):
    bd = q.shape[0]
    ids, mult = _decode_tables()
    rbt = jnp.tile(rel_bias.T, (DEC_T, 1))
    per = DEC_SEQ_PER_STEP if bd % DEC_SEQ_PER_STEP == 0 else 1
    new_spec = pl.BlockSpec((per, DEC_T, MIX_A), lambda b: (b, 0, 0))
    cache_spec = pl.BlockSpec((per, MIX_A, MAX_WINDOW), lambda b: (b, 0, 0))
    n_rows = DEC_T * N_HEADS
    const = lambda shape: pl.BlockSpec(shape, lambda b: (0, 0))
    return pl.pallas_call(
        _attn_decode_kernel,
        grid=(bd // per,),
        in_specs=[const((n_rows, N_BUCKETS)), const((n_rows, DEC_KEYS)), const((n_rows, DEC_KEYS)),
                  new_spec, new_spec, new_spec, cache_spec, cache_spec],
        out_specs=new_spec,
        out_shape=jax.ShapeDtypeStruct((bd, DEC_T, MIX_A), F32),
        scratch_shapes=[pltpu.VMEM((n_rows, DEC_KEYS), F32)],
        compiler_params=_cparams(1),
        name="attn_decode",
    )(rbt, ids, mult, q, k_new, v_new, cache_kt, cache_vt)


def _ssm_prep_kernel(are_ref, aim_ref, ldt_ref, btr_ref, bti_ref, cre_ref, cim_ref,
                     tit_ref, tinr_ref, tini_ref, toutr_ref, touti_ref, pw_ref, tintr_ref, tinti_ref):
    hi = lax.Precision.HIGHEST
    nt = (((1,), (1,)), ((), ()))
    a_re, a_im = are_ref[...], aim_ref[...]
    dt = jnp.exp(ldt_ref[...])
    decay = jnp.exp(a_re * dt)
    ab_re = decay * jnp.cos(a_im * dt)
    ab_im = decay * jnp.sin(a_im * dt)
    inv = 1.0 / (a_re * a_re + a_im * a_im)
    coef_re = ((ab_re - 1.0) * a_re + ab_im * a_im) * inv
    coef_im = (ab_im * a_re - (ab_re - 1.0) * a_im) * inv
    bt_re, bt_im = btr_ref[...], bti_ref[...]
    bb_re = coef_re * bt_re - coef_im * bt_im
    bb_im = coef_re * bt_im + coef_im * bt_re
    pw = [(jnp.ones_like(ab_re), jnp.zeros_like(ab_im))]
    for _ in range(CHUNK):
        pr, pi = pw[-1]
        pw.append((pr * ab_re - pi * ab_im, pr * ab_im + pi * ab_re))
    tin_re = jnp.concatenate(
        [pw[CHUNK - 1 - ti][0] * bb_re - pw[CHUNK - 1 - ti][1] * bb_im for ti in range(CHUNK)], axis=0).astype(BF16)
    tin_im = jnp.concatenate(
        [pw[CHUNK - 1 - ti][0] * bb_im + pw[CHUNK - 1 - ti][1] * bb_re for ti in range(CHUNK)], axis=0).astype(BF16)
    tinr_ref[...] = tin_re
    tini_ref[...] = tin_im
    p_i = lax.broadcasted_iota(jnp.int32, (SSM_STATE, SSM_STATE), 0)
    p_j = lax.broadcasted_iota(jnp.int32, (SSM_STATE, SSM_STATE), 1)
    eye = jnp.where(p_i == p_j, 1.0, 0.0).astype(BF16)
    tintr_ref[...] = lax.dot_general(eye, tin_re, nt, preferred_element_type=F32).astype(BF16)
    tinti_ref[...] = lax.dot_general(eye, tin_im, nt, preferred_element_type=F32).astype(BF16)
    c_re, c_im = cre_ref[...], cim_ref[...]
    cp_re = [c_re * pr - c_im * pi for pr, pi in pw]
    cp_im = [c_re * pi + c_im * pr for pr, pi in pw]
    toutr_ref[...] = jnp.concatenate(cp_re[1:], axis=0).astype(BF16)
    touti_ref[...] = (-jnp.concatenate(cp_im[1:], axis=0)).astype(BF16)
    kall = (lax.dot_general(jnp.concatenate(cp_re[:CHUNK], axis=0), bb_re, nt, precision=hi,
                            preferred_element_type=F32)
            - lax.dot_general(jnp.concatenate(cp_im[:CHUNK], axis=0), bb_im, nt, precision=hi,
                              preferred_element_type=F32))
    w = CHUNK * SSM_GROUP
    kb = kall.astype(BF16)
    ci_idx = lax.broadcasted_iota(jnp.int32, (SSM_GROUP, w), 0)
    col_idx = lax.broadcasted_iota(jnp.int32, (SSM_GROUP, w), 1)
    acc = jnp.zeros((w, w), F32)
    for ti in range(CHUNK):
        rows = ti * SSM_GROUP
        shifted = kb if ti == 0 else jnp.concatenate(
            [jnp.zeros((rows, SSM_GROUP), BF16), kb[:w - rows]], axis=0)
        place = jnp.where(col_idx == ci_idx + rows, 1.0, 0.0).astype(BF16)
        acc = acc + jnp.dot(shifted, place, preferred_element_type=F32)
    tit_ref[...] = acc.astype(BF16)
    zero = jnp.zeros_like(ab_re)
    pw_ref[...] = jnp.concatenate([pw[CHUNK][0], pw[CHUNK][1], pw[DEC_T][0], pw[DEC_T][1],
                                   zero, zero, zero, zero], axis=0)


def _ssm_prep(a_re, a_im, log_dt, b_re, b_im, c_re, c_im):
    g, p = a_re.shape
    w = CHUNK * SSM_GROUP
    row = lambda a: a[:, None, :]
    bt_re = jnp.transpose(b_re, (0, 2, 1))
    bt_im = jnp.transpose(b_im, (0, 2, 1))
    ldt = jnp.broadcast_to(log_dt[:, None, None], (g, 1, p))
    gspec = lambda shape: pl.BlockSpec((None,) + shape, lambda gi: (gi, 0, 0))
    return pl.pallas_call(
        _ssm_prep_kernel,
        grid=(g,),
        in_specs=[gspec((1, p)), gspec((1, p)), gspec((1, p)), gspec((SSM_GROUP, p)), gspec((SSM_GROUP, p)),
                  gspec((SSM_GROUP, p)), gspec((SSM_GROUP, p))],
        out_specs=[gspec((w, w)), gspec((w, p)), gspec((w, p)), gspec((w, p)), gspec((w, p)), gspec((8, p)),
                   gspec((p, w)), gspec((p, w))],
        out_shape=[jax.ShapeDtypeStruct((g, w, w), BF16)] + [jax.ShapeDtypeStruct((g, w, p), BF16)] * 4
                  + [jax.ShapeDtypeStruct((g, 8, p), F32)] + [jax.ShapeDtypeStruct((g, p, w), BF16)] * 2,
        compiler_params=_cparams(1),
        name="ssm_prep",
    )(row(a_re), row(a_im), ldt, bt_re, bt_im, c_re, c_im)


def _ssm_prompt_kernel(u_ref, tit_ref, tintr_ref, tinti_ref, toutr_ref, touti_ref, d_ref, pw_ref,
                       y_ref, sre_ref, sim_ref, ut_scr, yt_scr, pr_scr, pi_scr, sr_scr, si_scr):
    ngrp = LANES // SSM_GROUP
    nc = u_ref.shape[0] // CHUNK
    xt = [u_ref[pl.ds(ti, nc, stride=CHUNK), :].T for ti in range(CHUNK)]
    for j in range(ngrp):
        ut = jnp.concatenate([x[j * SSM_GROUP:(j + 1) * SSM_GROUP, :] for x in xt], axis=0).astype(BF16)
        ut_scr[j] = ut
        pr_scr[j] = jnp.dot(tintr_ref[j], ut, preferred_element_type=F32).T
        pi_scr[j] = jnp.dot(tinti_ref[j], ut, preferred_element_type=F32).T
    ar = [pw_ref[j, 0:1, :] for j in range(ngrp)]
    ai = [pw_ref[j, 1:2, :] for j in range(ngrp)]

    def step(c, carry):
        out = []
        for j in range(ngrp):
            sr, si = carry[j]
            sr_scr[j, pl.ds(c, 1), :] = sr
            si_scr[j, pl.ds(c, 1), :] = si
            out.append((ar[j] * sr - ai[j] * si + pr_scr[j, pl.ds(c, 1), :],
                        ar[j] * si + ai[j] * sr + pi_scr[j, pl.ds(c, 1), :]))
        return tuple(out)

    zero = jnp.zeros((1, SSM_STATE), F32)
    final = lax.fori_loop(0, nc, step, ((zero, zero),) * ngrp, unroll=8)
    for j in range(ngrp):
        sre_ref[j] = final[j][0]
        sim_ref[j] = final[j][1]
        yt = jnp.dot(tit_ref[j], ut_scr[j], preferred_element_type=F32)
        yt = yt + jnp.dot(toutr_ref[j], sr_scr[j].T.astype(BF16), preferred_element_type=F32)
        yt = yt + jnp.dot(touti_ref[j], si_scr[j].T.astype(BF16), preferred_element_type=F32)
        yt_scr[j] = yt
    d = d_ref[...]
    for to in range(CHUNK):
        v = jnp.concatenate([yt_scr[j, to * SSM_GROUP:(to + 1) * SSM_GROUP, :] for j in range(ngrp)], axis=0)
        rows = pl.ds(to, nc, stride=CHUNK)
        y_ref[rows, :] = v.T + d * u_ref[rows, :]


def _ssm_prompt(u, ops, d_skip):
    b, t, _ = u.shape
    p = SSM_STATE
    w = CHUNK * SSM_GROUP
    nc = t // CHUNK
    ngrp = LANES // SSM_GROUP
    g = N_GROUPS
    gspec = lambda shape: pl.BlockSpec((ngrp,) + shape, lambda bi, qi: (qi, 0, 0))
    io = pl.BlockSpec((None, t, LANES), lambda bi, qi: (bi, 0, qi))
    st = pl.BlockSpec((None, ngrp, 1, p), lambda bi, qi: (bi, qi, 0, 0))
    tit, _, _, toutr, touti, pw, tintr, tinti = ops
    return pl.pallas_call(
        _ssm_prompt_kernel,
        grid=(b, g // ngrp),
        in_specs=[io, gspec((w, w)), gspec((p, w)), gspec((p, w)), gspec((w, p)), gspec((w, p)),
                  pl.BlockSpec((None, 1, LANES), lambda bi, qi: (qi, 0, 0)), gspec((8, p))],
        out_specs=[io, st, st],
        out_shape=[jax.ShapeDtypeStruct(u.shape, F32),
                   jax.ShapeDtypeStruct((b, g, 1, p), F32), jax.ShapeDtypeStruct((b, g, 1, p), F32)],
        scratch_shapes=[pltpu.VMEM((ngrp, w, nc), BF16), pltpu.VMEM((ngrp, w, nc), F32)]
                       + [pltpu.VMEM((ngrp, nc, p), F32)] * 4,
        compiler_params=_cparams(2),
        name="ssm_prompt",
    )(u, tit, tintr, tinti, toutr, touti, d_skip.reshape(g // ngrp, 1, LANES), pw)


def _ssm_decode_kernel(u_ref, s0r_ref, s0i_ref, tit_ref, tinr_ref, tini_ref, toutr_ref, touti_ref, d_ref,
                       pw_ref, y_ref, sre_ref, sim_ref):
    nt = (((1,), (1,)), ((), ()))
    w = DEC_T * SSM_GROUP
    lo = (CHUNK - DEC_T) * SSM_GROUP
    u = u_ref[...]
    ub = u.astype(BF16)
    s0r = s0r_ref[...]
    s0i = s0i_ref[...]
    y = lax.dot_general(ub, tit_ref[0:w, 0:w], nt, preferred_element_type=F32)
    y = y + lax.dot_general(s0r.astype(BF16), toutr_ref[0:w, :], nt, preferred_element_type=F32)
    y = y + lax.dot_general(s0i.astype(BF16), touti_ref[0:w, :], nt, preferred_element_type=F32)
    y_ref[...] = y + d_ref[:, 0:w] * u
    ar = pw_ref[2:3, :]
    ai = pw_ref[3:4, :]
    sre_ref[...] = ar * s0r - ai * s0i + jnp.dot(ub, tinr_ref[lo:, :], preferred_element_type=F32)
    sim_ref[...] = ar * s0i + ai * s0r + jnp.dot(ub, tini_ref[lo:, :], preferred_element_type=F32)


def _ssm_decode(ug, s0_re, s0_im, ops, d_tile):
    g, bd, w = ug.shape
    p = SSM_STATE
    wc = CHUNK * SSM_GROUP
    gspec = lambda shape: pl.BlockSpec((None,) + shape, lambda gi: (gi, 0, 0))
    tit, tinr, tini, toutr, touti, pw = ops[:6]
    return pl.pallas_call(
        _ssm_decode_kernel,
        grid=(g,),
        in_specs=[gspec((bd, w)), gspec((bd, p)), gspec((bd, p)), gspec((wc, wc)), gspec((wc, p)),
                  gspec((wc, p)), gspec((wc, p)), gspec((wc, p)), gspec((1, wc)), gspec((8, p))],
        out_specs=[gspec((bd, w)), gspec((bd, p)), gspec((bd, p))],
        out_shape=[jax.ShapeDtypeStruct((g, bd, w), F32), jax.ShapeDtypeStruct((g, bd, p), F32),
                   jax.ShapeDtypeStruct((g, bd, p), F32)],
        compiler_params=_cparams(1),
        name="ssm_decode",
    )(ug, s0_re, s0_im, tit, tinr, tini, toutr, touti, d_tile, pw)


def _gelu_tanh(x):
    return 0.5 * x * (1.0 + jnp.tanh(math.sqrt(2.0 / math.pi) * (x + 0.044715 * (x * x * x))))


def _sigmoid(x):
    return 1.0 / (1.0 + jnp.exp(-x))


def _route(hn, wr_ref):
    n = hn.shape[0]
    h_hi = hn.astype(BF16)
    h_lo = (hn - h_hi.astype(F32)).astype(BF16)
    prod = jnp.dot(jnp.concatenate([h_hi, h_lo], axis=0), wr_ref[...], preferred_element_type=F32)
    logits = (prod[:n, :ROUTER_LANES] + prod[:n, ROUTER_LANES:]
              + prod[n:, :ROUTER_LANES] + prod[n:, ROUTER_LANES:])
    lidx = lax.broadcasted_iota(jnp.int32, (n, ROUTER_LANES), 1)
    is_e = lidx < N_EXPERTS
    is_g = jnp.logical_and(lidx >= N_EXPERTS, lidx < N_EXPERTS + N_EXPERT_GROUPS)
    gmax = jnp.max(jnp.where(is_g, logits, -jnp.inf), axis=1, keepdims=True)
    g_prob = 1.0 / jnp.sum(jnp.where(is_g, jnp.exp(logits - gmax), 0.0), axis=1, keepdims=True)
    g_sel = jnp.min(jnp.where(jnp.logical_and(is_g, logits == gmax), lidx - N_EXPERTS, N_EXPERT_GROUPS),
                    axis=1, keepdims=True)
    in_grp = jnp.logical_and(is_e, (lidx // EXPERTS_PER_GROUP) == g_sel)
    l1 = jnp.max(jnp.where(in_grp, logits, -jnp.inf), axis=1, keepdims=True)
    i1 = jnp.min(jnp.where(jnp.logical_and(in_grp, logits == l1), lidx, ROUTER_LANES), axis=1, keepdims=True)
    rest = jnp.logical_and(in_grp, lidx != i1)
    l2 = jnp.max(jnp.where(rest, logits, -jnp.inf), axis=1, keepdims=True)
    i2 = jnp.min(jnp.where(jnp.logical_and(rest, logits == l2), lidx, ROUTER_LANES), axis=1, keepdims=True)
    e2 = jnp.exp(l2 - l1)
    w1 = g_prob / (1.0 + e2)
    w2 = g_prob * e2 / (1.0 + e2)
    return jnp.where(lidx == i1, w1, 0.0) + jnp.where(lidx == i2, w2, 0.0)


def _post_body(oa, ys, x, ga_ref, gb_ref, wglu_ref, bglu_ref, wout_ref, gf_ref, wr_ref,
               x1_ref, hn_ref, gates_ref):
    z = _gelu_tanh(ys)
    gate = _sigmoid(jnp.dot(z.astype(BF16), wglu_ref[...], preferred_element_type=F32) + bglu_ref[...])
    ob = z * gate
    mixed = jnp.concatenate([_rms(oa, ga_ref[...]), _rms(ob, gb_ref[...])], axis=1).astype(BF16)
    x1 = x + jnp.dot(mixed, wout_ref[...], preferred_element_type=F32)
    x1_ref[...] = x1
    hn = _rms(x1, gf_ref[...])
    bits = lax.bitcast_convert_type(hn.astype(BF16).astype(F32), jnp.int32)
    half = hn.shape[1] // 2
    hn_ref[...] = (lax.shift_right_logical(bits[:, :half], 16) | (bits[:, half:] & jnp.int32(-65536)))
    gates_ref[...] = _route(hn, wr_ref)


def _post_pair_kernel(oa_ref, ys_ref, x_ref, *rest):
    oa = jnp.concatenate([oa_ref[j] for j in range(MIX_A // LANES)], axis=1)
    _post_body(oa, ys_ref[...], x_ref[...], *rest)


def _post_tok_kernel(oa_ref, ys_ref, x_ref, *rest):
    _post_body(oa_ref[...], ys_ref[...], x_ref[...], *rest)


def _post_mix(oa, ys, x, w, tm, pair):
    b, t, d = x.shape
    row = lambda width: pl.BlockSpec((None, tm, width), lambda bi, i: (bi, i, 0))
    const = lambda a: pl.BlockSpec(a.shape, lambda bi, i: (0,) * a.ndim)
    oa_spec = (pl.BlockSpec((None, MIX_A // LANES, tm, LANES), lambda bi, i: (bi, 0, i, 0)) if pair
               else row(MIX_A))
    weights = [w['g_out_a'], w['g_out_b'], w['w_glu'], w['b_glu'], w['w_out'], w['g_ffn'], w['w_router']]
    return pl.pallas_call(
        _post_pair_kernel if pair else _post_tok_kernel,
        grid=(b, t // tm),
        in_specs=[oa_spec, row(MIX_B), row(d)] + [const(a) for a in weights],
        out_specs=[row(d), row(d // 2), row(ROUTER_LANES)],
        out_shape=[jax.ShapeDtypeStruct((b, t, d), F32), jax.ShapeDtypeStruct((b, t, d // 2), jnp.int32),
                   jax.ShapeDtypeStruct((b, t, ROUTER_LANES), F32)],
        compiler_params=_cparams(2),
        name="post_mix_prompt" if pair else "post_mix_sample",
    )(oa, ys, x, *weights)


MOE_BLOCK = 2048
MOE_ROWS = 128
MOE_ALIGN = 8
PLAN_CHUNK = 256


def _moe_rows(tb):
    rows = 2 * tb + N_EXPERTS * (MOE_ALIGN - 1) + MOE_ROWS
    return -(-rows // MOE_ALIGN) * MOE_ALIGN


def _moe_plan_kernel(gates_ref, plan_ref, meta_ref):
    tb = gates_ref.shape[0]
    gates = gates_ref[...]
    hot = gates > 0.0
    onehot = jnp.where(hot, 1.0, 0.0).astype(BF16)
    ch = min(PLAN_CHUNK, tb)
    r_i = lax.broadcasted_iota(jnp.int32, (ch, ch), 0)
    c_i = lax.broadcasted_iota(jnp.int32, (ch, ch), 1)
    earlier = jnp.where(c_i < r_i, 1.0, 0.0).astype(BF16)
    carry = jnp.zeros((1, ROUTER_LANES), F32)
    ranks = []
    for k in range(tb // ch):
        oh = onehot[k * ch:(k + 1) * ch]
        ranks.append(jnp.dot(earlier, oh, preferred_element_type=F32) + carry)
        carry = carry + jnp.sum(oh.astype(F32), axis=0, keepdims=True)
    rank = jnp.concatenate(ranks, axis=0)
    seg = jnp.floor((carry + (MOE_ALIGN - 1.0)) * (1.0 / MOE_ALIGN)) * MOE_ALIGN
    l_i = lax.broadcasted_iota(jnp.int32, (ROUTER_LANES, ROUTER_LANES), 0)
    l_j = lax.broadcasted_iota(jnp.int32, (ROUTER_LANES, ROUTER_LANES), 1)
    before = jnp.where(l_i < l_j, 1.0, 0.0).astype(F32)
    offs = jnp.dot(jnp.broadcast_to(seg, (8, ROUTER_LANES)), before, precision=lax.Precision.HIGHEST,
                   preferred_element_type=F32)[0:1]
    pos = rank + offs
    lane = lax.broadcasted_iota(jnp.int32, (tb, ROUTER_LANES), 1)
    lane_a = jnp.min(jnp.where(hot, lane, ROUTER_LANES), axis=1, keepdims=True)
    lane_b = jnp.max(jnp.where(hot, lane, -1), axis=1, keepdims=True)
    pick = lambda sel, val: jnp.sum(jnp.where(sel, val, 0.0), axis=1, keepdims=True)
    sel_a, sel_b = lane == lane_a, lane == lane_b
    z = (jnp.where(lane == 0, pick(sel_a, pos), 0.0) + jnp.where(lane == 1, pick(sel_b, pos), 0.0)
         + jnp.where(lane == 2, pick(sel_a, gates), 0.0) + jnp.where(lane == 3, pick(sel_b, gates), 0.0))
    plan_ref[...] = z.T[0:8, :]
    meta_ref[...] = jnp.concatenate([offs, carry, jnp.zeros((6, ROUTER_LANES), F32)], axis=0)


def _moe_plan(gates, tb):
    n = gates.shape[0]
    nb = n // tb
    return pl.pallas_call(
        _moe_plan_kernel,
        grid=(nb,),
        in_specs=[pl.BlockSpec((tb, ROUTER_LANES), lambda i: (i, 0))],
        out_specs=[pl.BlockSpec((None, 8, tb), lambda i: (i, 0, 0)),
                   pl.BlockSpec((None, 8, ROUTER_LANES), lambda i: (i, 0, 0))],
        out_shape=[jax.ShapeDtypeStruct((nb, 8, tb), F32), jax.ShapeDtypeStruct((nb, 8, ROUTER_LANES), F32)],
        compiler_params=_cparams(1),
        name="moe_plan",
    )(gates)


def _moe_kernel(offs_ref, cnts_ref, pos_ref, w_ref, hn_ref, x1_ref, wg_ref, wu_ref, wd_ref, gfin_ref,
                y_ref, xs_scr, gcol_scr, otile_scr, tok_scr):
    blk = pl.program_id(0)
    e = pl.program_id(1)
    tb = hn_ref.shape[0]

    @pl.when(e == 0)
    def _group_rows():
        xs_scr[...] = jnp.zeros_like(xs_scr)
        gcol_scr[...] = jnp.zeros_like(gcol_scr)
        y_ref[...] = x1_ref[...]

        def clear_pad(ex, c):
            end = offs_ref[blk, ex] + cnts_ref[blk, ex]
            for k in range(MOE_ALIGN - 1):
                tok_scr[end + k] = 0
            return c

        lax.fori_loop(0, N_EXPERTS, clear_pad, 0)

        def place(t8, c):
            for u in range(8):
                t = t8 * 8 + u
                row = hn_ref[pl.ds(t, 1), :]
                for s in range(2):
                    p = pos_ref[s, t]
                    xs_scr[pl.ds(p, 1), :] = row
                    gcol_scr[pl.ds(p, 1), :] = jnp.full((1, LANES), w_ref[s, t], F32)
                    tok_scr[p] = t
            return c

        lax.fori_loop(0, tb // 8, place, 0)

    off = offs_ref[blk, e]
    cnt = cnts_ref[blk, e]

    @pl.when(cnt > 0)
    def _expert():
        wg = wg_ref[...]
        wu = wu_ref[...]
        wd = wd_ref[...]

        def tile(c, carry):
            r0 = pl.multiple_of(off + c * MOE_ROWS, MOE_ALIGN)
            xp = xs_scr[pl.ds(r0, MOE_ROWS), :]
            x = jnp.concatenate(
                [lax.bitcast_convert_type(xp << 16, F32).astype(BF16),
                 lax.bitcast_convert_type(xp & jnp.int32(-65536), F32).astype(BF16)], axis=1)
            a = jnp.dot(x, wg, preferred_element_type=F32)
            u = jnp.dot(x, wu, preferred_element_type=F32)
            g = gcol_scr[pl.ds(r0, MOE_ROWS), :]
            act = (a * _sigmoid(a)) * u * jnp.concatenate([g] * (D_EXPERT // LANES), axis=1)
            otile_scr[...] = jnp.dot(act.astype(BF16), wd, preferred_element_type=F32)
            valid = jnp.minimum(MOE_ROWS, cnt - c * MOE_ROWS)

            def add8(j8, cc):
                base = pl.multiple_of(j8 * 8, 8)
                toks = [tok_scr[r0 + base + u] for u in range(8)]
                rows = [y_ref[pl.ds(toks[u], 1), :] + otile_scr[pl.ds(base + u, 1), :] for u in range(8)]
                for u in range(8):
                    y_ref[pl.ds(toks[u], 1), :] = rows[u]
                return cc

            lax.fori_loop(0, (valid + 7) // 8, add8, 0)
            return carry

        lax.fori_loop(0, (cnt + MOE_ROWS - 1) // MOE_ROWS, tile, 0)

    @pl.when(e == pl.num_programs(1) - 1)
    def _fin():
        y_ref[...] = _rms(y_ref[...], gfin_ref[...])


def _moe(hn_packed, gates, x1, wg, wu, wd, g_final, tb):
    n, d = x1.shape
    nb = n // tb
    plan, meta = _moe_plan(gates, tb)
    pos = plan[:, 0:2, :].astype(jnp.int32)
    wts = plan[:, 2:4, :]
    offs = meta[:, 0, :N_EXPERTS].astype(jnp.int32)
    cnts = meta[:, 1, :N_EXPERTS].astype(jnp.int32)
    p_rows = _moe_rows(tb)
    smem = lambda: pl.BlockSpec((None, 2, tb), lambda i, e, *_: (i, 0, 0), memory_space=pltpu.SMEM)
    once = dict(pipeline_mode=pl.Buffered(1))
    grid_spec = pltpu.PrefetchScalarGridSpec(
        num_scalar_prefetch=2,
        grid=(nb, N_EXPERTS),
        in_specs=[smem(), smem(),
                  pl.BlockSpec((tb, d // 2), lambda i, e, *_: (i, 0), **once),
                  pl.BlockSpec((tb, d), lambda i, e, *_: (i, 0), **once),
                  pl.BlockSpec((None, d, D_EXPERT), lambda i, e, *_: (e, 0, 0)),
                  pl.BlockSpec((None, d, D_EXPERT), lambda i, e, *_: (e, 0, 0)),
                  pl.BlockSpec((None, D_EXPERT, d), lambda i, e, *_: (e, 0, 0)),
                  pl.BlockSpec((1, d), lambda i, e, *_: (0, 0))],
        out_specs=pl.BlockSpec((tb, d), lambda i, e, *_: (i, 0)),
        scratch_shapes=[pltpu.VMEM((p_rows, d // 2), jnp.int32),
                        pltpu.VMEM((p_rows, LANES), F32),
                        pltpu.VMEM((MOE_ROWS, d), F32),
                        pltpu.SMEM((p_rows,), jnp.int32)])
    return pl.pallas_call(
        _moe_kernel,
        grid_spec=grid_spec,
        out_shape=jax.ShapeDtypeStruct((n, d), F32),
        compiler_params=_cparams(2),
        name="moe",
    )(offs, cnts, pos, wts, hn_packed, x1, wg, wu, wd, g_final)


def kernel(x_prompt, x_sample, cache_k, cache_v, state_ssm_re, state_ssm_im, rel_bias, g_mix, w_in, g_out_a, g_out_b, w_out, ssm_a_re, ssm_a_im, ssm_log_dt, ssm_b_re, ssm_b_im, ssm_c_re, ssm_c_im, ssm_d, w_glu, b_glu, g_ffn, w_router_group, w_router_expert, w_expert_gate, w_expert_up, w_expert_down, g_final):
    depth = g_mix.shape[0]
    assert depth == 1, "kernel is written for the single-layer configuration of the problem"
    l = 0
    b, t, d = x_prompt.shape
    bd, ts, _ = x_sample.shape
    assert ts == DEC_T and t % ATT_BLOCK == 0 and cache_k.shape[2] == MAX_WINDOW
    keep = min(MAX_WINDOW, t)

    w_in_b = w_in[l].astype(BF16)
    gm = g_mix[l][None, :]
    w_router = jnp.concatenate(
        [jnp.transpose(w_router_expert[l], (1, 0, 2)).reshape(d, N_EXPERTS), w_router_group[l],
         jnp.zeros((d, ROUTER_LANES - N_EXPERTS - N_EXPERT_GROUPS), F32)], axis=1)
    w_router_hi = w_router.astype(BF16)
    w_router_lo = (w_router - w_router_hi.astype(F32)).astype(BF16)
    post_w = dict(
        g_out_a=g_out_a[l][None, :], g_out_b=g_out_b[l][None, :], w_glu=w_glu[l].astype(BF16),
        b_glu=b_glu[l][None, :], w_out=w_out[l].astype(BF16), g_ffn=g_ffn[l][None, :],
        w_router=jnp.concatenate([w_router_hi, w_router_lo], axis=1))
    wg = w_expert_gate[l].astype(BF16)
    wu = w_expert_up[l].astype(BF16)
    wd = w_expert_down[l].astype(BF16)
    gfin = g_final[None, :]
    ssm_ops = _ssm_prep(ssm_a_re[l], ssm_a_im[l], ssm_log_dt[l], ssm_b_re[l], ssm_b_im[l],
                        ssm_c_re[l], ssm_c_im[l])
    d_tile = jnp.tile(ssm_d[l].reshape(N_GROUPS, 1, SSM_GROUP), (1, 1, CHUNK))

    qp, kp, vp, kt_win, vt_win, u_tok = _inproj_pair(x_prompt, gm, w_in_b, tm=512, keep=keep)
    oa_p = _attn_prompt(qp, kp, vp, rel_bias)
    ys_p, rp, ip = _ssm_prompt(u_tok, ssm_ops, ssm_d[l])
    x1_p, hn_p, gates_p = _post_mix(oa_p, ys_p, x_prompt, post_w, tm=512, pair=True)
    y_p = _moe(hn_p.reshape(b * t, d // 2), gates_p.reshape(b * t, ROUTER_LANES), x1_p.reshape(b * t, d),
               wg, wu, wd, gfin, tb=min(MOE_BLOCK, b * t))

    n_s = bd * ts
    qs, ks, vs, us = _inproj_tok(x_sample.reshape(n_s, d), gm, w_in_b)
    ckt = jnp.transpose(cache_k[l], (0, 2, 3, 1)).reshape(bd, MIX_A, MAX_WINDOW)
    cvt = jnp.transpose(cache_v[l], (0, 2, 3, 1)).reshape(bd, MIX_A, MAX_WINDOW)
    oa_s = _attn_decode(qs.reshape(bd, ts, MIX_A), ks.reshape(bd, ts, MIX_A), vs.reshape(bd, ts, MIX_A),
                        ckt, cvt, rel_bias)
    ugs = jnp.transpose(us.reshape(bd, ts, N_GROUPS, SSM_GROUP), (2, 0, 1, 3)).reshape(N_GROUPS, bd, ts * SSM_GROUP)
    s0r = jnp.transpose(state_ssm_re[l], (1, 0, 2))
    s0i = jnp.transpose(state_ssm_im[l], (1, 0, 2))
    ygs, rs, is_ = _ssm_decode(ugs, s0r, s0i, ssm_ops, d_tile)
    ys_s = jnp.transpose(ygs.reshape(N_GROUPS, bd, ts, SSM_GROUP), (1, 2, 0, 3)).reshape(n_s, MIX_B)
    x1_s, hn_s, gates_s = _post_mix(oa_s.reshape(1, n_s, MIX_A), ys_s.reshape(1, n_s, MIX_B),
                                    x_sample.reshape(1, n_s, d), post_w, tm=n_s, pair=False)
    y_s = _moe(hn_s.reshape(n_s, d // 2), gates_s.reshape(n_s, ROUTER_LANES), x1_s.reshape(n_s, d),
               wg, wu, wd, gfin, tb=n_s)

    y_prompt = y_p.reshape(b, t, d)
    y_sample = y_s.reshape(bd, ts, d)
    k_win = jnp.transpose(kt_win.reshape(b, N_HEADS, HEAD_DIM, keep), (0, 3, 1, 2))[None]
    v_win = jnp.transpose(vt_win.reshape(b, N_HEADS, HEAD_DIM, keep), (0, 3, 1, 2))[None]
    k_new = ks.reshape(1, bd, ts, N_HEADS, HEAD_DIM)
    v_new = vs.reshape(1, bd, ts, N_HEADS, HEAD_DIM)
    return (y_prompt, y_sample, k_win, v_win, k_new, v_new,
            rp.reshape(1, b, N_GROUPS, SSM_STATE), ip.reshape(1, b, N_GROUPS, SSM_STATE),
            jnp.transpose(rs, (1, 0, 2))[None], jnp.transpose(is_, (1, 0, 2))[None])
```

```python
import functools
import math

import jax
import jax.numpy as jnp
import numpy as np
from jax import lax
from jax.experimental import pallas as pl
from jax.experimental.pallas import tpu as pltpu

F32 = jnp.float32
BF16 = jnp.bfloat16

D_MODEL = 1024
HEAD_DIM = 64
MIX_A = 512
N_HEADS = 8
MIX_B = 512
SSM_GROUP = 16
N_GROUPS = 32
SSM_STATE = 64
PROJ_COLS = 3 * MIX_A + MIX_B
DILATIONS = (1, 4, 16)
TAPS = 128
MAX_WINDOW = 2048
N_BUCKETS = 32
N_EXPERT_GROUPS = 4
EXPERTS_PER_GROUP = 8
N_EXPERTS = 32
D_EXPERT = 256
EPS = 1e-6
NEG_INF = -1e30
SCALE = HEAD_DIM ** -0.5

LANES = 128
ROUTER_LANES = 128
CHUNK = 16
ATT_BLOCK = 2048
ATT_UNROLL = 8
ATT_PITCH = 136
VMEM_LIMIT = 56 * 1024 * 1024


def _cparams(n_axes):
    return pltpu.CompilerParams(dimension_semantics=("arbitrary",) * n_axes,
                                vmem_limit_bytes=VMEM_LIMIT)


def _t5_bucket(distance):
    max_exact = N_BUCKETS // 2
    nf = jnp.maximum(distance, 1).astype(F32)
    large = max_exact + (jnp.log(nf / max_exact) / math.log(MAX_WINDOW / max_exact)
                         * (N_BUCKETS - max_exact)).astype(jnp.int32)
    large = jnp.minimum(large, N_BUCKETS - 1)
    return jnp.where(distance < max_exact, distance, large)


def _rms(x, g):
    return x * lax.rsqrt(jnp.mean(x * x, axis=-1, keepdims=True) + EPS) * g


def _inproj_pair_kernel(first_win_tile, x_ref, g_ref, w_ref, q_ref, k_ref, v_ref, kt_ref, vt_ref, u_ref):
    h = _rms(x_ref[...], g_ref[...])
    p = jnp.dot(h.astype(BF16), w_ref[...], preferred_element_type=F32)
    for j in range(MIX_A // LANES):
        q_ref[j] = p[:, LANES * j:LANES * (j + 1)] * SCALE
        k_ref[j] = p[:, MIX_A + LANES * j:MIX_A + LANES * (j + 1)]
        v_ref[j] = p[:, 2 * MIX_A + LANES * j:2 * MIX_A + LANES * (j + 1)]
    u_ref[...] = p[:, 3 * MIX_A:]

    @pl.when(pl.program_id(1) >= first_win_tile)
    def _window():
        kt_ref[...] = p[:, MIX_A:2 * MIX_A].T
        vt_ref[...] = p[:, 2 * MIX_A:3 * MIX_A].T


def _inproj_pair(x, g, w_bf16, tm, keep):
    b, t, d = x.shape
    npair = MIX_A // LANES
    first = (t - keep) // tm
    pair = jax.ShapeDtypeStruct((b, npair, t, LANES), F32)
    win = jax.ShapeDtypeStruct((b, MIX_A, keep), F32)
    tok = jax.ShapeDtypeStruct((b, t, MIX_A), F32)
    pair_spec = pl.BlockSpec((None, npair, tm, LANES), lambda bi, i: (bi, 0, i, 0))
    win_spec = pl.BlockSpec((None, MIX_A, tm), lambda bi, i: (bi, 0, jnp.maximum(i - first, 0)))
    tok_spec = pl.BlockSpec((None, tm, MIX_A), lambda bi, i: (bi, i, 0))
    return pl.pallas_call(
        functools.partial(_inproj_pair_kernel, first),
        grid=(b, t // tm),
        in_specs=[pl.BlockSpec((None, tm, d), lambda bi, i: (bi, i, 0)),
                  pl.BlockSpec((1, d), lambda bi, i: (0, 0)),
                  pl.BlockSpec((d, PROJ_COLS), lambda bi, i: (0, 0))],
        out_specs=[pair_spec, pair_spec, pair_spec, win_spec, win_spec, tok_spec],
        out_shape=[pair, pair, pair, win, win, tok],
        compiler_params=_cparams(2),
        name="inproj_prompt",
    )(x, g, w_bf16)


def _inproj_tok_kernel(x_ref, g_ref, w_ref, q_ref, k_ref, v_ref, u_ref):
    h = _rms(x_ref[...], g_ref[...])
    p = jnp.dot(h.astype(BF16), w_ref[...], preferred_element_type=F32)
    q_ref[...] = p[:, :MIX_A] * SCALE
    k_ref[...] = p[:, MIX_A:2 * MIX_A]
    v_ref[...] = p[:, 2 * MIX_A:3 * MIX_A]
    u_ref[...] = p[:, 3 * MIX_A:]


def _inproj_tok(x, g, w_bf16):
    n, d = x.shape
    out = jax.ShapeDtypeStruct((n, MIX_A), F32)
    return pl.pallas_call(
        _inproj_tok_kernel,
        out_shape=[out, out, out, out],
        compiler_params=pltpu.CompilerParams(vmem_limit_bytes=VMEM_LIMIT),
        name="inproj_sample",
    )(x, g, w_bf16)


def _prompt_bias_ids():
    r = jnp.arange(TAPS, dtype=jnp.int32)[:, None]
    kpos = jnp.arange(2 * TAPS, dtype=jnp.int32)[None, :] - TAPS
    rel = r - kpos
    valid = (rel >= 0) & (rel <= TAPS)
    ids = [jnp.where(valid, _t5_bucket(d * rel), -1) for d in DILATIONS]
    return jnp.stack(ids).astype(jnp.int32)


def _attn_prompt_kernel(rb_ref, ids_ref, q_ref, kc_ref, kp_ref, vc_ref, vp_ref, o_ref,
                        bias_scr, fm_scr, kcat, vcat, m0_scr, l0_scr, o0_scr, m1_scr, l1_scr, o1_scr,
                        m2_scr, l2_scr, o2_scr):
    bi = pl.program_id(0)
    p = pl.program_id(1)
    i = pl.program_id(2)
    blk = ATT_BLOCK
    npair = N_HEADS // 2

    @pl.when((bi == 0) & (p == 0) & (i == 0))
    def _build_bias():
        col = lax.broadcasted_iota(jnp.int32, (2 * TAPS, 2 * TAPS), 1)
        fm_scr[0] = jnp.zeros((2 * TAPS, 2 * TAPS), F32)
        fm_scr[1] = jnp.where(col < TAPS, NEG_INF, 0.0).astype(F32)
        for c in range(len(DILATIONS)):
            ids = ids_ref[c]

            def head_body(h, carry, ids=ids, c=c):
                tile = jnp.full((TAPS, 2 * TAPS), NEG_INF, F32)
                for bkt in range(N_BUCKETS):
                    tile = jnp.where(ids == bkt, rb_ref[bkt, h], tile)
                row0 = pl.multiple_of((h % 2) * TAPS, TAPS)
                bias_scr[c * npair + h // 2, pl.ds(row0, TAPS), :] = tile
                return carry

            lax.fori_loop(0, N_HEADS, head_body, 0)

    kcat[pl.ds(0, blk), :] = kp_ref[...]
    kcat[pl.ds(blk, blk), :] = kc_ref[...]
    vcat[pl.ds(0, blk), :] = vp_ref[...]
    vcat[pl.ds(blk, blk), :] = vc_ref[...]

    lane = lax.broadcasted_iota(jnp.int32, (1, LANES), 1)
    low = lane < HEAD_DIM
    nt = (((1,), (1,)), ((), ()))
    ones = jnp.ones((2 * TAPS, LANES), BF16)
    stats = ((m0_scr, l0_scr, o0_scr), (m1_scr, l1_scr, o1_scr), (m2_scr, l2_scr, o2_scr))

    def rows(ref, start, d):
        if d == 1:
            return ref[pl.ds(start, TAPS), :]
        return ref[pl.ds(start, TAPS, stride=d), :]

    def sub_block(c, d, qs, first, dst_rows):
        q = rows(q_ref, qs, d)
        q2 = jnp.concatenate([jnp.where(low, q, 0.0), jnp.where(low, 0.0, q)], axis=0).astype(BF16)
        k = jnp.concatenate([rows(kcat, blk + qs - d * TAPS, d), rows(kcat, blk + qs, d)],
                            axis=0).astype(BF16)
        v = jnp.concatenate([rows(vcat, blk + qs - d * TAPS, d), rows(vcat, blk + qs, d)],
                            axis=0).astype(BF16)
        s_ = lax.dot_general(q2, k, nt, preferred_element_type=F32) + bias_scr[c * npair + p]
        if first is not None:
            s_ = s_ + fm_scr[first]
        m = jnp.max(s_, axis=1, keepdims=True)
        pe = jnp.exp(s_ - m).astype(BF16)
        ol = jnp.dot(pe, jnp.concatenate([v, ones], axis=1), preferred_element_type=F32)
        o, l = ol[:, :LANES], ol[:, LANES:]
        m_ref, l_ref, a_ref = stats[c]
        m_ref[dst_rows, :] = jnp.where(low, m[:TAPS], m[TAPS:])
        l_ref[dst_rows, :] = jnp.where(low, l[:TAPS], l[TAPS:])
        a_ref[dst_rows, :] = jnp.where(low, o[:TAPS], o[TAPS:])

    at_start = jnp.where(i == 0, 1, 0)
    n_sub = blk // TAPS
    for c, d in enumerate(DILATIONS):
        n_grp = blk // (d * TAPS)

        def group_body(s8, carry, c=c, d=d, n_grp=n_grp):
            for u in range(ATT_UNROLL):
                s = s8 * ATT_UNROLL + u
                if n_grp >= ATT_UNROLL:
                    qs = pl.multiple_of(s * TAPS, TAPS)
                    first = jnp.where(s8 == 0, at_start, 0) if u == 0 else None
                    dst_rows = pl.ds(qs, TAPS)
                else:
                    r = s8 * (ATT_UNROLL // n_grp) + u // n_grp
                    g = u % n_grp
                    qs = r + d * TAPS * g
                    first = at_start if g == 0 else None
                    if n_grp > 1:
                        dst_rows = pl.ds(qs, TAPS, stride=d)
                    else:
                        dst_rows = pl.ds(pl.multiple_of(r * ATT_PITCH, 8), TAPS)
                sub_block(c, d, qs, first, dst_rows)
            return carry

        lax.fori_loop(0, n_sub // ATT_UNROLL, group_body, 0)

    d_last = DILATIONS[-1]

    def merge_body(j, carry):
        tok = pl.ds(pl.multiple_of(j * d_last, d_last), d_last)
        dil = pl.ds(j, d_last, stride=ATT_PITCH)
        m0, m1, m2 = m0_scr[tok, :], m1_scr[tok, :], m2_scr[dil, :]
        m = jnp.maximum(jnp.maximum(m0, m1), m2)
        a0, a1, a2 = jnp.exp(m0 - m), jnp.exp(m1 - m), jnp.exp(m2 - m)
        num = o0_scr[tok, :] * a0 + o1_scr[tok, :] * a1 + o2_scr[dil, :] * a2
        den = l0_scr[tok, :] * a0 + l1_scr[tok, :] * a1 + l2_scr[dil, :] * a2
        o_ref[tok, :] = num / den
        return carry

    lax.fori_loop(0, blk // d_last, merge_body, 0, unroll=4)


def _attn_prompt(q, k, v, rel_bias):
    b, npair, t, _ = q.shape
    blk = ATT_BLOCK
    cur = pl.BlockSpec((None, None, blk, LANES), lambda bi, p, i: (bi, p, i, 0))
    prev = pl.BlockSpec((None, None, blk, LANES), lambda bi, p, i: (bi, p, jnp.maximum(i - 1, 0), 0))
    n_tiles = len(DILATIONS) * npair
    stats = ([pltpu.VMEM((blk, LANES), F32)] * 6
             + [pltpu.VMEM((DILATIONS[-1] * ATT_PITCH, LANES), F32)] * 3)
    return pl.pallas_call(
        _attn_prompt_kernel,
        grid=(b, npair, t // blk),
        in_specs=[pl.BlockSpec(memory_space=pltpu.SMEM),
                  pl.BlockSpec((len(DILATIONS), TAPS, 2 * TAPS), lambda bi, p, i: (0, 0, 0)),
                  cur, cur, prev, cur, prev],
        out_specs=cur,
        out_shape=jax.ShapeDtypeStruct(q.shape, F32),
        scratch_shapes=[pltpu.VMEM((n_tiles, 2 * TAPS, 2 * TAPS), F32),
                        pltpu.VMEM((2, 2 * TAPS, 2 * TAPS), F32),
                        pltpu.VMEM((2 * blk, LANES), F32),
                        pltpu.VMEM((2 * blk, LANES), F32)] + stats,
        compiler_params=_cparams(3),
        name="attn_prompt",
    )(rel_bias, _prompt_bias_ids(), q, k, k, v, v)


DEC_T = 4
DEC_NEW_PAD = 128
DEC_SEQ_PER_STEP = 2
DEC_KEYS = MAX_WINDOW + DEC_NEW_PAD


def _decode_tables():
    qi = jnp.arange(DEC_T, dtype=jnp.int32)[:, None]
    rho = jnp.arange(MAX_WINDOW, dtype=jnp.int32)[None, :]
    dist_b = MAX_WINDOW + qi - rho
    mult_b = jnp.zeros_like(dist_b)
    for d in DILATIONS:
        mult_b = mult_b + (((dist_b % d) == 0) & (dist_b <= d * TAPS)).astype(jnp.int32)
    jj = jnp.arange(DEC_NEW_PAD, dtype=jnp.int32)[None, :]
    dist_n = qi - jj
    mult_n = jnp.where(dist_n == 0, len(DILATIONS), jnp.where((dist_n > 0) & (jj < DEC_T), 1, 0))
    dist = jnp.concatenate([dist_b, jnp.maximum(dist_n, 0)], axis=1)
    mult = jnp.concatenate([mult_b, mult_n.astype(jnp.int32)], axis=1)
    ids = jnp.where(mult > 0, _t5_bucket(dist), -1).astype(jnp.int32)
    ids = jnp.repeat(ids, N_HEADS, axis=0)
    mult = jnp.repeat(mult, N_HEADS, axis=0).astype(F32)
    return ids, mult


def _attn_decode_kernel(rbt_ref, ids_ref, mult_ref, q_ref, kn_ref, vn_ref, kt_ref, vt_ref, o_ref, bias_scr):
    n_rows = DEC_T * N_HEADS

    @pl.when(pl.program_id(0) == 0)
    def _build_bias():
        ids = ids_ref[...]
        tile = jnp.full((n_rows, DEC_KEYS), NEG_INF, F32)
        for bkt in range(N_BUCKETS):
            tile = jnp.where(ids == bkt, rbt_ref[:, bkt:bkt + 1], tile)
        bias_scr[...] = tile

    lane = lax.broadcasted_iota(jnp.int32, (N_HEADS, MIX_A), 1)
    head = lax.broadcasted_iota(jnp.int32, (N_HEADS, MIX_A), 0)
    hmask = (lane // HEAD_DIM) == head

    zpad = jnp.zeros((DEC_NEW_PAD - DEC_T, MIX_A), F32)
    nt = (((1,), (1,)), ((), ()))
    for j in range(q_ref.shape[0]):
        q = q_ref[j]
        qm = jnp.where(hmask[None], q[:, None, :], 0.0).reshape(n_rows, MIX_A).astype(BF16)
        kn = jnp.concatenate([kn_ref[j], zpad], axis=0).astype(BF16)
        vn = jnp.concatenate([vn_ref[j], zpad], axis=0).astype(BF16)
        kt = kt_ref[j].astype(BF16)
        vt = vt_ref[j].astype(BF16)
        s = jnp.concatenate([jnp.dot(qm, kt, preferred_element_type=F32),
                             lax.dot_general(qm, kn, nt, preferred_element_type=F32)], axis=1)
        s = s + bias_scr[...]
        m = jnp.max(s, axis=1, keepdims=True)
        pe = jnp.exp(s - m) * mult_ref[...]
        l = jnp.sum(pe, axis=1, keepdims=True)
        pb = pe.astype(BF16)
        o = (lax.dot_general(pb[:, :MAX_WINDOW], vt, nt, preferred_element_type=F32)
             + jnp.dot(pb[:, MAX_WINDOW:], vn, preferred_element_type=F32)) / l
        o3 = o.reshape(DEC_T, N_HEADS, MIX_A)
        o_ref[j] = jnp.sum(jnp.where(hmask[None], o3, 0.0), axis=1)


def _attn_decode(q, k_new, v_new, cache_kt, cache_vt, rel_bias):
    bd = q.shape[0]
    ids, mult = _decode_tables()
    rbt = jnp.tile(rel_bias.T, (DEC_T, 1))
    per = DEC_SEQ_PER_STEP if bd % DEC_SEQ_PER_STEP == 0 else 1
    new_spec = pl.BlockSpec((per, DEC_T, MIX_A), lambda b: (b, 0, 0))
    cache_spec = pl.BlockSpec((per, MIX_A, MAX_WINDOW), lambda b: (b, 0, 0))
    n_rows = DEC_T * N_HEADS
    const = lambda shape: pl.BlockSpec(shape, lambda b: (0, 0))
    return pl.pallas_call(
        _attn_decode_kernel,
        grid=(bd // per,),
        in_specs=[const((n_rows, N_BUCKETS)), const((n_rows, DEC_KEYS)), const((n_rows, DEC_KEYS)),
                  new_spec, new_spec, new_spec, cache_spec, cache_spec],
        out_specs=new_spec,
        out_shape=jax.ShapeDtypeStruct((bd, DEC_T, MIX_A), F32),
        scratch_shapes=[pltpu.VMEM((n_rows, DEC_KEYS), F32)],
        compiler_params=_cparams(1),
        name="attn_decode",
    )(rbt, ids, mult, q, k_new, v_new, cache_kt, cache_vt)


def _ssm_prep_kernel(are_ref, aim_ref, ldt_ref, btr_ref, bti_ref, cre_ref, cim_ref,
                     tit_ref, tinr_ref, tini_ref, toutr_ref, touti_ref, pw_ref, tintr_ref, tinti_ref):
    hi = lax.Precision.HIGHEST
    nt = (((1,), (1,)), ((), ()))
    a_re, a_im = are_ref[...], aim_ref[...]
    dt = jnp.exp(ldt_ref[...])
    decay = jnp.exp(a_re * dt)
    ab_re = decay * jnp.cos(a_im * dt)
    ab_im = decay * jnp.sin(a_im * dt)
    inv = 1.0 / (a_re * a_re + a_im * a_im)
    coef_re = ((ab_re - 1.0) * a_re + ab_im * a_im) * inv
    coef_im = (ab_im * a_re - (ab_re - 1.0) * a_im) * inv
    bt_re, bt_im = btr_ref[...], bti_ref[...]
    bb_re = coef_re * bt_re - coef_im * bt_im
    bb_im = coef_re * bt_im + coef_im * bt_re
    pw = [(jnp.ones_like(ab_re), jnp.zeros_like(ab_im))]
    for _ in range(CHUNK):
        pr, pi = pw[-1]
        pw.append((pr * ab_re - pi * ab_im, pr * ab_im + pi * ab_re))
    tin_re = jnp.concatenate(
        [pw[CHUNK - 1 - ti][0] * bb_re - pw[CHUNK - 1 - ti][1] * bb_im for ti in range(CHUNK)], axis=0).astype(BF16)
    tin_im = jnp.concatenate(
        [pw[CHUNK - 1 - ti][0] * bb_im + pw[CHUNK - 1 - ti][1] * bb_re for ti in range(CHUNK)], axis=0).astype(BF16)
    tinr_ref[...] = tin_re
    tini_ref[...] = tin_im
    p_i = lax.broadcasted_iota(jnp.int32, (SSM_STATE, SSM_STATE), 0)
    p_j = lax.broadcasted_iota(jnp.int32, (SSM_STATE, SSM_STATE), 1)
    eye = jnp.where(p_i == p_j, 1.0, 0.0).astype(BF16)
    tintr_ref[...] = lax.dot_general(eye, tin_re, nt, preferred_element_type=F32).astype(BF16)
    tinti_ref[...] = lax.dot_general(eye, tin_im, nt, preferred_element_type=F32).astype(BF16)
    c_re, c_im = cre_ref[...], cim_ref[...]
    cp_re = [c_re * pr - c_im * pi for pr, pi in pw]
    cp_im = [c_re * pi + c_im * pr for pr, pi in pw]
    toutr_ref[...] = jnp.concatenate(cp_re[1:], axis=0).astype(BF16)
    touti_ref[...] = (-jnp.concatenate(cp_im[1:], axis=0)).astype(BF16)
    kall = (lax.dot_general(jnp.concatenate(cp_re[:CHUNK], axis=0), bb_re, nt, precision=hi,
                            preferred_element_type=F32)
            - lax.dot_general(jnp.concatenate(cp_im[:CHUNK], axis=0), bb_im, nt, precision=hi,
                              preferred_element_type=F32))
    w = CHUNK * SSM_GROUP
    kb = kall.astype(BF16)
    ci_idx = lax.broadcasted_iota(jnp.int32, (SSM_GROUP, w), 0)
    col_idx = lax.broadcasted_iota(jnp.int32, (SSM_GROUP, w), 1)
    acc = jnp.zeros((w, w), F32)
    for ti in range(CHUNK):
        rows = ti * SSM_GROUP
        shifted = kb if ti == 0 else jnp.concatenate(
            [jnp.zeros((rows, SSM_GROUP), BF16), kb[:w - rows]], axis=0)
        place = jnp.where(col_idx == ci_idx + rows, 1.0, 0.0).astype(BF16)
        acc = acc + jnp.dot(shifted, place, preferred_element_type=F32)
    tit_ref[...] = acc.astype(BF16)
    zero = jnp.zeros_like(ab_re)
    pw_ref[...] = jnp.concatenate([pw[CHUNK][0], pw[CHUNK][1], pw[DEC_T][0], pw[DEC_T][1],
                                   zero, zero, zero, zero], axis=0)


def _ssm_prep(a_re, a_im, log_dt, b_re, b_im, c_re, c_im):
    g, p = a_re.shape
    w = CHUNK * SSM_GROUP
    row = lambda a: a[:, None, :]
    bt_re = jnp.transpose(b_re, (0, 2, 1))
    bt_im = jnp.transpose(b_im, (0, 2, 1))
    ldt = jnp.broadcast_to(log_dt[:, None, None], (g, 1, p))
    gspec = lambda shape: pl.BlockSpec((None,) + shape, lambda gi: (gi, 0, 0))
    return pl.pallas_call(
        _ssm_prep_kernel,
        grid=(g,),
        in_specs=[gspec((1, p)), gspec((1, p)), gspec((1, p)), gspec((SSM_GROUP, p)), gspec((SSM_GROUP, p)),
                  gspec((SSM_GROUP, p)), gspec((SSM_GROUP, p))],
        out_specs=[gspec((w, w)), gspec((w, p)), gspec((w, p)), gspec((w, p)), gspec((w, p)), gspec((8, p)),
                   gspec((p, w)), gspec((p, w))],
        out_shape=[jax.ShapeDtypeStruct((g, w, w), BF16)] + [jax.ShapeDtypeStruct((g, w, p), BF16)] * 4
                  + [jax.ShapeDtypeStruct((g, 8, p), F32)] + [jax.ShapeDtypeStruct((g, p, w), BF16)] * 2,
        compiler_params=_cparams(1),
        name="ssm_prep",
    )(row(a_re), row(a_im), ldt, bt_re, bt_im, c_re, c_im)


def _ssm_prompt_kernel(u_ref, tit_ref, tintr_ref, tinti_ref, toutr_ref, touti_ref, d_ref, pw_ref,
                       y_ref, sre_ref, sim_ref, ut_scr, yt_scr, pr_scr, pi_scr, sr_scr, si_scr):
    ngrp = LANES // SSM_GROUP
    nc = u_ref.shape[0] // CHUNK
    xt = [u_ref[pl.ds(ti, nc, stride=CHUNK), :].T for ti in range(CHUNK)]
    for j in range(ngrp):
        ut = jnp.concatenate([x[j * SSM_GROUP:(j + 1) * SSM_GROUP, :] for x in xt], axis=0).astype(BF16)
        ut_scr[j] = ut
        pr_scr[j] = jnp.dot(tintr_ref[j], ut, preferred_element_type=F32).T
        pi_scr[j] = jnp.dot(tinti_ref[j], ut, preferred_element_type=F32).T
    ar = [pw_ref[j, 0:1, :] for j in range(ngrp)]
    ai = [pw_ref[j, 1:2, :] for j in range(ngrp)]

    def step(c, carry):
        out = []
        for j in range(ngrp):
            sr, si = carry[j]
            sr_scr[j, pl.ds(c, 1), :] = sr
            si_scr[j, pl.ds(c, 1), :] = si
            out.append((ar[j] * sr - ai[j] * si + pr_scr[j, pl.ds(c, 1), :],
                        ar[j] * si + ai[j] * sr + pi_scr[j, pl.ds(c, 1), :]))
        return tuple(out)

    zero = jnp.zeros((1, SSM_STATE), F32)
    final = lax.fori_loop(0, nc, step, ((zero, zero),) * ngrp, unroll=8)
    for j in range(ngrp):
        sre_ref[j] = final[j][0]
        sim_ref[j] = final[j][1]
        yt = jnp.dot(tit_ref[j], ut_scr[j], preferred_element_type=F32)
        yt = yt + jnp.dot(toutr_ref[j], sr_scr[j].T.astype(BF16), preferred_element_type=F32)
        yt = yt + jnp.dot(touti_ref[j], si_scr[j].T.astype(BF16), preferred_element_type=F32)
        yt_scr[j] = yt
    d = d_ref[...]
    for to in range(CHUNK):
        v = jnp.concatenate([yt_scr[j, to * SSM_GROUP:(to + 1) * SSM_GROUP, :] for j in range(ngrp)], axis=0)
        rows = pl.ds(to, nc, stride=CHUNK)
        y_ref[rows, :] = v.T + d * u_ref[rows, :]


def _ssm_prompt(u, ops, d_skip):
    b, t, _ = u.shape
    p = SSM_STATE
    w = CHUNK * SSM_GROUP
    nc = t // CHUNK
    ngrp = LANES // SSM_GROUP
    g = N_GROUPS
    gspec = lambda shape: pl.BlockSpec((ngrp,) + shape, lambda bi, qi: (qi, 0, 0))
    io = pl.BlockSpec((None, t, LANES), lambda bi, qi: (bi, 0, qi))
    st = pl.BlockSpec((None, ngrp, 1, p), lambda bi, qi: (bi, qi, 0, 0))
    tit, _, _, toutr, touti, pw, tintr, tinti = ops
    return pl.pallas_call(
        _ssm_prompt_kernel,
        grid=(b, g // ngrp),
        in_specs=[io, gspec((w, w)), gspec((p, w)), gspec((p, w)), gspec((w, p)), gspec((w, p)),
                  pl.BlockSpec((None, 1, LANES), lambda bi, qi: (qi, 0, 0)), gspec((8, p))],
        out_specs=[io, st, st],
        out_shape=[jax.ShapeDtypeStruct(u.shape, F32),
                   jax.ShapeDtypeStruct((b, g, 1, p), F32), jax.ShapeDtypeStruct((b, g, 1, p), F32)],
        scratch_shapes=[pltpu.VMEM((ngrp, w, nc), BF16), pltpu.VMEM((ngrp, w, nc), F32)]
                       + [pltpu.VMEM((ngrp, nc, p), F32)] * 4,
        compiler_params=_cparams(2),
        name="ssm_prompt",
    )(u, tit, tintr, tinti, toutr, touti, d_skip.reshape(g // ngrp, 1, LANES), pw)


def _ssm_decode_kernel(u_ref, s0r_ref, s0i_ref, tit_ref, tinr_ref, tini_ref, toutr_ref, touti_ref, d_ref,
                       pw_ref, y_ref, sre_ref, sim_ref):
    nt = (((1,), (1,)), ((), ()))
    w = DEC_T * SSM_GROUP
    lo = (CHUNK - DEC_T) * SSM_GROUP
    u = u_ref[...]
    ub = u.astype(BF16)
    s0r = s0r_ref[...]
    s0i = s0i_ref[...]
    y = lax.dot_general(ub, tit_ref[0:w, 0:w], nt, preferred_element_type=F32)
    y = y + lax.dot_general(s0r.astype(BF16), toutr_ref[0:w, :], nt, preferred_element_type=F32)
    y = y + lax.dot_general(s0i.astype(BF16), touti_ref[0:w, :], nt, preferred_element_type=F32)
    y_ref[...] = y + d_ref[:, 0:w] * u
    ar = pw_ref[2:3, :]
    ai = pw_ref[3:4, :]
    sre_ref[...] = ar * s0r - ai * s0i + jnp.dot(ub, tinr_ref[lo:, :], preferred_element_type=F32)
    sim_ref[...] = ar * s0i + ai * s0r + jnp.dot(ub, tini_ref[lo:, :], preferred_element_type=F32)


def _ssm_decode(ug, s0_re, s0_im, ops, d_tile):
    g, bd, w = ug.shape
    p = SSM_STATE
    wc = CHUNK * SSM_GROUP
    gspec = lambda shape: pl.BlockSpec((None,) + shape, lambda gi: (gi, 0, 0))
    tit, tinr, tini, toutr, touti, pw = ops[:6]
    return pl.pallas_call(
        _ssm_decode_kernel,
        grid=(g,),
        in_specs=[gspec((bd, w)), gspec((bd, p)), gspec((bd, p)), gspec((wc, wc)), gspec((wc, p)),
                  gspec((wc, p)), gspec((wc, p)), gspec((wc, p)), gspec((1, wc)), gspec((8, p))],
        out_specs=[gspec((bd, w)), gspec((bd, p)), gspec((bd, p))],
        out_shape=[jax.ShapeDtypeStruct((g, bd, w), F32), jax.ShapeDtypeStruct((g, bd, p), F32),
                   jax.ShapeDtypeStruct((g, bd, p), F32)],
        compiler_params=_cparams(1),
        name="ssm_decode",
    )(ug, s0_re, s0_im, tit, tinr, tini, toutr, touti, d_tile, pw)


def _gelu_tanh(x):
    return 0.5 * x * (1.0 + jnp.tanh(math.sqrt(2.0 / math.pi) * (x + 0.044715 * (x * x * x))))


def _sigmoid(x):
    return 1.0 / (1.0 + jnp.exp(-x))


def _route(hn, wr_ref):
    n = hn.shape[0]
    h_hi = hn.astype(BF16)
    h_lo = (hn - h_hi.astype(F32)).astype(BF16)
    prod = jnp.dot(jnp.concatenate([h_hi, h_lo], axis=0), wr_ref[...], preferred_element_type=F32)
    logits = (prod[:n, :ROUTER_LANES] + prod[:n, ROUTER_LANES:]
              + prod[n:, :ROUTER_LANES] + prod[n:, ROUTER_LANES:])
    lidx = lax.broadcasted_iota(jnp.int32, (n, ROUTER_LANES), 1)
    is_e = lidx < N_EXPERTS
    is_g = jnp.logical_and(lidx >= N_EXPERTS, lidx < N_EXPERTS + N_EXPERT_GROUPS)
    gmax = jnp.max(jnp.where(is_g, logits, -jnp.inf), axis=1, keepdims=True)
    g_prob = 1.0 / jnp.sum(jnp.where(is_g, jnp.exp(logits - gmax), 0.0), axis=1, keepdims=True)
    g_sel = jnp.min(jnp.where(jnp.logical_and(is_g, logits == gmax), lidx - N_EXPERTS, N_EXPERT_GROUPS),
                    axis=1, keepdims=True)
    in_grp = jnp.logical_and(is_e, (lidx // EXPERTS_PER_GROUP) == g_sel)
    l1 = jnp.max(jnp.where(in_grp, logits, -jnp.inf), axis=1, keepdims=True)
    i1 = jnp.min(jnp.where(jnp.logical_and(in_grp, logits == l1), lidx, ROUTER_LANES), axis=1, keepdims=True)
    rest = jnp.logical_and(in_grp, lidx != i1)
    l2 = jnp.max(jnp.where(rest, logits, -jnp.inf), axis=1, keepdims=True)
    i2 = jnp.min(jnp.where(jnp.logical_and(rest, logits == l2), lidx, ROUTER_LANES), axis=1, keepdims=True)
    e2 = jnp.exp(l2 - l1)
    w1 = g_prob / (1.0 + e2)
    w2 = g_prob * e2 / (1.0 + e2)
    return jnp.where(lidx == i1, w1, 0.0) + jnp.where(lidx == i2, w2, 0.0)


def _post_body(oa, ys, x, ga_ref, gb_ref, wglu_ref, bglu_ref, wout_ref, gf_ref, wr_ref,
               x1_ref, hn_ref, gates_ref):
    z = _gelu_tanh(ys)
    gate = _sigmoid(jnp.dot(z.astype(BF16), wglu_ref[...], preferred_element_type=F32) + bglu_ref[...])
    ob = z * gate
    mixed = jnp.concatenate([_rms(oa, ga_ref[...]), _rms(ob, gb_ref[...])], axis=1).astype(BF16)
    x1 = x + jnp.dot(mixed, wout_ref[...], preferred_element_type=F32)
    x1_ref[...] = x1
    hn = _rms(x1, gf_ref[...])
    bits = lax.bitcast_convert_type(hn.astype(BF16).astype(F32), jnp.int32)
    half = hn.shape[1] // 2
    hn_ref[...] = (lax.shift_right_logical(bits[:, :half], 16) | (bits[:, half:] & jnp.int32(-65536)))
    gates_ref[...] = _route(hn, wr_ref)


def _post_pair_kernel(oa_ref, ys_ref, x_ref, *rest):
    oa = jnp.concatenate([oa_ref[j] for j in range(MIX_A // LANES)], axis=1)
    _post_body(oa, ys_ref[...], x_ref[...], *rest)


def _post_tok_kernel(oa_ref, ys_ref, x_ref, *rest):
    _post_body(oa_ref[...], ys_ref[...], x_ref[...], *rest)


def _post_mix(oa, ys, x, w, tm, pair):
    b, t, d = x.shape
    row = lambda width: pl.BlockSpec((None, tm, width), lambda bi, i: (bi, i, 0))
    const = lambda a: pl.BlockSpec(a.shape, lambda bi, i: (0,) * a.ndim)
    oa_spec = (pl.BlockSpec((None, MIX_A // LANES, tm, LANES), lambda bi, i: (bi, 0, i, 0)) if pair
               else row(MIX_A))
    weights = [w['g_out_a'], w['g_out_b'], w['w_glu'], w['b_glu'], w['w_out'], w['g_ffn'], w['w_router']]
    return pl.pallas_call(
        _post_pair_kernel if pair else _post_tok_kernel,
        grid=(b, t // tm),
        in_specs=[oa_spec, row(MIX_B), row(d)] + [const(a) for a in weights],
        out_specs=[row(d), row(d // 2), row(ROUTER_LANES)],
        out_shape=[jax.ShapeDtypeStruct((b, t, d), F32), jax.ShapeDtypeStruct((b, t, d // 2), jnp.int32),
                   jax.ShapeDtypeStruct((b, t, ROUTER_LANES), F32)],
        compiler_params=_cparams(2),
        name="post_mix_prompt" if pair else "post_mix_sample",
    )(oa, ys, x, *weights)


MOE_BLOCK = 2048
MOE_ROWS = 128
MOE_ALIGN = 8
MOE_EXPERTS_PER_STEP = 2
MOE_OUT_PITCH = 136
MOE_VMEM_LIMIT = 60 * 1024 * 1024
PLAN_CHUNK = 256


def _moe_rows(tb):
    rows = 2 * tb + N_EXPERTS * (MOE_ALIGN - 1) + MOE_ROWS
    return -(-rows // MOE_ALIGN) * MOE_ALIGN


def _moe_plan_kernel(gates_ref, plan_ref, meta_ref):
    tb = gates_ref.shape[0]
    gates = gates_ref[...]
    hot = gates > 0.0
    onehot = jnp.where(hot, 1.0, 0.0).astype(BF16)
    ch = min(PLAN_CHUNK, tb)
    r_i = lax.broadcasted_iota(jnp.int32, (ch, ch), 0)
    c_i = lax.broadcasted_iota(jnp.int32, (ch, ch), 1)
    earlier = jnp.where(c_i < r_i, 1.0, 0.0).astype(BF16)
    carry = jnp.zeros((1, ROUTER_LANES), F32)
    ranks = []
    for k in range(tb // ch):
        oh = onehot[k * ch:(k + 1) * ch]
        ranks.append(jnp.dot(earlier, oh, preferred_element_type=F32) + carry)
        carry = carry + jnp.sum(oh.astype(F32), axis=0, keepdims=True)
    rank = jnp.concatenate(ranks, axis=0)
    seg = jnp.floor((carry + (MOE_ALIGN - 1.0)) * (1.0 / MOE_ALIGN)) * MOE_ALIGN
    l_i = lax.broadcasted_iota(jnp.int32, (ROUTER_LANES, ROUTER_LANES), 0)
    l_j = lax.broadcasted_iota(jnp.int32, (ROUTER_LANES, ROUTER_LANES), 1)
    before = jnp.where(l_i < l_j, 1.0, 0.0).astype(F32)
    offs = jnp.dot(jnp.broadcast_to(seg, (8, ROUTER_LANES)), before, precision=lax.Precision.HIGHEST,
                   preferred_element_type=F32)[0:1]
    pos = rank + offs
    lane = lax.broadcasted_iota(jnp.int32, (tb, ROUTER_LANES), 1)
    lane_a = jnp.min(jnp.where(hot, lane, ROUTER_LANES), axis=1, keepdims=True)
    lane_b = jnp.max(jnp.where(hot, lane, -1), axis=1, keepdims=True)
    pick = lambda sel, val: jnp.sum(jnp.where(sel, val, 0.0), axis=1, keepdims=True)
    sel_a, sel_b = lane == lane_a, lane == lane_b
    z = (jnp.where(lane == 0, pick(sel_a, pos), 0.0) + jnp.where(lane == 1, pick(sel_b, pos), 0.0)
         + jnp.where(lane == 2, pick(sel_a, gates), 0.0) + jnp.where(lane == 3, pick(sel_b, gates), 0.0))
    plan_ref[...] = z.T[0:8, :]
    meta_ref[...] = jnp.concatenate([offs, carry, jnp.zeros((6, ROUTER_LANES), F32)], axis=0)


def _moe_plan(gates, tb):
    n = gates.shape[0]
    nb = n // tb
    return pl.pallas_call(
        _moe_plan_kernel,
        grid=(nb,),
        in_specs=[pl.BlockSpec((tb, ROUTER_LANES), lambda i: (i, 0))],
        out_specs=[pl.BlockSpec((None, 8, tb), lambda i: (i, 0, 0)),
                   pl.BlockSpec((None, 8, ROUTER_LANES), lambda i: (i, 0, 0))],
        out_shape=[jax.ShapeDtypeStruct((nb, 8, tb), F32), jax.ShapeDtypeStruct((nb, 8, ROUTER_LANES), F32)],
        compiler_params=_cparams(1),
        name="moe_plan",
    )(gates)


def _moe_kernel(offs_ref, cnts_ref, pos_ref, w_ref, hn_ref, x1_ref, wg_ref, wu_ref, wd_ref, gfin_ref,
                y_ref, xs_scr, gcol_scr, otile_scr, yacc_scr, tok_scr):
    blk = pl.program_id(0)
    step = pl.program_id(1)
    tb = hn_ref.shape[0]
    n_lt = D_MODEL // LANES

    @pl.when(step == 0)
    def _group_rows():
        xs_scr[...] = jnp.zeros_like(xs_scr)
        gcol_scr[...] = jnp.zeros_like(gcol_scr)
        yacc_scr[...] = jnp.zeros_like(yacc_scr)

        def clear_pad(ex, c):
            end = offs_ref[blk, ex] + cnts_ref[blk, ex]
            for k in range(MOE_ALIGN - 1):
                tok_scr[end + k] = 0
            return c

        lax.fori_loop(0, N_EXPERTS, clear_pad, 0)

        def place(t8, c):
            for u in range(8):
                t = t8 * 8 + u
                row = hn_ref[pl.ds(t, 1), :]
                for s in range(2):
                    p = pos_ref[s, t]
                    xs_scr[pl.ds(p, 1), :] = row
                    gcol_scr[pl.ds(p, 1), :] = jnp.full((1, LANES), w_ref[s, t], F32)
                    tok_scr[p] = t
            return c

        lax.fori_loop(0, tb // 8, place, 0)

    def expert(k, carry):
        e = step * MOE_EXPERTS_PER_STEP + k
        off = offs_ref[blk, e]
        cnt = cnts_ref[blk, e]

        def tile(c, cc):
            r0 = pl.multiple_of(off + c * MOE_ROWS, MOE_ALIGN)
            xw = xs_scr[pl.ds(r0, MOE_ROWS), :]
            x = jnp.concatenate(
                [lax.bitcast_convert_type(xw << 16, F32).astype(BF16),
                 lax.bitcast_convert_type(xw & jnp.int32(-65536), F32).astype(BF16)], axis=1)
            g = gcol_scr[pl.ds(r0, MOE_ROWS), :]
            a = jnp.dot(x, wg_ref[k], preferred_element_type=F32)
            u = jnp.dot(x, wu_ref[k], preferred_element_type=F32)
            act = (a * _sigmoid(a)) * u * jnp.concatenate([g] * (D_EXPERT // LANES), axis=1)
            out = jnp.dot(act.astype(BF16), wd_ref[k], preferred_element_type=F32)
            for s in range(n_lt):
                otile_scr[pl.ds(s * MOE_OUT_PITCH, MOE_ROWS), :] = out[:, s * LANES:(s + 1) * LANES]
            valid = jnp.minimum(MOE_ROWS, cnt - c * MOE_ROWS)

            def add8(j8, c3):
                base = pl.multiple_of(j8 * 8, 8)
                dst = [pl.ds(pl.multiple_of(tok_scr[r0 + base + v] * n_lt, n_lt), n_lt) for v in range(8)]
                rows = [yacc_scr[dst[v], :] + otile_scr[pl.ds(base + v, n_lt, stride=MOE_OUT_PITCH), :]
                        for v in range(8)]
                for v in range(8):
                    yacc_scr[dst[v], :] = rows[v]
                return c3

            lax.fori_loop(0, (valid + 7) // 8, add8, 0)
            return cc

        lax.fori_loop(0, (cnt + MOE_ROWS - 1) // MOE_ROWS, tile, 0)
        return carry

    lax.fori_loop(0, MOE_EXPERTS_PER_STEP, expert, 0)

    @pl.when(step == pl.num_programs(1) - 1)
    def _fin():
        moe = jnp.concatenate([yacc_scr[pl.ds(s, tb, stride=n_lt), :] for s in range(n_lt)], axis=1)
        y_ref[...] = _rms(x1_ref[...] + moe, gfin_ref[...])


def _moe(hn_packed, gates, x1, wg, wu, wd, g_final, tb):
    n, d = x1.shape
    nb = n // tb
    plan, meta = _moe_plan(gates, tb)
    pos = plan[:, 0:2, :].astype(jnp.int32)
    wts = plan[:, 2:4, :]
    offs = meta[:, 0, :N_EXPERTS].astype(jnp.int32)
    cnts = meta[:, 1, :N_EXPERTS].astype(jnp.int32)
    p_rows = _moe_rows(tb)
    per = MOE_EXPERTS_PER_STEP
    n_lt = d // LANES
    smem = lambda: pl.BlockSpec((None, 2, tb), lambda i, e, *_: (i, 0, 0), memory_space=pltpu.SMEM)
    once = dict(pipeline_mode=pl.Buffered(1))
    grid_spec = pltpu.PrefetchScalarGridSpec(
        num_scalar_prefetch=2,
        grid=(nb, N_EXPERTS // per),
        in_specs=[smem(), smem(),
                  pl.BlockSpec((tb, d // 2), lambda i, e, *_: (i, 0), **once),
                  pl.BlockSpec((tb, d), lambda i, e, *_: (i, 0), **once),
                  pl.BlockSpec((per, d, D_EXPERT), lambda i, e, *_: (e, 0, 0)),
                  pl.BlockSpec((per, d, D_EXPERT), lambda i, e, *_: (e, 0, 0)),
                  pl.BlockSpec((per, D_EXPERT, d), lambda i, e, *_: (e, 0, 0)),
                  pl.BlockSpec((1, d), lambda i, e, *_: (0, 0))],
        out_specs=pl.BlockSpec((tb, d), lambda i, e, *_: (i, 0)),
        scratch_shapes=[pltpu.VMEM((p_rows, d // 2), jnp.int32),
                        pltpu.VMEM((p_rows, LANES), F32),
                        pltpu.VMEM((n_lt * MOE_OUT_PITCH, LANES), F32),
                        pltpu.VMEM((tb * n_lt, LANES), F32),
                        pltpu.SMEM((p_rows,), jnp.int32)])
    return pl.pallas_call(
        _moe_kernel,
        grid_spec=grid_spec,
        out_shape=jax.ShapeDtypeStruct((n, d), F32),
        compiler_params=pltpu.CompilerParams(dimension_semantics=("arbitrary", "arbitrary"),
                                             vmem_limit_bytes=MOE_VMEM_LIMIT),
        name="moe",
    )(offs, cnts, pos, wts, hn_packed, x1, wg, wu, wd, g_final)


def kernel(x_prompt, x_sample, cache_k, cache_v, state_ssm_re, state_ssm_im, rel_bias, g_mix, w_in, g_out_a, g_out_b, w_out, ssm_a_re, ssm_a_im, ssm_log_dt, ssm_b_re, ssm_b_im, ssm_c_re, ssm_c_im, ssm_d, w_glu, b_glu, g_ffn, w_router_group, w_router_expert, w_expert_gate, w_expert_up, w_expert_down, g_final):
    depth = g_mix.shape[0]
    assert depth == 1, "kernel is written for the single-layer configuration of the problem"
    l = 0
    b, t, d = x_prompt.shape
    bd, ts, _ = x_sample.shape
    assert ts == DEC_T and t % ATT_BLOCK == 0 and cache_k.shape[2] == MAX_WINDOW
    keep = min(MAX_WINDOW, t)

    w_in_b = w_in[l].astype(BF16)
    gm = g_mix[l][None, :]
    w_router = jnp.concatenate(
        [jnp.transpose(w_router_expert[l], (1, 0, 2)).reshape(d, N_EXPERTS), w_router_group[l],
         jnp.zeros((d, ROUTER_LANES - N_EXPERTS - N_EXPERT_GROUPS), F32)], axis=1)
    w_router_hi = w_router.astype(BF16)
    w_router_lo = (w_router - w_router_hi.astype(F32)).astype(BF16)
    post_w = dict(
        g_out_a=g_out_a[l][None, :], g_out_b=g_out_b[l][None, :], w_glu=w_glu[l].astype(BF16),
        b_glu=b_glu[l][None, :], w_out=w_out[l].astype(BF16), g_ffn=g_ffn[l][None, :],
        w_router=jnp.concatenate([w_router_hi, w_router_lo], axis=1))
    wg = w_expert_gate[l].astype(BF16)
    wu = w_expert_up[l].astype(BF16)
    wd = w_expert_down[l].astype(BF16)
    gfin = g_final[None, :]
    ssm_ops = _ssm_prep(ssm_a_re[l], ssm_a_im[l], ssm_log_dt[l], ssm_b_re[l], ssm_b_im[l],
                        ssm_c_re[l], ssm_c_im[l])
    d_tile = jnp.tile(ssm_d[l].reshape(N_GROUPS, 1, SSM_GROUP), (1, 1, CHUNK))

    qp, kp, vp, kt_win, vt_win, u_tok = _inproj_pair(x_prompt, gm, w_in_b, tm=512, keep=keep)
    oa_p = _attn_prompt(qp, kp, vp, rel_bias)
    ys_p, rp, ip = _ssm_prompt(u_tok, ssm_ops, ssm_d[l])
    x1_p, hn_p, gates_p = _post_mix(oa_p, ys_p, x_prompt, post_w, tm=512, pair=True)
    y_p = _moe(hn_p.reshape(b * t, d // 2), gates_p.reshape(b * t, ROUTER_LANES), x1_p.reshape(b * t, d),
               wg, wu, wd, gfin, tb=min(MOE_BLOCK, b * t))

    n_s = bd * ts
    qs, ks, vs, us = _inproj_tok(x_sample.reshape(n_s, d), gm, w_in_b)
    ckt = jnp.transpose(cache_k[l], (0, 2, 3, 1)).reshape(bd, MIX_A, MAX_WINDOW)
    cvt = jnp.transpose(cache_v[l], (0, 2, 3, 1)).reshape(bd, MIX_A, MAX_WINDOW)
    oa_s = _attn_decode(qs.reshape(bd, ts, MIX_A), ks.reshape(bd, ts, MIX_A), vs.reshape(bd, ts, MIX_A),
                        ckt, cvt, rel_bias)
    ugs = jnp.transpose(us.reshape(bd, ts, N_GROUPS, SSM_GROUP), (2, 0, 1, 3)).reshape(N_GROUPS, bd, ts * SSM_GROUP)
    s0r = jnp.transpose(state_ssm_re[l], (1, 0, 2))
    s0i = jnp.transpose(state_ssm_im[l], (1, 0, 2))
    ygs, rs, is_ = _ssm_decode(ugs, s0r, s0i, ssm_ops, d_tile)
    ys_s = jnp.transpose(ygs.reshape(N_GROUPS, bd, ts, SSM_GROUP), (1, 2, 0, 3)).reshape(n_s, MIX_B)
    x1_s, hn_s, gates_s = _post_mix(oa_s.reshape(1, n_s, MIX_A), ys_s.reshape(1, n_s, MIX_B),
                                    x_sample.reshape(1, n_s, d), post_w, tm=n_s, pair=False)
    y_s = _moe(hn_s.reshape(n_s, d // 2), gates_s.reshape(n_s, ROUTER_LANES), x1_s.reshape(n_s, d),
               wg, wu, wd, gfin, tb=n_s)

    y_prompt = y_p.reshape(b, t, d)
    y_sample = y_s.reshape(bd, ts, d)
    k_win = jnp.transpose(kt_win.reshape(b, N_HEADS, HEAD_DIM, keep), (0, 3, 1, 2))[None]
    v_win = jnp.transpose(vt_win.reshape(b, N_HEADS, HEAD_DIM, keep), (0, 3, 1, 2))[None]
    k_new = ks.reshape(1, bd, ts, N_HEADS, HEAD_DIM)
    v_new = vs.reshape(1, bd, ts, N_HEADS, HEAD_DIM)
    return (y_prompt, y_sample, k_win, v_win, k_new, v_new,
            rp.reshape(1, b, N_GROUPS, SSM_STATE), ip.reshape(1, b, N_GROUPS, SSM_STATE),
            jnp.transpose(rs, (1, 0, 2))[None], jnp.transpose(is_, (1, 0, 2))[None])
```

```python
import functools
import math

import jax
import jax.numpy as jnp
import numpy as np
from jax import lax
from jax.experimental import pallas as pl
from jax.experimental.pallas import tpu as pltpu

F32 = jnp.float32
BF16 = jnp.bfloat16

D_MODEL = 1024
HEAD_DIM = 64
MIX_A = 512
N_HEADS = 8
MIX_B = 512
SSM_GROUP = 16
N_GROUPS = 32
SSM_STATE = 64
PROJ_COLS = 3 * MIX_A + MIX_B
DILATIONS = (1, 4, 16)
TAPS = 128
MAX_WINDOW = 2048
N_BUCKETS = 32
N_EXPERT_GROUPS = 4
EXPERTS_PER_GROUP = 8
N_EXPERTS = 32
D_EXPERT = 256
EPS = 1e-6
NEG_INF = -1e30
SCALE = HEAD_DIM ** -0.5

LANES = 128
ROUTER_LANES = 128
CHUNK = 16
ATT_BLOCK = 2048
ATT_UNROLL = 16
ATT_PITCH = 136
VMEM_LIMIT = 56 * 1024 * 1024


def _cparams(n_axes):
    return pltpu.CompilerParams(dimension_semantics=("arbitrary",) * n_axes,
                                vmem_limit_bytes=VMEM_LIMIT)


def _t5_bucket(distance):
    max_exact = N_BUCKETS // 2
    nf = jnp.maximum(distance, 1).astype(F32)
    large = max_exact + (jnp.log(nf / max_exact) / math.log(MAX_WINDOW / max_exact)
                         * (N_BUCKETS - max_exact)).astype(jnp.int32)
    large = jnp.minimum(large, N_BUCKETS - 1)
    return jnp.where(distance < max_exact, distance, large)


def _rms(x, g):
    return x * lax.rsqrt(jnp.mean(x * x, axis=-1, keepdims=True) + EPS) * g


def _inproj_pair_kernel(first_win_tile, x_ref, g_ref, w_ref, q_ref, k_ref, v_ref, kt_ref, vt_ref, u_ref):
    h = _rms(x_ref[...], g_ref[...])
    p = jnp.dot(h.astype(BF16), w_ref[...], preferred_element_type=F32)
    for j in range(MIX_A // LANES):
        q_ref[j] = p[:, LANES * j:LANES * (j + 1)] * SCALE
        k_ref[j] = p[:, MIX_A + LANES * j:MIX_A + LANES * (j + 1)]
        v_ref[j] = p[:, 2 * MIX_A + LANES * j:2 * MIX_A + LANES * (j + 1)]
    u_ref[...] = p[:, 3 * MIX_A:]

    @pl.when(pl.program_id(1) >= first_win_tile)
    def _window():
        kt_ref[...] = p[:, MIX_A:2 * MIX_A].T
        vt_ref[...] = p[:, 2 * MIX_A:3 * MIX_A].T


def _inproj_pair(x, g, w_bf16, tm, keep):
    b, t, d = x.shape
    npair = MIX_A // LANES
    first = (t - keep) // tm
    pair = jax.ShapeDtypeStruct((b, npair, t, LANES), F32)
    win = jax.ShapeDtypeStruct((b, MIX_A, keep), F32)
    tok = jax.ShapeDtypeStruct((b, t, MIX_A), F32)
    pair_spec = pl.BlockSpec((None, npair, tm, LANES), lambda bi, i: (bi, 0, i, 0))
    win_spec = pl.BlockSpec((None, MIX_A, tm), lambda bi, i: (bi, 0, jnp.maximum(i - first, 0)))
    tok_spec = pl.BlockSpec((None, tm, MIX_A), lambda bi, i: (bi, i, 0))
    return pl.pallas_call(
        functools.partial(_inproj_pair_kernel, first),
        grid=(b, t // tm),
        in_specs=[pl.BlockSpec((None, tm, d), lambda bi, i: (bi, i, 0)),
                  pl.BlockSpec((1, d), lambda bi, i: (0, 0)),
                  pl.BlockSpec((d, PROJ_COLS), lambda bi, i: (0, 0))],
        out_specs=[pair_spec, pair_spec, pair_spec, win_spec, win_spec, tok_spec],
        out_shape=[pair, pair, pair, win, win, tok],
        compiler_params=_cparams(2),
        name="inproj_prompt",
    )(x, g, w_bf16)


def _inproj_tok_kernel(x_ref, g_ref, w_ref, q_ref, k_ref, v_ref, u_ref):
    h = _rms(x_ref[...], g_ref[...])
    p = jnp.dot(h.astype(BF16), w_ref[...], preferred_element_type=F32)
    q_ref[...] = p[:, :MIX_A] * SCALE
    k_ref[...] = p[:, MIX_A:2 * MIX_A]
    v_ref[...] = p[:, 2 * MIX_A:3 * MIX_A]
    u_ref[...] = p[:, 3 * MIX_A:]


def _inproj_tok(x, g, w_bf16):
    n, d = x.shape
    out = jax.ShapeDtypeStruct((n, MIX_A), F32)
    return pl.pallas_call(
        _inproj_tok_kernel,
        out_shape=[out, out, out, out],
        compiler_params=pltpu.CompilerParams(vmem_limit_bytes=VMEM_LIMIT),
        name="inproj_sample",
    )(x, g, w_bf16)


def _prompt_bias_ids():
    r = jnp.arange(TAPS, dtype=jnp.int32)[:, None]
    kpos = jnp.arange(2 * TAPS, dtype=jnp.int32)[None, :] - TAPS
    rel = r - kpos
    valid = (rel >= 0) & (rel <= TAPS)
    ids = [jnp.where(valid, _t5_bucket(d * rel), -1) for d in DILATIONS]
    return jnp.stack(ids).astype(jnp.int32)


def _attn_prompt_kernel(rb_ref, ids_ref, q_ref, kc_ref, kp_ref, vc_ref, vp_ref, o_ref,
                        bias_scr, fm_scr, kcat, vcat, m0_scr, l0_scr, o0_scr, m1_scr, l1_scr, o1_scr,
                        m2_scr, l2_scr, o2_scr):
    bi = pl.program_id(0)
    p = pl.program_id(1)
    i = pl.program_id(2)
    blk = ATT_BLOCK
    npair = N_HEADS // 2

    @pl.when((bi == 0) & (p == 0) & (i == 0))
    def _build_bias():
        col = lax.broadcasted_iota(jnp.int32, (2 * TAPS, 2 * TAPS), 1)
        fm_scr[0] = jnp.zeros((2 * TAPS, 2 * TAPS), F32)
        fm_scr[1] = jnp.where(col < TAPS, NEG_INF, 0.0).astype(F32)
        for c in range(len(DILATIONS)):
            ids = ids_ref[c]

            def head_body(h, carry, ids=ids, c=c):
                tile = jnp.full((TAPS, 2 * TAPS), NEG_INF, F32)
                for bkt in range(N_BUCKETS):
                    tile = jnp.where(ids == bkt, rb_ref[bkt, h], tile)
                row0 = pl.multiple_of((h % 2) * TAPS, TAPS)
                bias_scr[c * npair + h // 2, pl.ds(row0, TAPS), :] = tile
                return carry

            lax.fori_loop(0, N_HEADS, head_body, 0)

    kcat[pl.ds(0, blk), :] = kp_ref[...]
    kcat[pl.ds(blk, blk), :] = kc_ref[...]
    vcat[pl.ds(0, blk), :] = vp_ref[...]
    vcat[pl.ds(blk, blk), :] = vc_ref[...]

    lane = lax.broadcasted_iota(jnp.int32, (1, LANES), 1)
    low = lane < HEAD_DIM
    nt = (((1,), (1,)), ((), ()))
    ones = jnp.ones((2 * TAPS, LANES), BF16)
    stats = ((m0_scr, l0_scr, o0_scr), (m1_scr, l1_scr, o1_scr), (m2_scr, l2_scr, o2_scr))

    def rows(ref, start, d):
        if d == 1:
            return ref[pl.ds(start, TAPS), :]
        return ref[pl.ds(start, TAPS, stride=d), :]

    def sub_block(c, d, qs, first, dst_rows):
        q = rows(q_ref, qs, d)
        q2 = jnp.concatenate([jnp.where(low, q, 0.0), jnp.where(low, 0.0, q)], axis=0).astype(BF16)
        k = jnp.concatenate([rows(kcat, blk + qs - d * TAPS, d), rows(kcat, blk + qs, d)],
                            axis=0).astype(BF16)
        v = jnp.concatenate([rows(vcat, blk + qs - d * TAPS, d), rows(vcat, blk + qs, d)],
                            axis=0).astype(BF16)
        s_ = lax.dot_general(q2, k, nt, preferred_element_type=F32) + bias_scr[c * npair + p]
        if first is not None:
            s_ = s_ + fm_scr[first]
        m = jnp.max(s_, axis=1, keepdims=True)
        pe = jnp.exp(s_ - m).astype(BF16)
        ol = jnp.dot(pe, jnp.concatenate([v, ones], axis=1), preferred_element_type=F32)
        o, l = ol[:, :LANES], ol[:, LANES:]
        m_ref, l_ref, a_ref = stats[c]
        m_ref[dst_rows, :] = jnp.where(low, m[:TAPS], m[TAPS:])
        l_ref[dst_rows, :] = jnp.where(low, l[:TAPS], l[TAPS:])
        a_ref[dst_rows, :] = jnp.where(low, o[:TAPS], o[TAPS:])

    at_start = jnp.where(i == 0, 1, 0)
    n_sub = blk // TAPS
    for c, d in enumerate(DILATIONS):
        n_grp = blk // (d * TAPS)

        def group_body(s8, carry, c=c, d=d, n_grp=n_grp):
            for u in range(ATT_UNROLL):
                s = s8 * ATT_UNROLL + u
                if n_grp >= ATT_UNROLL:
                    qs = pl.multiple_of(s * TAPS, TAPS)
                    first = jnp.where(s8 == 0, at_start, 0) if u == 0 else None
                    dst_rows = pl.ds(qs, TAPS)
                else:
                    r = s8 * (ATT_UNROLL // n_grp) + u // n_grp
                    g = u % n_grp
                    qs = r + d * TAPS * g
                    first = at_start if g == 0 else None
                    if n_grp > 1:
                        dst_rows = pl.ds(qs, TAPS, stride=d)
                    else:
                        dst_rows = pl.ds(pl.multiple_of(r * ATT_PITCH, 8), TAPS)
                sub_block(c, d, qs, first, dst_rows)
            return carry

        lax.fori_loop(0, n_sub // ATT_UNROLL, group_body, 0)

    d_last = DILATIONS[-1]

    def merge_body(j, carry):
        tok = pl.ds(pl.multiple_of(j * d_last, d_last), d_last)
        dil = pl.ds(j, d_last, stride=ATT_PITCH)
        m0, m1, m2 = m0_scr[tok, :], m1_scr[tok, :], m2_scr[dil, :]
        m = jnp.maximum(jnp.maximum(m0, m1), m2)
        a0, a1, a2 = jnp.exp(m0 - m), jnp.exp(m1 - m), jnp.exp(m2 - m)
        num = o0_scr[tok, :] * a0 + o1_scr[tok, :] * a1 + o2_scr[dil, :] * a2
        den = l0_scr[tok, :] * a0 + l1_scr[tok, :] * a1 + l2_scr[dil, :] * a2
        o_ref[tok, :] = num / den
        return carry

    lax.fori_loop(0, blk // d_last, merge_body, 0, unroll=4)


def _attn_prompt(q, k, v, rel_bias):
    b, npair, t, _ = q.shape
    blk = ATT_BLOCK
    cur = pl.BlockSpec((None, None, blk, LANES), lambda bi, p, i: (bi, p, i, 0))
    prev = pl.BlockSpec((None, None, blk, LANES), lambda bi, p, i: (bi, p, jnp.maximum(i - 1, 0), 0))
    n_tiles = len(DILATIONS) * npair
    stats = ([pltpu.VMEM((blk, LANES), F32)] * 6
             + [pltpu.VMEM((DILATIONS[-1] * ATT_PITCH, LANES), F32)] * 3)
    return pl.pallas_call(
        _attn_prompt_kernel,
        grid=(b, npair, t // blk),
        in_specs=[pl.BlockSpec(memory_space=pltpu.SMEM),
                  pl.BlockSpec((len(DILATIONS), TAPS, 2 * TAPS), lambda bi, p, i: (0, 0, 0)),
                  cur, cur, prev, cur, prev],
        out_specs=cur,
        out_shape=jax.ShapeDtypeStruct(q.shape, F32),
        scratch_shapes=[pltpu.VMEM((n_tiles, 2 * TAPS, 2 * TAPS), F32),
                        pltpu.VMEM((2, 2 * TAPS, 2 * TAPS), F32),
                        pltpu.VMEM((2 * blk, LANES), F32),
                        pltpu.VMEM((2 * blk, LANES), F32)] + stats,
        compiler_params=_cparams(3),
        name="attn_prompt",
    )(rel_bias, _prompt_bias_ids(), q, k, k, v, v)


DEC_T = 4
DEC_NEW_PAD = 128
DEC_SEQ_PER_STEP = 2
DEC_KEYS = MAX_WINDOW + DEC_NEW_PAD


def _decode_tables():
    qi = jnp.arange(DEC_T, dtype=jnp.int32)[:, None]
    rho = jnp.arange(MAX_WINDOW, dtype=jnp.int32)[None, :]
    dist_b = MAX_WINDOW + qi - rho
    mult_b = jnp.zeros_like(dist_b)
    for d in DILATIONS:
        mult_b = mult_b + (((dist_b % d) == 0) & (dist_b <= d * TAPS)).astype(jnp.int32)
    jj = jnp.arange(DEC_NEW_PAD, dtype=jnp.int32)[None, :]
    dist_n = qi - jj
    mult_n = jnp.where(dist_n == 0, len(DILATIONS), jnp.where((dist_n > 0) & (jj < DEC_T), 1, 0))
    dist = jnp.concatenate([dist_b, jnp.maximum(dist_n, 0)], axis=1)
    mult = jnp.concatenate([mult_b, mult_n.astype(jnp.int32)], axis=1)
    ids = jnp.where(mult > 0, _t5_bucket(dist), -1).astype(jnp.int32)
    ids = jnp.repeat(ids, N_HEADS, axis=0)
    mult = jnp.repeat(mult, N_HEADS, axis=0).astype(F32)
    return ids, mult


def _attn_decode_kernel(rbt_ref, ids_ref, mult_ref, q_ref, kn_ref, vn_ref, kt_ref, vt_ref, o_ref, bias_scr):
    n_rows = DEC_T * N_HEADS

    @pl.when(pl.program_id(0) == 0)
    def _build_bias():
        ids = ids_ref[...]
        tile = jnp.full((n_rows, DEC_KEYS), NEG_INF, F32)
        for bkt in range(N_BUCKETS):
            tile = jnp.where(ids == bkt, rbt_ref[:, bkt:bkt + 1], tile)
        bias_scr[...] = tile

    lane = lax.broadcasted_iota(jnp.int32, (N_HEADS, MIX_A), 1)
    head = lax.broadcasted_iota(jnp.int32, (N_HEADS, MIX_A), 0)
    hmask = (lane // HEAD_DIM) == head

    zpad = jnp.zeros((DEC_NEW_PAD - DEC_T, MIX_A), F32)
    nt = (((1,), (1,)), ((), ()))
    for j in range(q_ref.shape[0]):
        q = q_ref[j]
        qm = jnp.where(hmask[None], q[:, None, :], 0.0).reshape(n_rows, MIX_A).astype(BF16)
        kn = jnp.concatenate([kn_ref[j], zpad], axis=0).astype(BF16)
        vn = jnp.concatenate([vn_ref[j], zpad], axis=0).astype(BF16)
        kt = kt_ref[j].astype(BF16)
        vt = vt_ref[j].astype(BF16)
        s = jnp.concatenate([jnp.dot(qm, kt, preferred_element_type=F32),
                             lax.dot_general(qm, kn, nt, preferred_element_type=F32)], axis=1)
        s = s + bias_scr[...]
        m = jnp.max(s, axis=1, keepdims=True)
        pe = jnp.exp(s - m) * mult_ref[...]
        l = jnp.sum(pe, axis=1, keepdims=True)
        pb = pe.astype(BF16)
        o = (lax.dot_general(pb[:, :MAX_WINDOW], vt, nt, preferred_element_type=F32)
             + jnp.dot(pb[:, MAX_WINDOW:], vn, preferred_element_type=F32)) / l
        o3 = o.reshape(DEC_T, N_HEADS, MIX_A)
        o_ref[j] = jnp.sum(jnp.where(hmask[None], o3, 0.0), axis=1)


def _attn_decode(q, k_new, v_new, cache_kt, cache_vt, rel_bias):
    bd = q.shape[0]
    ids, mult = _decode_tables()
    rbt = jnp.tile(rel_bias.T, (DEC_T, 1))
    per = DEC_SEQ_PER_STEP if bd % DEC_SEQ_PER_STEP == 0 else 1
    new_spec = pl.BlockSpec((per, DEC_T, MIX_A), lambda b: (b, 0, 0))
    cache_spec = pl.BlockSpec((per, MIX_A, MAX_WINDOW), lambda b: (b, 0, 0))
    n_rows = DEC_T * N_HEADS
    const = lambda shape: pl.BlockSpec(shape, lambda b: (0, 0))
    return pl.pallas_call(
        _attn_decode_kernel,
        grid=(bd // per,),
        in_specs=[const((n_rows, N_BUCKETS)), const((n_rows, DEC_KEYS)), const((n_rows, DEC_KEYS)),
                  new_spec, new_spec, new_spec, cache_spec, cache_spec],
        out_specs=new_spec,
        out_shape=jax.ShapeDtypeStruct((bd, DEC_T, MIX_A), F32),
        scratch_shapes=[pltpu.VMEM((n_rows, DEC_KEYS), F32)],
        compiler_params=_cparams(1),
        name="attn_decode",
    )(rbt, ids, mult, q, k_new, v_new, cache_kt, cache_vt)


def _ssm_prep_kernel(are_ref, aim_ref, ldt_ref, btr_ref, bti_ref, cre_ref, cim_ref,
                     tit_ref, tinr_ref, tini_ref, toutr_ref, touti_ref, pw_ref, tintr_ref, tinti_ref):
    hi = lax.Precision.HIGHEST
    nt = (((1,), (1,)), ((), ()))
    a_re, a_im = are_ref[...], aim_ref[...]
    dt = jnp.exp(ldt_ref[...])
    decay = jnp.exp(a_re * dt)
    ab_re = decay * jnp.cos(a_im * dt)
    ab_im = decay * jnp.sin(a_im * dt)
    inv = 1.0 / (a_re * a_re + a_im * a_im)
    coef_re = ((ab_re - 1.0) * a_re + ab_im * a_im) * inv
    coef_im = (ab_im * a_re - (ab_re - 1.0) * a_im) * inv
    bt_re, bt_im = btr_ref[...], bti_ref[...]
    bb_re = coef_re * bt_re - coef_im * bt_im
    bb_im = coef_re * bt_im + coef_im * bt_re
    pw = [(jnp.ones_like(ab_re), jnp.zeros_like(ab_im))]
    for _ in range(CHUNK):
        pr, pi = pw[-1]
        pw.append((pr * ab_re - pi * ab_im, pr * ab_im + pi * ab_re))
    tin_re = jnp.concatenate(
        [pw[CHUNK - 1 - ti][0] * bb_re - pw[CHUNK - 1 - ti][1] * bb_im for ti in range(CHUNK)], axis=0).astype(BF16)
    tin_im = jnp.concatenate(
        [pw[CHUNK - 1 - ti][0] * bb_im + pw[CHUNK - 1 - ti][1] * bb_re for ti in range(CHUNK)], axis=0).astype(BF16)
    tinr_ref[...] = tin_re
    tini_ref[...] = tin_im
    p_i = lax.broadcasted_iota(jnp.int32, (SSM_STATE, SSM_STATE), 0)
    p_j = lax.broadcasted_iota(jnp.int32, (SSM_STATE, SSM_STATE), 1)
    eye = jnp.where(p_i == p_j, 1.0, 0.0).astype(BF16)
    tintr_ref[...] = lax.dot_general(eye, tin_re, nt, preferred_element_type=F32).astype(BF16)
    tinti_ref[...] = lax.dot_general(eye, tin_im, nt, preferred_element_type=F32).astype(BF16)
    c_re, c_im = cre_ref[...], cim_ref[...]
    cp_re = [c_re * pr - c_im * pi for pr, pi in pw]
    cp_im = [c_re * pi + c_im * pr for pr, pi in pw]
    toutr_ref[...] = jnp.concatenate(cp_re[1:], axis=0).astype(BF16)
    touti_ref[...] = (-jnp.concatenate(cp_im[1:], axis=0)).astype(BF16)
    kall = (lax.dot_general(jnp.concatenate(cp_re[:CHUNK], axis=0), bb_re, nt, precision=hi,
                            preferred_element_type=F32)
            - lax.dot_general(jnp.concatenate(cp_im[:CHUNK], axis=0), bb_im, nt, precision=hi,
                              preferred_element_type=F32))
    w = CHUNK * SSM_GROUP
    kb = kall.astype(BF16)
    ci_idx = lax.broadcasted_iota(jnp.int32, (SSM_GROUP, w), 0)
    col_idx = lax.broadcasted_iota(jnp.int32, (SSM_GROUP, w), 1)
    acc = jnp.zeros((w, w), F32)
    for ti in range(CHUNK):
        rows = ti * SSM_GROUP
        shifted = kb if ti == 0 else jnp.concatenate(
            [jnp.zeros((rows, SSM_GROUP), BF16), kb[:w - rows]], axis=0)
        place = jnp.where(col_idx == ci_idx + rows, 1.0, 0.0).astype(BF16)
        acc = acc + jnp.dot(shifted, place, preferred_element_type=F32)
    tit_ref[...] = acc.astype(BF16)
    zero = jnp.zeros_like(ab_re)
    pw_ref[...] = jnp.concatenate([pw[CHUNK][0], pw[CHUNK][1], pw[DEC_T][0], pw[DEC_T][1],
                                   zero, zero, zero, zero], axis=0)


def _ssm_prep(a_re, a_im, log_dt, b_re, b_im, c_re, c_im):
    g, p = a_re.shape
    w = CHUNK * SSM_GROUP
    row = lambda a: a[:, None, :]
    bt_re = jnp.transpose(b_re, (0, 2, 1))
    bt_im = jnp.transpose(b_im, (0, 2, 1))
    ldt = jnp.broadcast_to(log_dt[:, None, None], (g, 1, p))
    gspec = lambda shape: pl.BlockSpec((None,) + shape, lambda gi: (gi, 0, 0))
    return pl.pallas_call(
        _ssm_prep_kernel,
        grid=(g,),
        in_specs=[gspec((1, p)), gspec((1, p)), gspec((1, p)), gspec((SSM_GROUP, p)), gspec((SSM_GROUP, p)),
                  gspec((SSM_GROUP, p)), gspec((SSM_GROUP, p))],
        out_specs=[gspec((w, w)), gspec((w, p)), gspec((w, p)), gspec((w, p)), gspec((w, p)), gspec((8, p)),
                   gspec((p, w)), gspec((p, w))],
        out_shape=[jax.ShapeDtypeStruct((g, w, w), BF16)] + [jax.ShapeDtypeStruct((g, w, p), BF16)] * 4
                  + [jax.ShapeDtypeStruct((g, 8, p), F32)] + [jax.ShapeDtypeStruct((g, p, w), BF16)] * 2,
        compiler_params=_cparams(1),
        name="ssm_prep",
    )(row(a_re), row(a_im), ldt, bt_re, bt_im, c_re, c_im)


def _ssm_prompt_kernel(u_ref, tit_ref, tintr_ref, tinti_ref, toutr_ref, touti_ref, d_ref, pw_ref,
                       y_ref, sre_ref, sim_ref, ut_scr, yt_scr, pr_scr, pi_scr, sr_scr, si_scr):
    ngrp = LANES // SSM_GROUP
    nc = u_ref.shape[0] // CHUNK
    xt = [u_ref[pl.ds(ti, nc, stride=CHUNK), :].T for ti in range(CHUNK)]
    for j in range(ngrp):
        ut = jnp.concatenate([x[j * SSM_GROUP:(j + 1) * SSM_GROUP, :] for x in xt], axis=0).astype(BF16)
        ut_scr[j] = ut
        pr_scr[j] = jnp.dot(tintr_ref[j], ut, preferred_element_type=F32).T
        pi_scr[j] = jnp.dot(tinti_ref[j], ut, preferred_element_type=F32).T
    ar = [pw_ref[j, 0:1, :] for j in range(ngrp)]
    ai = [pw_ref[j, 1:2, :] for j in range(ngrp)]

    def step(c, carry):
        out = []
        for j in range(ngrp):
            sr, si = carry[j]
            sr_scr[j, pl.ds(c, 1), :] = sr
            si_scr[j, pl.ds(c, 1), :] = si
            out.append((ar[j] * sr - ai[j] * si + pr_scr[j, pl.ds(c, 1), :],
                        ar[j] * si + ai[j] * sr + pi_scr[j, pl.ds(c, 1), :]))
        return tuple(out)

    zero = jnp.zeros((1, SSM_STATE), F32)
    final = lax.fori_loop(0, nc, step, ((zero, zero),) * ngrp, unroll=8)
    for j in range(ngrp):
        sre_ref[j] = final[j][0]
        sim_ref[j] = final[j][1]
        yt = jnp.dot(tit_ref[j], ut_scr[j], preferred_element_type=F32)
        yt = yt + jnp.dot(toutr_ref[j], sr_scr[j].T.astype(BF16), preferred_element_type=F32)
        yt = yt + jnp.dot(touti_ref[j], si_scr[j].T.astype(BF16), preferred_element_type=F32)
        yt_scr[j] = yt
    d = d_ref[...]
    for to in range(CHUNK):
        v = jnp.concatenate([yt_scr[j, to * SSM_GROUP:(to + 1) * SSM_GROUP, :] for j in range(ngrp)], axis=0)
        rows = pl.ds(to, nc, stride=CHUNK)
        y_ref[rows, :] = v.T + d * u_ref[rows, :]


def _ssm_prompt(u, ops, d_skip):
    b, t, _ = u.shape
    p = SSM_STATE
    w = CHUNK * SSM_GROUP
    nc = t // CHUNK
    ngrp = LANES // SSM_GROUP
    g = N_GROUPS
    gspec = lambda shape: pl.BlockSpec((ngrp,) + shape, lambda bi, qi: (qi, 0, 0))
    io = pl.BlockSpec((None, t, LANES), lambda bi, qi: (bi, 0, qi))
    st = pl.BlockSpec((None, ngrp, 1, p), lambda bi, qi: (bi, qi, 0, 0))
    tit, _, _, toutr, touti, pw, tintr, tinti = ops
    return pl.pallas_call(
        _ssm_prompt_kernel,
        grid=(b, g // ngrp),
        in_specs=[io, gspec((w, w)), gspec((p, w)), gspec((p, w)), gspec((w, p)), gspec((w, p)),
                  pl.BlockSpec((None, 1, LANES), lambda bi, qi: (qi, 0, 0)), gspec((8, p))],
        out_specs=[io, st, st],
        out_shape=[jax.ShapeDtypeStruct(u.shape, F32),
                   jax.ShapeDtypeStruct((b, g, 1, p), F32), jax.ShapeDtypeStruct((b, g, 1, p), F32)],
        scratch_shapes=[pltpu.VMEM((ngrp, w, nc), BF16), pltpu.VMEM((ngrp, w, nc), F32)]
                       + [pltpu.VMEM((ngrp, nc, p), F32)] * 4,
        compiler_params=_cparams(2),
        name="ssm_prompt",
    )(u, tit, tintr, tinti, toutr, touti, d_skip.reshape(g // ngrp, 1, LANES), pw)


def _ssm_decode_kernel(u_ref, s0r_ref, s0i_ref, tit_ref, tinr_ref, tini_ref, toutr_ref, touti_ref, d_ref,
                       pw_ref, y_ref, sre_ref, sim_ref):
    nt = (((1,), (1,)), ((), ()))
    w = DEC_T * SSM_GROUP
    lo = (CHUNK - DEC_T) * SSM_GROUP
    u = u_ref[...]
    ub = u.astype(BF16)
    s0r = s0r_ref[...]
    s0i = s0i_ref[...]
    y = lax.dot_general(ub, tit_ref[0:w, 0:w], nt, preferred_element_type=F32)
    y = y + lax.dot_general(s0r.astype(BF16), toutr_ref[0:w, :], nt, preferred_element_type=F32)
    y = y + lax.dot_general(s0i.astype(BF16), touti_ref[0:w, :], nt, preferred_element_type=F32)
    y_ref[...] = y + d_ref[:, 0:w] * u
    ar = pw_ref[2:3, :]
    ai = pw_ref[3:4, :]
    sre_ref[...] = ar * s0r - ai * s0i + jnp.dot(ub, tinr_ref[lo:, :], preferred_element_type=F32)
    sim_ref[...] = ar * s0i + ai * s0r + jnp.dot(ub, tini_ref[lo:, :], preferred_element_type=F32)


def _ssm_decode(ug, s0_re, s0_im, ops, d_tile):
    g, bd, w = ug.shape
    p = SSM_STATE
    wc = CHUNK * SSM_GROUP
    gspec = lambda shape: pl.BlockSpec((None,) + shape, lambda gi: (gi, 0, 0))
    tit, tinr, tini, toutr, touti, pw = ops[:6]
    return pl.pallas_call(
        _ssm_decode_kernel,
        grid=(g,),
        in_specs=[gspec((bd, w)), gspec((bd, p)), gspec((bd, p)), gspec((wc, wc)), gspec((wc, p)),
                  gspec((wc, p)), gspec((wc, p)), gspec((wc, p)), gspec((1, wc)), gspec((8, p))],
        out_specs=[gspec((bd, w)), gspec((bd, p)), gspec((bd, p))],
        out_shape=[jax.ShapeDtypeStruct((g, bd, w), F32), jax.ShapeDtypeStruct((g, bd, p), F32),
                   jax.ShapeDtypeStruct((g, bd, p), F32)],
        compiler_params=_cparams(1),
        name="ssm_decode",
    )(ug, s0_re, s0_im, tit, tinr, tini, toutr, touti, d_tile, pw)


def _gelu_tanh(x):
    return 0.5 * x * (1.0 + jnp.tanh(math.sqrt(2.0 / math.pi) * (x + 0.044715 * (x * x * x))))


def _sigmoid(x):
    return 1.0 / (1.0 + jnp.exp(-x))


def _route(hn, wr_ref):
    n = hn.shape[0]
    h_hi = hn.astype(BF16)
    h_lo = (hn - h_hi.astype(F32)).astype(BF16)
    prod = jnp.dot(jnp.concatenate([h_hi, h_lo], axis=0), wr_ref[...], preferred_element_type=F32)
    logits = (prod[:n, :ROUTER_LANES] + prod[:n, ROUTER_LANES:]
              + prod[n:, :ROUTER_LANES] + prod[n:, ROUTER_LANES:])
    lidx = lax.broadcasted_iota(jnp.int32, (n, ROUTER_LANES), 1)
    is_e = lidx < N_EXPERTS
    is_g = jnp.logical_and(lidx >= N_EXPERTS, lidx < N_EXPERTS + N_EXPERT_GROUPS)
    gmax = jnp.max(jnp.where(is_g, logits, -jnp.inf), axis=1, keepdims=True)
    g_prob = 1.0 / jnp.sum(jnp.where(is_g, jnp.exp(logits - gmax), 0.0), axis=1, keepdims=True)
    g_sel = jnp.min(jnp.where(jnp.logical_and(is_g, logits == gmax), lidx - N_EXPERTS, N_EXPERT_GROUPS),
                    axis=1, keepdims=True)
    in_grp = jnp.logical_and(is_e, (lidx // EXPERTS_PER_GROUP) == g_sel)
    l1 = jnp.max(jnp.where(in_grp, logits, -jnp.inf), axis=1, keepdims=True)
    i1 = jnp.min(jnp.where(jnp.logical_and(in_grp, logits == l1), lidx, ROUTER_LANES), axis=1, keepdims=True)
    rest = jnp.logical_and(in_grp, lidx != i1)
    l2 = jnp.max(jnp.where(rest, logits, -jnp.inf), axis=1, keepdims=True)
    i2 = jnp.min(jnp.where(jnp.logical_and(rest, logits == l2), lidx, ROUTER_LANES), axis=1, keepdims=True)
    e2 = jnp.exp(l2 - l1)
    w1 = g_prob / (1.0 + e2)
    w2 = g_prob * e2 / (1.0 + e2)
    return jnp.where(lidx == i1, w1, 0.0) + jnp.where(lidx == i2, w2, 0.0)


def _post_body(oa, ys, x, ga_ref, gb_ref, wglu_ref, bglu_ref, wout_ref, gf_ref, wr_ref,
               x1_ref, hn_ref, gates_ref):
    z = _gelu_tanh(ys)
    gate = _sigmoid(jnp.dot(z.astype(BF16), wglu_ref[...], preferred_element_type=F32) + bglu_ref[...])
    ob = z * gate
    mixed = jnp.concatenate([_rms(oa, ga_ref[...]), _rms(ob, gb_ref[...])], axis=1).astype(BF16)
    x1 = x + jnp.dot(mixed, wout_ref[...], preferred_element_type=F32)
    x1_ref[...] = x1
    hn = _rms(x1, gf_ref[...])
    bits = lax.bitcast_convert_type(hn.astype(BF16).astype(F32), jnp.int32)
    half = hn.shape[1] // 2
    hn_ref[...] = (lax.shift_right_logical(bits[:, :half], 16) | (bits[:, half:] & jnp.int32(-65536)))
    gates_ref[...] = _route(hn, wr_ref)


def _post_pair_kernel(oa_ref, ys_ref, x_ref, *rest):
    oa = jnp.concatenate([oa_ref[j] for j in range(MIX_A // LANES)], axis=1)
    _post_body(oa, ys_ref[...], x_ref[...], *rest)


def _post_tok_kernel(oa_ref, ys_ref, x_ref, *rest):
    _post_body(oa_ref[...], ys_ref[...], x_ref[...], *rest)


def _post_mix(oa, ys, x, w, tm, pair):
    b, t, d = x.shape
    row = lambda width: pl.BlockSpec((None, tm, width), lambda bi, i: (bi, i, 0))
    const = lambda a: pl.BlockSpec(a.shape, lambda bi, i: (0,) * a.ndim)
    oa_spec = (pl.BlockSpec((None, MIX_A // LANES, tm, LANES), lambda bi, i: (bi, 0, i, 0)) if pair
               else row(MIX_A))
    weights = [w['g_out_a'], w['g_out_b'], w['w_glu'], w['b_glu'], w['w_out'], w['g_ffn'], w['w_router']]
    return pl.pallas_call(
        _post_pair_kernel if pair else _post_tok_kernel,
        grid=(b, t // tm),
        in_specs=[oa_spec, row(MIX_B), row(d)] + [const(a) for a in weights],
        out_specs=[row(d), row(d // 2), row(ROUTER_LANES)],
        out_shape=[jax.ShapeDtypeStruct((b, t, d), F32), jax.ShapeDtypeStruct((b, t, d // 2), jnp.int32),
                   jax.ShapeDtypeStruct((b, t, ROUTER_LANES), F32)],
        compiler_params=_cparams(2),
        name="post_mix_prompt" if pair else "post_mix_sample",
    )(oa, ys, x, *weights)


MOE_BLOCK = 2048
MOE_ROWS = 128
MOE_ALIGN = 8
MOE_EXPERTS_PER_STEP = 2
MOE_OUT_PITCH = 136
MOE_VMEM_LIMIT = 60 * 1024 * 1024
PLAN_CHUNK = 256


def _moe_rows(tb):
    rows = 2 * tb + N_EXPERTS * (MOE_ALIGN - 1) + MOE_ROWS
    return -(-rows // MOE_ALIGN) * MOE_ALIGN


def _moe_plan_kernel(gates_ref, plan_ref, meta_ref):
    tb = gates_ref.shape[0]
    gates = gates_ref[...]
    hot = gates > 0.0
    onehot = jnp.where(hot, 1.0, 0.0).astype(BF16)
    ch = min(PLAN_CHUNK, tb)
    r_i = lax.broadcasted_iota(jnp.int32, (ch, ch), 0)
    c_i = lax.broadcasted_iota(jnp.int32, (ch, ch), 1)
    earlier = jnp.where(c_i < r_i, 1.0, 0.0).astype(BF16)
    carry = jnp.zeros((1, ROUTER_LANES), F32)
    ranks = []
    for k in range(tb // ch):
        oh = onehot[k * ch:(k + 1) * ch]
        ranks.append(jnp.dot(earlier, oh, preferred_element_type=F32) + carry)
        carry = carry + jnp.sum(oh.astype(F32), axis=0, keepdims=True)
    rank = jnp.concatenate(ranks, axis=0)
    seg = jnp.floor((carry + (MOE_ALIGN - 1.0)) * (1.0 / MOE_ALIGN)) * MOE_ALIGN
    l_i = lax.broadcasted_iota(jnp.int32, (ROUTER_LANES, ROUTER_LANES), 0)
    l_j = lax.broadcasted_iota(jnp.int32, (ROUTER_LANES, ROUTER_LANES), 1)
    before = jnp.where(l_i < l_j, 1.0, 0.0).astype(F32)
    offs = jnp.dot(jnp.broadcast_to(seg, (8, ROUTER_LANES)), before, precision=lax.Precision.HIGHEST,
                   preferred_element_type=F32)[0:1]
    pos = rank + offs
    lane = lax.broadcasted_iota(jnp.int32, (tb, ROUTER_LANES), 1)
    lane_a = jnp.min(jnp.where(hot, lane, ROUTER_LANES), axis=1, keepdims=True)
    lane_b = jnp.max(jnp.where(hot, lane, -1), axis=1, keepdims=True)
    pick = lambda sel, val: jnp.sum(jnp.where(sel, val, 0.0), axis=1, keepdims=True)
    sel_a, sel_b = lane == lane_a, lane == lane_b
    z = (jnp.where(lane == 0, pick(sel_a, pos), 0.0) + jnp.where(lane == 1, pick(sel_b, pos), 0.0)
         + jnp.where(lane == 2, pick(sel_a, gates), 0.0) + jnp.where(lane == 3, pick(sel_b, gates), 0.0))
    plan_ref[...] = z.T[0:8, :]
    meta_ref[...] = jnp.concatenate([offs, carry, jnp.zeros((6, ROUTER_LANES), F32)], axis=0)


def _moe_plan(gates, tb):
    n = gates.shape[0]
    nb = n // tb
    return pl.pallas_call(
        _moe_plan_kernel,
        grid=(nb,),
        in_specs=[pl.BlockSpec((tb, ROUTER_LANES), lambda i: (i, 0))],
        out_specs=[pl.BlockSpec((None, 8, tb), lambda i: (i, 0, 0)),
                   pl.BlockSpec((None, 8, ROUTER_LANES), lambda i: (i, 0, 0))],
        out_shape=[jax.ShapeDtypeStruct((nb, 8, tb), F32), jax.ShapeDtypeStruct((nb, 8, ROUTER_LANES), F32)],
        compiler_params=_cparams(1),
        name="moe_plan",
    )(gates)


def _moe_kernel(offs_ref, cnts_ref, pos_ref, w_ref, hn_ref, x1_ref, wg_ref, wu_ref, wd_ref, gfin_ref,
                y_ref, xs_scr, gcol_scr, otile_scr, yacc_scr, tok_scr):
    blk = pl.program_id(0)
    step = pl.program_id(1)
    tb = hn_ref.shape[0]
    n_lt = D_MODEL // LANES

    @pl.when(step == 0)
    def _group_rows():
        xs_scr[...] = jnp.zeros_like(xs_scr)
        gcol_scr[...] = jnp.zeros_like(gcol_scr)
        yacc_scr[...] = jnp.zeros_like(yacc_scr)

        def clear_pad(ex, c):
            end = offs_ref[blk, ex] + cnts_ref[blk, ex]
            for k in range(MOE_ALIGN - 1):
                tok_scr[end + k] = 0
            return c

        lax.fori_loop(0, N_EXPERTS, clear_pad, 0)

        def place(t8, c):
            for u in range(8):
                t = t8 * 8 + u
                row = hn_ref[pl.ds(t, 1), :]
                for s in range(2):
                    p = pos_ref[s, t]
                    xs_scr[pl.ds(p, 1), :] = row
                    gcol_scr[pl.ds(p, 1), :] = jnp.full((1, LANES), w_ref[s, t], F32)
                    tok_scr[p] = t
            return c

        lax.fori_loop(0, tb // 8, place, 0)

    slot_rows = n_lt * MOE_OUT_PITCH

    def compute(k, r0, slot):
        xw = xs_scr[pl.ds(r0, MOE_ROWS), :]
        x = jnp.concatenate(
            [lax.bitcast_convert_type(xw << 16, F32).astype(BF16),
             lax.bitcast_convert_type(xw & jnp.int32(-65536), F32).astype(BF16)], axis=1)
        g = gcol_scr[pl.ds(r0, MOE_ROWS), :]
        a = jnp.dot(x, wg_ref[k], preferred_element_type=F32)
        u = jnp.dot(x, wu_ref[k], preferred_element_type=F32)
        act = (a * _sigmoid(a)) * u * jnp.concatenate([g] * (D_EXPERT // LANES), axis=1)
        out = jnp.dot(act.astype(BF16), wd_ref[k], preferred_element_type=F32)
        for s in range(n_lt):
            otile_scr[pl.ds(slot * slot_rows + s * MOE_OUT_PITCH, MOE_ROWS), :] = out[:, s * LANES:(s + 1) * LANES]

    def scatter_add(r0, valid, slot):
        def add8(j8, c3):
            base = pl.multiple_of(j8 * 8, 8)
            dst = [pl.ds(pl.multiple_of(tok_scr[r0 + base + v] * n_lt, n_lt), n_lt) for v in range(8)]
            rows = [yacc_scr[dst[v], :]
                    + otile_scr[pl.ds(slot * slot_rows + base + v, n_lt, stride=MOE_OUT_PITCH), :]
                    for v in range(8)]
            for v in range(8):
                yacc_scr[dst[v], :] = rows[v]
            return c3

        lax.fori_loop(0, (valid + 7) // 8, add8, 0)

    offs = [offs_ref[blk, step * MOE_EXPERTS_PER_STEP + k] for k in range(MOE_EXPERTS_PER_STEP)]
    cnts = [cnts_ref[blk, step * MOE_EXPERTS_PER_STEP + k] for k in range(MOE_EXPERTS_PER_STEP)]
    for k in range(MOE_EXPERTS_PER_STEP):
        compute(k, pl.multiple_of(offs[k], MOE_ALIGN), k)
    for k in range(MOE_EXPERTS_PER_STEP):
        scatter_add(offs[k], jnp.minimum(MOE_ROWS, cnts[k]), k)
    for k in range(MOE_EXPERTS_PER_STEP):
        def more(c, cc, k=k):
            r0 = pl.multiple_of(offs[k] + c * MOE_ROWS, MOE_ALIGN)
            compute(k, r0, 0)
            scatter_add(r0, jnp.minimum(MOE_ROWS, cnts[k] - c * MOE_ROWS), 0)
            return cc

        lax.fori_loop(1, (cnts[k] + MOE_ROWS - 1) // MOE_ROWS, more, 0)

    @pl.when(step == pl.num_programs(1) - 1)
    def _fin():
        moe = jnp.concatenate([yacc_scr[pl.ds(s, tb, stride=n_lt), :] for s in range(n_lt)], axis=1)
        y_ref[...] = _rms(x1_ref[...] + moe, gfin_ref[...])


def _moe(hn_packed, gates, x1, wg, wu, wd, g_final, tb):
    n, d = x1.shape
    nb = n // tb
    plan, meta = _moe_plan(gates, tb)
    pos = plan[:, 0:2, :].astype(jnp.int32)
    wts = plan[:, 2:4, :]
    offs = meta[:, 0, :N_EXPERTS].astype(jnp.int32)
    cnts = meta[:, 1, :N_EXPERTS].astype(jnp.int32)
    p_rows = _moe_rows(tb)
    per = MOE_EXPERTS_PER_STEP
    n_lt = d // LANES
    smem = lambda: pl.BlockSpec((None, 2, tb), lambda i, e, *_: (i, 0, 0), memory_space=pltpu.SMEM)
    once = dict(pipeline_mode=pl.Buffered(1))
    grid_spec = pltpu.PrefetchScalarGridSpec(
        num_scalar_prefetch=2,
        grid=(nb, N_EXPERTS // per),
        in_specs=[smem(), smem(),
                  pl.BlockSpec((tb, d // 2), lambda i, e, *_: (i, 0), **once),
                  pl.BlockSpec((tb, d), lambda i, e, *_: (i, 0), **once),
                  pl.BlockSpec((per, d, D_EXPERT), lambda i, e, *_: (e, 0, 0)),
                  pl.BlockSpec((per, d, D_EXPERT), lambda i, e, *_: (e, 0, 0)),
                  pl.BlockSpec((per, D_EXPERT, d), lambda i, e, *_: (e, 0, 0)),
                  pl.BlockSpec((1, d), lambda i, e, *_: (0, 0))],
        out_specs=pl.BlockSpec((tb, d), lambda i, e, *_: (i, 0)),
        scratch_shapes=[pltpu.VMEM((p_rows, d // 2), jnp.int32),
                        pltpu.VMEM((p_rows, LANES), F32),
                        pltpu.VMEM((per * n_lt * MOE_OUT_PITCH, LANES), F32),
                        pltpu.VMEM((tb * n_lt, LANES), F32),
                        pltpu.SMEM((p_rows,), jnp.int32)])
    return pl.pallas_call(
        _moe_kernel,
        grid_spec=grid_spec,
        out_shape=jax.ShapeDtypeStruct((n, d), F32),
        compiler_params=pltpu.CompilerParams(dimension_semantics=("arbitrary", "arbitrary"),
                                             vmem_limit_bytes=MOE_VMEM_LIMIT),
        name="moe",
    )(offs, cnts, pos, wts, hn_packed, x1, wg, wu, wd, g_final)


def kernel(x_prompt, x_sample, cache_k, cache_v, state_ssm_re, state_ssm_im, rel_bias, g_mix, w_in, g_out_a, g_out_b, w_out, ssm_a_re, ssm_a_im, ssm_log_dt, ssm_b_re, ssm_b_im, ssm_c_re, ssm_c_im, ssm_d, w_glu, b_glu, g_ffn, w_router_group, w_router_expert, w_expert_gate, w_expert_up, w_expert_down, g_final):
    depth = g_mix.shape[0]
    assert depth == 1, "kernel is written for the single-layer configuration of the problem"
    l = 0
    b, t, d = x_prompt.shape
    bd, ts, _ = x_sample.shape
    assert ts == DEC_T and t % ATT_BLOCK == 0 and cache_k.shape[2] == MAX_WINDOW
    keep = min(MAX_WINDOW, t)

    w_in_b = w_in[l].astype(BF16)
    gm = g_mix[l][None, :]
    w_router = jnp.concatenate(
        [jnp.transpose(w_router_expert[l], (1, 0, 2)).reshape(d, N_EXPERTS), w_router_group[l],
         jnp.zeros((d, ROUTER_LANES - N_EXPERTS - N_EXPERT_GROUPS), F32)], axis=1)
    w_router_hi = w_router.astype(BF16)
    w_router_lo = (w_router - w_router_hi.astype(F32)).astype(BF16)
    post_w = dict(
        g_out_a=g_out_a[l][None, :], g_out_b=g_out_b[l][None, :], w_glu=w_glu[l].astype(BF16),
        b_glu=b_glu[l][None, :], w_out=w_out[l].astype(BF16), g_ffn=g_ffn[l][None, :],
        w_router=jnp.concatenate([w_router_hi, w_router_lo], axis=1))
    wg = w_expert_gate[l].astype(BF16)
    wu = w_expert_up[l].astype(BF16)
    wd = w_expert_down[l].astype(BF16)
    gfin = g_final[None, :]
    ssm_ops = _ssm_prep(ssm_a_re[l], ssm_a_im[l], ssm_log_dt[l], ssm_b_re[l], ssm_b_im[l],
                        ssm_c_re[l], ssm_c_im[l])
    d_tile = jnp.tile(ssm_d[l].reshape(N_GROUPS, 1, SSM_GROUP), (1, 1, CHUNK))

    qp, kp, vp, kt_win, vt_win, u_tok = _inproj_pair(x_prompt, gm, w_in_b, tm=512, keep=keep)
    oa_p = _attn_prompt(qp, kp, vp, rel_bias)
    ys_p, rp, ip = _ssm_prompt(u_tok, ssm_ops, ssm_d[l])
    x1_p, hn_p, gates_p = _post_mix(oa_p, ys_p, x_prompt, post_w, tm=512, pair=True)
    y_p = _moe(hn_p.reshape(b * t, d // 2), gates_p.reshape(b * t, ROUTER_LANES), x1_p.reshape(b * t, d),
               wg, wu, wd, gfin, tb=min(MOE_BLOCK, b * t))

    n_s = bd * ts
    qs, ks, vs, us = _inproj_tok(x_sample.reshape(n_s, d), gm, w_in_b)
    ckt = jnp.transpose(cache_k[l], (0, 2, 3, 1)).reshape(bd, MIX_A, MAX_WINDOW)
    cvt = jnp.transpose(cache_v[l], (0, 2, 3, 1)).reshape(bd, MIX_A, MAX_WINDOW)
    oa_s = _attn_decode(qs.reshape(bd, ts, MIX_A), ks.reshape(bd, ts, MIX_A), vs.reshape(bd, ts, MIX_A),
                        ckt, cvt, rel_bias)
    ugs = jnp.transpose(us.reshape(bd, ts, N_GROUPS, SSM_GROUP), (2, 0, 1, 3)).reshape(N_GROUPS, bd, ts * SSM_GROUP)
    s0r = jnp.transpose(state_ssm_re[l], (1, 0, 2))
    s0i = jnp.transpose(state_ssm_im[l], (1, 0, 2))
    ygs, rs, is_ = _ssm_decode(ugs, s0r, s0i, ssm_ops, d_tile)
    ys_s = jnp.transpose(ygs.reshape(N_GROUPS, bd, ts, SSM_GROUP), (1, 2, 0, 3)).reshape(n_s, MIX_B)
    x1_s, hn_s, gates_s = _post_mix(oa_s.reshape(1, n_s, MIX_A), ys_s.reshape(1, n_s, MIX_B),
                                    x_sample.reshape(1, n_s, d), post_w, tm=n_s, pair=False)
    y_s = _moe(hn_s.reshape(n_s, d // 2), gates_s.reshape(n_s, ROUTER_LANES), x1_s.reshape(n_s, d),
               wg, wu, wd, gfin, tb=n_s)

    y_prompt = y_p.reshape(b, t, d)
    y_sample = y_s.reshape(bd, ts, d)
    k_win = jnp.transpose(kt_win.reshape(b, N_HEADS, HEAD_DIM, keep), (0, 3, 1, 2))[None]
    v_win = jnp.transpose(vt_win.reshape(b, N_HEADS, HEAD_DIM, keep), (0, 3, 1, 2))[None]
    k_new = ks.reshape(1, bd, ts, N_HEADS, HEAD_DIM)
    v_new = vs.reshape(1, bd, ts, N_HEADS, HEAD_DIM)
    return (y_prompt, y_sample, k_win, v_win, k_new, v_new,
            rp.reshape(1, b, N_GROUPS, SSM_STATE), ip.reshape(1, b, N_GROUPS, SSM_STATE),
            jnp.transpose(rs, (1, 0, 2))[None], jnp.transpose(is_, (1, 0, 2))[None])
```

```python
import functools
import math

import jax
import jax.numpy as jnp
import numpy as np
from jax import lax
from jax.experimental import pallas as pl
from jax.experimental.pallas import tpu as pltpu

F32 = jnp.float32
BF16 = jnp.bfloat16

D_MODEL = 1024
HEAD_DIM = 64
MIX_A = 512
N_HEADS = 8
MIX_B = 512
SSM_GROUP = 16
N_GROUPS = 32
SSM_STATE = 64
PROJ_COLS = 3 * MIX_A + MIX_B
DILATIONS = (1, 4, 16)
TAPS = 128
MAX_WINDOW = 2048
N_BUCKETS = 32
N_EXPERT_GROUPS = 4
EXPERTS_PER_GROUP = 8
N_EXPERTS = 32
EXPERT_TOP_K = 2
D_EXPERT = 256
EPS = 1e-6
NEG_INF = -1e30
SCALE = HEAD_DIM ** -0.5

LANES = 128
ROUTER_LANES = 128
CHUNK = 16
ATT_BLOCK = 2048
ATT_UNROLL = 16
ATT_PITCH = 136
VMEM_LIMIT = 56 * 1024 * 1024


def _cparams(n_axes):
    return pltpu.CompilerParams(dimension_semantics=("arbitrary",) * n_axes,
                                vmem_limit_bytes=VMEM_LIMIT)


def _t5_bucket(distance):
    max_exact = N_BUCKETS // 2
    nf = jnp.maximum(distance, 1).astype(F32)
    large = max_exact + jnp.floor(jnp.log(nf / max_exact) / math.log(MAX_WINDOW / max_exact)
                                  * (N_BUCKETS - max_exact)).astype(jnp.int32)
    large = jnp.minimum(large, N_BUCKETS - 1)
    return jnp.where(distance < max_exact, distance, large)


def _rms(x, g):
    return x * lax.rsqrt(jnp.mean(x * x, axis=-1, keepdims=True) + EPS) * g


def _inproj_pair_kernel(first_win_tile, x_ref, g_ref, w_ref, q_ref, k_ref, v_ref, kt_ref, vt_ref, u_ref):
    h = _rms(x_ref[...], g_ref[...])
    p = jnp.dot(h.astype(BF16), w_ref[...], preferred_element_type=F32)
    for j in range(MIX_A // LANES):
        q_ref[j] = p[:, LANES * j:LANES * (j + 1)] * SCALE
        k_ref[j] = p[:, MIX_A + LANES * j:MIX_A + LANES * (j + 1)]
        v_ref[j] = p[:, 2 * MIX_A + LANES * j:2 * MIX_A + LANES * (j + 1)]
    u_ref[...] = p[:, 3 * MIX_A:]

    @pl.when(pl.program_id(1) >= first_win_tile)
    def _window():
        kt_ref[...] = p[:, MIX_A:2 * MIX_A].T
        vt_ref[...] = p[:, 2 * MIX_A:3 * MIX_A].T


def _inproj_pair(x, g, w_bf16, tm, keep):
    b, t, d = x.shape
    npair = MIX_A // LANES
    first = (t - keep) // tm
    pair = jax.ShapeDtypeStruct((b, npair, t, LANES), F32)
    win = jax.ShapeDtypeStruct((b, MIX_A, keep), F32)
    tok = jax.ShapeDtypeStruct((b, t, MIX_A), F32)
    pair_spec = pl.BlockSpec((None, npair, tm, LANES), lambda bi, i: (bi, 0, i, 0))
    win_spec = pl.BlockSpec((None, MIX_A, tm), lambda bi, i: (bi, 0, jnp.maximum(i - first, 0)))
    tok_spec = pl.BlockSpec((None, tm, MIX_A), lambda bi, i: (bi, i, 0))
    return pl.pallas_call(
        functools.partial(_inproj_pair_kernel, first),
        grid=(b, t // tm),
        in_specs=[pl.BlockSpec((None, tm, d), lambda bi, i: (bi, i, 0)),
                  pl.BlockSpec((1, d), lambda bi, i: (0, 0)),
                  pl.BlockSpec((d, PROJ_COLS), lambda bi, i: (0, 0))],
        out_specs=[pair_spec, pair_spec, pair_spec, win_spec, win_spec, tok_spec],
        out_shape=[pair, pair, pair, win, win, tok],
        compiler_params=_cparams(2),
        name="inproj_prompt",
    )(x, g, w_bf16)


def _inproj_tok_kernel(x_ref, g_ref, w_ref, q_ref, k_ref, v_ref, u_ref):
    h = _rms(x_ref[...], g_ref[...])
    p = jnp.dot(h.astype(BF16), w_ref[...], preferred_element_type=F32)
    q_ref[...] = p[:, :MIX_A] * SCALE
    k_ref[...] = p[:, MIX_A:2 * MIX_A]
    v_ref[...] = p[:, 2 * MIX_A:3 * MIX_A]
    u_ref[...] = p[:, 3 * MIX_A:]


def _inproj_tok(x, g, w_bf16):
    n, d = x.shape
    out = jax.ShapeDtypeStruct((n, MIX_A), F32)
    return pl.pallas_call(
        _inproj_tok_kernel,
        out_shape=[out, out, out, out],
        compiler_params=pltpu.CompilerParams(vmem_limit_bytes=VMEM_LIMIT),
        name="inproj_sample",
    )(x, g, w_bf16)


def _prompt_bias_ids():
    r = jnp.arange(TAPS, dtype=jnp.int32)[:, None]
    kpos = jnp.arange(2 * TAPS, dtype=jnp.int32)[None, :] - TAPS
    rel = r - kpos
    valid = (rel >= 0) & (rel <= TAPS)
    ids = [jnp.where(valid, _t5_bucket(d * rel), -1) for d in DILATIONS]
    return jnp.stack(ids).astype(jnp.int32)


def _attn_prompt_kernel(rb_ref, ids_ref, q_ref, kc_ref, kp_ref, vc_ref, vp_ref, o_ref,
                        bias_scr, fm_scr, kcat, vcat, m0_scr, l0_scr, o0_scr, m1_scr, l1_scr, o1_scr,
                        m2_scr, l2_scr, o2_scr):
    bi = pl.program_id(0)
    p = pl.program_id(1)
    i = pl.program_id(2)
    blk = ATT_BLOCK
    npair = N_HEADS // 2

    @pl.when((bi == 0) & (p == 0) & (i == 0))
    def _build_bias():
        col = lax.broadcasted_iota(jnp.int32, (2 * TAPS, 2 * TAPS), 1)
        fm_scr[0] = jnp.zeros((2 * TAPS, 2 * TAPS), F32)
        fm_scr[1] = jnp.where(col < TAPS, NEG_INF, 0.0).astype(F32)
        for c in range(len(DILATIONS)):
            ids = ids_ref[c]

            for h in range(N_HEADS):
                tile = jnp.where(ids < 0, NEG_INF, 0.0).astype(F32)
                for bkt in range(N_BUCKETS):
                    tile = tile + jnp.where(ids == bkt, rb_ref[bkt:bkt + 1, h:h + 1], 0.0)
                bias_scr[c * npair + h // 2, (h % 2) * TAPS:(h % 2 + 1) * TAPS, :] = tile

    kcat[pl.ds(0, blk), :] = kp_ref[...]
    kcat[pl.ds(blk, blk), :] = kc_ref[...]
    vcat[pl.ds(0, blk), :] = vp_ref[...]
    vcat[pl.ds(blk, blk), :] = vc_ref[...]

    lane = lax.broadcasted_iota(jnp.int32, (1, LANES), 1)
    low = lane < HEAD_DIM
    nt = (((1,), (1,)), ((), ()))
    ones = jnp.ones((2 * TAPS, LANES), BF16)
    stats = ((m0_scr, l0_scr, o0_scr), (m1_scr, l1_scr, o1_scr), (m2_scr, l2_scr, o2_scr))

    def rows(ref, start, d):
        if d == 1:
            return ref[pl.ds(start, TAPS), :]
        return ref[pl.ds(start, TAPS, stride=d), :]

    def sub_block(c, d, qs, first, dst_rows):
        q = rows(q_ref, qs, d)
        q2 = jnp.concatenate([jnp.where(low, q, 0.0), jnp.where(low, 0.0, q)], axis=0).astype(BF16)
        k = jnp.concatenate([rows(kcat, blk + qs - d * TAPS, d), rows(kcat, blk + qs, d)],
                            axis=0).astype(BF16)
        v = jnp.concatenate([rows(vcat, blk + qs - d * TAPS, d), rows(vcat, blk + qs, d)],
                            axis=0).astype(BF16)
        s_ = lax.dot_general(q2, k, nt, preferred_element_type=F32) + bias_scr[c * npair + p]
        if first is not None:
            s_ = s_ + fm_scr[first]
        m = jnp.max(s_, axis=1, keepdims=True)
        pe = jnp.exp(s_ - m).astype(BF16)
        ol = jnp.dot(pe, jnp.concatenate([v, ones], axis=1), preferred_element_type=F32)
        o, l = ol[:, :LANES], ol[:, LANES:]
        m_ref, l_ref, a_ref = stats[c]
        m_ref[dst_rows, :] = jnp.where(low, m[:TAPS], m[TAPS:])
        l_ref[dst_rows, :] = jnp.where(low, l[:TAPS], l[TAPS:])
        a_ref[dst_rows, :] = jnp.where(low, o[:TAPS], o[TAPS:])

    at_start = jnp.where(i == 0, 1, 0)
    n_sub = blk // TAPS
    for c, d in enumerate(DILATIONS):
        n_grp = blk // (d * TAPS)

        def group_body(s8, carry, c=c, d=d, n_grp=n_grp):
            for u in range(ATT_UNROLL):
                s = s8 * ATT_UNROLL + u
                if n_grp >= ATT_UNROLL:
                    qs = pl.multiple_of(s * TAPS, TAPS)
                    first = jnp.where(s8 == 0, at_start, 0) if u == 0 else None
                    dst_rows = pl.ds(qs, TAPS)
                else:
                    r = s8 * (ATT_UNROLL // n_grp) + u // n_grp
                    g = u % n_grp
                    qs = r + d * TAPS * g
                    first = at_start if g == 0 else None
                    if n_grp > 1:
                        dst_rows = pl.ds(qs, TAPS, stride=d)
                    else:
                        dst_rows = pl.ds(pl.multiple_of(r * ATT_PITCH, 8), TAPS)
                sub_block(c, d, qs, first, dst_rows)
            return carry

        lax.fori_loop(0, n_sub // ATT_UNROLL, group_body, 0)

    d_last = DILATIONS[-1]

    def merge_body(j, carry):
        tok = pl.ds(pl.multiple_of(j * d_last, d_last), d_last)
        dil = pl.ds(j, d_last, stride=ATT_PITCH)
        m0, m1, m2 = m0_scr[tok, :], m1_scr[tok, :], m2_scr[dil, :]
        m = jnp.maximum(jnp.maximum(m0, m1), m2)
        a0, a1, a2 = jnp.exp(m0 - m), jnp.exp(m1 - m), jnp.exp(m2 - m)
        num = o0_scr[tok, :] * a0 + o1_scr[tok, :] * a1 + o2_scr[dil, :] * a2
        den = l0_scr[tok, :] * a0 + l1_scr[tok, :] * a1 + l2_scr[dil, :] * a2
        o_ref[tok, :] = num / den
        return carry

    lax.fori_loop(0, blk // d_last, merge_body, 0, unroll=4)


def _attn_prompt(q, k, v, rel_bias):
    b, npair, t, _ = q.shape
    blk = ATT_BLOCK
    cur = pl.BlockSpec((None, None, blk, LANES), lambda bi, p, i: (bi, p, i, 0))
    prev = pl.BlockSpec((None, None, blk, LANES), lambda bi, p, i: (bi, p, jnp.maximum(i - 1, 0), 0))
    n_tiles = len(DILATIONS) * npair
    stats = ([pltpu.VMEM((blk, LANES), F32)] * 6
             + [pltpu.VMEM((DILATIONS[-1] * ATT_PITCH, LANES), F32)] * 3)
    return pl.pallas_call(
        _attn_prompt_kernel,
        grid=(b, npair, t // blk),
        in_specs=[pl.BlockSpec((N_BUCKETS, N_HEADS), lambda bi, p, i: (0, 0)),
                  pl.BlockSpec((len(DILATIONS), TAPS, 2 * TAPS), lambda bi, p, i: (0, 0, 0)),
                  cur, cur, prev, cur, prev],
        out_specs=cur,
        out_shape=jax.ShapeDtypeStruct(q.shape, F32),
        scratch_shapes=[pltpu.VMEM((n_tiles, 2 * TAPS, 2 * TAPS), F32),
                        pltpu.VMEM((2, 2 * TAPS, 2 * TAPS), F32),
                        pltpu.VMEM((2 * blk, LANES), F32),
                        pltpu.VMEM((2 * blk, LANES), F32)] + stats,
        compiler_params=_cparams(3),
        name="attn_prompt",
    )(rel_bias, _prompt_bias_ids(), q, k, k, v, v)


DEC_T = 4
DEC_NEW_PAD = 128
DEC_SEQ_PER_STEP = 2
DEC_KEYS = MAX_WINDOW + DEC_NEW_PAD


def _decode_tables():
    qi = jnp.arange(DEC_T, dtype=jnp.int32)[:, None]
    rho = jnp.arange(MAX_WINDOW, dtype=jnp.int32)[None, :]
    dist_b = MAX_WINDOW + qi - rho
    mult_b = jnp.zeros_like(dist_b)
    for d in DILATIONS:
        mult_b = mult_b + (((dist_b % d) == 0) & (dist_b <= d * TAPS)).astype(jnp.int32)
    jj = jnp.arange(DEC_NEW_PAD, dtype=jnp.int32)[None, :]
    dist_n = qi - jj
    mult_n = jnp.where(dist_n == 0, len(DILATIONS), jnp.where((dist_n > 0) & (jj < DEC_T), 1, 0))
    dist = jnp.concatenate([dist_b, jnp.maximum(dist_n, 0)], axis=1)
    mult = jnp.concatenate([mult_b, mult_n.astype(jnp.int32)], axis=1)
    ids = jnp.where(mult > 0, _t5_bucket(dist), -1).astype(jnp.int32)
    ids = jnp.repeat(ids, N_HEADS, axis=0)
    mult = jnp.repeat(mult, N_HEADS, axis=0).astype(F32)
    return ids, mult


def _attn_decode_kernel(rbt_ref, ids_ref, mult_ref, q_ref, kn_ref, vn_ref, kt_ref, vt_ref, o_ref, bias_scr):
    n_rows = DEC_T * N_HEADS

    @pl.when(pl.program_id(0) == 0)
    def _build_bias():
        ids = ids_ref[...]
        tile = jnp.where(ids < 0, NEG_INF, 0.0).astype(F32)
        for bkt in range(N_BUCKETS):
            tile = tile + jnp.where(ids == bkt, rbt_ref[:, bkt:bkt + 1], 0.0)
        bias_scr[...] = tile

    lane = lax.broadcasted_iota(jnp.int32, (N_HEADS, MIX_A), 1)
    head = lax.broadcasted_iota(jnp.int32, (N_HEADS, MIX_A), 0)
    hmask = (lane // HEAD_DIM) == head

    zpad = jnp.zeros((DEC_NEW_PAD - DEC_T, MIX_A), F32)
    nt = (((1,), (1,)), ((), ()))
    for j in range(q_ref.shape[0]):
        q = q_ref[j]
        qm = jnp.where(hmask[None], q[:, None, :], 0.0).reshape(n_rows, MIX_A).astype(BF16)
        kn = jnp.concatenate([kn_ref[j], zpad], axis=0).astype(BF16)
        vn = jnp.concatenate([vn_ref[j], zpad], axis=0).astype(BF16)
        kt = kt_ref[j].astype(BF16)
        vt = vt_ref[j].astype(BF16)
        s = jnp.concatenate([jnp.dot(qm, kt, preferred_element_type=F32),
                             lax.dot_general(qm, kn, nt, preferred_element_type=F32)], axis=1)
        s = s + bias_scr[...]
        m = jnp.max(s, axis=1, keepdims=True)
        pe = jnp.exp(s - m) * mult_ref[...]
        l = jnp.sum(pe, axis=1, keepdims=True)
        pb = pe.astype(BF16)
        o = (lax.dot_general(pb[:, :MAX_WINDOW], vt, nt, preferred_element_type=F32)
             + jnp.dot(pb[:, MAX_WINDOW:], vn, preferred_element_type=F32)) / l
        o3 = o.reshape(DEC_T, N_HEADS, MIX_A)
        o_ref[j] = jnp.sum(jnp.where(hmask[None], o3, 0.0), axis=1)


def _attn_decode(q, k_new, v_new, cache_kt, cache_vt, rel_bias):
    bd = q.shape[0]
    ids, mult = _decode_tables()
    rbt = jnp.tile(rel_bias.T, (DEC_T, 1))
    per = DEC_SEQ_PER_STEP if bd % DEC_SEQ_PER_STEP == 0 else 1
    new_spec = pl.BlockSpec((per, DEC_T, MIX_A), lambda b: (b, 0, 0))
    cache_spec = pl.BlockSpec((per, MIX_A, MAX_WINDOW), lambda b: (b, 0, 0))
    n_rows = DEC_T * N_HEADS
    const = lambda shape: pl.BlockSpec(shape, lambda b: (0, 0))
    return pl.pallas_call(
        _attn_decode_kernel,
        grid=(bd // per,),
        in_specs=[const((n_rows, N_BUCKETS)), const((n_rows, DEC_KEYS)), const((n_rows, DEC_KEYS)),
                  new_spec, new_spec, new_spec, cache_spec, cache_spec],
        out_specs=new_spec,
        out_shape=jax.ShapeDtypeStruct((bd, DEC_T, MIX_A), F32),
        scratch_shapes=[pltpu.VMEM((n_rows, DEC_KEYS), F32)],
        compiler_params=_cparams(1),
        name="attn_decode",
    )(rbt, ids, mult, q, k_new, v_new, cache_kt, cache_vt)


def _ssm_prep_kernel(are_ref, aim_ref, ldt_ref, btr_ref, bti_ref, cre_ref, cim_ref,
                     tit_ref, tinr_ref, tini_ref, toutr_ref, touti_ref, pw_ref, tintr_ref, tinti_ref):
    hi = lax.Precision.HIGHEST
    nt = (((1,), (1,)), ((), ()))
    a_re, a_im = are_ref[...], aim_ref[...]
    dt = jnp.exp(ldt_ref[...])
    decay = jnp.exp(a_re * dt)
    ab_re = decay * jnp.cos(a_im * dt)
    ab_im = decay * jnp.sin(a_im * dt)
    inv = 1.0 / (a_re * a_re + a_im * a_im)
    coef_re = ((ab_re - 1.0) * a_re + ab_im * a_im) * inv
    coef_im = (ab_im * a_re - (ab_re - 1.0) * a_im) * inv
    bt_re, bt_im = btr_ref[...], bti_ref[...]
    bb_re = coef_re * bt_re - coef_im * bt_im
    bb_im = coef_re * bt_im + coef_im * bt_re
    pw = [(jnp.ones_like(ab_re), jnp.zeros_like(ab_im))]
    for _ in range(CHUNK):
        pr, pi = pw[-1]
        pw.append((pr * ab_re - pi * ab_im, pr * ab_im + pi * ab_re))
    tin_re = jnp.concatenate(
        [pw[CHUNK - 1 - ti][0] * bb_re - pw[CHUNK - 1 - ti][1] * bb_im for ti in range(CHUNK)], axis=0).astype(BF16)
    tin_im = jnp.concatenate(
        [pw[CHUNK - 1 - ti][0] * bb_im + pw[CHUNK - 1 - ti][1] * bb_re for ti in range(CHUNK)], axis=0).astype(BF16)
    tinr_ref[...] = tin_re
    tini_ref[...] = tin_im
    p_i = lax.broadcasted_iota(jnp.int32, (SSM_STATE, SSM_STATE), 0)
    p_j = lax.broadcasted_iota(jnp.int32, (SSM_STATE, SSM_STATE), 1)
    eye = jnp.where(p_i == p_j, 1.0, 0.0).astype(BF16)
    tintr_ref[...] = lax.dot_general(eye, tin_re, nt, preferred_element_type=F32).astype(BF16)
    tinti_ref[...] = lax.dot_general(eye, tin_im, nt, preferred_element_type=F32).astype(BF16)
    c_re, c_im = cre_ref[...], cim_ref[...]
    cp_re = [c_re * pr - c_im * pi for pr, pi in pw]
    cp_im = [c_re * pi + c_im * pr for pr, pi in pw]
    toutr_ref[...] = jnp.concatenate(cp_re[1:], axis=0).astype(BF16)
    touti_ref[...] = (-jnp.concatenate(cp_im[1:], axis=0)).astype(BF16)
    kall = (lax.dot_general(jnp.concatenate(cp_re[:CHUNK], axis=0), bb_re, nt, precision=hi,
                            preferred_element_type=F32)
            - lax.dot_general(jnp.concatenate(cp_im[:CHUNK], axis=0), bb_im, nt, precision=hi,
                              preferred_element_type=F32))
    w = CHUNK * SSM_GROUP
    kb = kall.astype(BF16)
    ci_idx = lax.broadcasted_iota(jnp.int32, (SSM_GROUP, w), 0)
    col_idx = lax.broadcasted_iota(jnp.int32, (SSM_GROUP, w), 1)
    acc = jnp.zeros((w, w), F32)
    for ti in range(CHUNK):
        rows = ti * SSM_GROUP
        shifted = kb if ti == 0 else jnp.concatenate(
            [jnp.zeros((rows, SSM_GROUP), BF16), kb[:w - rows]], axis=0)
        place = jnp.where(col_idx == ci_idx + rows, 1.0, 0.0).astype(BF16)
        acc = acc + jnp.dot(shifted, place, preferred_element_type=F32)
    tit_ref[...] = acc.astype(BF16)
    zero = jnp.zeros_like(ab_re)
    pw_ref[...] = jnp.concatenate([pw[CHUNK][0], pw[CHUNK][1], pw[DEC_T][0], pw[DEC_T][1],
                                   zero, zero, zero, zero], axis=0)


def _ssm_prep(a_re, a_im, log_dt, b_re, b_im, c_re, c_im):
    g, p = a_re.shape
    w = CHUNK * SSM_GROUP
    row = lambda a: a[:, None, :]
    bt_re = jnp.transpose(b_re, (0, 2, 1))
    bt_im = jnp.transpose(b_im, (0, 2, 1))
    ldt = jnp.broadcast_to(log_dt[:, None, None], (g, 1, p))
    gspec = lambda shape: pl.BlockSpec((None,) + shape, lambda gi: (gi, 0, 0))
    return pl.pallas_call(
        _ssm_prep_kernel,
        grid=(g,),
        in_specs=[gspec((1, p)), gspec((1, p)), gspec((1, p)), gspec((SSM_GROUP, p)), gspec((SSM_GROUP, p)),
                  gspec((SSM_GROUP, p)), gspec((SSM_GROUP, p))],
        out_specs=[gspec((w, w)), gspec((w, p)), gspec((w, p)), gspec((w, p)), gspec((w, p)), gspec((8, p)),
                   gspec((p, w)), gspec((p, w))],
        out_shape=[jax.ShapeDtypeStruct((g, w, w), BF16)] + [jax.ShapeDtypeStruct((g, w, p), BF16)] * 4
                  + [jax.ShapeDtypeStruct((g, 8, p), F32)] + [jax.ShapeDtypeStruct((g, p, w), BF16)] * 2,
        compiler_params=_cparams(1),
        name="ssm_prep",
    )(row(a_re), row(a_im), ldt, bt_re, bt_im, c_re, c_im)


def _ssm_prompt_kernel(u_ref, tit_ref, tintr_ref, tinti_ref, toutr_ref, touti_ref, d_ref, pw_ref,
                       y_ref, sre_ref, sim_ref, ut_scr, yt_scr, pr_scr, pi_scr, sr_scr, si_scr):
    ngrp = LANES // SSM_GROUP
    nc = u_ref.shape[0] // CHUNK
    xt = [u_ref[pl.ds(ti, nc, stride=CHUNK), :].T for ti in range(CHUNK)]
    for j in range(ngrp):
        ut = jnp.concatenate([x[j * SSM_GROUP:(j + 1) * SSM_GROUP, :] for x in xt], axis=0).astype(BF16)
        ut_scr[j] = ut
        pr_scr[j] = jnp.dot(tintr_ref[j], ut, preferred_element_type=F32).T
        pi_scr[j] = jnp.dot(tinti_ref[j], ut, preferred_element_type=F32).T
    ar = [pw_ref[j, 0:1, :] for j in range(ngrp)]
    ai = [pw_ref[j, 1:2, :] for j in range(ngrp)]

    def step(c, carry):
        out = []
        for j in range(ngrp):
            sr, si = carry[j]
            sr_scr[j, pl.ds(c, 1), :] = sr
            si_scr[j, pl.ds(c, 1), :] = si
            out.append((ar[j] * sr - ai[j] * si + pr_scr[j, pl.ds(c, 1), :],
                        ar[j] * si + ai[j] * sr + pi_scr[j, pl.ds(c, 1), :]))
        return tuple(out)

    zero = jnp.zeros((1, SSM_STATE), F32)
    final = lax.fori_loop(0, nc, step, ((zero, zero),) * ngrp, unroll=8)
    for j in range(ngrp):
        sre_ref[j] = final[j][0]
        sim_ref[j] = final[j][1]
        yt = jnp.dot(tit_ref[j], ut_scr[j], preferred_element_type=F32)
        yt = yt + jnp.dot(toutr_ref[j], sr_scr[j].T.astype(BF16), preferred_element_type=F32)
        yt = yt + jnp.dot(touti_ref[j], si_scr[j].T.astype(BF16), preferred_element_type=F32)
        yt_scr[j] = yt
    d = d_ref[...]
    for to in range(CHUNK):
        v = jnp.concatenate([yt_scr[j, to * SSM_GROUP:(to + 1) * SSM_GROUP, :] for j in range(ngrp)], axis=0)
        rows = pl.ds(to, nc, stride=CHUNK)
        y_ref[rows, :] = v.T + d * u_ref[rows, :]


def _ssm_prompt(u, ops, d_skip):
    b, t, _ = u.shape
    p = SSM_STATE
    w = CHUNK * SSM_GROUP
    nc = t // CHUNK
    ngrp = LANES // SSM_GROUP
    g = N_GROUPS
    gspec = lambda shape: pl.BlockSpec((ngrp,) + shape, lambda bi, qi: (qi, 0, 0))
    io = pl.BlockSpec((None, t, LANES), lambda bi, qi: (bi, 0, qi))
    st = pl.BlockSpec((None, ngrp, 1, p), lambda bi, qi: (bi, qi, 0, 0))
    tit, _, _, toutr, touti, pw, tintr, tinti = ops
    return pl.pallas_call(
        _ssm_prompt_kernel,
        grid=(b, g // ngrp),
        in_specs=[io, gspec((w, w)), gspec((p, w)), gspec((p, w)), gspec((w, p)), gspec((w, p)),
                  pl.BlockSpec((None, 1, LANES), lambda bi, qi: (qi, 0, 0)), gspec((8, p))],
        out_specs=[io, st, st],
        out_shape=[jax.ShapeDtypeStruct(u.shape, F32),
                   jax.ShapeDtypeStruct((b, g, 1, p), F32), jax.ShapeDtypeStruct((b, g, 1, p), F32)],
        scratch_shapes=[pltpu.VMEM((ngrp, w, nc), BF16), pltpu.VMEM((ngrp, w, nc), F32)]
                       + [pltpu.VMEM((ngrp, nc, p), F32)] * 4,
        compiler_params=_cparams(2),
        name="ssm_prompt",
    )(u, tit, tintr, tinti, toutr, touti, d_skip.reshape(g // ngrp, 1, LANES), pw)


def _ssm_decode_kernel(u_ref, s0r_ref, s0i_ref, tit_ref, tinr_ref, tini_ref, toutr_ref, touti_ref, d_ref,
                       pw_ref, y_ref, sre_ref, sim_ref):
    nt = (((1,), (1,)), ((), ()))
    w = DEC_T * SSM_GROUP
    lo = (CHUNK - DEC_T) * SSM_GROUP
    u = u_ref[...]
    ub = u.astype(BF16)
    s0r = s0r_ref[...]
    s0i = s0i_ref[...]
    y = lax.dot_general(ub, tit_ref[0:w, 0:w], nt, preferred_element_type=F32)
    y = y + lax.dot_general(s0r.astype(BF16), toutr_ref[0:w, :], nt, preferred_element_type=F32)
    y = y + lax.dot_general(s0i.astype(BF16), touti_ref[0:w, :], nt, preferred_element_type=F32)
    y_ref[...] = y + d_ref[:, 0:w] * u
    ar = pw_ref[2:3, :]
    ai = pw_ref[3:4, :]
    sre_ref[...] = ar * s0r - ai * s0i + jnp.dot(ub, tinr_ref[lo:, :], preferred_element_type=F32)
    sim_ref[...] = ar * s0i + ai * s0r + jnp.dot(ub, tini_ref[lo:, :], preferred_element_type=F32)


def _ssm_decode(ug, s0_re, s0_im, ops, d_tile):
    g, bd, w = ug.shape
    p = SSM_STATE
    wc = CHUNK * SSM_GROUP
    gspec = lambda shape: pl.BlockSpec((None,) + shape, lambda gi: (gi, 0, 0))
    tit, tinr, tini, toutr, touti, pw = ops[:6]
    return pl.pallas_call(
        _ssm_decode_kernel,
        grid=(g,),
        in_specs=[gspec((bd, w)), gspec((bd, p)), gspec((bd, p)), gspec((wc, wc)), gspec((wc, p)),
                  gspec((wc, p)), gspec((wc, p)), gspec((wc, p)), gspec((1, wc)), gspec((8, p))],
        out_specs=[gspec((bd, w)), gspec((bd, p)), gspec((bd, p))],
        out_shape=[jax.ShapeDtypeStruct((g, bd, w), F32), jax.ShapeDtypeStruct((g, bd, p), F32),
                   jax.ShapeDtypeStruct((g, bd, p), F32)],
        compiler_params=_cparams(1),
        name="ssm_decode",
    )(ug, s0_re, s0_im, tit, tinr, tini, toutr, touti, d_tile, pw)


def _gelu_tanh(x):
    return 0.5 * x * (1.0 + jnp.tanh(math.sqrt(2.0 / math.pi) * (x + 0.044715 * (x * x * x))))


def _sigmoid(x):
    return 1.0 / (1.0 + jnp.exp(-x))


def _route(hn, wr_ref):
    n = hn.shape[0]
    h_hi = hn.astype(BF16)
    h_lo = (hn - h_hi.astype(F32)).astype(BF16)
    prod = jnp.dot(jnp.concatenate([h_hi, h_lo], axis=0), wr_ref[...], preferred_element_type=F32)
    logits = (prod[:n, :ROUTER_LANES] + prod[:n, ROUTER_LANES:]
              + prod[n:, :ROUTER_LANES] + prod[n:, ROUTER_LANES:])
    lidx = lax.broadcasted_iota(jnp.int32, (n, ROUTER_LANES), 1)
    is_e = lidx < N_EXPERTS
    is_g = jnp.logical_and(lidx >= N_EXPERTS, lidx < N_EXPERTS + N_EXPERT_GROUPS)
    gmax = jnp.max(jnp.where(is_g, logits, -jnp.inf), axis=1, keepdims=True)
    g_prob = 1.0 / jnp.sum(jnp.where(is_g, jnp.exp(logits - gmax), 0.0), axis=1, keepdims=True)
    g_sel = jnp.min(jnp.where(jnp.logical_and(is_g, logits == gmax), lidx - N_EXPERTS, N_EXPERT_GROUPS),
                    axis=1, keepdims=True)
    in_grp = jnp.logical_and(is_e, (lidx // EXPERTS_PER_GROUP) == g_sel)
    l1 = jnp.max(jnp.where(in_grp, logits, -jnp.inf), axis=1, keepdims=True)
    i1 = jnp.min(jnp.where(jnp.logical_and(in_grp, logits == l1), lidx, ROUTER_LANES), axis=1, keepdims=True)
    rest = jnp.logical_and(in_grp, lidx != i1)
    l2 = jnp.max(jnp.where(rest, logits, -jnp.inf), axis=1, keepdims=True)
    i2 = jnp.min(jnp.where(jnp.logical_and(rest, logits == l2), lidx, ROUTER_LANES), axis=1, keepdims=True)
    e2 = jnp.exp(l2 - l1)
    w1 = g_prob / (1.0 + e2)
    w2 = g_prob * e2 / (1.0 + e2)
    return jnp.where(lidx == i1, w1, 0.0) + jnp.where(lidx == i2, w2, 0.0)


def _post_body(oa, ys, x, ga_ref, gb_ref, wglu_ref, bglu_ref, wout_ref, gf_ref, wr_ref,
               x1_ref, hn_ref, gates_ref):
    z = _gelu_tanh(ys)
    gate = _sigmoid(jnp.dot(z.astype(BF16), wglu_ref[...], preferred_element_type=F32) + bglu_ref[...])
    ob = z * gate
    mixed = jnp.concatenate([_rms(oa, ga_ref[...]), _rms(ob, gb_ref[...])], axis=1).astype(BF16)
    x1 = x + jnp.dot(mixed, wout_ref[...], preferred_element_type=F32)
    x1_ref[...] = x1
    hn = _rms(x1, gf_ref[...])
    half = hn.shape[1] // 2
    hn_ref[...] = pltpu.pack_elementwise([hn[:, :half], hn[:, half:]], packed_dtype=jnp.bfloat16)
    gates_ref[...] = _route(hn, wr_ref)


def _post_pair_kernel(oa_ref, ys_ref, x_ref, *rest):
    oa = jnp.concatenate([oa_ref[j] for j in range(MIX_A // LANES)], axis=1)
    _post_body(oa, ys_ref[...], x_ref[...], *rest)


def _post_tok_kernel(oa_ref, ys_ref, x_ref, *rest):
    _post_body(oa_ref[...], ys_ref[...], x_ref[...], *rest)


def _post_mix(oa, ys, x, w, tm, pair):
    b, t, d = x.shape
    row = lambda width: pl.BlockSpec((None, tm, width), lambda bi, i: (bi, i, 0))
    const = lambda a: pl.BlockSpec(a.shape, lambda bi, i: (0,) * a.ndim)
    oa_spec = (pl.BlockSpec((None, MIX_A // LANES, tm, LANES), lambda bi, i: (bi, 0, i, 0)) if pair
               else row(MIX_A))
    weights = [w['g_out_a'], w['g_out_b'], w['w_glu'], w['b_glu'], w['w_out'], w['g_ffn'], w['w_router']]
    return pl.pallas_call(
        _post_pair_kernel if pair else _post_tok_kernel,
        grid=(b, t // tm),
        in_specs=[oa_spec, row(MIX_B), row(d)] + [const(a) for a in weights],
        out_specs=[row(d), row(d // 2), row(ROUTER_LANES)],
        out_shape=[jax.ShapeDtypeStruct((b, t, d), F32), jax.ShapeDtypeStruct((b, t, d // 2), jnp.uint32),
                   jax.ShapeDtypeStruct((b, t, ROUTER_LANES), F32)],
        compiler_params=_cparams(2),
        name="post_mix_prompt" if pair else "post_mix_sample",
    )(oa, ys, x, *weights)


MOE_BLOCK = 2048
MOE_TILE_SLACK = 1.25
MOE_ALIGN = 8
MOE_EXPERTS_PER_STEP = 2
MOE_VMEM_LIMIT = 60 * 1024 * 1024
PLAN_CHUNK = 256
BF16_ROWS = 16


def _moe_tile(tb):
    mean = EXPERT_TOP_K * tb / N_EXPERTS
    rows = -(-int(MOE_TILE_SLACK * mean) // BF16_ROWS) * BF16_ROWS
    pitch = rows // 8 + (1 - (rows // 8) % 2)
    return rows, 8 * pitch


def _moe_rows(tb):
    rows = EXPERT_TOP_K * tb + N_EXPERTS * (MOE_ALIGN - 1) + _moe_tile(tb)[0]
    return -(-rows // MOE_ALIGN) * MOE_ALIGN


def _moe_plan_kernel(gates_ref, plan_ref, meta_ref, grow_ref):
    tb = gates_ref.shape[0]
    gates = gates_ref[...]
    hot = gates > 0.0
    onehot = jnp.where(hot, 1.0, 0.0).astype(BF16)
    ch = min(PLAN_CHUNK, tb)
    r_i = lax.broadcasted_iota(jnp.int32, (ch, ch), 0)
    c_i = lax.broadcasted_iota(jnp.int32, (ch, ch), 1)
    earlier = jnp.where(c_i < r_i, 1.0, 0.0).astype(BF16)
    carry = jnp.zeros((1, ROUTER_LANES), F32)
    ranks = []
    for k in range(tb // ch):
        oh = onehot[k * ch:(k + 1) * ch]
        ranks.append(jnp.dot(earlier, oh, preferred_element_type=F32) + carry)
        carry = carry + jnp.sum(oh.astype(F32), axis=0, keepdims=True)
    rank = jnp.concatenate(ranks, axis=0)
    seg = jnp.floor((carry + (MOE_ALIGN - 1.0)) * (1.0 / MOE_ALIGN)) * MOE_ALIGN
    l_i = lax.broadcasted_iota(jnp.int32, (ROUTER_LANES, ROUTER_LANES), 0)
    l_j = lax.broadcasted_iota(jnp.int32, (ROUTER_LANES, ROUTER_LANES), 1)
    before = jnp.where(l_i < l_j, 1.0, 0.0).astype(F32)
    offs = jnp.dot(jnp.broadcast_to(seg, (8, ROUTER_LANES)), before, precision=lax.Precision.HIGHEST,
                   preferred_element_type=F32)[0:1]
    pos = rank + offs
    lane = lax.broadcasted_iota(jnp.int32, (tb, ROUTER_LANES), 1)
    lane_a = jnp.min(jnp.where(hot, lane, ROUTER_LANES), axis=1, keepdims=True)
    lane_b = jnp.max(jnp.where(hot, lane, -1), axis=1, keepdims=True)
    pick = lambda sel, val: jnp.sum(jnp.where(sel, val, 0.0), axis=1, keepdims=True)
    sel_a, sel_b = lane == lane_a, lane == lane_b
    z = jnp.where(lane == 0, pick(sel_a, pos), 0.0) + jnp.where(lane == 1, pick(sel_b, pos), 0.0)
    plan_ref[...] = z.T[0:8, :]
    meta_ref[...] = jnp.concatenate([offs, carry, jnp.zeros((6, ROUTER_LANES), F32)], axis=0)
    grow_ref[...] = jnp.concatenate([jnp.broadcast_to(pick(sel_a, gates), (tb, LANES)),
                                     jnp.broadcast_to(pick(sel_b, gates), (tb, LANES))], axis=1)


def _moe_plan(gates, tb):
    n = gates.shape[0]
    nb = n // tb
    return pl.pallas_call(
        _moe_plan_kernel,
        grid=(nb,),
        in_specs=[pl.BlockSpec((tb, ROUTER_LANES), lambda i: (i, 0))],
        out_specs=[pl.BlockSpec((None, 8, tb), lambda i: (i, 0, 0)),
                   pl.BlockSpec((None, 8, ROUTER_LANES), lambda i: (i, 0, 0)),
                   pl.BlockSpec((tb, EXPERT_TOP_K * LANES), lambda i: (i, 0))],
        out_shape=[jax.ShapeDtypeStruct((nb, 8, tb), F32), jax.ShapeDtypeStruct((nb, 8, ROUTER_LANES), F32),
                   jax.ShapeDtypeStruct((n, EXPERT_TOP_K * LANES), F32)],
        compiler_params=_cparams(1),
        name="moe_plan",
    )(gates)


def _moe_kernel(tile_rows, pitch, offs_ref, cnts_ref, pos_ref, hn_ref, grow_ref, x1_ref, wg_ref, wu_ref, wd_ref,
                gfin_ref, y_ref, xs_scr, gcol_scr, otile_scr, yacc_scr, tok_scr):
    blk = pl.program_id(0)
    step = pl.program_id(1)
    tb = hn_ref.shape[0]
    n_lt = D_MODEL // LANES

    @pl.when(step == 0)
    def _group_rows():
        xs_scr[...] = jnp.zeros_like(xs_scr)
        gcol_scr[...] = jnp.zeros_like(gcol_scr)
        yacc_scr[...] = jnp.zeros_like(yacc_scr)

        def clear_pad(ex, c):
            end = offs_ref[blk, ex] + cnts_ref[blk, ex]
            for k in range(MOE_ALIGN - 1):
                tok_scr[end + k] = tb
            return c

        lax.fori_loop(0, N_EXPERTS, clear_pad, 0)

        def place(t8, c):
            for u in range(8):
                t = t8 * 8 + u
                row = hn_ref[pl.ds(t, 1), :]
                gate = grow_ref[pl.ds(t, 1), :]
                for s in range(EXPERT_TOP_K):
                    p = pos_ref[s, t]
                    xs_scr[pl.ds(p, 1), :] = row
                    gcol_scr[pl.ds(p, 1), :] = gate[:, s * LANES:(s + 1) * LANES]
                    tok_scr[p] = t
            return c

        lax.fori_loop(0, tb // 8, place, 0)

    slot_rows = n_lt * pitch

    def compute(k, r0, slot):
        xw = xs_scr[pl.ds(r0, tile_rows), :]
        x = jnp.concatenate(
            [pltpu.unpack_elementwise(xw, index=i, packed_dtype=jnp.bfloat16, unpacked_dtype=F32).astype(BF16)
             for i in range(2)], axis=1)
        g = gcol_scr[pl.ds(r0, tile_rows), :]
        a = jnp.dot(x, wg_ref[k], preferred_element_type=F32)
        u = jnp.dot(x, wu_ref[k], preferred_element_type=F32)
        act = (a * _sigmoid(a)) * u * jnp.concatenate([g] * (D_EXPERT // LANES), axis=1)
        out = jnp.dot(act.astype(BF16), wd_ref[k], preferred_element_type=F32)
        for s in range(n_lt):
            otile_scr[pl.ds(slot * slot_rows + s * pitch, tile_rows), :] = out[:, s * LANES:(s + 1) * LANES]

    def scatter_add(r0, valid, slot):
        def add8(j8, c3):
            base = pl.multiple_of(j8 * 8, 8)
            dst = [pl.ds(pl.multiple_of(tok_scr[r0 + base + v] * n_lt, n_lt), n_lt) for v in range(8)]
            rows = [yacc_scr[dst[v], :]
                    + otile_scr[pl.ds(slot * slot_rows + base + v, n_lt, stride=pitch), :]
                    for v in range(8)]
            for v in range(8):
                yacc_scr[dst[v], :] = rows[v]
            return c3

        lax.fori_loop(0, (valid + 7) // 8, add8, 0)

    offs = [offs_ref[blk, step * MOE_EXPERTS_PER_STEP + k] for k in range(MOE_EXPERTS_PER_STEP)]
    cnts = [cnts_ref[blk, step * MOE_EXPERTS_PER_STEP + k] for k in range(MOE_EXPERTS_PER_STEP)]
    for k in range(MOE_EXPERTS_PER_STEP):
        compute(k, pl.multiple_of(offs[k], MOE_ALIGN), k)
    for k in range(MOE_EXPERTS_PER_STEP):
        scatter_add(offs[k], jnp.minimum(tile_rows, cnts[k]), k)
    for k in range(MOE_EXPERTS_PER_STEP):
        def more(c, cc, k=k):
            r0 = pl.multiple_of(offs[k] + c * tile_rows, MOE_ALIGN)
            compute(k, r0, 0)
            scatter_add(r0, jnp.minimum(tile_rows, cnts[k] - c * tile_rows), 0)
            return cc

        lax.fori_loop(1, (cnts[k] + tile_rows - 1) // tile_rows, more, 0)

    @pl.when(step == pl.num_programs(1) - 1)
    def _fin():
        moe = jnp.concatenate([yacc_scr[pl.ds(s, tb, stride=n_lt), :] for s in range(n_lt)], axis=1)
        y_ref[...] = _rms(x1_ref[...] + moe, gfin_ref[...])


def _moe(hn_packed, gates, x1, wg, wu, wd, g_final, tb):
    n, d = x1.shape
    nb = n // tb
    plan, meta, gate_rows = _moe_plan(gates, tb)
    pos = plan[:, 0:2, :].astype(jnp.int32)
    offs = meta[:, 0, :N_EXPERTS].astype(jnp.int32)
    cnts = meta[:, 1, :N_EXPERTS].astype(jnp.int32)
    p_rows = _moe_rows(tb)
    tile_rows, pitch = _moe_tile(tb)
    per = MOE_EXPERTS_PER_STEP
    n_lt = d // LANES
    smem = lambda: pl.BlockSpec((None, 2, tb), lambda i, e, *_: (i, 0, 0), memory_space=pltpu.SMEM)
    once = dict(pipeline_mode=pl.Buffered(1))
    grid_spec = pltpu.PrefetchScalarGridSpec(
        num_scalar_prefetch=2,
        grid=(nb, N_EXPERTS // per),
        in_specs=[smem(),
                  pl.BlockSpec((tb, d // 2), lambda i, e, *_: (i, 0), **once),
                  pl.BlockSpec((tb, EXPERT_TOP_K * LANES), lambda i, e, *_: (i, 0), **once),
                  pl.BlockSpec((tb, d), lambda i, e, *_: (i, 0), **once),
                  pl.BlockSpec((per, d, D_EXPERT), lambda i, e, *_: (e, 0, 0)),
                  pl.BlockSpec((per, d, D_EXPERT), lambda i, e, *_: (e, 0, 0)),
                  pl.BlockSpec((per, D_EXPERT, d), lambda i, e, *_: (e, 0, 0)),
                  pl.BlockSpec((1, d), lambda i, e, *_: (0, 0))],
        out_specs=pl.BlockSpec((tb, d), lambda i, e, *_: (i, 0)),
        scratch_shapes=[pltpu.VMEM((p_rows, d // 2), jnp.uint32),
                        pltpu.VMEM((p_rows, LANES), F32),
                        pltpu.VMEM((per * n_lt * pitch, LANES), F32),
                        pltpu.VMEM(((tb + 1) * n_lt, LANES), F32),
                        pltpu.SMEM((p_rows,), jnp.int32)])
    return pl.pallas_call(
        functools.partial(_moe_kernel, tile_rows, pitch),
        grid_spec=grid_spec,
        out_shape=jax.ShapeDtypeStruct((n, d), F32),
        compiler_params=pltpu.CompilerParams(dimension_semantics=("arbitrary", "arbitrary"),
                                             vmem_limit_bytes=MOE_VMEM_LIMIT),
        name="moe",
    )(offs, cnts, pos, hn_packed, gate_rows, x1, wg, wu, wd, g_final)


def kernel(x_prompt, x_sample, cache_k, cache_v, state_ssm_re, state_ssm_im, rel_bias, g_mix, w_in, g_out_a, g_out_b, w_out, ssm_a_re, ssm_a_im, ssm_log_dt, ssm_b_re, ssm_b_im, ssm_c_re, ssm_c_im, ssm_d, w_glu, b_glu, g_ffn, w_router_group, w_router_expert, w_expert_gate, w_expert_up, w_expert_down, g_final):
    depth = g_mix.shape[0]
    assert depth == 1, "kernel is written for the single-layer configuration of the problem"
    l = 0
    b, t, d = x_prompt.shape
    bd, ts, _ = x_sample.shape
    assert ts == DEC_T and t % ATT_BLOCK == 0 and cache_k.shape[2] == MAX_WINDOW
    keep = min(MAX_WINDOW, t)

    w_in_b = w_in[l].astype(BF16)
    gm = g_mix[l][None, :]
    w_router = jnp.concatenate(
        [jnp.transpose(w_router_expert[l], (1, 0, 2)).reshape(d, N_EXPERTS), w_router_group[l],
         jnp.zeros((d, ROUTER_LANES - N_EXPERTS - N_EXPERT_GROUPS), F32)], axis=1)
    w_router_hi = w_router.astype(BF16)
    w_router_lo = (w_router - w_router_hi.astype(F32)).astype(BF16)
    post_w = dict(
        g_out_a=g_out_a[l][None, :], g_out_b=g_out_b[l][None, :], w_glu=w_glu[l].astype(BF16),
        b_glu=b_glu[l][None, :], w_out=w_out[l].astype(BF16), g_ffn=g_ffn[l][None, :],
        w_router=jnp.concatenate([w_router_hi, w_router_lo], axis=1))
    wg = w_expert_gate[l].astype(BF16)
    wu = w_expert_up[l].astype(BF16)
    wd = w_expert_down[l].astype(BF16)
    gfin = g_final[None, :]
    ssm_ops = _ssm_prep(ssm_a_re[l], ssm_a_im[l], ssm_log_dt[l], ssm_b_re[l], ssm_b_im[l],
                        ssm_c_re[l], ssm_c_im[l])
    d_tile = jnp.tile(ssm_d[l].reshape(N_GROUPS, 1, SSM_GROUP), (1, 1, CHUNK))

    qp, kp, vp, kt_win, vt_win, u_tok = _inproj_pair(x_prompt, gm, w_in_b, tm=512, keep=keep)
    oa_p = _attn_prompt(qp, kp, vp, rel_bias)
    ys_p, rp, ip = _ssm_prompt(u_tok, ssm_ops, ssm_d[l])
    x1_p, hn_p, gates_p = _post_mix(oa_p, ys_p, x_prompt, post_w, tm=512, pair=True)
    y_p = _moe(hn_p.reshape(b * t, d // 2), gates_p.reshape(b * t, ROUTER_LANES), x1_p.reshape(b * t, d),
               wg, wu, wd, gfin, tb=min(MOE_BLOCK, b * t))

    n_s = bd * ts
    qs, ks, vs, us = _inproj_tok(x_sample.reshape(n_s, d), gm, w_in_b)
    ckt = jnp.transpose(cache_k[l], (0, 2, 3, 1)).reshape(bd, MIX_A, MAX_WINDOW)
    cvt = jnp.transpose(cache_v[l], (0, 2, 3, 1)).reshape(bd, MIX_A, MAX_WINDOW)
    oa_s = _attn_decode(qs.reshape(bd, ts, MIX_A), ks.reshape(bd, ts, MIX_A), vs.reshape(bd, ts, MIX_A),
                        ckt, cvt, rel_bias)
    ugs = jnp.transpose(us.reshape(bd, ts, N_GROUPS, SSM_GROUP), (2, 0, 1, 3)).reshape(N_GROUPS, bd, ts * SSM_GROUP)
    s0r = jnp.transpose(state_ssm_re[l], (1, 0, 2))
    s0i = jnp.transpose(state_ssm_im[l], (1, 0, 2))
    ygs, rs, is_ = _ssm_decode(ugs, s0r, s0i, ssm_ops, d_tile)
    ys_s = jnp.transpose(ygs.reshape(N_GROUPS, bd, ts, SSM_GROUP), (1, 2, 0, 3)).reshape(n_s, MIX_B)
    x1_s, hn_s, gates_s = _post_mix(oa_s.reshape(1, n_s, MIX_A), ys_s.reshape(1, n_s, MIX_B),
                                    x_sample.reshape(1, n_s, d), post_w, tm=n_s, pair=False)
    y_s = _moe(hn_s.reshape(n_s, d // 2), gates_s.reshape(n_s, ROUTER_LANES), x1_s.reshape(n_s, d),
               wg, wu, wd, gfin, tb=n_s)

    y_prompt = y_p.reshape(b, t, d)
    y_sample = y_s.reshape(bd, ts, d)
    k_win = jnp.transpose(kt_win.reshape(b, N_HEADS, HEAD_DIM, keep), (0, 3, 1, 2))[None]
    v_win = jnp.transpose(vt_win.reshape(b, N_HEADS, HEAD_DIM, keep), (0, 3, 1, 2))[None]
    k_new = ks.reshape(1, bd, ts, N_HEADS, HEAD_DIM)
    v_new = vs.reshape(1, bd, ts, N_HEADS, HEAD_DIM)
    return (y_prompt, y_sample, k_win, v_win, k_new, v_new,
            rp.reshape(1, b, N_GROUPS, SSM_STATE), ip.reshape(1, b, N_GROUPS, SSM_STATE),
            jnp.transpose(rs, (1, 0, 2))[None], jnp.transpose(is_, (1, 0, 2))[None])
```

```python
import functools
import math

import jax
import jax.numpy as jnp
from jax import lax
from jax.experimental import pallas as pl
from jax.experimental.pallas import tpu as pltpu

F32 = jnp.float32
BF16 = jnp.bfloat16

D_MODEL = 1024
HEAD_DIM = 64
MIX_A = 512
N_HEADS = 8
MIX_B = 512
SSM_GROUP = 16
N_GROUPS = 32
SSM_STATE = 64
PROJ_COLS = 3 * MIX_A + MIX_B
DILATIONS = (1, 4, 16)
TAPS = 128
MAX_WINDOW = 2048
N_BUCKETS = 32
N_EXPERT_GROUPS = 4
EXPERTS_PER_GROUP = 8
N_EXPERTS = 32
EXPERT_TOP_K = 2
D_EXPERT = 256
EPS = 1e-6
NEG_INF = -1e30
SCALE = HEAD_DIM ** -0.5

LANES = 128
ROUTER_LANES = 128
CHUNK = 16
ATT_BLOCK = 2048
ATT_UNROLL = 16
ATT_PITCH = 136
VMEM_LIMIT = 56 * 1024 * 1024


def _cparams(n_axes):
    return pltpu.CompilerParams(dimension_semantics=("arbitrary",) * n_axes,
                                vmem_limit_bytes=VMEM_LIMIT)


def _t5_bucket(distance):
    max_exact = N_BUCKETS // 2
    nf = jnp.maximum(distance, 1).astype(F32)
    large = max_exact + jnp.floor(jnp.log(nf / max_exact) / math.log(MAX_WINDOW / max_exact)
                                  * (N_BUCKETS - max_exact)).astype(jnp.int32)
    large = jnp.minimum(large, N_BUCKETS - 1)
    return jnp.where(distance < max_exact, distance, large)


def _rms(x, g):
    return x * lax.rsqrt(jnp.mean(x * x, axis=-1, keepdims=True) + EPS) * g


def _inproj_pair_kernel(first_win_tile, x_ref, g_ref, w_ref, q_ref, k_ref, v_ref, kt_ref, vt_ref, u_ref):
    h = _rms(x_ref[...], g_ref[...])
    p = jnp.dot(h.astype(BF16), w_ref[...], preferred_element_type=F32)
    for j in range(MIX_A // LANES):
        q_ref[j] = p[:, LANES * j:LANES * (j + 1)] * SCALE
        k_ref[j] = p[:, MIX_A + LANES * j:MIX_A + LANES * (j + 1)]
        v_ref[j] = p[:, 2 * MIX_A + LANES * j:2 * MIX_A + LANES * (j + 1)]
    u_ref[...] = p[:, 3 * MIX_A:]

    @pl.when(pl.program_id(1) >= first_win_tile)
    def _window():
        kt_ref[...] = p[:, MIX_A:2 * MIX_A].T
        vt_ref[...] = p[:, 2 * MIX_A:3 * MIX_A].T


def _inproj_pair(x, g, w_bf16, tm, keep):
    b, t, d = x.shape
    npair = MIX_A // LANES
    first = (t - keep) // tm
    pair = jax.ShapeDtypeStruct((b, npair, t, LANES), F32)
    win = jax.ShapeDtypeStruct((b, MIX_A, keep), F32)
    tok = jax.ShapeDtypeStruct((b, t, MIX_A), F32)
    pair_spec = pl.BlockSpec((None, npair, tm, LANES), lambda bi, i: (bi, 0, i, 0))
    win_spec = pl.BlockSpec((None, MIX_A, tm), lambda bi, i: (bi, 0, jnp.maximum(i - first, 0)))
    tok_spec = pl.BlockSpec((None, tm, MIX_A), lambda bi, i: (bi, i, 0))
    return pl.pallas_call(
        functools.partial(_inproj_pair_kernel, first),
        grid=(b, t // tm),
        in_specs=[pl.BlockSpec((None, tm, d), lambda bi, i: (bi, i, 0)),
                  pl.BlockSpec((1, d), lambda bi, i: (0, 0)),
                  pl.BlockSpec((d, PROJ_COLS), lambda bi, i: (0, 0))],
        out_specs=[pair_spec, pair_spec, pair_spec, win_spec, win_spec, tok_spec],
        out_shape=[pair, pair, pair, win, win, tok],
        compiler_params=_cparams(2),
        name="inproj_prompt",
    )(x, g, w_bf16)


def _inproj_tok_kernel(x_ref, g_ref, w_ref, q_ref, k_ref, v_ref, u_ref):
    h = _rms(x_ref[...], g_ref[...])
    p = jnp.dot(h.astype(BF16), w_ref[...], preferred_element_type=F32)
    q_ref[...] = p[:, :MIX_A] * SCALE
    k_ref[...] = p[:, MIX_A:2 * MIX_A]
    v_ref[...] = p[:, 2 * MIX_A:3 * MIX_A]
    u_ref[...] = p[:, 3 * MIX_A:]


def _inproj_tok(x, g, w_bf16):
    n, d = x.shape
    out = jax.ShapeDtypeStruct((n, MIX_A), F32)
    return pl.pallas_call(
        _inproj_tok_kernel,
        out_shape=[out, out, out, out],
        compiler_params=pltpu.CompilerParams(vmem_limit_bytes=VMEM_LIMIT),
        name="inproj_sample",
    )(x, g, w_bf16)


def _prompt_bias_ids():
    r = jnp.arange(TAPS, dtype=jnp.int32)[:, None]
    kpos = jnp.arange(2 * TAPS, dtype=jnp.int32)[None, :] - TAPS
    rel = r - kpos
    valid = (rel >= 0) & (rel <= TAPS)
    ids = [jnp.where(valid, _t5_bucket(d * rel), -1) for d in DILATIONS]
    return jnp.stack(ids).astype(jnp.int32)


def _attn_prompt_kernel(rb_ref, ids_ref, q_ref, kc_ref, kp_ref, vc_ref, vp_ref, o_ref,
                        bias_scr, fm_scr, kcat, vcat, m0_scr, l0_scr, o0_scr, m1_scr, l1_scr, o1_scr,
                        m2_scr, l2_scr, o2_scr):
    bi = pl.program_id(0)
    p = pl.program_id(1)
    i = pl.program_id(2)
    blk = ATT_BLOCK
    npair = N_HEADS // 2

    @pl.when((bi == 0) & (p == 0) & (i == 0))
    def _build_bias():
        col = lax.broadcasted_iota(jnp.int32, (2 * TAPS, 2 * TAPS), 1)
        fm_scr[0] = jnp.zeros((2 * TAPS, 2 * TAPS), F32)
        fm_scr[1] = jnp.where(col < TAPS, NEG_INF, 0.0).astype(F32)
        for c in range(len(DILATIONS)):
            ids = ids_ref[c]

            for h in range(N_HEADS):
                tile = jnp.where(ids < 0, NEG_INF, 0.0).astype(F32)
                for bkt in range(N_BUCKETS):
                    tile = tile + jnp.where(ids == bkt, rb_ref[bkt:bkt + 1, h:h + 1], 0.0)
                bias_scr[c * npair + h // 2, (h % 2) * TAPS:(h % 2 + 1) * TAPS, :] = tile

    kcat[pl.ds(0, blk), :] = kp_ref[...]
    kcat[pl.ds(blk, blk), :] = kc_ref[...]
    vcat[pl.ds(0, blk), :] = vp_ref[...]
    vcat[pl.ds(blk, blk), :] = vc_ref[...]

    lane = lax.broadcasted_iota(jnp.int32, (1, LANES), 1)
    low = lane < HEAD_DIM
    nt = (((1,), (1,)), ((), ()))
    ones = jnp.ones((2 * TAPS, LANES), BF16)
    stats = ((m0_scr, l0_scr, o0_scr), (m1_scr, l1_scr, o1_scr), (m2_scr, l2_scr, o2_scr))

    def rows(ref, start, d):
        if d == 1:
            return ref[pl.ds(start, TAPS), :]
        return ref[pl.ds(start, TAPS, stride=d), :]

    def sub_block(c, d, qs, first, dst_rows):
        q = rows(q_ref, qs, d)
        q2 = jnp.concatenate([jnp.where(low, q, 0.0), jnp.where(low, 0.0, q)], axis=0).astype(BF16)
        k = jnp.concatenate([rows(kcat, blk + qs - d * TAPS, d), rows(kcat, blk + qs, d)],
                            axis=0).astype(BF16)
        v = jnp.concatenate([rows(vcat, blk + qs - d * TAPS, d), rows(vcat, blk + qs, d)],
                            axis=0).astype(BF16)
        s_ = lax.dot_general(q2, k, nt, preferred_element_type=F32) + bias_scr[c * npair + p]
        if first is not None:
            s_ = s_ + fm_scr[first]
        m = jnp.max(s_, axis=1, keepdims=True)
        pe = jnp.exp(s_ - m).astype(BF16)
        ol = jnp.dot(pe, jnp.concatenate([v, ones], axis=1), preferred_element_type=F32)
        o, l = ol[:, :LANES], ol[:, LANES:]
        m_ref, l_ref, a_ref = stats[c]
        m_ref[dst_rows, :] = jnp.where(low, m[:TAPS], m[TAPS:])
        l_ref[dst_rows, :] = jnp.where(low, l[:TAPS], l[TAPS:])
        a_ref[dst_rows, :] = jnp.where(low, o[:TAPS], o[TAPS:])

    at_start = jnp.where(i == 0, 1, 0)
    n_sub = blk // TAPS
    for c, d in enumerate(DILATIONS):
        n_grp = blk // (d * TAPS)

        def group_body(s8, carry, c=c, d=d, n_grp=n_grp):
            for u in range(ATT_UNROLL):
                s = s8 * ATT_UNROLL + u
                if n_grp >= ATT_UNROLL:
                    qs = pl.multiple_of(s * TAPS, TAPS)
                    first = jnp.where(s8 == 0, at_start, 0) if u == 0 else None
                    dst_rows = pl.ds(qs, TAPS)
                else:
                    r = s8 * (ATT_UNROLL // n_grp) + u // n_grp
                    g = u % n_grp
                    qs = r + d * TAPS * g
                    first = at_start if g == 0 else None
                    if n_grp > 1:
                        dst_rows = pl.ds(qs, TAPS, stride=d)
                    else:
                        dst_rows = pl.ds(pl.multiple_of(r * ATT_PITCH, 8), TAPS)
                sub_block(c, d, qs, first, dst_rows)
            return carry

        lax.fori_loop(0, n_sub // ATT_UNROLL, group_body, 0)

    d_last = DILATIONS[-1]

    def merge_body(j, carry):
        tok = pl.ds(pl.multiple_of(j * d_last, d_last), d_last)
        dil = pl.ds(j, d_last, stride=ATT_PITCH)
        m0, m1, m2 = m0_scr[tok, :], m1_scr[tok, :], m2_scr[dil, :]
        m = jnp.maximum(jnp.maximum(m0, m1), m2)
        a0, a1, a2 = jnp.exp(m0 - m), jnp.exp(m1 - m), jnp.exp(m2 - m)
        num = o0_scr[tok, :] * a0 + o1_scr[tok, :] * a1 + o2_scr[dil, :] * a2
        den = l0_scr[tok, :] * a0 + l1_scr[tok, :] * a1 + l2_scr[dil, :] * a2
        o_ref[tok, :] = num / den
        return carry

    lax.fori_loop(0, blk // d_last, merge_body, 0, unroll=4)


def _attn_prompt(q, k, v, rel_bias):
    b, npair, t, _ = q.shape
    blk = ATT_BLOCK
    cur = pl.BlockSpec((None, None, blk, LANES), lambda bi, p, i: (bi, p, i, 0))
    prev = pl.BlockSpec((None, None, blk, LANES), lambda bi, p, i: (bi, p, jnp.maximum(i - 1, 0), 0))
    n_tiles = len(DILATIONS) * npair
    stats = ([pltpu.VMEM((blk, LANES), F32)] * 6
             + [pltpu.VMEM((DILATIONS[-1] * ATT_PITCH, LANES), F32)] * 3)
    return pl.pallas_call(
        _attn_prompt_kernel,
        grid=(b, npair, t // blk),
        in_specs=[pl.BlockSpec((N_BUCKETS, N_HEADS), lambda bi, p, i: (0, 0)),
                  pl.BlockSpec((len(DILATIONS), TAPS, 2 * TAPS), lambda bi, p, i: (0, 0, 0)),
                  cur, cur, prev, cur, prev],
        out_specs=cur,
        out_shape=jax.ShapeDtypeStruct(q.shape, F32),
        scratch_shapes=[pltpu.VMEM((n_tiles, 2 * TAPS, 2 * TAPS), F32),
                        pltpu.VMEM((2, 2 * TAPS, 2 * TAPS), F32),
                        pltpu.VMEM((2 * blk, LANES), F32),
                        pltpu.VMEM((2 * blk, LANES), F32)] + stats,
        compiler_params=_cparams(3),
        name="attn_prompt",
    )(rel_bias, _prompt_bias_ids(), q, k, k, v, v)


DEC_T = 4
DEC_NEW_PAD = 128
DEC_SEQ_PER_STEP = 2
DEC_KEYS = MAX_WINDOW + DEC_NEW_PAD


def _decode_tables():
    qi = jnp.arange(DEC_T, dtype=jnp.int32)[:, None]
    rho = jnp.arange(MAX_WINDOW, dtype=jnp.int32)[None, :]
    dist_b = MAX_WINDOW + qi - rho
    mult_b = jnp.zeros_like(dist_b)
    for d in DILATIONS:
        mult_b = mult_b + (((dist_b % d) == 0) & (dist_b <= d * TAPS)).astype(jnp.int32)
    jj = jnp.arange(DEC_NEW_PAD, dtype=jnp.int32)[None, :]
    dist_n = qi - jj
    mult_n = jnp.where(dist_n == 0, len(DILATIONS), jnp.where((dist_n > 0) & (jj < DEC_T), 1, 0))
    dist = jnp.concatenate([dist_b, jnp.maximum(dist_n, 0)], axis=1)
    mult = jnp.concatenate([mult_b, mult_n.astype(jnp.int32)], axis=1)
    ids = jnp.where(mult > 0, _t5_bucket(dist), -1).astype(jnp.int32)
    ids = jnp.repeat(ids, N_HEADS, axis=0)
    mult = jnp.repeat(mult, N_HEADS, axis=0).astype(F32)
    return ids, mult


def _attn_decode_kernel(rbt_ref, ids_ref, mult_ref, q_ref, kn_ref, vn_ref, kt_ref, vt_ref, o_ref, bias_scr):
    n_rows = DEC_T * N_HEADS

    @pl.when(pl.program_id(0) == 0)
    def _build_bias():
        ids = ids_ref[...]
        tile = jnp.where(ids < 0, NEG_INF, 0.0).astype(F32)
        for bkt in range(N_BUCKETS):
            tile = tile + jnp.where(ids == bkt, rbt_ref[:, bkt:bkt + 1], 0.0)
        bias_scr[...] = tile

    lane = lax.broadcasted_iota(jnp.int32, (N_HEADS, MIX_A), 1)
    head = lax.broadcasted_iota(jnp.int32, (N_HEADS, MIX_A), 0)
    hmask = (lane // HEAD_DIM) == head

    zpad = jnp.zeros((DEC_NEW_PAD - DEC_T, MIX_A), F32)
    nt = (((1,), (1,)), ((), ()))
    for j in range(q_ref.shape[0]):
        q = q_ref[j]
        qm = jnp.where(hmask[None], q[:, None, :], 0.0).reshape(n_rows, MIX_A).astype(BF16)
        kn = jnp.concatenate([kn_ref[j], zpad], axis=0).astype(BF16)
        vn = jnp.concatenate([vn_ref[j], zpad], axis=0).astype(BF16)
        kt = kt_ref[j].astype(BF16)
        vt = vt_ref[j].astype(BF16)
        s = jnp.concatenate([jnp.dot(qm, kt, preferred_element_type=F32),
                             lax.dot_general(qm, kn, nt, preferred_element_type=F32)], axis=1)
        s = s + bias_scr[...]
        m = jnp.max(s, axis=1, keepdims=True)
        pe = jnp.exp(s - m) * mult_ref[...]
        l = jnp.sum(pe, axis=1, keepdims=True)
        pb = pe.astype(BF16)
        o = (lax.dot_general(pb[:, :MAX_WINDOW], vt, nt, preferred_element_type=F32)
             + jnp.dot(pb[:, MAX_WINDOW:], vn, preferred_element_type=F32)) / l
        o3 = o.reshape(DEC_T, N_HEADS, MIX_A)
        o_ref[j] = jnp.sum(jnp.where(hmask[None], o3, 0.0), axis=1)


def _attn_decode(q, k_new, v_new, cache_kt, cache_vt, rel_bias):
    bd = q.shape[0]
    ids, mult = _decode_tables()
    rbt = jnp.tile(rel_bias.T, (DEC_T, 1))
    per = DEC_SEQ_PER_STEP if bd % DEC_SEQ_PER_STEP == 0 else 1
    new_spec = pl.BlockSpec((per, DEC_T, MIX_A), lambda b: (b, 0, 0))
    cache_spec = pl.BlockSpec((per, MIX_A, MAX_WINDOW), lambda b: (b, 0, 0))
    n_rows = DEC_T * N_HEADS
    const = lambda shape: pl.BlockSpec(shape, lambda b: (0, 0))
    return pl.pallas_call(
        _attn_decode_kernel,
        grid=(bd // per,),
        in_specs=[const((n_rows, N_BUCKETS)), const((n_rows, DEC_KEYS)), const((n_rows, DEC_KEYS)),
                  new_spec, new_spec, new_spec, cache_spec, cache_spec],
        out_specs=new_spec,
        out_shape=jax.ShapeDtypeStruct((bd, DEC_T, MIX_A), F32),
        scratch_shapes=[pltpu.VMEM((n_rows, DEC_KEYS), F32)],
        compiler_params=_cparams(1),
        name="attn_decode",
    )(rbt, ids, mult, q, k_new, v_new, cache_kt, cache_vt)


def _ssm_prep_kernel(are_ref, aim_ref, ldt_ref, btr_ref, bti_ref, cre_ref, cim_ref,
                     tit_ref, tinr_ref, tini_ref, toutr_ref, touti_ref, pw_ref, tintr_ref, tinti_ref):
    hi = lax.Precision.HIGHEST
    nt = (((1,), (1,)), ((), ()))
    a_re, a_im = are_ref[...], aim_ref[...]
    dt = jnp.exp(ldt_ref[...])
    decay = jnp.exp(a_re * dt)
    ab_re = decay * jnp.cos(a_im * dt)
    ab_im = decay * jnp.sin(a_im * dt)
    inv = 1.0 / (a_re * a_re + a_im * a_im)
    coef_re = ((ab_re - 1.0) * a_re + ab_im * a_im) * inv
    coef_im = (ab_im * a_re - (ab_re - 1.0) * a_im) * inv
    bt_re, bt_im = btr_ref[...], bti_ref[...]
    bb_re = coef_re * bt_re - coef_im * bt_im
    bb_im = coef_re * bt_im + coef_im * bt_re
    pw = [(jnp.ones_like(ab_re), jnp.zeros_like(ab_im))]
    for _ in range(CHUNK):
        pr, pi = pw[-1]
        pw.append((pr * ab_re - pi * ab_im, pr * ab_im + pi * ab_re))
    tin_re = jnp.concatenate(
        [pw[CHUNK - 1 - ti][0] * bb_re - pw[CHUNK - 1 - ti][1] * bb_im for ti in range(CHUNK)], axis=0).astype(BF16)
    tin_im = jnp.concatenate(
        [pw[CHUNK - 1 - ti][0] * bb_im + pw[CHUNK - 1 - ti][1] * bb_re for ti in range(CHUNK)], axis=0).astype(BF16)
    tinr_ref[...] = tin_re
    tini_ref[...] = tin_im
    p_i = lax.broadcasted_iota(jnp.int32, (SSM_STATE, SSM_STATE), 0)
    p_j = lax.broadcasted_iota(jnp.int32, (SSM_STATE, SSM_STATE), 1)
    eye = jnp.where(p_i == p_j, 1.0, 0.0).astype(BF16)
    tintr_ref[...] = lax.dot_general(eye, tin_re, nt, preferred_element_type=F32).astype(BF16)
    tinti_ref[...] = lax.dot_general(eye, tin_im, nt, preferred_element_type=F32).astype(BF16)
    c_re, c_im = cre_ref[...], cim_ref[...]
    cp_re = [c_re * pr - c_im * pi for pr, pi in pw]
    cp_im = [c_re * pi + c_im * pr for pr, pi in pw]
    toutr_ref[...] = jnp.concatenate(cp_re[1:], axis=0).astype(BF16)
    touti_ref[...] = (-jnp.concatenate(cp_im[1:], axis=0)).astype(BF16)
    kall = (lax.dot_general(jnp.concatenate(cp_re[:CHUNK], axis=0), bb_re, nt, precision=hi,
                            preferred_element_type=F32)
            - lax.dot_general(jnp.concatenate(cp_im[:CHUNK], axis=0), bb_im, nt, precision=hi,
                              preferred_element_type=F32))
    w = CHUNK * SSM_GROUP
    kb = kall.astype(BF16)
    ci_idx = lax.broadcasted_iota(jnp.int32, (SSM_GROUP, w), 0)
    col_idx = lax.broadcasted_iota(jnp.int32, (SSM_GROUP, w), 1)
    acc = jnp.zeros((w, w), F32)
    for ti in range(CHUNK):
        rows = ti * SSM_GROUP
        shifted = kb if ti == 0 else jnp.concatenate(
            [jnp.zeros((rows, SSM_GROUP), BF16), kb[:w - rows]], axis=0)
        place = jnp.where(col_idx == ci_idx + rows, 1.0, 0.0).astype(BF16)
        acc = acc + jnp.dot(shifted, place, preferred_element_type=F32)
    tit_ref[...] = acc.astype(BF16)
    zero = jnp.zeros_like(ab_re)
    pw_ref[...] = jnp.concatenate([pw[CHUNK][0], pw[CHUNK][1], pw[DEC_T][0], pw[DEC_T][1],
                                   zero, zero, zero, zero], axis=0)


def _ssm_prep(a_re, a_im, log_dt, b_re, b_im, c_re, c_im):
    g, p = a_re.shape
    w = CHUNK * SSM_GROUP
    row = lambda a: a[:, None, :]
    bt_re = jnp.transpose(b_re, (0, 2, 1))
    bt_im = jnp.transpose(b_im, (0, 2, 1))
    ldt = jnp.broadcast_to(log_dt[:, None, None], (g, 1, p))
    gspec = lambda shape: pl.BlockSpec((None,) + shape, lambda gi: (gi, 0, 0))
    return pl.pallas_call(
        _ssm_prep_kernel,
        grid=(g,),
        in_specs=[gspec((1, p)), gspec((1, p)), gspec((1, p)), gspec((SSM_GROUP, p)), gspec((SSM_GROUP, p)),
                  gspec((SSM_GROUP, p)), gspec((SSM_GROUP, p))],
        out_specs=[gspec((w, w)), gspec((w, p)), gspec((w, p)), gspec((w, p)), gspec((w, p)), gspec((8, p)),
                   gspec((p, w)), gspec((p, w))],
        out_shape=[jax.ShapeDtypeStruct((g, w, w), BF16)] + [jax.ShapeDtypeStruct((g, w, p), BF16)] * 4
                  + [jax.ShapeDtypeStruct((g, 8, p), F32)] + [jax.ShapeDtypeStruct((g, p, w), BF16)] * 2,
        compiler_params=_cparams(1),
        name="ssm_prep",
    )(row(a_re), row(a_im), ldt, bt_re, bt_im, c_re, c_im)


def _ssm_prompt_kernel(u_ref, tit_ref, tintr_ref, tinti_ref, toutr_ref, touti_ref, d_ref, pw_ref,
                       y_ref, sre_ref, sim_ref, ut_scr, yt_scr, pr_scr, pi_scr, sr_scr, si_scr):
    ngrp = LANES // SSM_GROUP
    nc = u_ref.shape[0] // CHUNK
    xt = [u_ref[pl.ds(ti, nc, stride=CHUNK), :].T for ti in range(CHUNK)]
    lane_pad = jnp.zeros((nc, LANES - SSM_STATE), F32)
    for j in range(ngrp):
        ut = jnp.concatenate([x[j * SSM_GROUP:(j + 1) * SSM_GROUP, :] for x in xt], axis=0).astype(BF16)
        ut_scr[j] = ut
        rows_j = pl.ds(j, nc, stride=ngrp)
        pr_scr[rows_j, :] = jnp.concatenate(
            [jnp.dot(tintr_ref[j], ut, preferred_element_type=F32).T, lane_pad], axis=1)
        pi_scr[rows_j, :] = jnp.concatenate(
            [jnp.dot(tinti_ref[j], ut, preferred_element_type=F32).T, lane_pad], axis=1)
    pad1 = jnp.zeros((ngrp, LANES - SSM_STATE), F32)
    ar = jnp.concatenate([jnp.concatenate([pw_ref[j, 0:1, :] for j in range(ngrp)], axis=0), pad1], axis=1)
    ai = jnp.concatenate([jnp.concatenate([pw_ref[j, 1:2, :] for j in range(ngrp)], axis=0), pad1], axis=1)

    def step(c, carry):
        sr, si = carry
        rows_c = pl.ds(pl.multiple_of(c * ngrp, ngrp), ngrp)
        sr_scr[rows_c, :] = sr
        si_scr[rows_c, :] = si
        return (ar * sr - ai * si + pr_scr[rows_c, :], ar * si + ai * sr + pi_scr[rows_c, :])

    zero = jnp.zeros((ngrp, LANES), F32)
    fin_r, fin_i = lax.fori_loop(0, nc, step, (zero, zero), unroll=8)
    for j in range(ngrp):
        sre_ref[j] = fin_r[j:j + 1, :SSM_STATE]
        sim_ref[j] = fin_i[j:j + 1, :SSM_STATE]
        rows_j = pl.ds(j, nc, stride=ngrp)
        yt = jnp.dot(tit_ref[j], ut_scr[j], preferred_element_type=F32)
        yt = yt + jnp.dot(toutr_ref[j], sr_scr[rows_j, :][:, :SSM_STATE].T.astype(BF16),
                          preferred_element_type=F32)
        yt = yt + jnp.dot(touti_ref[j], si_scr[rows_j, :][:, :SSM_STATE].T.astype(BF16),
                          preferred_element_type=F32)
        yt_scr[j] = yt
    d = d_ref[...]
    for to in range(CHUNK):
        v = jnp.concatenate([yt_scr[j, to * SSM_GROUP:(to + 1) * SSM_GROUP, :] for j in range(ngrp)], axis=0)
        rows = pl.ds(to, nc, stride=CHUNK)
        y_ref[rows, :] = v.T + d * u_ref[rows, :]


def _ssm_prompt(u, ops, d_skip):
    b, t, _ = u.shape
    p = SSM_STATE
    w = CHUNK * SSM_GROUP
    nc = t // CHUNK
    ngrp = LANES // SSM_GROUP
    g = N_GROUPS
    gspec = lambda shape: pl.BlockSpec((ngrp,) + shape, lambda bi, qi: (qi, 0, 0))
    io = pl.BlockSpec((None, t, LANES), lambda bi, qi: (bi, 0, qi))
    st = pl.BlockSpec((None, ngrp, 1, p), lambda bi, qi: (bi, qi, 0, 0))
    tit, _, _, toutr, touti, pw, tintr, tinti = ops
    return pl.pallas_call(
        _ssm_prompt_kernel,
        grid=(b, g // ngrp),
        in_specs=[io, gspec((w, w)), gspec((p, w)), gspec((p, w)), gspec((w, p)), gspec((w, p)),
                  pl.BlockSpec((None, 1, LANES), lambda bi, qi: (qi, 0, 0)), gspec((8, p))],
        out_specs=[io, st, st],
        out_shape=[jax.ShapeDtypeStruct(u.shape, F32),
                   jax.ShapeDtypeStruct((b, g, 1, p), F32), jax.ShapeDtypeStruct((b, g, 1, p), F32)],
        scratch_shapes=[pltpu.VMEM((ngrp, w, nc), BF16), pltpu.VMEM((ngrp, w, nc), F32)]
                       + [pltpu.VMEM((nc * ngrp, LANES), F32)] * 4,
        compiler_params=_cparams(2),
        name="ssm_prompt",
    )(u, tit, tintr, tinti, toutr, touti, d_skip.reshape(g // ngrp, 1, LANES), pw)


def _ssm_decode_kernel(u_ref, s0r_ref, s0i_ref, tit_ref, tinr_ref, tini_ref, toutr_ref, touti_ref, d_ref,
                       pw_ref, y_ref, sre_ref, sim_ref):
    nt = (((1,), (1,)), ((), ()))
    w = DEC_T * SSM_GROUP
    lo = (CHUNK - DEC_T) * SSM_GROUP
    u = u_ref[...]
    ub = u.astype(BF16)
    s0r = s0r_ref[...]
    s0i = s0i_ref[...]
    y = lax.dot_general(ub, tit_ref[0:w, 0:w], nt, preferred_element_type=F32)
    y = y + lax.dot_general(s0r.astype(BF16), toutr_ref[0:w, :], nt, preferred_element_type=F32)
    y = y + lax.dot_general(s0i.astype(BF16), touti_ref[0:w, :], nt, preferred_element_type=F32)
    y_ref[...] = y + d_ref[:, 0:w] * u
    ar = pw_ref[2:3, :]
    ai = pw_ref[3:4, :]
    sre_ref[...] = ar * s0r - ai * s0i + jnp.dot(ub, tinr_ref[lo:, :], preferred_element_type=F32)
    sim_ref[...] = ar * s0i + ai * s0r + jnp.dot(ub, tini_ref[lo:, :], preferred_element_type=F32)


def _ssm_decode(ug, s0_re, s0_im, ops, d_tile):
    g, bd, w = ug.shape
    p = SSM_STATE
    wc = CHUNK * SSM_GROUP
    gspec = lambda shape: pl.BlockSpec((None,) + shape, lambda gi: (gi, 0, 0))
    tit, tinr, tini, toutr, touti, pw = ops[:6]
    return pl.pallas_call(
        _ssm_decode_kernel,
        grid=(g,),
        in_specs=[gspec((bd, w)), gspec((bd, p)), gspec((bd, p)), gspec((wc, wc)), gspec((wc, p)),
                  gspec((wc, p)), gspec((wc, p)), gspec((wc, p)), gspec((1, wc)), gspec((8, p))],
        out_specs=[gspec((bd, w)), gspec((bd, p)), gspec((bd, p))],
        out_shape=[jax.ShapeDtypeStruct((g, bd, w), F32), jax.ShapeDtypeStruct((g, bd, p), F32),
                   jax.ShapeDtypeStruct((g, bd, p), F32)],
        compiler_params=_cparams(1),
        name="ssm_decode",
    )(ug, s0_re, s0_im, tit, tinr, tini, toutr, touti, d_tile, pw)


def _gelu_tanh(x):
    return 0.5 * x * (1.0 + jnp.tanh(math.sqrt(2.0 / math.pi) * (x + 0.044715 * (x * x * x))))


def _sigmoid(x):
    return 1.0 / (1.0 + jnp.exp(-x))


def _route(hn, wr_ref):
    n = hn.shape[0]
    h_hi = hn.astype(BF16)
    h_lo = (hn - h_hi.astype(F32)).astype(BF16)
    prod = jnp.dot(jnp.concatenate([h_hi, h_lo], axis=0), wr_ref[...], preferred_element_type=F32)
    logits = (prod[:n, :ROUTER_LANES] + prod[:n, ROUTER_LANES:]
              + prod[n:, :ROUTER_LANES] + prod[n:, ROUTER_LANES:])
    lidx = lax.broadcasted_iota(jnp.int32, (n, ROUTER_LANES), 1)
    is_e = lidx < N_EXPERTS
    is_g = jnp.logical_and(lidx >= N_EXPERTS, lidx < N_EXPERTS + N_EXPERT_GROUPS)
    gmax = jnp.max(jnp.where(is_g, logits, -jnp.inf), axis=1, keepdims=True)
    g_prob = 1.0 / jnp.sum(jnp.where(is_g, jnp.exp(logits - gmax), 0.0), axis=1, keepdims=True)
    g_sel = jnp.min(jnp.where(jnp.logical_and(is_g, logits == gmax), lidx - N_EXPERTS, N_EXPERT_GROUPS),
                    axis=1, keepdims=True)
    in_grp = jnp.logical_and(is_e, (lidx // EXPERTS_PER_GROUP) == g_sel)
    l1 = jnp.max(jnp.where(in_grp, logits, -jnp.inf), axis=1, keepdims=True)
    i1 = jnp.min(jnp.where(jnp.logical_and(in_grp, logits == l1), lidx, ROUTER_LANES), axis=1, keepdims=True)
    rest = jnp.logical_and(in_grp, lidx != i1)
    l2 = jnp.max(jnp.where(rest, logits, -jnp.inf), axis=1, keepdims=True)
    i2 = jnp.min(jnp.where(jnp.logical_and(rest, logits == l2), lidx, ROUTER_LANES), axis=1, keepdims=True)
    e2 = jnp.exp(l2 - l1)
    w1 = g_prob / (1.0 + e2)
    w2 = g_prob * e2 / (1.0 + e2)
    return jnp.where(lidx == i1, w1, 0.0) + jnp.where(lidx == i2, w2, 0.0)


def _post_body(oa, ys, x, ga_ref, gb_ref, wglu_ref, bglu_ref, wout_ref, gf_ref, wr_ref,
               x1_ref, hn_ref, gates_ref):
    z = _gelu_tanh(ys)
    gate = _sigmoid(jnp.dot(z.astype(BF16), wglu_ref[...], preferred_element_type=F32) + bglu_ref[...])
    ob = z * gate
    mixed = jnp.concatenate([_rms(oa, ga_ref[...]), _rms(ob, gb_ref[...])], axis=1).astype(BF16)
    x1 = x + jnp.dot(mixed, wout_ref[...], preferred_element_type=F32)
    x1_ref[...] = x1
    hn = _rms(x1, gf_ref[...])
    half = hn.shape[1] // 2
    hn_ref[...] = pltpu.pack_elementwise([hn[:, :half], hn[:, half:]], packed_dtype=jnp.bfloat16)
    gates_ref[...] = _route(hn, wr_ref)


def _post_pair_kernel(oa_ref, ys_ref, x_ref, *rest):
    oa = jnp.concatenate([oa_ref[j] for j in range(MIX_A // LANES)], axis=1)
    _post_body(oa, ys_ref[...], x_ref[...], *rest)


def _post_tok_kernel(oa_ref, ys_ref, x_ref, *rest):
    _post_body(oa_ref[...], ys_ref[...], x_ref[...], *rest)


def _post_mix(oa, ys, x, w, tm, pair):
    b, t, d = x.shape
    row = lambda width: pl.BlockSpec((None, tm, width), lambda bi, i: (bi, i, 0))
    const = lambda a: pl.BlockSpec(a.shape, lambda bi, i: (0,) * a.ndim)
    oa_spec = (pl.BlockSpec((None, MIX_A // LANES, tm, LANES), lambda bi, i: (bi, 0, i, 0)) if pair
               else row(MIX_A))
    weights = [w['g_out_a'], w['g_out_b'], w['w_glu'], w['b_glu'], w['w_out'], w['g_ffn'], w['w_router']]
    return pl.pallas_call(
        _post_pair_kernel if pair else _post_tok_kernel,
        grid=(b, t // tm),
        in_specs=[oa_spec, row(MIX_B), row(d)] + [const(a) for a in weights],
        out_specs=[row(d), row(d // 2), row(ROUTER_LANES)],
        out_shape=[jax.ShapeDtypeStruct((b, t, d), F32), jax.ShapeDtypeStruct((b, t, d // 2), jnp.uint32),
                   jax.ShapeDtypeStruct((b, t, ROUTER_LANES), F32)],
        compiler_params=_cparams(2),
        name="post_mix_prompt" if pair else "post_mix_sample",
    )(oa, ys, x, *weights)


MOE_BLOCK = 2048
MOE_TILE_SLACK = 1.25
MOE_ALIGN = 8
MOE_EXPERTS_PER_STEP = 2
MOE_VMEM_LIMIT = 60 * 1024 * 1024
PLAN_CHUNK = 256
BF16_ROWS = 16


def _moe_tile(tb):
    mean = EXPERT_TOP_K * tb / N_EXPERTS
    rows = -(-int(MOE_TILE_SLACK * mean) // BF16_ROWS) * BF16_ROWS
    pitch = rows // 8 + (1 - (rows // 8) % 2)
    return rows, 8 * pitch


def _moe_rows(tb):
    rows = EXPERT_TOP_K * tb + N_EXPERTS * (MOE_ALIGN - 1) + _moe_tile(tb)[0]
    return -(-rows // MOE_ALIGN) * MOE_ALIGN


def _moe_plan_kernel(gates_ref, plan_ref, meta_ref, grow_ref):
    tb = gates_ref.shape[0]
    gates = gates_ref[...]
    hot = gates > 0.0
    onehot = jnp.where(hot, 1.0, 0.0).astype(BF16)
    ch = min(PLAN_CHUNK, tb)
    r_i = lax.broadcasted_iota(jnp.int32, (ch, ch), 0)
    c_i = lax.broadcasted_iota(jnp.int32, (ch, ch), 1)
    earlier = jnp.where(c_i < r_i, 1.0, 0.0).astype(BF16)
    carry = jnp.zeros((1, ROUTER_LANES), F32)
    ranks = []
    for k in range(tb // ch):
        oh = onehot[k * ch:(k + 1) * ch]
        ranks.append(jnp.dot(earlier, oh, preferred_element_type=F32) + carry)
        carry = carry + jnp.sum(oh.astype(F32), axis=0, keepdims=True)
    rank = jnp.concatenate(ranks, axis=0)
    seg = jnp.floor((carry + (MOE_ALIGN - 1.0)) * (1.0 / MOE_ALIGN)) * MOE_ALIGN
    l_i = lax.broadcasted_iota(jnp.int32, (ROUTER_LANES, ROUTER_LANES), 0)
    l_j = lax.broadcasted_iota(jnp.int32, (ROUTER_LANES, ROUTER_LANES), 1)
    before = jnp.where(l_i < l_j, 1.0, 0.0).astype(F32)
    offs = jnp.dot(jnp.broadcast_to(seg, (8, ROUTER_LANES)), before, precision=lax.Precision.HIGHEST,
                   preferred_element_type=F32)[0:1]
    pos = rank + offs
    lane = lax.broadcasted_iota(jnp.int32, (tb, ROUTER_LANES), 1)
    lane_a = jnp.min(jnp.where(hot, lane, ROUTER_LANES), axis=1, keepdims=True)
    lane_b = jnp.max(jnp.where(hot, lane, -1), axis=1, keepdims=True)
    pick = lambda sel, val: jnp.sum(jnp.where(sel, val, 0.0), axis=1, keepdims=True)
    sel_a, sel_b = lane == lane_a, lane == lane_b
    z = jnp.where(lane == 0, pick(sel_a, pos), 0.0) + jnp.where(lane == 1, pick(sel_b, pos), 0.0)
    plan_ref[...] = z.T[0:8, :]
    meta_ref[...] = jnp.concatenate([offs, carry, jnp.zeros((6, ROUTER_LANES), F32)], axis=0)
    grow_ref[...] = jnp.concatenate([jnp.broadcast_to(pick(sel_a, gates), (tb, LANES)),
                                     jnp.broadcast_to(pick(sel_b, gates), (tb, LANES))], axis=1)


def _moe_plan(gates, tb):
    n = gates.shape[0]
    nb = n // tb
    return pl.pallas_call(
        _moe_plan_kernel,
        grid=(nb,),
        in_specs=[pl.BlockSpec((tb, ROUTER_LANES), lambda i: (i, 0))],
        out_specs=[pl.BlockSpec((None, 8, tb), lambda i: (i, 0, 0)),
                   pl.BlockSpec((None, 8, ROUTER_LANES), lambda i: (i, 0, 0)),
                   pl.BlockSpec((tb, EXPERT_TOP_K * LANES), lambda i: (i, 0))],
        out_shape=[jax.ShapeDtypeStruct((nb, 8, tb), F32), jax.ShapeDtypeStruct((nb, 8, ROUTER_LANES), F32),
                   jax.ShapeDtypeStruct((n, EXPERT_TOP_K * LANES), F32)],
        compiler_params=_cparams(1),
        name="moe_plan",
    )(gates)


def _moe_kernel(tile_rows, pitch, offs_ref, cnts_ref, pos_ref, hn_ref, grow_ref, x1_ref, wg_ref, wu_ref, wd_ref,
                gfin_ref, y_ref, xs_scr, gcol_scr, otile_scr, yacc_scr, tok_scr):
    blk = pl.program_id(0)
    step = pl.program_id(1)
    tb = hn_ref.shape[0]
    n_lt = D_MODEL // LANES

    @pl.when(step == 0)
    def _group_rows():
        xs_scr[...] = jnp.zeros_like(xs_scr)
        gcol_scr[...] = jnp.zeros_like(gcol_scr)
        yacc_scr[...] = jnp.zeros_like(yacc_scr)

        def clear_pad(ex, c):
            end = offs_ref[blk, ex] + cnts_ref[blk, ex]
            for k in range(MOE_ALIGN - 1):
                tok_scr[end + k] = tb
            return c

        lax.fori_loop(0, N_EXPERTS, clear_pad, 0)

        def place(t8, c):
            for u in range(8):
                t = t8 * 8 + u
                row = hn_ref[pl.ds(t, 1), :]
                gate = grow_ref[pl.ds(t, 1), :]
                for s in range(EXPERT_TOP_K):
                    p = pos_ref[s, t]
                    xs_scr[pl.ds(p, 1), :] = row
                    gcol_scr[pl.ds(p, 1), :] = gate[:, s * LANES:(s + 1) * LANES]
                    tok_scr[p] = t
            return c

        lax.fori_loop(0, tb // 8, place, 0)

    slot_rows = n_lt * pitch

    def compute(k, r0, slot):
        xw = xs_scr[pl.ds(r0, tile_rows), :]
        x = jnp.concatenate(
            [pltpu.unpack_elementwise(xw, index=i, packed_dtype=jnp.bfloat16, unpacked_dtype=F32).astype(BF16)
             for i in range(2)], axis=1)
        g = gcol_scr[pl.ds(r0, tile_rows), :]
        a = jnp.dot(x, wg_ref[k], preferred_element_type=F32)
        u = jnp.dot(x, wu_ref[k], preferred_element_type=F32)
        act = (a * _sigmoid(a)) * u * jnp.concatenate([g] * (D_EXPERT // LANES), axis=1)
        out = jnp.dot(act.astype(BF16), wd_ref[k], preferred_element_type=F32)
        for s in range(n_lt):
            otile_scr[pl.ds(slot * slot_rows + s * pitch, tile_rows), :] = out[:, s * LANES:(s + 1) * LANES]

    def scatter_add(r0, valid, slot):
        def add8(j8, c3):
            base = pl.multiple_of(j8 * 8, 8)
            dst = [pl.ds(pl.multiple_of(tok_scr[r0 + base + v] * n_lt, n_lt), n_lt) for v in range(8)]
            rows = [yacc_scr[dst[v], :]
                    + otile_scr[pl.ds(slot * slot_rows + base + v, n_lt, stride=pitch), :]
                    for v in range(8)]
            for v in range(8):
                yacc_scr[dst[v], :] = rows[v]
            return c3

        lax.fori_loop(0, (valid + 7) // 8, add8, 0)

    offs = [offs_ref[blk, step * MOE_EXPERTS_PER_STEP + k] for k in range(MOE_EXPERTS_PER_STEP)]
    cnts = [cnts_ref[blk, step * MOE_EXPERTS_PER_STEP + k] for k in range(MOE_EXPERTS_PER_STEP)]
    for k in range(MOE_EXPERTS_PER_STEP):
        compute(k, pl.multiple_of(offs[k], MOE_ALIGN), k)
    for k in range(MOE_EXPERTS_PER_STEP):
        scatter_add(offs[k], jnp.minimum(tile_rows, cnts[k]), k)
    for k in range(MOE_EXPERTS_PER_STEP):
        def more(c, cc, k=k):
            r0 = pl.multiple_of(offs[k] + c * tile_rows, MOE_ALIGN)
            compute(k, r0, 0)
            scatter_add(r0, jnp.minimum(tile_rows, cnts[k] - c * tile_rows), 0)
            return cc

        lax.fori_loop(1, (cnts[k] + tile_rows - 1) // tile_rows, more, 0)

    @pl.when(step == pl.num_programs(1) - 1)
    def _fin():
        moe = jnp.concatenate([yacc_scr[pl.ds(s, tb, stride=n_lt), :] for s in range(n_lt)], axis=1)
        y_ref[...] = _rms(x1_ref[...] + moe, gfin_ref[...])


def _moe(hn_packed, gates, x1, wg, wu, wd, g_final, tb):
    n, d = x1.shape
    nb = n // tb
    plan, meta, gate_rows = _moe_plan(gates, tb)
    pos = plan[:, 0:2, :].astype(jnp.int32)
    offs = meta[:, 0, :N_EXPERTS].astype(jnp.int32)
    cnts = meta[:, 1, :N_EXPERTS].astype(jnp.int32)
    p_rows = _moe_rows(tb)
    tile_rows, pitch = _moe_tile(tb)
    per = MOE_EXPERTS_PER_STEP
    n_lt = d // LANES
    smem = lambda: pl.BlockSpec((None, 2, tb), lambda i, e, *_: (i, 0, 0), memory_space=pltpu.SMEM)
    once = dict(pipeline_mode=pl.Buffered(1))
    grid_spec = pltpu.PrefetchScalarGridSpec(
        num_scalar_prefetch=2,
        grid=(nb, N_EXPERTS // per),
        in_specs=[smem(),
                  pl.BlockSpec((tb, d // 2), lambda i, e, *_: (i, 0), **once),
                  pl.BlockSpec((tb, EXPERT_TOP_K * LANES), lambda i, e, *_: (i, 0), **once),
                  pl.BlockSpec((tb, d), lambda i, e, *_: (i, 0), **once),
                  pl.BlockSpec((per, d, D_EXPERT), lambda i, e, *_: (e, 0, 0)),
                  pl.BlockSpec((per, d, D_EXPERT), lambda i, e, *_: (e, 0, 0)),
                  pl.BlockSpec((per, D_EXPERT, d), lambda i, e, *_: (e, 0, 0)),
                  pl.BlockSpec((1, d), lambda i, e, *_: (0, 0))],
        out_specs=pl.BlockSpec((tb, d), lambda i, e, *_: (i, 0)),
        scratch_shapes=[pltpu.VMEM((p_rows, d // 2), jnp.uint32),
                        pltpu.VMEM((p_rows, LANES), F32),
                        pltpu.VMEM((per * n_lt * pitch, LANES), F32),
                        pltpu.VMEM(((tb + 1) * n_lt, LANES), F32),
                        pltpu.SMEM((p_rows,), jnp.int32)])
    return pl.pallas_call(
        functools.partial(_moe_kernel, tile_rows, pitch),
        grid_spec=grid_spec,
        out_shape=jax.ShapeDtypeStruct((n, d), F32),
        compiler_params=pltpu.CompilerParams(dimension_semantics=("arbitrary", "arbitrary"),
                                             vmem_limit_bytes=MOE_VMEM_LIMIT),
        name="moe",
    )(offs, cnts, pos, hn_packed, gate_rows, x1, wg, wu, wd, g_final)


def kernel(x_prompt, x_sample, cache_k, cache_v, state_ssm_re, state_ssm_im, rel_bias, g_mix, w_in, g_out_a, g_out_b, w_out, ssm_a_re, ssm_a_im, ssm_log_dt, ssm_b_re, ssm_b_im, ssm_c_re, ssm_c_im, ssm_d, w_glu, b_glu, g_ffn, w_router_group, w_router_expert, w_expert_gate, w_expert_up, w_expert_down, g_final):
    depth = g_mix.shape[0]
    assert depth == 1, "kernel is written for the single-layer configuration of the problem"
    l = 0
    b, t, d = x_prompt.shape
    bd, ts, _ = x_sample.shape
    assert ts == DEC_T and t % ATT_BLOCK == 0 and cache_k.shape[2] == MAX_WINDOW
    keep = min(MAX_WINDOW, t)

    w_in_b = w_in[l].astype(BF16)
    gm = g_mix[l][None, :]
    w_router = jnp.concatenate(
        [jnp.transpose(w_router_expert[l], (1, 0, 2)).reshape(d, N_EXPERTS), w_router_group[l],
         jnp.zeros((d, ROUTER_LANES - N_EXPERTS - N_EXPERT_GROUPS), F32)], axis=1)
    w_router_hi = w_router.astype(BF16)
    w_router_lo = (w_router - w_router_hi.astype(F32)).astype(BF16)
    post_w = dict(
        g_out_a=g_out_a[l][None, :], g_out_b=g_out_b[l][None, :], w_glu=w_glu[l].astype(BF16),
        b_glu=b_glu[l][None, :], w_out=w_out[l].astype(BF16), g_ffn=g_ffn[l][None, :],
        w_router=jnp.concatenate([w_router_hi, w_router_lo], axis=1))
    wg = w_expert_gate[l].astype(BF16)
    wu = w_expert_up[l].astype(BF16)
    wd = w_expert_down[l].astype(BF16)
    gfin = g_final[None, :]
    ssm_ops = _ssm_prep(ssm_a_re[l], ssm_a_im[l], ssm_log_dt[l], ssm_b_re[l], ssm_b_im[l],
                        ssm_c_re[l], ssm_c_im[l])
    d_tile = jnp.tile(ssm_d[l].reshape(N_GROUPS, 1, SSM_GROUP), (1, 1, CHUNK))

    qp, kp, vp, kt_win, vt_win, u_tok = _inproj_pair(x_prompt, gm, w_in_b, tm=512, keep=keep)
    oa_p = _attn_prompt(qp, kp, vp, rel_bias)
    ys_p, rp, ip = _ssm_prompt(u_tok, ssm_ops, ssm_d[l])
    x1_p, hn_p, gates_p = _post_mix(oa_p, ys_p, x_prompt, post_w, tm=512, pair=True)
    y_p = _moe(hn_p.reshape(b * t, d // 2), gates_p.reshape(b * t, ROUTER_LANES), x1_p.reshape(b * t, d),
               wg, wu, wd, gfin, tb=min(MOE_BLOCK, b * t))

    n_s = bd * ts
    qs, ks, vs, us = _inproj_tok(x_sample.reshape(n_s, d), gm, w_in_b)
    ckt = jnp.transpose(cache_k[l], (0, 2, 3, 1)).reshape(bd, MIX_A, MAX_WINDOW)
    cvt = jnp.transpose(cache_v[l], (0, 2, 3, 1)).reshape(bd, MIX_A, MAX_WINDOW)
    oa_s = _attn_decode(qs.reshape(bd, ts, MIX_A), ks.reshape(bd, ts, MIX_A), vs.reshape(bd, ts, MIX_A),
                        ckt, cvt, rel_bias)
    ugs = jnp.transpose(us.reshape(bd, ts, N_GROUPS, SSM_GROUP), (2, 0, 1, 3)).reshape(N_GROUPS, bd, ts * SSM_GROUP)
    s0r = jnp.transpose(state_ssm_re[l], (1, 0, 2))
    s0i = jnp.transpose(state_ssm_im[l], (1, 0, 2))
    ygs, rs, is_ = _ssm_decode(ugs, s0r, s0i, ssm_ops, d_tile)
    ys_s = jnp.transpose(ygs.reshape(N_GROUPS, bd, ts, SSM_GROUP), (1, 2, 0, 3)).reshape(n_s, MIX_B)
    x1_s, hn_s, gates_s = _post_mix(oa_s.reshape(1, n_s, MIX_A), ys_s.reshape(1, n_s, MIX_B),
                                    x_sample.reshape(1, n_s, d), post_w, tm=n_s, pair=False)
    y_s = _moe(hn_s.reshape(n_s, d // 2), gates_s.reshape(n_s, ROUTER_LANES), x1_s.reshape(n_s, d),
               wg, wu, wd, gfin, tb=n_s)

    y_prompt = y_p.reshape(b, t, d)
    y_sample = y_s.reshape(bd, ts, d)
    k_win = jnp.transpose(kt_win.reshape(b, N_HEADS, HEAD_DIM, keep), (0, 3, 1, 2))[None]
    v_win = jnp.transpose(vt_win.reshape(b, N_HEADS, HEAD_DIM, keep), (0, 3, 1, 2))[None]
    k_new = ks.reshape(1, bd, ts, N_HEADS, HEAD_DIM)
    v_new = vs.reshape(1, bd, ts, N_HEADS, HEAD_DIM)
    return (y_prompt, y_sample, k_win, v_win, k_new, v_new,
            rp.reshape(1, b, N_GROUPS, SSM_STATE), ip.reshape(1, b, N_GROUPS, SSM_STATE),
            jnp.transpose(rs, (1, 0, 2))[None], jnp.transpose(is_, (1, 0, 2))[None])
```

```python
import functools
import math

import jax
import jax.numpy as jnp
from jax import lax
from jax.experimental import pallas as pl
from jax.experimental.pallas import tpu as pltpu

F32 = jnp.float32
BF16 = jnp.bfloat16

D_MODEL = 1024
HEAD_DIM = 64
MIX_A = 512
N_HEADS = 8
MIX_B = 512
SSM_GROUP = 16
N_GROUPS = 32
SSM_STATE = 64
PROJ_COLS = 3 * MIX_A + MIX_B
DILATIONS = (1, 4, 16)
TAPS = 128
MAX_WINDOW = 2048
N_BUCKETS = 32
N_EXPERT_GROUPS = 4
EXPERTS_PER_GROUP = 8
N_EXPERTS = 32
EXPERT_TOP_K = 2
D_EXPERT = 256
EPS = 1e-6
NEG_INF = -1e30
SCALE = HEAD_DIM ** -0.5

LANES = 128
ROUTER_LANES = 128
CHUNK = 16
ATT_BLOCK = 2048
ATT_PITCH = 136
VMEM_LIMIT = 56 * 1024 * 1024


def _cparams(n_axes):
    return pltpu.CompilerParams(dimension_semantics=("arbitrary",) * n_axes,
                                vmem_limit_bytes=VMEM_LIMIT)


def _t5_bucket(distance):
    max_exact = N_BUCKETS // 2
    nf = jnp.maximum(distance, 1).astype(F32)
    large = max_exact + jnp.floor(jnp.log(nf / max_exact) / math.log(MAX_WINDOW / max_exact)
                                  * (N_BUCKETS - max_exact)).astype(jnp.int32)
    large = jnp.minimum(large, N_BUCKETS - 1)
    return jnp.where(distance < max_exact, distance, large)


def _rms(x, g):
    return x * lax.rsqrt(jnp.mean(x * x, axis=-1, keepdims=True) + EPS) * g


def _inproj_pair_kernel(first_win_tile, x_ref, g_ref, w_ref, q_ref, k_ref, v_ref, kt_ref, vt_ref, u_ref):
    h = _rms(x_ref[...], g_ref[...])
    p = jnp.dot(h.astype(BF16), w_ref[...], preferred_element_type=F32)
    for j in range(MIX_A // LANES):
        q_ref[j] = p[:, LANES * j:LANES * (j + 1)] * SCALE
        k_ref[j] = p[:, MIX_A + LANES * j:MIX_A + LANES * (j + 1)]
        v_ref[j] = p[:, 2 * MIX_A + LANES * j:2 * MIX_A + LANES * (j + 1)]
    u_ref[...] = p[:, 3 * MIX_A:]

    @pl.when(pl.program_id(1) >= first_win_tile)
    def _window():
        kt_ref[...] = p[:, MIX_A:2 * MIX_A].T
        vt_ref[...] = p[:, 2 * MIX_A:3 * MIX_A].T


def _inproj_pair(x, g, w_bf16, tm, keep):
    b, t, d = x.shape
    npair = MIX_A // LANES
    first = (t - keep) // tm
    pair = jax.ShapeDtypeStruct((b, npair, t, LANES), F32)
    win = jax.ShapeDtypeStruct((b, MIX_A, keep), F32)
    tok = jax.ShapeDtypeStruct((b, t, MIX_A), F32)
    pair_spec = pl.BlockSpec((None, npair, tm, LANES), lambda bi, i: (bi, 0, i, 0))
    win_spec = pl.BlockSpec((None, MIX_A, tm), lambda bi, i: (bi, 0, jnp.maximum(i - first, 0)))
    tok_spec = pl.BlockSpec((None, tm, MIX_A), lambda bi, i: (bi, i, 0))
    return pl.pallas_call(
        functools.partial(_inproj_pair_kernel, first),
        grid=(b, t // tm),
        in_specs=[pl.BlockSpec((None, tm, d), lambda bi, i: (bi, i, 0)),
                  pl.BlockSpec((1, d), lambda bi, i: (0, 0)),
                  pl.BlockSpec((d, PROJ_COLS), lambda bi, i: (0, 0))],
        out_specs=[pair_spec, pair_spec, pair_spec, win_spec, win_spec, tok_spec],
        out_shape=[pair, pair, pair, win, win, tok],
        compiler_params=_cparams(2),
        name="inproj_prompt",
    )(x, g, w_bf16)


def _inproj_tok_kernel(x_ref, g_ref, w_ref, q_ref, k_ref, v_ref, u_ref):
    h = _rms(x_ref[...], g_ref[...])
    p = jnp.dot(h.astype(BF16), w_ref[...], preferred_element_type=F32)
    q_ref[...] = p[:, :MIX_A] * SCALE
    k_ref[...] = p[:, MIX_A:2 * MIX_A]
    v_ref[...] = p[:, 2 * MIX_A:3 * MIX_A]
    u_ref[...] = p[:, 3 * MIX_A:]


def _inproj_tok(x, g, w_bf16):
    n, d = x.shape
    out = jax.ShapeDtypeStruct((n, MIX_A), F32)
    return pl.pallas_call(
        _inproj_tok_kernel,
        out_shape=[out, out, out, out],
        compiler_params=pltpu.CompilerParams(vmem_limit_bytes=VMEM_LIMIT),
        name="inproj_sample",
    )(x, g, w_bf16)


def _prompt_bias_ids():
    r = jnp.arange(TAPS, dtype=jnp.int32)[:, None]
    kpos = jnp.arange(2 * TAPS, dtype=jnp.int32)[None, :] - TAPS
    rel = r - kpos
    valid = (rel >= 0) & (rel <= TAPS)
    ids = [jnp.where(valid, _t5_bucket(d * rel), -1) for d in DILATIONS]
    return jnp.stack(ids).astype(jnp.int32)


def _attn_prompt_kernel(rb_ref, ids_ref, q_ref, kc_ref, kp_ref, vc_ref, vp_ref, o_ref,
                        bias_scr, fm_scr, m0_scr, l0_scr, o0_scr, m1_scr, l1_scr, o1_scr,
                        m2_scr, l2_scr, o2_scr):
    bi = pl.program_id(0)
    p = pl.program_id(1)
    i = pl.program_id(2)
    blk = ATT_BLOCK
    npair = N_HEADS // 2

    @pl.when((bi == 0) & (p == 0) & (i == 0))
    def _build_bias():
        col = lax.broadcasted_iota(jnp.int32, (2 * TAPS, 2 * TAPS), 1)
        fm_scr[0] = jnp.zeros((2 * TAPS, 2 * TAPS), F32)
        fm_scr[1] = jnp.where(col < TAPS, NEG_INF, 0.0).astype(F32)
        for c in range(len(DILATIONS)):
            ids = ids_ref[c]

            for h in range(N_HEADS):
                tile = jnp.where(ids < 0, NEG_INF, 0.0).astype(F32)
                for bkt in range(N_BUCKETS):
                    tile = tile + jnp.where(ids == bkt, rb_ref[bkt:bkt + 1, h:h + 1], 0.0)
                bias_scr[c * npair + h // 2, (h % 2) * TAPS:(h % 2 + 1) * TAPS, :] = tile

    lane = lax.broadcasted_iota(jnp.int32, (1, LANES), 1)
    low = lane < HEAD_DIM
    nt = (((1,), (1,)), ((), ()))
    ones = jnp.ones((2 * TAPS, LANES), BF16)
    stats = ((m0_scr, l0_scr, o0_scr), (m1_scr, l1_scr, o1_scr), (m2_scr, l2_scr, o2_scr))

    def rows(ref, start, d):
        if d == 1:
            return ref[pl.ds(start, TAPS), :]
        return ref[pl.ds(start, TAPS, stride=d), :]

    def sub_block(c, d, qs, first, dst_rows):
        q = rows(q_ref, qs, d)
        q2 = jnp.concatenate([jnp.where(low, q, 0.0), jnp.where(low, 0.0, q)], axis=0).astype(BF16)
        back = qs - d * TAPS
        k_back = rows(kp_ref, blk + back, d) if back < 0 else rows(kc_ref, back, d)
        v_back = rows(vp_ref, blk + back, d) if back < 0 else rows(vc_ref, back, d)
        k = jnp.concatenate([k_back, rows(kc_ref, qs, d)], axis=0).astype(BF16)
        v = jnp.concatenate([v_back, rows(vc_ref, qs, d)], axis=0).astype(BF16)
        s_ = lax.dot_general(q2, k, nt, preferred_element_type=F32) + bias_scr[c * npair + p]
        if first is not None:
            s_ = s_ + fm_scr[first]
        m = jnp.max(s_, axis=1, keepdims=True)
        pe = jnp.exp(s_ - m).astype(BF16)
        ol = jnp.dot(pe, jnp.concatenate([v, ones], axis=1), preferred_element_type=F32)
        o, l = ol[:, :LANES], ol[:, LANES:]
        m_ref, l_ref, a_ref = stats[c]
        m_ref[dst_rows, :] = jnp.where(low, m[:TAPS], m[TAPS:])
        l_ref[dst_rows, :] = jnp.where(low, l[:TAPS], l[TAPS:])
        a_ref[dst_rows, :] = jnp.where(low, o[:TAPS], o[TAPS:])

    at_start = jnp.where(i == 0, 1, 0)
    for c, d in enumerate(DILATIONS):
        n_grp = blk // (d * TAPS)
        for r in range(d):
            for g in range(n_grp):
                qs = r + d * TAPS * g
                first = at_start if g == 0 else None
                if d == 1:
                    dst_rows = pl.ds(qs, TAPS)
                elif n_grp > 1:
                    dst_rows = pl.ds(qs, TAPS, stride=d)
                else:
                    dst_rows = pl.ds(r * ATT_PITCH, TAPS)
                sub_block(c, d, qs, first, dst_rows)

    d_last = DILATIONS[-1]

    def merge_body(j, carry):
        tok = pl.ds(pl.multiple_of(j * d_last, d_last), d_last)
        dil = pl.ds(j, d_last, stride=ATT_PITCH)
        m0, m1, m2 = m0_scr[tok, :], m1_scr[tok, :], m2_scr[dil, :]
        m = jnp.maximum(jnp.maximum(m0, m1), m2)
        a0, a1, a2 = jnp.exp(m0 - m), jnp.exp(m1 - m), jnp.exp(m2 - m)
        num = o0_scr[tok, :] * a0 + o1_scr[tok, :] * a1 + o2_scr[dil, :] * a2
        den = l0_scr[tok, :] * a0 + l1_scr[tok, :] * a1 + l2_scr[dil, :] * a2
        o_ref[tok, :] = num / den
        return carry

    lax.fori_loop(0, blk // d_last, merge_body, 0, unroll=4)


def _attn_prompt(q, k, v, rel_bias):
    b, npair, t, _ = q.shape
    blk = ATT_BLOCK
    cur = pl.BlockSpec((None, None, blk, LANES), lambda bi, p, i: (bi, p, i, 0))
    prev = pl.BlockSpec((None, None, blk, LANES), lambda bi, p, i: (bi, p, jnp.maximum(i - 1, 0), 0))
    n_tiles = len(DILATIONS) * npair
    stats = ([pltpu.VMEM((blk, LANES), F32)] * 6
             + [pltpu.VMEM((DILATIONS[-1] * ATT_PITCH, LANES), F32)] * 3)
    return pl.pallas_call(
        _attn_prompt_kernel,
        grid=(b, npair, t // blk),
        in_specs=[pl.BlockSpec((N_BUCKETS, N_HEADS), lambda bi, p, i: (0, 0)),
                  pl.BlockSpec((len(DILATIONS), TAPS, 2 * TAPS), lambda bi, p, i: (0, 0, 0)),
                  cur, cur, prev, cur, prev],
        out_specs=cur,
        out_shape=jax.ShapeDtypeStruct(q.shape, F32),
        scratch_shapes=[pltpu.VMEM((n_tiles, 2 * TAPS, 2 * TAPS), F32),
                        pltpu.VMEM((2, 2 * TAPS, 2 * TAPS), F32)] + stats,
        compiler_params=_cparams(3),
        name="attn_prompt",
    )(rel_bias, _prompt_bias_ids(), q, k, k, v, v)


DEC_T = 4
DEC_NEW_PAD = 128
DEC_SEQ_PER_STEP = 2
DEC_KEYS = MAX_WINDOW + DEC_NEW_PAD


def _decode_tables():
    qi = jnp.arange(DEC_T, dtype=jnp.int32)[:, None]
    rho = jnp.arange(MAX_WINDOW, dtype=jnp.int32)[None, :]
    dist_b = MAX_WINDOW + qi - rho
    mult_b = jnp.zeros_like(dist_b)
    for d in DILATIONS:
        mult_b = mult_b + (((dist_b % d) == 0) & (dist_b <= d * TAPS)).astype(jnp.int32)
    jj = jnp.arange(DEC_NEW_PAD, dtype=jnp.int32)[None, :]
    dist_n = qi - jj
    mult_n = jnp.where(dist_n == 0, len(DILATIONS), jnp.where((dist_n > 0) & (jj < DEC_T), 1, 0))
    dist = jnp.concatenate([dist_b, jnp.maximum(dist_n, 0)], axis=1)
    mult = jnp.concatenate([mult_b, mult_n.astype(jnp.int32)], axis=1)
    ids = jnp.where(mult > 0, _t5_bucket(dist), -1).astype(jnp.int32)
    ids = jnp.repeat(ids, N_HEADS, axis=0)
    mult = jnp.repeat(mult, N_HEADS, axis=0).astype(F32)
    return ids, mult


def _attn_decode_kernel(rbt_ref, ids_ref, mult_ref, q_ref, kn_ref, vn_ref, kt_ref, vt_ref, o_ref, bias_scr):
    n_rows = DEC_T * N_HEADS

    @pl.when(pl.program_id(0) == 0)
    def _build_bias():
        ids = ids_ref[...]
        tile = jnp.where(ids < 0, NEG_INF, 0.0).astype(F32)
        for bkt in range(N_BUCKETS):
            tile = tile + jnp.where(ids == bkt, rbt_ref[:, bkt:bkt + 1], 0.0)
        bias_scr[...] = tile

    lane = lax.broadcasted_iota(jnp.int32, (N_HEADS, MIX_A), 1)
    head = lax.broadcasted_iota(jnp.int32, (N_HEADS, MIX_A), 0)
    hmask = (lane // HEAD_DIM) == head

    zpad = jnp.zeros((DEC_NEW_PAD - DEC_T, MIX_A), F32)
    nt = (((1,), (1,)), ((), ()))
    for j in range(q_ref.shape[0]):
        q = q_ref[j]
        qm = jnp.where(hmask[None], q[:, None, :], 0.0).reshape(n_rows, MIX_A).astype(BF16)
        kn = jnp.concatenate([kn_ref[j], zpad], axis=0).astype(BF16)
        vn = jnp.concatenate([vn_ref[j], zpad], axis=0).astype(BF16)
        kt = kt_ref[j].astype(BF16)
        vt = vt_ref[j].astype(BF16)
        s = jnp.concatenate([jnp.dot(qm, kt, preferred_element_type=F32),
                             lax.dot_general(qm, kn, nt, preferred_element_type=F32)], axis=1)
        s = s + bias_scr[...]
        m = jnp.max(s, axis=1, keepdims=True)
        pe = jnp.exp(s - m) * mult_ref[...]
        l = jnp.sum(pe, axis=1, keepdims=True)
        pb = pe.astype(BF16)
        o = (lax.dot_general(pb[:, :MAX_WINDOW], vt, nt, preferred_element_type=F32)
             + jnp.dot(pb[:, MAX_WINDOW:], vn, preferred_element_type=F32)) / l
        o3 = o.reshape(DEC_T, N_HEADS, MIX_A)
        o_ref[j] = jnp.sum(jnp.where(hmask[None], o3, 0.0), axis=1)


def _attn_decode(q, k_new, v_new, cache_kt, cache_vt, rel_bias):
    bd = q.shape[0]
    ids, mult = _decode_tables()
    rbt = jnp.tile(rel_bias.T, (DEC_T, 1))
    per = DEC_SEQ_PER_STEP if bd % DEC_SEQ_PER_STEP == 0 else 1
    new_spec = pl.BlockSpec((per, DEC_T, MIX_A), lambda b: (b, 0, 0))
    cache_spec = pl.BlockSpec((per, MIX_A, MAX_WINDOW), lambda b: (b, 0, 0))
    n_rows = DEC_T * N_HEADS
    const = lambda shape: pl.BlockSpec(shape, lambda b: (0, 0))
    return pl.pallas_call(
        _attn_decode_kernel,
        grid=(bd // per,),
        in_specs=[const((n_rows, N_BUCKETS)), const((n_rows, DEC_KEYS)), const((n_rows, DEC_KEYS)),
                  new_spec, new_spec, new_spec, cache_spec, cache_spec],
        out_specs=new_spec,
        out_shape=jax.ShapeDtypeStruct((bd, DEC_T, MIX_A), F32),
        scratch_shapes=[pltpu.VMEM((n_rows, DEC_KEYS), F32)],
        compiler_params=_cparams(1),
        name="attn_decode",
    )(rbt, ids, mult, q, k_new, v_new, cache_kt, cache_vt)


def _ssm_prep_kernel(are_ref, aim_ref, ldt_ref, btr_ref, bti_ref, cre_ref, cim_ref,
                     tit_ref, tinr_ref, tini_ref, toutr_ref, touti_ref, pw_ref, tintr_ref, tinti_ref):
    hi = lax.Precision.HIGHEST
    nt = (((1,), (1,)), ((), ()))
    a_re, a_im = are_ref[...], aim_ref[...]
    dt = jnp.exp(ldt_ref[...])
    decay = jnp.exp(a_re * dt)
    ab_re = decay * jnp.cos(a_im * dt)
    ab_im = decay * jnp.sin(a_im * dt)
    inv = 1.0 / (a_re * a_re + a_im * a_im)
    coef_re = ((ab_re - 1.0) * a_re + ab_im * a_im) * inv
    coef_im = (ab_im * a_re - (ab_re - 1.0) * a_im) * inv
    bt_re, bt_im = btr_ref[...], bti_ref[...]
    bb_re = coef_re * bt_re - coef_im * bt_im
    bb_im = coef_re * bt_im + coef_im * bt_re
    pw = [(jnp.ones_like(ab_re), jnp.zeros_like(ab_im))]
    for _ in range(CHUNK):
        pr, pi = pw[-1]
        pw.append((pr * ab_re - pi * ab_im, pr * ab_im + pi * ab_re))
    tin_re = jnp.concatenate(
        [pw[CHUNK - 1 - ti][0] * bb_re - pw[CHUNK - 1 - ti][1] * bb_im for ti in range(CHUNK)], axis=0).astype(BF16)
    tin_im = jnp.concatenate(
        [pw[CHUNK - 1 - ti][0] * bb_im + pw[CHUNK - 1 - ti][1] * bb_re for ti in range(CHUNK)], axis=0).astype(BF16)
    tinr_ref[...] = tin_re
    tini_ref[...] = tin_im
    p_i = lax.broadcasted_iota(jnp.int32, (SSM_STATE, SSM_STATE), 0)
    p_j = lax.broadcasted_iota(jnp.int32, (SSM_STATE, SSM_STATE), 1)
    eye = jnp.where(p_i == p_j, 1.0, 0.0).astype(BF16)
    tintr_ref[...] = lax.dot_general(eye, tin_re, nt, preferred_element_type=F32).astype(BF16)
    tinti_ref[...] = lax.dot_general(eye, tin_im, nt, preferred_element_type=F32).astype(BF16)
    c_re, c_im = cre_ref[...], cim_ref[...]
    cp_re = [c_re * pr - c_im * pi for pr, pi in pw]
    cp_im = [c_re * pi + c_im * pr for pr, pi in pw]
    toutr_ref[...] = jnp.concatenate(cp_re[1:], axis=0).astype(BF16)
    touti_ref[...] = (-jnp.concatenate(cp_im[1:], axis=0)).astype(BF16)
    kall = (lax.dot_general(jnp.concatenate(cp_re[:CHUNK], axis=0), bb_re, nt, precision=hi,
                            preferred_element_type=F32)
            - lax.dot_general(jnp.concatenate(cp_im[:CHUNK], axis=0), bb_im, nt, precision=hi,
                              preferred_element_type=F32))
    w = CHUNK * SSM_GROUP
    kb = kall.astype(BF16)
    ci_idx = lax.broadcasted_iota(jnp.int32, (SSM_GROUP, w), 0)
    col_idx = lax.broadcasted_iota(jnp.int32, (SSM_GROUP, w), 1)
    acc = jnp.zeros((w, w), F32)
    for ti in range(CHUNK):
        rows = ti * SSM_GROUP
        shifted = kb if ti == 0 else jnp.concatenate(
            [jnp.zeros((rows, SSM_GROUP), BF16), kb[:w - rows]], axis=0)
        place = jnp.where(col_idx == ci_idx + rows, 1.0, 0.0).astype(BF16)
        acc = acc + jnp.dot(shifted, place, preferred_element_type=F32)
    tit_ref[...] = acc.astype(BF16)
    zero = jnp.zeros_like(ab_re)
    pw_ref[...] = jnp.concatenate([pw[CHUNK][0], pw[CHUNK][1], pw[DEC_T][0], pw[DEC_T][1],
                                   zero, zero, zero, zero], axis=0)


def _ssm_prep(a_re, a_im, log_dt, b_re, b_im, c_re, c_im):
    g, p = a_re.shape
    w = CHUNK * SSM_GROUP
    row = lambda a: a[:, None, :]
    bt_re = jnp.transpose(b_re, (0, 2, 1))
    bt_im = jnp.transpose(b_im, (0, 2, 1))
    ldt = jnp.broadcast_to(log_dt[:, None, None], (g, 1, p))
    gspec = lambda shape: pl.BlockSpec((None,) + shape, lambda gi: (gi, 0, 0))
    return pl.pallas_call(
        _ssm_prep_kernel,
        grid=(g,),
        in_specs=[gspec((1, p)), gspec((1, p)), gspec((1, p)), gspec((SSM_GROUP, p)), gspec((SSM_GROUP, p)),
                  gspec((SSM_GROUP, p)), gspec((SSM_GROUP, p))],
        out_specs=[gspec((w, w)), gspec((w, p)), gspec((w, p)), gspec((w, p)), gspec((w, p)), gspec((8, p)),
                   gspec((p, w)), gspec((p, w))],
        out_shape=[jax.ShapeDtypeStruct((g, w, w), BF16)] + [jax.ShapeDtypeStruct((g, w, p), BF16)] * 4
                  + [jax.ShapeDtypeStruct((g, 8, p), F32)] + [jax.ShapeDtypeStruct((g, p, w), BF16)] * 2,
        compiler_params=_cparams(1),
        name="ssm_prep",
    )(row(a_re), row(a_im), ldt, bt_re, bt_im, c_re, c_im)


def _ssm_prompt_kernel(u_ref, tit_ref, tintr_ref, tinti_ref, toutr_ref, touti_ref, d_ref, pw_ref,
                       y_ref, sre_ref, sim_ref, ut_scr, yt_scr, pr_scr, pi_scr, sr_scr, si_scr):
    ngrp = LANES // SSM_GROUP
    nc = u_ref.shape[0] // CHUNK
    xt = [u_ref[pl.ds(ti, nc, stride=CHUNK), :].T for ti in range(CHUNK)]
    lane_pad = jnp.zeros((nc, LANES - SSM_STATE), F32)
    for j in range(ngrp):
        ut = jnp.concatenate([x[j * SSM_GROUP:(j + 1) * SSM_GROUP, :] for x in xt], axis=0).astype(BF16)
        ut_scr[j] = ut
        rows_j = pl.ds(j, nc, stride=ngrp)
        pr_scr[rows_j, :] = jnp.concatenate(
            [jnp.dot(tintr_ref[j], ut, preferred_element_type=F32).T, lane_pad], axis=1)
        pi_scr[rows_j, :] = jnp.concatenate(
            [jnp.dot(tinti_ref[j], ut, preferred_element_type=F32).T, lane_pad], axis=1)
    pad1 = jnp.zeros((ngrp, LANES - SSM_STATE), F32)
    ar = jnp.concatenate([jnp.concatenate([pw_ref[j, 0:1, :] for j in range(ngrp)], axis=0), pad1], axis=1)
    ai = jnp.concatenate([jnp.concatenate([pw_ref[j, 1:2, :] for j in range(ngrp)], axis=0), pad1], axis=1)

    def step(c, carry):
        sr, si = carry
        rows_c = pl.ds(pl.multiple_of(c * ngrp, ngrp), ngrp)
        sr_scr[rows_c, :] = sr
        si_scr[rows_c, :] = si
        return (ar * sr - ai * si + pr_scr[rows_c, :], ar * si + ai * sr + pi_scr[rows_c, :])

    zero = jnp.zeros((ngrp, LANES), F32)
    fin_r, fin_i = lax.fori_loop(0, nc, step, (zero, zero), unroll=8)
    for j in range(ngrp):
        sre_ref[j] = fin_r[j:j + 1, :SSM_STATE]
        sim_ref[j] = fin_i[j:j + 1, :SSM_STATE]
        rows_j = pl.ds(j, nc, stride=ngrp)
        yt = jnp.dot(tit_ref[j], ut_scr[j], preferred_element_type=F32)
        yt = yt + jnp.dot(toutr_ref[j], sr_scr[rows_j, :][:, :SSM_STATE].T.astype(BF16),
                          preferred_element_type=F32)
        yt = yt + jnp.dot(touti_ref[j], si_scr[rows_j, :][:, :SSM_STATE].T.astype(BF16),
                          preferred_element_type=F32)
        yt_scr[j] = yt
    d = d_ref[...]
    for to in range(CHUNK):
        v = jnp.concatenate([yt_scr[j, to * SSM_GROUP:(to + 1) * SSM_GROUP, :] for j in range(ngrp)], axis=0)
        rows = pl.ds(to, nc, stride=CHUNK)
        y_ref[rows, :] = v.T + d * u_ref[rows, :]


def _ssm_prompt(u, ops, d_skip):
    b, t, _ = u.shape
    p = SSM_STATE
    w = CHUNK * SSM_GROUP
    nc = t // CHUNK
    ngrp = LANES // SSM_GROUP
    g = N_GROUPS
    gspec = lambda shape: pl.BlockSpec((ngrp,) + shape, lambda bi, qi: (qi, 0, 0))
    io = pl.BlockSpec((None, t, LANES), lambda bi, qi: (bi, 0, qi))
    st = pl.BlockSpec((None, ngrp, 1, p), lambda bi, qi: (bi, qi, 0, 0))
    tit, _, _, toutr, touti, pw, tintr, tinti = ops
    return pl.pallas_call(
        _ssm_prompt_kernel,
        grid=(b, g // ngrp),
        in_specs=[io, gspec((w, w)), gspec((p, w)), gspec((p, w)), gspec((w, p)), gspec((w, p)),
                  pl.BlockSpec((None, 1, LANES), lambda bi, qi: (qi, 0, 0)), gspec((8, p))],
        out_specs=[io, st, st],
        out_shape=[jax.ShapeDtypeStruct(u.shape, F32),
                   jax.ShapeDtypeStruct((b, g, 1, p), F32), jax.ShapeDtypeStruct((b, g, 1, p), F32)],
        scratch_shapes=[pltpu.VMEM((ngrp, w, nc), BF16), pltpu.VMEM((ngrp, w, nc), F32)]
                       + [pltpu.VMEM((nc * ngrp, LANES), F32)] * 4,
        compiler_params=_cparams(2),
        name="ssm_prompt",
    )(u, tit, tintr, tinti, toutr, touti, d_skip.reshape(g // ngrp, 1, LANES), pw)


def _ssm_decode_kernel(u_ref, s0r_ref, s0i_ref, tit_ref, tinr_ref, tini_ref, toutr_ref, touti_ref, d_ref,
                       pw_ref, y_ref, sre_ref, sim_ref):
    nt = (((1,), (1,)), ((), ()))
    w = DEC_T * SSM_GROUP
    lo = (CHUNK - DEC_T) * SSM_GROUP
    u = u_ref[...]
    ub = u.astype(BF16)
    s0r = s0r_ref[...]
    s0i = s0i_ref[...]
    y = lax.dot_general(ub, tit_ref[0:w, 0:w], nt, preferred_element_type=F32)
    y = y + lax.dot_general(s0r.astype(BF16), toutr_ref[0:w, :], nt, preferred_element_type=F32)
    y = y + lax.dot_general(s0i.astype(BF16), touti_ref[0:w, :], nt, preferred_element_type=F32)
    y_ref[...] = y + d_ref[:, 0:w] * u
    ar = pw_ref[2:3, :]
    ai = pw_ref[3:4, :]
    sre_ref[...] = ar * s0r - ai * s0i + jnp.dot(ub, tinr_ref[lo:, :], preferred_element_type=F32)
    sim_ref[...] = ar * s0i + ai * s0r + jnp.dot(ub, tini_ref[lo:, :], preferred_element_type=F32)


def _ssm_decode(ug, s0_re, s0_im, ops, d_tile):
    g, bd, w = ug.shape
    p = SSM_STATE
    wc = CHUNK * SSM_GROUP
    gspec = lambda shape: pl.BlockSpec((None,) + shape, lambda gi: (gi, 0, 0))
    tit, tinr, tini, toutr, touti, pw = ops[:6]
    return pl.pallas_call(
        _ssm_decode_kernel,
        grid=(g,),
        in_specs=[gspec((bd, w)), gspec((bd, p)), gspec((bd, p)), gspec((wc, wc)), gspec((wc, p)),
                  gspec((wc, p)), gspec((wc, p)), gspec((wc, p)), gspec((1, wc)), gspec((8, p))],
        out_specs=[gspec((bd, w)), gspec((bd, p)), gspec((bd, p))],
        out_shape=[jax.ShapeDtypeStruct((g, bd, w), F32), jax.ShapeDtypeStruct((g, bd, p), F32),
                   jax.ShapeDtypeStruct((g, bd, p), F32)],
        compiler_params=_cparams(1),
        name="ssm_decode",
    )(ug, s0_re, s0_im, tit, tinr, tini, toutr, touti, d_tile, pw)


def _gelu_tanh(x):
    return 0.5 * x * (1.0 + jnp.tanh(math.sqrt(2.0 / math.pi) * (x + 0.044715 * (x * x * x))))


def _sigmoid(x):
    return 1.0 / (1.0 + jnp.exp(-x))


def _route(hn, wr_ref):
    n = hn.shape[0]
    h_hi = hn.astype(BF16)
    h_lo = (hn - h_hi.astype(F32)).astype(BF16)
    prod = jnp.dot(jnp.concatenate([h_hi, h_lo], axis=0), wr_ref[...], preferred_element_type=F32)
    logits = (prod[:n, :ROUTER_LANES] + prod[:n, ROUTER_LANES:]
              + prod[n:, :ROUTER_LANES] + prod[n:, ROUTER_LANES:])
    lidx = lax.broadcasted_iota(jnp.int32, (n, ROUTER_LANES), 1)
    is_e = lidx < N_EXPERTS
    is_g = jnp.logical_and(lidx >= N_EXPERTS, lidx < N_EXPERTS + N_EXPERT_GROUPS)
    gmax = jnp.max(jnp.where(is_g, logits, -jnp.inf), axis=1, keepdims=True)
    g_prob = 1.0 / jnp.sum(jnp.where(is_g, jnp.exp(logits - gmax), 0.0), axis=1, keepdims=True)
    g_sel = jnp.min(jnp.where(jnp.logical_and(is_g, logits == gmax), lidx - N_EXPERTS, N_EXPERT_GROUPS),
                    axis=1, keepdims=True)
    in_grp = jnp.logical_and(is_e, (lidx // EXPERTS_PER_GROUP) == g_sel)
    l1 = jnp.max(jnp.where(in_grp, logits, -jnp.inf), axis=1, keepdims=True)
    i1 = jnp.min(jnp.where(jnp.logical_and(in_grp, logits == l1), lidx, ROUTER_LANES), axis=1, keepdims=True)
    rest = jnp.logical_and(in_grp, lidx != i1)
    l2 = jnp.max(jnp.where(rest, logits, -jnp.inf), axis=1, keepdims=True)
    i2 = jnp.min(jnp.where(jnp.logical_and(rest, logits == l2), lidx, ROUTER_LANES), axis=1, keepdims=True)
    e2 = jnp.exp(l2 - l1)
    w1 = g_prob / (1.0 + e2)
    w2 = g_prob * e2 / (1.0 + e2)
    return jnp.where(lidx == i1, w1, 0.0) + jnp.where(lidx == i2, w2, 0.0)


def _post_body(oa, ys, x, ga_ref, gb_ref, wglu_ref, bglu_ref, wout_ref, gf_ref, wr_ref,
               x1_ref, hn_ref, gates_ref):
    z = _gelu_tanh(ys)
    gate = _sigmoid(jnp.dot(z.astype(BF16), wglu_ref[...], preferred_element_type=F32) + bglu_ref[...])
    ob = z * gate
    mixed = jnp.concatenate([_rms(oa, ga_ref[...]), _rms(ob, gb_ref[...])], axis=1).astype(BF16)
    x1 = x + jnp.dot(mixed, wout_ref[...], preferred_element_type=F32)
    x1_ref[...] = x1
    hn = _rms(x1, gf_ref[...])
    half = hn.shape[1] // 2
    hn_ref[...] = pltpu.pack_elementwise([hn[:, :half], hn[:, half:]], packed_dtype=jnp.bfloat16)
    gates_ref[...] = _route(hn, wr_ref)


def _post_pair_kernel(oa_ref, ys_ref, x_ref, *rest):
    oa = jnp.concatenate([oa_ref[j] for j in range(MIX_A // LANES)], axis=1)
    _post_body(oa, ys_ref[...], x_ref[...], *rest)


def _post_tok_kernel(oa_ref, ys_ref, x_ref, *rest):
    _post_body(oa_ref[...], ys_ref[...], x_ref[...], *rest)


def _post_mix(oa, ys, x, w, tm, pair):
    b, t, d = x.shape
    row = lambda width: pl.BlockSpec((None, tm, width), lambda bi, i: (bi, i, 0))
    const = lambda a: pl.BlockSpec(a.shape, lambda bi, i: (0,) * a.ndim)
    oa_spec = (pl.BlockSpec((None, MIX_A // LANES, tm, LANES), lambda bi, i: (bi, 0, i, 0)) if pair
               else row(MIX_A))
    weights = [w['g_out_a'], w['g_out_b'], w['w_glu'], w['b_glu'], w['w_out'], w['g_ffn'], w['w_router']]
    return pl.pallas_call(
        _post_pair_kernel if pair else _post_tok_kernel,
        grid=(b, t // tm),
        in_specs=[oa_spec, row(MIX_B), row(d)] + [const(a) for a in weights],
        out_specs=[row(d), row(d // 2), row(ROUTER_LANES)],
        out_shape=[jax.ShapeDtypeStruct((b, t, d), F32), jax.ShapeDtypeStruct((b, t, d // 2), jnp.uint32),
                   jax.ShapeDtypeStruct((b, t, ROUTER_LANES), F32)],
        compiler_params=_cparams(2),
        name="post_mix_prompt" if pair else "post_mix_sample",
    )(oa, ys, x, *weights)


MOE_BLOCK = 2048
MOE_TILE_SLACK = 1.25
MOE_ALIGN = 8
MOE_EXPERTS_PER_STEP = 2
MOE_VMEM_LIMIT = 60 * 1024 * 1024
PLAN_CHUNK = 256
BF16_ROWS = 16


def _moe_tile(tb):
    mean = EXPERT_TOP_K * tb / N_EXPERTS
    rows = -(-int(MOE_TILE_SLACK * mean) // BF16_ROWS) * BF16_ROWS
    pitch = rows // 8 + (1 - (rows // 8) % 2)
    return rows, 8 * pitch


def _moe_rows(tb):
    rows = EXPERT_TOP_K * tb + N_EXPERTS * (MOE_ALIGN - 1) + _moe_tile(tb)[0]
    return -(-rows // MOE_ALIGN) * MOE_ALIGN


def _moe_plan_kernel(gates_ref, plan_ref, meta_ref, grow_ref):
    tb = gates_ref.shape[0]
    gates = gates_ref[...]
    hot = gates > 0.0
    onehot = jnp.where(hot, 1.0, 0.0).astype(BF16)
    ch = min(PLAN_CHUNK, tb)
    r_i = lax.broadcasted_iota(jnp.int32, (ch, ch), 0)
    c_i = lax.broadcasted_iota(jnp.int32, (ch, ch), 1)
    earlier = jnp.where(c_i < r_i, 1.0, 0.0).astype(BF16)
    carry = jnp.zeros((1, ROUTER_LANES), F32)
    ranks = []
    for k in range(tb // ch):
        oh = onehot[k * ch:(k + 1) * ch]
        ranks.append(jnp.dot(earlier, oh, preferred_element_type=F32) + carry)
        carry = carry + jnp.sum(oh.astype(F32), axis=0, keepdims=True)
    rank = jnp.concatenate(ranks, axis=0)
    seg = jnp.floor((carry + (MOE_ALIGN - 1.0)) * (1.0 / MOE_ALIGN)) * MOE_ALIGN
    l_i = lax.broadcasted_iota(jnp.int32, (ROUTER_LANES, ROUTER_LANES), 0)
    l_j = lax.broadcasted_iota(jnp.int32, (ROUTER_LANES, ROUTER_LANES), 1)
    before = jnp.where(l_i < l_j, 1.0, 0.0).astype(F32)
    offs = jnp.dot(jnp.broadcast_to(seg, (8, ROUTER_LANES)), before, precision=lax.Precision.HIGHEST,
                   preferred_element_type=F32)[0:1]
    pos = rank + offs
    lane = lax.broadcasted_iota(jnp.int32, (tb, ROUTER_LANES), 1)
    lane_a = jnp.min(jnp.where(hot, lane, ROUTER_LANES), axis=1, keepdims=True)
    lane_b = jnp.max(jnp.where(hot, lane, -1), axis=1, keepdims=True)
    pick = lambda sel, val: jnp.sum(jnp.where(sel, val, 0.0), axis=1, keepdims=True)
    sel_a, sel_b = lane == lane_a, lane == lane_b
    z = jnp.where(lane == 0, pick(sel_a, pos), 0.0) + jnp.where(lane == 1, pick(sel_b, pos), 0.0)
    plan_ref[...] = z.T[0:8, :]
    meta_ref[...] = jnp.concatenate([offs, carry, jnp.zeros((6, ROUTER_LANES), F32)], axis=0)
    grow_ref[...] = jnp.concatenate([jnp.broadcast_to(pick(sel_a, gates), (tb, LANES)),
                                     jnp.broadcast_to(pick(sel_b, gates), (tb, LANES))], axis=1)


def _moe_plan(gates, tb):
    n = gates.shape[0]
    nb = n // tb
    return pl.pallas_call(
        _moe_plan_kernel,
        grid=(nb,),
        in_specs=[pl.BlockSpec((tb, ROUTER_LANES), lambda i: (i, 0))],
        out_specs=[pl.BlockSpec((None, 8, tb), lambda i: (i, 0, 0)),
                   pl.BlockSpec((None, 8, ROUTER_LANES), lambda i: (i, 0, 0)),
                   pl.BlockSpec((tb, EXPERT_TOP_K * LANES), lambda i: (i, 0))],
        out_shape=[jax.ShapeDtypeStruct((nb, 8, tb), F32), jax.ShapeDtypeStruct((nb, 8, ROUTER_LANES), F32),
                   jax.ShapeDtypeStruct((n, EXPERT_TOP_K * LANES), F32)],
        compiler_params=_cparams(1),
        name="moe_plan",
    )(gates)


def _moe_kernel(tile_rows, pitch, offs_ref, cnts_ref, pos_ref, hn_ref, grow_ref, x1_ref, wg_ref, wu_ref, wd_ref,
                gfin_ref, y_ref, xs_scr, gcol_scr, otile_scr, yacc_scr, tok_scr):
    blk = pl.program_id(0)
    step = pl.program_id(1)
    tb = hn_ref.shape[0]
    n_lt = D_MODEL // LANES

    @pl.when(step == 0)
    def _group_rows():
        xs_scr[...] = jnp.zeros_like(xs_scr)
        gcol_scr[...] = jnp.zeros_like(gcol_scr)
        yacc_scr[...] = jnp.zeros_like(yacc_scr)

        def clear_pad(ex, c):
            end = offs_ref[blk, ex] + cnts_ref[blk, ex]
            for k in range(MOE_ALIGN - 1):
                tok_scr[end + k] = tb
            return c

        lax.fori_loop(0, N_EXPERTS, clear_pad, 0)

        def place(t8, c):
            for u in range(8):
                t = t8 * 8 + u
                row = hn_ref[pl.ds(t, 1), :]
                gate = grow_ref[pl.ds(t, 1), :]
                for s in range(EXPERT_TOP_K):
                    p = pos_ref[s, t]
                    xs_scr[pl.ds(p, 1), :] = row
                    gcol_scr[pl.ds(p, 1), :] = gate[:, s * LANES:(s + 1) * LANES]
                    tok_scr[p] = t
            return c

        lax.fori_loop(0, tb // 8, place, 0)

    slot_rows = n_lt * pitch

    def compute(k, r0, slot):
        xw = xs_scr[pl.ds(r0, tile_rows), :]
        x = jnp.concatenate(
            [pltpu.unpack_elementwise(xw, index=i, packed_dtype=jnp.bfloat16, unpacked_dtype=F32).astype(BF16)
             for i in range(2)], axis=1)
        g = gcol_scr[pl.ds(r0, tile_rows), :]
        a = jnp.dot(x, wg_ref[k], preferred_element_type=F32)
        u = jnp.dot(x, wu_ref[k], preferred_element_type=F32)
        act = (a * _sigmoid(a)) * u * jnp.concatenate([g] * (D_EXPERT // LANES), axis=1)
        out = jnp.dot(act.astype(BF16), wd_ref[k], preferred_element_type=F32)
        for s in range(n_lt):
            otile_scr[pl.ds(slot * slot_rows + s * pitch, tile_rows), :] = out[:, s * LANES:(s + 1) * LANES]

    def scatter_add(r0, valid, slot):
        def add8(j8, c3):
            base = pl.multiple_of(j8 * 8, 8)
            dst = [pl.ds(pl.multiple_of(tok_scr[r0 + base + v] * n_lt, n_lt), n_lt) for v in range(8)]
            rows = [yacc_scr[dst[v], :]
                    + otile_scr[pl.ds(slot * slot_rows + base + v, n_lt, stride=pitch), :]
                    for v in range(8)]
            for v in range(8):
                yacc_scr[dst[v], :] = rows[v]
            return c3

        lax.fori_loop(0, (valid + 7) // 8, add8, 0)

    offs = [offs_ref[blk, step * MOE_EXPERTS_PER_STEP + k] for k in range(MOE_EXPERTS_PER_STEP)]
    cnts = [cnts_ref[blk, step * MOE_EXPERTS_PER_STEP + k] for k in range(MOE_EXPERTS_PER_STEP)]
    for k in range(MOE_EXPERTS_PER_STEP):
        compute(k, pl.multiple_of(offs[k], MOE_ALIGN), k)
    for k in range(MOE_EXPERTS_PER_STEP):
        scatter_add(offs[k], jnp.minimum(tile_rows, cnts[k]), k)
    for k in range(MOE_EXPERTS_PER_STEP):
        def more(c, cc, k=k):
            r0 = pl.multiple_of(offs[k] + c * tile_rows, MOE_ALIGN)
            compute(k, r0, 0)
            scatter_add(r0, jnp.minimum(tile_rows, cnts[k] - c * tile_rows), 0)
            return cc

        lax.fori_loop(1, (cnts[k] + tile_rows - 1) // tile_rows, more, 0)

    @pl.when(step == pl.num_programs(1) - 1)
    def _fin():
        moe = jnp.concatenate([yacc_scr[pl.ds(s, tb, stride=n_lt), :] for s in range(n_lt)], axis=1)
        y_ref[...] = _rms(x1_ref[...] + moe, gfin_ref[...])


def _moe(hn_packed, gates, x1, wg, wu, wd, g_final, tb):
    n, d = x1.shape
    nb = n // tb
    plan, meta, gate_rows = _moe_plan(gates, tb)
    pos = plan[:, 0:2, :].astype(jnp.int32)
    offs = meta[:, 0, :N_EXPERTS].astype(jnp.int32)
    cnts = meta[:, 1, :N_EXPERTS].astype(jnp.int32)
    p_rows = _moe_rows(tb)
    tile_rows, pitch = _moe_tile(tb)
    per = MOE_EXPERTS_PER_STEP
    n_lt = d // LANES
    smem = lambda: pl.BlockSpec((None, 2, tb), lambda i, e, *_: (i, 0, 0), memory_space=pltpu.SMEM)
    once = dict(pipeline_mode=pl.Buffered(1))
    grid_spec = pltpu.PrefetchScalarGridSpec(
        num_scalar_prefetch=2,
        grid=(nb, N_EXPERTS // per),
        in_specs=[smem(),
                  pl.BlockSpec((tb, d // 2), lambda i, e, *_: (i, 0), **once),
                  pl.BlockSpec((tb, EXPERT_TOP_K * LANES), lambda i, e, *_: (i, 0), **once),
                  pl.BlockSpec((tb, d), lambda i, e, *_: (i, 0), **once),
                  pl.BlockSpec((per, d, D_EXPERT), lambda i, e, *_: (e, 0, 0)),
                  pl.BlockSpec((per, d, D_EXPERT), lambda i, e, *_: (e, 0, 0)),
                  pl.BlockSpec((per, D_EXPERT, d), lambda i, e, *_: (e, 0, 0)),
                  pl.BlockSpec((1, d), lambda i, e, *_: (0, 0))],
        out_specs=pl.BlockSpec((tb, d), lambda i, e, *_: (i, 0)),
        scratch_shapes=[pltpu.VMEM((p_rows, d // 2), jnp.uint32),
                        pltpu.VMEM((p_rows, LANES), F32),
                        pltpu.VMEM((per * n_lt * pitch, LANES), F32),
                        pltpu.VMEM(((tb + 1) * n_lt, LANES), F32),
                        pltpu.SMEM((p_rows,), jnp.int32)])
    return pl.pallas_call(
        functools.partial(_moe_kernel, tile_rows, pitch),
        grid_spec=grid_spec,
        out_shape=jax.ShapeDtypeStruct((n, d), F32),
        compiler_params=pltpu.CompilerParams(dimension_semantics=("arbitrary", "arbitrary"),
                                             vmem_limit_bytes=MOE_VMEM_LIMIT),
        name="moe",
    )(offs, cnts, pos, hn_packed, gate_rows, x1, wg, wu, wd, g_final)


def kernel(x_prompt, x_sample, cache_k, cache_v, state_ssm_re, state_ssm_im, rel_bias, g_mix, w_in, g_out_a, g_out_b, w_out, ssm_a_re, ssm_a_im, ssm_log_dt, ssm_b_re, ssm_b_im, ssm_c_re, ssm_c_im, ssm_d, w_glu, b_glu, g_ffn, w_router_group, w_router_expert, w_expert_gate, w_expert_up, w_expert_down, g_final):
    depth = g_mix.shape[0]
    assert depth == 1, "kernel is written for the single-layer configuration of the problem"
    l = 0
    b, t, d = x_prompt.shape
    bd, ts, _ = x_sample.shape
    assert ts == DEC_T and t % ATT_BLOCK == 0 and cache_k.shape[2] == MAX_WINDOW
    keep = min(MAX_WINDOW, t)

    w_in_b = w_in[l].astype(BF16)
    gm = g_mix[l][None, :]
    w_router = jnp.concatenate(
        [jnp.transpose(w_router_expert[l], (1, 0, 2)).reshape(d, N_EXPERTS), w_router_group[l],
         jnp.zeros((d, ROUTER_LANES - N_EXPERTS - N_EXPERT_GROUPS), F32)], axis=1)
    w_router_hi = w_router.astype(BF16)
    w_router_lo = (w_router - w_router_hi.astype(F32)).astype(BF16)
    post_w = dict(
        g_out_a=g_out_a[l][None, :], g_out_b=g_out_b[l][None, :], w_glu=w_glu[l].astype(BF16),
        b_glu=b_glu[l][None, :], w_out=w_out[l].astype(BF16), g_ffn=g_ffn[l][None, :],
        w_router=jnp.concatenate([w_router_hi, w_router_lo], axis=1))
    wg = w_expert_gate[l].astype(BF16)
    wu = w_expert_up[l].astype(BF16)
    wd = w_expert_down[l].astype(BF16)
    gfin = g_final[None, :]
    ssm_ops = _ssm_prep(ssm_a_re[l], ssm_a_im[l], ssm_log_dt[l], ssm_b_re[l], ssm_b_im[l],
                        ssm_c_re[l], ssm_c_im[l])
    d_tile = jnp.tile(ssm_d[l].reshape(N_GROUPS, 1, SSM_GROUP), (1, 1, CHUNK))

    qp, kp, vp, kt_win, vt_win, u_tok = _inproj_pair(x_prompt, gm, w_in_b, tm=512, keep=keep)
    oa_p = _attn_prompt(qp, kp, vp, rel_bias)
    ys_p, rp, ip = _ssm_prompt(u_tok, ssm_ops, ssm_d[l])
    x1_p, hn_p, gates_p = _post_mix(oa_p, ys_p, x_prompt, post_w, tm=512, pair=True)
    y_p = _moe(hn_p.reshape(b * t, d // 2), gates_p.reshape(b * t, ROUTER_LANES), x1_p.reshape(b * t, d),
               wg, wu, wd, gfin, tb=min(MOE_BLOCK, b * t))

    n_s = bd * ts
    qs, ks, vs, us = _inproj_tok(x_sample.reshape(n_s, d), gm, w_in_b)
    ckt = jnp.transpose(cache_k[l], (0, 2, 3, 1)).reshape(bd, MIX_A, MAX_WINDOW)
    cvt = jnp.transpose(cache_v[l], (0, 2, 3, 1)).reshape(bd, MIX_A, MAX_WINDOW)
    oa_s = _attn_decode(qs.reshape(bd, ts, MIX_A), ks.reshape(bd, ts, MIX_A), vs.reshape(bd, ts, MIX_A),
                        ckt, cvt, rel_bias)
    ugs = jnp.transpose(us.reshape(bd, ts, N_GROUPS, SSM_GROUP), (2, 0, 1, 3)).reshape(N_GROUPS, bd, ts * SSM_GROUP)
    s0r = jnp.transpose(state_ssm_re[l], (1, 0, 2))
    s0i = jnp.transpose(state_ssm_im[l], (1, 0, 2))
    ygs, rs, is_ = _ssm_decode(ugs, s0r, s0i, ssm_ops, d_tile)
    ys_s = jnp.transpose(ygs.reshape(N_GROUPS, bd, ts, SSM_GROUP), (1, 2, 0, 3)).reshape(n_s, MIX_B)
    x1_s, hn_s, gates_s = _post_mix(oa_s.reshape(1, n_s, MIX_A), ys_s.reshape(1, n_s, MIX_B),
                                    x_sample.reshape(1, n_s, d), post_w, tm=n_s, pair=False)
    y_s = _moe(hn_s.reshape(n_s, d // 2), gates_s.reshape(n_s, ROUTER_LANES), x1_s.reshape(n_s, d),
               wg, wu, wd, gfin, tb=n_s)

    y_prompt = y_p.reshape(b, t, d)
    y_sample = y_s.reshape(bd, ts, d)
    k_win = jnp.transpose(kt_win.reshape(b, N_HEADS, HEAD_DIM, keep), (0, 3, 1, 2))[None]
    v_win = jnp.transpose(vt_win.reshape(b, N_HEADS, HEAD_DIM, keep), (0, 3, 1, 2))[None]
    k_new = ks.reshape(1, bd, ts, N_HEADS, HEAD_DIM)
    v_new = vs.reshape(1, bd, ts, N_HEADS, HEAD_DIM)
    return (y_prompt, y_sample, k_win, v_win, k_new, v_new,
            rp.reshape(1, b, N_GROUPS, SSM_STATE), ip.reshape(1, b, N_GROUPS, SSM_STATE),
            jnp.transpose(rs, (1, 0, 2))[None], jnp.transpose(is_, (1, 0, 2))[None])
```

```python
import functools
import math

import jax
import jax.numpy as jnp
from jax import lax
from jax.experimental import pallas as pl
from jax.experimental.pallas import tpu as pltpu

F32 = jnp.float32
BF16 = jnp.bfloat16

D_MODEL = 1024
HEAD_DIM = 64
MIX_A = 512
N_HEADS = 8
MIX_B = 512
SSM_GROUP = 16
N_GROUPS = 32
SSM_STATE = 64
PROJ_COLS = 3 * MIX_A + MIX_B
DILATIONS = (1, 4, 16)
TAPS = 128
MAX_WINDOW = 2048
N_BUCKETS = 32
N_EXPERT_GROUPS = 4
EXPERTS_PER_GROUP = 8
N_EXPERTS = 32
EXPERT_TOP_K = 2
D_EXPERT = 256
EPS = 1e-6
NEG_INF = -1e30
SCALE = HEAD_DIM ** -0.5

LANES = 128
ROUTER_LANES = 128
CHUNK = 16
ATT_BLOCK = 2048
ATT_PITCH = 136
VMEM_LIMIT = 56 * 1024 * 1024


def _cparams(n_axes):
    return pltpu.CompilerParams(dimension_semantics=("arbitrary",) * n_axes,
                                vmem_limit_bytes=VMEM_LIMIT)


def _t5_bucket(distance):
    max_exact = N_BUCKETS // 2
    nf = jnp.maximum(distance, 1).astype(F32)
    large = max_exact + jnp.floor(jnp.log(nf / max_exact) / math.log(MAX_WINDOW / max_exact)
                                  * (N_BUCKETS - max_exact)).astype(jnp.int32)
    large = jnp.minimum(large, N_BUCKETS - 1)
    return jnp.where(distance < max_exact, distance, large)


def _rms(x, g):
    return x * lax.rsqrt(jnp.mean(x * x, axis=-1, keepdims=True) + EPS) * g


def _inproj_pair_kernel(first_win_tile, x_ref, g_ref, w_ref, q_ref, k_ref, v_ref, kt_ref, vt_ref, u_ref):
    h = _rms(x_ref[...], g_ref[...])
    p = jnp.dot(h.astype(BF16), w_ref[...], preferred_element_type=F32)
    for j in range(MIX_A // LANES):
        q_ref[j] = p[:, LANES * j:LANES * (j + 1)] * SCALE
        k_ref[j] = p[:, MIX_A + LANES * j:MIX_A + LANES * (j + 1)]
        v_ref[j] = p[:, 2 * MIX_A + LANES * j:2 * MIX_A + LANES * (j + 1)]
    u_ref[...] = p[:, 3 * MIX_A:]

    @pl.when(pl.program_id(1) >= first_win_tile)
    def _window():
        kt_ref[...] = p[:, MIX_A:2 * MIX_A].T
        vt_ref[...] = p[:, 2 * MIX_A:3 * MIX_A].T


def _inproj_pair(x, g, w_bf16, tm, keep):
    b, t, d = x.shape
    npair = MIX_A // LANES
    first = (t - keep) // tm
    pair = jax.ShapeDtypeStruct((b, npair, t, LANES), F32)
    win = jax.ShapeDtypeStruct((b, MIX_A, keep), F32)
    tok = jax.ShapeDtypeStruct((b, t, MIX_A), F32)
    pair_spec = pl.BlockSpec((None, npair, tm, LANES), lambda bi, i: (bi, 0, i, 0))
    win_spec = pl.BlockSpec((None, MIX_A, tm), lambda bi, i: (bi, 0, jnp.maximum(i - first, 0)))
    tok_spec = pl.BlockSpec((None, tm, MIX_A), lambda bi, i: (bi, i, 0))
    return pl.pallas_call(
        functools.partial(_inproj_pair_kernel, first),
        grid=(b, t // tm),
        in_specs=[pl.BlockSpec((None, tm, d), lambda bi, i: (bi, i, 0)),
                  pl.BlockSpec((1, d), lambda bi, i: (0, 0)),
                  pl.BlockSpec((d, PROJ_COLS), lambda bi, i: (0, 0))],
        out_specs=[pair_spec, pair_spec, pair_spec, win_spec, win_spec, tok_spec],
        out_shape=[pair, pair, pair, win, win, tok],
        compiler_params=_cparams(2),
        name="inproj_prompt",
    )(x, g, w_bf16)


def _inproj_tok_kernel(x_ref, g_ref, w_ref, q_ref, k_ref, v_ref, u_ref):
    h = _rms(x_ref[...], g_ref[...])
    p = jnp.dot(h.astype(BF16), w_ref[...], preferred_element_type=F32)
    q_ref[...] = p[:, :MIX_A] * SCALE
    k_ref[...] = p[:, MIX_A:2 * MIX_A]
    v_ref[...] = p[:, 2 * MIX_A:3 * MIX_A]
    u_ref[...] = p[:, 3 * MIX_A:]


def _inproj_tok(x, g, w_bf16):
    n, d = x.shape
    out = jax.ShapeDtypeStruct((n, MIX_A), F32)
    return pl.pallas_call(
        _inproj_tok_kernel,
        out_shape=[out, out, out, out],
        compiler_params=pltpu.CompilerParams(vmem_limit_bytes=VMEM_LIMIT),
        name="inproj_sample",
    )(x, g, w_bf16)


def _prompt_bias_ids():
    r = jnp.arange(TAPS, dtype=jnp.int32)[:, None]
    kpos = jnp.arange(2 * TAPS, dtype=jnp.int32)[None, :] - TAPS
    rel = r - kpos
    valid = (rel >= 0) & (rel <= TAPS)
    ids = [jnp.where(valid, _t5_bucket(d * rel), -1) for d in DILATIONS]
    return jnp.stack(ids).astype(jnp.int32)


def _attn_prompt_kernel(rb_ref, ids_ref, q_ref, kc_ref, kp_ref, vc_ref, vp_ref, o_ref,
                        bias_scr, fm_scr, m0_scr, l0_scr, o0_scr, m1_scr, l1_scr, o1_scr,
                        m2_scr, l2_scr, o2_scr):
    bi = pl.program_id(0)
    p = pl.program_id(1)
    i = pl.program_id(2)
    blk = ATT_BLOCK
    npair = N_HEADS // 2

    @pl.when((bi == 0) & (p == 0) & (i == 0))
    def _build_bias():
        col = lax.broadcasted_iota(jnp.int32, (2 * TAPS, 2 * TAPS), 1)
        fm_scr[0] = jnp.zeros((2 * TAPS, 2 * TAPS), F32)
        fm_scr[1] = jnp.where(col < TAPS, NEG_INF, 0.0).astype(F32)
        for c in range(len(DILATIONS)):
            ids = ids_ref[c]

            for h in range(N_HEADS):
                tile = jnp.where(ids < 0, NEG_INF, 0.0).astype(F32)
                for bkt in range(N_BUCKETS):
                    tile = tile + jnp.where(ids == bkt, rb_ref[bkt:bkt + 1, h:h + 1], 0.0)
                bias_scr[c * npair + h // 2, (h % 2) * TAPS:(h % 2 + 1) * TAPS, :] = tile

    lane = lax.broadcasted_iota(jnp.int32, (1, LANES), 1)
    low = lane < HEAD_DIM
    nt = (((1,), (1,)), ((), ()))
    ones = jnp.ones((2 * TAPS, LANES), BF16)
    stats = ((m0_scr, l0_scr, o0_scr), (m1_scr, l1_scr, o1_scr), (m2_scr, l2_scr, o2_scr))

    def rows(ref, start, d):
        if d == 1:
            return ref[pl.ds(start, TAPS), :]
        return ref[pl.ds(start, TAPS, stride=d), :]

    def sub_block(c, d, qs, first, dst_rows):
        q = rows(q_ref, qs, d)
        q2 = jnp.concatenate([jnp.where(low, q, 0.0), jnp.where(low, 0.0, q)], axis=0).astype(BF16)
        back = qs - d * TAPS
        k_back = rows(kp_ref, blk + back, d) if back < 0 else rows(kc_ref, back, d)
        v_back = rows(vp_ref, blk + back, d) if back < 0 else rows(vc_ref, back, d)
        k = jnp.concatenate([k_back, rows(kc_ref, qs, d)], axis=0).astype(BF16)
        v = jnp.concatenate([v_back, rows(vc_ref, qs, d)], axis=0).astype(BF16)
        s_ = lax.dot_general(q2, k, nt, preferred_element_type=F32) + bias_scr[c * npair + p]
        if first is not None:
            s_ = s_ + fm_scr[first]
        m = jnp.max(s_, axis=1, keepdims=True)
        pe = jnp.exp(s_ - m).astype(BF16)
        ol = jnp.dot(pe, jnp.concatenate([v, ones], axis=1), preferred_element_type=F32)
        o, l = ol[:, :LANES], ol[:, LANES:]
        m_ref, l_ref, a_ref = stats[c]
        m_ref[dst_rows, :] = jnp.where(low, m[:TAPS], m[TAPS:])
        l_ref[dst_rows, :] = jnp.where(low, l[:TAPS], l[TAPS:])
        a_ref[dst_rows, :] = jnp.where(low, o[:TAPS], o[TAPS:])

    at_start = jnp.where(i == 0, 1, 0)
    for c, d in enumerate(DILATIONS):
        n_grp = blk // (d * TAPS)
        for r in range(d):
            for g in range(n_grp):
                qs = r + d * TAPS * g
                first = at_start if g == 0 else None
                if d == 1:
                    dst_rows = pl.ds(qs, TAPS)
                elif n_grp > 1:
                    dst_rows = pl.ds(qs, TAPS, stride=d)
                else:
                    dst_rows = pl.ds(r * ATT_PITCH, TAPS)
                sub_block(c, d, qs, first, dst_rows)

    d_last = DILATIONS[-1]

    def merge_body(j, carry):
        tok = pl.ds(pl.multiple_of(j * d_last, d_last), d_last)
        dil = pl.ds(j, d_last, stride=ATT_PITCH)
        m0, m1, m2 = m0_scr[tok, :], m1_scr[tok, :], m2_scr[dil, :]
        m = jnp.maximum(jnp.maximum(m0, m1), m2)
        a0, a1, a2 = jnp.exp(m0 - m), jnp.exp(m1 - m), jnp.exp(m2 - m)
        num = o0_scr[tok, :] * a0 + o1_scr[tok, :] * a1 + o2_scr[dil, :] * a2
        den = l0_scr[tok, :] * a0 + l1_scr[tok, :] * a1 + l2_scr[dil, :] * a2
        o_ref[tok, :] = num / den
        return carry

    lax.fori_loop(0, blk // d_last, merge_body, 0, unroll=4)


def _attn_prompt(q, k, v, rel_bias):
    b, npair, t, _ = q.shape
    blk = ATT_BLOCK
    cur = pl.BlockSpec((None, None, blk, LANES), lambda bi, p, i: (bi, p, i, 0))
    prev = pl.BlockSpec((None, None, blk, LANES), lambda bi, p, i: (bi, p, jnp.maximum(i - 1, 0), 0))
    n_tiles = len(DILATIONS) * npair
    stats = ([pltpu.VMEM((blk, LANES), F32)] * 6
             + [pltpu.VMEM((DILATIONS[-1] * ATT_PITCH, LANES), F32)] * 3)
    return pl.pallas_call(
        _attn_prompt_kernel,
        grid=(b, npair, t // blk),
        in_specs=[pl.BlockSpec((N_BUCKETS, N_HEADS), lambda bi, p, i: (0, 0)),
                  pl.BlockSpec((len(DILATIONS), TAPS, 2 * TAPS), lambda bi, p, i: (0, 0, 0)),
                  cur, cur, prev, cur, prev],
        out_specs=cur,
        out_shape=jax.ShapeDtypeStruct(q.shape, F32),
        scratch_shapes=[pltpu.VMEM((n_tiles, 2 * TAPS, 2 * TAPS), F32),
                        pltpu.VMEM((2, 2 * TAPS, 2 * TAPS), F32)] + stats,
        compiler_params=_cparams(3),
        name="attn_prompt",
    )(rel_bias, _prompt_bias_ids(), q, k, k, v, v)


DEC_T = 4
DEC_NEW_PAD = 128
DEC_SEQ_PER_STEP = 2
DEC_KEYS = MAX_WINDOW + DEC_NEW_PAD


def _decode_tables():
    qi = jnp.arange(DEC_T, dtype=jnp.int32)[:, None]
    rho = jnp.arange(MAX_WINDOW, dtype=jnp.int32)[None, :]
    dist_b = MAX_WINDOW + qi - rho
    mult_b = jnp.zeros_like(dist_b)
    for d in DILATIONS:
        mult_b = mult_b + (((dist_b % d) == 0) & (dist_b <= d * TAPS)).astype(jnp.int32)
    jj = jnp.arange(DEC_NEW_PAD, dtype=jnp.int32)[None, :]
    dist_n = qi - jj
    mult_n = jnp.where(dist_n == 0, len(DILATIONS), jnp.where((dist_n > 0) & (jj < DEC_T), 1, 0))
    dist = jnp.concatenate([dist_b, jnp.maximum(dist_n, 0)], axis=1)
    mult = jnp.concatenate([mult_b, mult_n.astype(jnp.int32)], axis=1)
    ids = jnp.where(mult > 0, _t5_bucket(dist), -1).astype(jnp.int32)
    ids = jnp.repeat(ids, N_HEADS, axis=0)
    mult = jnp.repeat(mult, N_HEADS, axis=0).astype(F32)
    return ids, mult


def _attn_decode_kernel(rbt_ref, ids_ref, mult_ref, q_ref, kn_ref, vn_ref, kt_ref, vt_ref, o_ref, bias_scr):
    n_rows = DEC_T * N_HEADS

    @pl.when(pl.program_id(0) == 0)
    def _build_bias():
        ids = ids_ref[...]
        tile = jnp.where(ids < 0, NEG_INF, 0.0).astype(F32)
        for bkt in range(N_BUCKETS):
            tile = tile + jnp.where(ids == bkt, rbt_ref[:, bkt:bkt + 1], 0.0)
        bias_scr[...] = tile

    lane = lax.broadcasted_iota(jnp.int32, (N_HEADS, MIX_A), 1)
    head = lax.broadcasted_iota(jnp.int32, (N_HEADS, MIX_A), 0)
    hmask = (lane // HEAD_DIM) == head

    zpad = jnp.zeros((DEC_NEW_PAD - DEC_T, MIX_A), F32)
    nt = (((1,), (1,)), ((), ()))
    for j in range(q_ref.shape[0]):
        q = q_ref[j]
        qm = jnp.where(hmask[None], q[:, None, :], 0.0).reshape(n_rows, MIX_A).astype(BF16)
        kn = jnp.concatenate([kn_ref[j], zpad], axis=0).astype(BF16)
        vn = jnp.concatenate([vn_ref[j], zpad], axis=0).astype(BF16)
        kt = kt_ref[j].astype(BF16)
        vt = vt_ref[j].astype(BF16)
        s = jnp.concatenate([jnp.dot(qm, kt, preferred_element_type=F32),
                             lax.dot_general(qm, kn, nt, preferred_element_type=F32)], axis=1)
        s = s + bias_scr[...]
        m = jnp.max(s, axis=1, keepdims=True)
        pe = jnp.exp(s - m) * mult_ref[...]
        l = jnp.sum(pe, axis=1, keepdims=True)
        pb = pe.astype(BF16)
        o = (lax.dot_general(pb[:, :MAX_WINDOW], vt, nt, preferred_element_type=F32)
             + jnp.dot(pb[:, MAX_WINDOW:], vn, preferred_element_type=F32)) / l
        o3 = o.reshape(DEC_T, N_HEADS, MIX_A)
        o_ref[j] = jnp.sum(jnp.where(hmask[None], o3, 0.0), axis=1)


def _attn_decode(q, k_new, v_new, cache_kt, cache_vt, rel_bias):
    bd = q.shape[0]
    ids, mult = _decode_tables()
    rbt = jnp.tile(rel_bias.T, (DEC_T, 1))
    per = DEC_SEQ_PER_STEP if bd % DEC_SEQ_PER_STEP == 0 else 1
    new_spec = pl.BlockSpec((per, DEC_T, MIX_A), lambda b: (b, 0, 0))
    cache_spec = pl.BlockSpec((per, MIX_A, MAX_WINDOW), lambda b: (b, 0, 0))
    n_rows = DEC_T * N_HEADS
    const = lambda shape: pl.BlockSpec(shape, lambda b: (0, 0))
    return pl.pallas_call(
        _attn_decode_kernel,
        grid=(bd // per,),
        in_specs=[const((n_rows, N_BUCKETS)), const((n_rows, DEC_KEYS)), const((n_rows, DEC_KEYS)),
                  new_spec, new_spec, new_spec, cache_spec, cache_spec],
        out_specs=new_spec,
        out_shape=jax.ShapeDtypeStruct((bd, DEC_T, MIX_A), F32),
        scratch_shapes=[pltpu.VMEM((n_rows, DEC_KEYS), F32)],
        compiler_params=_cparams(1),
        name="attn_decode",
    )(rbt, ids, mult, q, k_new, v_new, cache_kt, cache_vt)


def _ssm_prep_kernel(are_ref, aim_ref, ldt_ref, btr_ref, bti_ref, cre_ref, cim_ref,
                     tit_ref, tinr_ref, tini_ref, toutr_ref, touti_ref, pw_ref, tintr_ref, tinti_ref):
    hi = lax.Precision.HIGHEST
    nt = (((1,), (1,)), ((), ()))
    a_re, a_im = are_ref[...], aim_ref[...]
    dt = jnp.exp(ldt_ref[...])
    decay = jnp.exp(a_re * dt)
    ab_re = decay * jnp.cos(a_im * dt)
    ab_im = decay * jnp.sin(a_im * dt)
    inv = 1.0 / (a_re * a_re + a_im * a_im)
    coef_re = ((ab_re - 1.0) * a_re + ab_im * a_im) * inv
    coef_im = (ab_im * a_re - (ab_re - 1.0) * a_im) * inv
    bt_re, bt_im = btr_ref[...], bti_ref[...]
    bb_re = coef_re * bt_re - coef_im * bt_im
    bb_im = coef_re * bt_im + coef_im * bt_re
    pw = [(jnp.ones_like(ab_re), jnp.zeros_like(ab_im))]
    for _ in range(CHUNK):
        pr, pi = pw[-1]
        pw.append((pr * ab_re - pi * ab_im, pr * ab_im + pi * ab_re))
    tin_re = jnp.concatenate(
        [pw[CHUNK - 1 - ti][0] * bb_re - pw[CHUNK - 1 - ti][1] * bb_im for ti in range(CHUNK)], axis=0).astype(BF16)
    tin_im = jnp.concatenate(
        [pw[CHUNK - 1 - ti][0] * bb_im + pw[CHUNK - 1 - ti][1] * bb_re for ti in range(CHUNK)], axis=0).astype(BF16)
    tinr_ref[...] = tin_re
    tini_ref[...] = tin_im
    p_i = lax.broadcasted_iota(jnp.int32, (SSM_STATE, SSM_STATE), 0)
    p_j = lax.broadcasted_iota(jnp.int32, (SSM_STATE, SSM_STATE), 1)
    eye = jnp.where(p_i == p_j, 1.0, 0.0).astype(BF16)
    tintr_ref[...] = lax.dot_general(eye, tin_re, nt, preferred_element_type=F32).astype(BF16)
    tinti_ref[...] = lax.dot_general(eye, tin_im, nt, preferred_element_type=F32).astype(BF16)
    c_re, c_im = cre_ref[...], cim_ref[...]
    cp_re = [c_re * pr - c_im * pi for pr, pi in pw]
    cp_im = [c_re * pi + c_im * pr for pr, pi in pw]
    toutr_ref[...] = jnp.concatenate(cp_re[1:], axis=0).astype(BF16)
    touti_ref[...] = (-jnp.concatenate(cp_im[1:], axis=0)).astype(BF16)
    kall = (lax.dot_general(jnp.concatenate(cp_re[:CHUNK], axis=0), bb_re, nt, precision=hi,
                            preferred_element_type=F32)
            - lax.dot_general(jnp.concatenate(cp_im[:CHUNK], axis=0), bb_im, nt, precision=hi,
                              preferred_element_type=F32))
    w = CHUNK * SSM_GROUP
    kb = kall.astype(BF16)
    ci_idx = lax.broadcasted_iota(jnp.int32, (SSM_GROUP, w), 0)
    col_idx = lax.broadcasted_iota(jnp.int32, (SSM_GROUP, w), 1)
    acc = jnp.zeros((w, w), F32)
    for ti in range(CHUNK):
        rows = ti * SSM_GROUP
        shifted = kb if ti == 0 else jnp.concatenate(
            [jnp.zeros((rows, SSM_GROUP), BF16), kb[:w - rows]], axis=0)
        place = jnp.where(col_idx == ci_idx + rows, 1.0, 0.0).astype(BF16)
        acc = acc + jnp.dot(shifted, place, preferred_element_type=F32)
    tit_ref[...] = acc.astype(BF16)
    zero = jnp.zeros_like(ab_re)
    pw_ref[...] = jnp.concatenate([pw[CHUNK][0], pw[CHUNK][1], pw[DEC_T][0], pw[DEC_T][1],
                                   zero, zero, zero, zero], axis=0)


def _ssm_prep(a_re, a_im, log_dt, b_re, b_im, c_re, c_im):
    g, p = a_re.shape
    w = CHUNK * SSM_GROUP
    row = lambda a: a[:, None, :]
    bt_re = jnp.transpose(b_re, (0, 2, 1))
    bt_im = jnp.transpose(b_im, (0, 2, 1))
    ldt = jnp.broadcast_to(log_dt[:, None, None], (g, 1, p))
    gspec = lambda shape: pl.BlockSpec((None,) + shape, lambda gi: (gi, 0, 0))
    return pl.pallas_call(
        _ssm_prep_kernel,
        grid=(g,),
        in_specs=[gspec((1, p)), gspec((1, p)), gspec((1, p)), gspec((SSM_GROUP, p)), gspec((SSM_GROUP, p)),
                  gspec((SSM_GROUP, p)), gspec((SSM_GROUP, p))],
        out_specs=[gspec((w, w)), gspec((w, p)), gspec((w, p)), gspec((w, p)), gspec((w, p)), gspec((8, p)),
                   gspec((p, w)), gspec((p, w))],
        out_shape=[jax.ShapeDtypeStruct((g, w, w), BF16)] + [jax.ShapeDtypeStruct((g, w, p), BF16)] * 4
                  + [jax.ShapeDtypeStruct((g, 8, p), F32)] + [jax.ShapeDtypeStruct((g, p, w), BF16)] * 2,
        compiler_params=_cparams(1),
        name="ssm_prep",
    )(row(a_re), row(a_im), ldt, bt_re, bt_im, c_re, c_im)


def _ssm_prompt_kernel(u_ref, tit_ref, tintr_ref, tinti_ref, toutr_ref, touti_ref, d_ref, pw_ref,
                       y_ref, sre_ref, sim_ref, ut_scr, yt_scr, pr_scr, pi_scr, sr_scr, si_scr):
    ngrp = LANES // SSM_GROUP
    nc = u_ref.shape[0] // CHUNK
    xt = [u_ref[pl.ds(ti, nc, stride=CHUNK), :].T for ti in range(CHUNK)]
    lane_pad = jnp.zeros((nc, LANES - SSM_STATE), F32)
    for j in range(ngrp):
        ut = jnp.concatenate([x[j * SSM_GROUP:(j + 1) * SSM_GROUP, :] for x in xt], axis=0).astype(BF16)
        ut_scr[j] = ut
        rows_j = pl.ds(j, nc, stride=ngrp)
        pr_scr[rows_j, :] = jnp.concatenate(
            [jnp.dot(tintr_ref[j], ut, preferred_element_type=F32).T, lane_pad], axis=1)
        pi_scr[rows_j, :] = jnp.concatenate(
            [jnp.dot(tinti_ref[j], ut, preferred_element_type=F32).T, lane_pad], axis=1)
    pad1 = jnp.zeros((ngrp, LANES - SSM_STATE), F32)
    ar = jnp.concatenate([jnp.concatenate([pw_ref[j, 0:1, :] for j in range(ngrp)], axis=0), pad1], axis=1)
    ai = jnp.concatenate([jnp.concatenate([pw_ref[j, 1:2, :] for j in range(ngrp)], axis=0), pad1], axis=1)

    def step(c, carry):
        sr, si = carry
        rows_c = pl.ds(pl.multiple_of(c * ngrp, ngrp), ngrp)
        sr_scr[rows_c, :] = sr
        si_scr[rows_c, :] = si
        return (ar * sr - ai * si + pr_scr[rows_c, :], ar * si + ai * sr + pi_scr[rows_c, :])

    zero = jnp.zeros((ngrp, LANES), F32)
    fin_r, fin_i = lax.fori_loop(0, nc, step, (zero, zero), unroll=8)
    for j in range(ngrp):
        sre_ref[j] = fin_r[j:j + 1, :SSM_STATE]
        sim_ref[j] = fin_i[j:j + 1, :SSM_STATE]
        rows_j = pl.ds(j, nc, stride=ngrp)
        yt = jnp.dot(tit_ref[j], ut_scr[j], preferred_element_type=F32)
        yt = yt + jnp.dot(toutr_ref[j], sr_scr[rows_j, :][:, :SSM_STATE].T.astype(BF16),
                          preferred_element_type=F32)
        yt = yt + jnp.dot(touti_ref[j], si_scr[rows_j, :][:, :SSM_STATE].T.astype(BF16),
                          preferred_element_type=F32)
        yt_scr[j] = yt
    d = d_ref[...]
    for to in range(CHUNK):
        v = jnp.concatenate([yt_scr[j, to * SSM_GROUP:(to + 1) * SSM_GROUP, :] for j in range(ngrp)], axis=0)
        rows = pl.ds(to, nc, stride=CHUNK)
        y_ref[rows, :] = v.T + d * u_ref[rows, :]


def _ssm_prompt(u, ops, d_skip):
    b, t, _ = u.shape
    p = SSM_STATE
    w = CHUNK * SSM_GROUP
    nc = t // CHUNK
    ngrp = LANES // SSM_GROUP
    g = N_GROUPS
    gspec = lambda shape: pl.BlockSpec((ngrp,) + shape, lambda bi, qi: (qi, 0, 0))
    io = pl.BlockSpec((None, t, LANES), lambda bi, qi: (bi, 0, qi))
    st = pl.BlockSpec((None, ngrp, 1, p), lambda bi, qi: (bi, qi, 0, 0))
    tit, _, _, toutr, touti, pw, tintr, tinti = ops
    return pl.pallas_call(
        _ssm_prompt_kernel,
        grid=(b, g // ngrp),
        in_specs=[io, gspec((w, w)), gspec((p, w)), gspec((p, w)), gspec((w, p)), gspec((w, p)),
                  pl.BlockSpec((None, 1, LANES), lambda bi, qi: (qi, 0, 0)), gspec((8, p))],
        out_specs=[io, st, st],
        out_shape=[jax.ShapeDtypeStruct(u.shape, F32),
                   jax.ShapeDtypeStruct((b, g, 1, p), F32), jax.ShapeDtypeStruct((b, g, 1, p), F32)],
        scratch_shapes=[pltpu.VMEM((ngrp, w, nc), BF16), pltpu.VMEM((ngrp, w, nc), F32)]
                       + [pltpu.VMEM((nc * ngrp, LANES), F32)] * 4,
        compiler_params=_cparams(2),
        name="ssm_prompt",
    )(u, tit, tintr, tinti, toutr, touti, d_skip.reshape(g // ngrp, 1, LANES), pw)


def _ssm_decode_kernel(u_ref, s0r_ref, s0i_ref, tit_ref, tinr_ref, tini_ref, toutr_ref, touti_ref, d_ref,
                       pw_ref, y_ref, sre_ref, sim_ref):
    nt = (((1,), (1,)), ((), ()))
    w = DEC_T * SSM_GROUP
    lo = (CHUNK - DEC_T) * SSM_GROUP
    u = u_ref[...]
    ub = u.astype(BF16)
    s0r = s0r_ref[...]
    s0i = s0i_ref[...]
    y = lax.dot_general(ub, tit_ref[0:w, 0:w], nt, preferred_element_type=F32)
    y = y + lax.dot_general(s0r.astype(BF16), toutr_ref[0:w, :], nt, preferred_element_type=F32)
    y = y + lax.dot_general(s0i.astype(BF16), touti_ref[0:w, :], nt, preferred_element_type=F32)
    y_ref[...] = y + d_ref[:, 0:w] * u
    ar = pw_ref[2:3, :]
    ai = pw_ref[3:4, :]
    sre_ref[...] = ar * s0r - ai * s0i + jnp.dot(ub, tinr_ref[lo:, :], preferred_element_type=F32)
    sim_ref[...] = ar * s0i + ai * s0r + jnp.dot(ub, tini_ref[lo:, :], preferred_element_type=F32)


def _ssm_decode(ug, s0_re, s0_im, ops, d_tile):
    g, bd, w = ug.shape
    p = SSM_STATE
    wc = CHUNK * SSM_GROUP
    gspec = lambda shape: pl.BlockSpec((None,) + shape, lambda gi: (gi, 0, 0))
    tit, tinr, tini, toutr, touti, pw = ops[:6]
    return pl.pallas_call(
        _ssm_decode_kernel,
        grid=(g,),
        in_specs=[gspec((bd, w)), gspec((bd, p)), gspec((bd, p)), gspec((wc, wc)), gspec((wc, p)),
                  gspec((wc, p)), gspec((wc, p)), gspec((wc, p)), gspec((1, wc)), gspec((8, p))],
        out_specs=[gspec((bd, w)), gspec((bd, p)), gspec((bd, p))],
        out_shape=[jax.ShapeDtypeStruct((g, bd, w), F32), jax.ShapeDtypeStruct((g, bd, p), F32),
                   jax.ShapeDtypeStruct((g, bd, p), F32)],
        compiler_params=_cparams(1),
        name="ssm_decode",
    )(ug, s0_re, s0_im, tit, tinr, tini, toutr, touti, d_tile, pw)


def _gelu_tanh(x):
    return 0.5 * x * (1.0 + jnp.tanh(math.sqrt(2.0 / math.pi) * (x + 0.044715 * (x * x * x))))


def _sigmoid(x):
    return 1.0 / (1.0 + jnp.exp(-x))


def _route(hn, wr_ref):
    n = hn.shape[0]
    h_hi = hn.astype(BF16)
    h_lo = (hn - h_hi.astype(F32)).astype(BF16)
    prod = jnp.dot(jnp.concatenate([h_hi, h_lo], axis=0), wr_ref[...], preferred_element_type=F32)
    logits = (prod[:n, :ROUTER_LANES] + prod[:n, ROUTER_LANES:]
              + prod[n:, :ROUTER_LANES] + prod[n:, ROUTER_LANES:])
    lidx = lax.broadcasted_iota(jnp.int32, (n, ROUTER_LANES), 1)
    is_e = lidx < N_EXPERTS
    is_g = jnp.logical_and(lidx >= N_EXPERTS, lidx < N_EXPERTS + N_EXPERT_GROUPS)
    gmax = jnp.max(jnp.where(is_g, logits, -jnp.inf), axis=1, keepdims=True)
    g_prob = 1.0 / jnp.sum(jnp.where(is_g, jnp.exp(logits - gmax), 0.0), axis=1, keepdims=True)
    g_sel = jnp.min(jnp.where(jnp.logical_and(is_g, logits == gmax), lidx - N_EXPERTS, N_EXPERT_GROUPS),
                    axis=1, keepdims=True)
    in_grp = jnp.logical_and(is_e, (lidx // EXPERTS_PER_GROUP) == g_sel)
    l1 = jnp.max(jnp.where(in_grp, logits, -jnp.inf), axis=1, keepdims=True)
    i1 = jnp.min(jnp.where(jnp.logical_and(in_grp, logits == l1), lidx, ROUTER_LANES), axis=1, keepdims=True)
    rest = jnp.logical_and(in_grp, lidx != i1)
    l2 = jnp.max(jnp.where(rest, logits, -jnp.inf), axis=1, keepdims=True)
    i2 = jnp.min(jnp.where(jnp.logical_and(rest, logits == l2), lidx, ROUTER_LANES), axis=1, keepdims=True)
    e2 = jnp.exp(l2 - l1)
    w1 = g_prob / (1.0 + e2)
    w2 = g_prob * e2 / (1.0 + e2)
    return jnp.where(lidx == i1, w1, 0.0) + jnp.where(lidx == i2, w2, 0.0)


def _post_body(oa, ys, x, ga_ref, gb_ref, wglu_ref, bglu_ref, wout_ref, gf_ref, wr_ref,
               x1_ref, hn_ref, gates_ref):
    z = _gelu_tanh(ys)
    gate = _sigmoid(jnp.dot(z.astype(BF16), wglu_ref[...], preferred_element_type=F32) + bglu_ref[...])
    ob = z * gate
    mixed = jnp.concatenate([_rms(oa, ga_ref[...]), _rms(ob, gb_ref[...])], axis=1).astype(BF16)
    x1 = x + jnp.dot(mixed, wout_ref[...], preferred_element_type=F32)
    x1_ref[...] = x1
    hn = _rms(x1, gf_ref[...])
    half = hn.shape[1] // 2
    hn_ref[...] = pltpu.pack_elementwise([hn[:, :half], hn[:, half:]], packed_dtype=jnp.bfloat16)
    gates_ref[...] = _route(hn, wr_ref)


def _post_pair_kernel(oa_ref, ys_ref, x_ref, *rest):
    oa = jnp.concatenate([oa_ref[j] for j in range(MIX_A // LANES)], axis=1)
    _post_body(oa, ys_ref[...], x_ref[...], *rest)


def _post_tok_kernel(oa_ref, ys_ref, x_ref, *rest):
    _post_body(oa_ref[...], ys_ref[...], x_ref[...], *rest)


def _post_mix(oa, ys, x, w, tm, pair):
    b, t, d = x.shape
    row = lambda width: pl.BlockSpec((None, tm, width), lambda bi, i: (bi, i, 0))
    const = lambda a: pl.BlockSpec(a.shape, lambda bi, i: (0,) * a.ndim)
    oa_spec = (pl.BlockSpec((None, MIX_A // LANES, tm, LANES), lambda bi, i: (bi, 0, i, 0)) if pair
               else row(MIX_A))
    weights = [w['g_out_a'], w['g_out_b'], w['w_glu'], w['b_glu'], w['w_out'], w['g_ffn'], w['w_router']]
    return pl.pallas_call(
        _post_pair_kernel if pair else _post_tok_kernel,
        grid=(b, t // tm),
        in_specs=[oa_spec, row(MIX_B), row(d)] + [const(a) for a in weights],
        out_specs=[row(d), row(d // 2), row(ROUTER_LANES)],
        out_shape=[jax.ShapeDtypeStruct((b, t, d), F32), jax.ShapeDtypeStruct((b, t, d // 2), jnp.uint32),
                   jax.ShapeDtypeStruct((b, t, ROUTER_LANES), F32)],
        compiler_params=_cparams(2),
        name="post_mix_prompt" if pair else "post_mix_sample",
    )(oa, ys, x, *weights)


MOE_BLOCK = 2048
MOE_TILE_SLACK = 1.25
MOE_ALIGN = 8
MOE_EXPERTS_PER_STEP = 2
MOE_VMEM_LIMIT = 60 * 1024 * 1024
PLAN_CHUNK = 256
BF16_ROWS = 16


def _moe_tile(tb):
    mean = EXPERT_TOP_K * tb / N_EXPERTS
    rows = -(-int(MOE_TILE_SLACK * mean) // BF16_ROWS) * BF16_ROWS
    pitch = rows // 8 + (1 - (rows // 8) % 2)
    return rows, 8 * pitch


def _moe_rows(tb):
    rows = EXPERT_TOP_K * tb + N_EXPERTS * (MOE_ALIGN - 1) + _moe_tile(tb)[0]
    return -(-rows // MOE_ALIGN) * MOE_ALIGN


def _moe_plan_kernel(gates_ref, plan_ref, meta_ref, grow_ref):
    tb = gates_ref.shape[0]
    gates = gates_ref[...]
    hot = gates > 0.0
    onehot = jnp.where(hot, 1.0, 0.0).astype(BF16)
    ch = min(PLAN_CHUNK, tb)
    r_i = lax.broadcasted_iota(jnp.int32, (ch, ch), 0)
    c_i = lax.broadcasted_iota(jnp.int32, (ch, ch), 1)
    earlier = jnp.where(c_i < r_i, 1.0, 0.0).astype(BF16)
    carry = jnp.zeros((1, ROUTER_LANES), F32)
    ranks = []
    for k in range(tb // ch):
        oh = onehot[k * ch:(k + 1) * ch]
        ranks.append(jnp.dot(earlier, oh, preferred_element_type=F32) + carry)
        carry = carry + jnp.sum(oh.astype(F32), axis=0, keepdims=True)
    rank = jnp.concatenate(ranks, axis=0)
    seg = jnp.floor((carry + (MOE_ALIGN - 1.0)) * (1.0 / MOE_ALIGN)) * MOE_ALIGN
    l_i = lax.broadcasted_iota(jnp.int32, (ROUTER_LANES, ROUTER_LANES), 0)
    l_j = lax.broadcasted_iota(jnp.int32, (ROUTER_LANES, ROUTER_LANES), 1)
    before = jnp.where(l_i < l_j, 1.0, 0.0).astype(F32)
    offs = jnp.dot(jnp.broadcast_to(seg, (8, ROUTER_LANES)), before, precision=lax.Precision.HIGHEST,
                   preferred_element_type=F32)[0:1]
    pos = rank + offs
    lane = lax.broadcasted_iota(jnp.int32, (tb, ROUTER_LANES), 1)
    lane_a = jnp.min(jnp.where(hot, lane, ROUTER_LANES), axis=1, keepdims=True)
    lane_b = jnp.max(jnp.where(hot, lane, -1), axis=1, keepdims=True)
    pick = lambda sel, val: jnp.sum(jnp.where(sel, val, 0.0), axis=1, keepdims=True)
    sel_a, sel_b = lane == lane_a, lane == lane_b
    z = jnp.where(lane == 0, pick(sel_a, pos), 0.0) + jnp.where(lane == 1, pick(sel_b, pos), 0.0)
    plan_ref[...] = z.T[0:8, :]
    meta_ref[...] = jnp.concatenate([offs, carry, jnp.zeros((6, ROUTER_LANES), F32)], axis=0)
    grow_ref[...] = jnp.concatenate([jnp.broadcast_to(pick(sel_a, gates), (tb, LANES)),
                                     jnp.broadcast_to(pick(sel_b, gates), (tb, LANES))], axis=1)


def _moe_plan(gates, tb):
    n = gates.shape[0]
    nb = n // tb
    return pl.pallas_call(
        _moe_plan_kernel,
        grid=(nb,),
        in_specs=[pl.BlockSpec((tb, ROUTER_LANES), lambda i: (i, 0))],
        out_specs=[pl.BlockSpec((None, 8, tb), lambda i: (i, 0, 0)),
                   pl.BlockSpec((None, 8, ROUTER_LANES), lambda i: (i, 0, 0)),
                   pl.BlockSpec((tb, EXPERT_TOP_K * LANES), lambda i: (i, 0))],
        out_shape=[jax.ShapeDtypeStruct((nb, 8, tb), F32), jax.ShapeDtypeStruct((nb, 8, ROUTER_LANES), F32),
                   jax.ShapeDtypeStruct((n, EXPERT_TOP_K * LANES), F32)],
        compiler_params=_cparams(1),
        name="moe_plan",
    )(gates)


def _moe_kernel(tile_rows, pitch, offs_ref, cnts_ref, pos_ref, hn_ref, grow_ref, x1_ref, wg_ref, wu_ref, wd_ref,
                gfin_ref, y_ref, xs_scr, gcol_scr, otile_scr, yacc_scr, tok_scr):
    blk = pl.program_id(0)
    step = pl.program_id(1)
    tb = hn_ref.shape[0]
    n_lt = D_MODEL // LANES

    @pl.when(step == 0)
    def _group_rows():
        xs_scr[...] = jnp.zeros_like(xs_scr)
        gcol_scr[...] = jnp.zeros_like(gcol_scr)
        yacc_scr[...] = jnp.zeros_like(yacc_scr)

        def clear_pad(ex, c):
            end = offs_ref[blk, ex] + cnts_ref[blk, ex]
            for k in range(MOE_ALIGN - 1):
                tok_scr[end + k] = tb
            return c

        lax.fori_loop(0, N_EXPERTS, clear_pad, 0)

        def place(t8, c):
            for u in range(8):
                t = t8 * 8 + u
                row = hn_ref[pl.ds(t, 1), :]
                gate = grow_ref[pl.ds(t, 1), :]
                for s in range(EXPERT_TOP_K):
                    p = pos_ref[s, t]
                    xs_scr[pl.ds(p, 1), :] = row
                    gcol_scr[pl.ds(p, 1), :] = gate[:, s * LANES:(s + 1) * LANES]
                    tok_scr[p] = t
            return c

        lax.fori_loop(0, tb // 8, place, 0)

    slot_rows = n_lt * pitch

    def compute(k, r0, slot):
        xw = xs_scr[pl.ds(r0, tile_rows), :]
        x = jnp.concatenate(
            [pltpu.unpack_elementwise(xw, index=i, packed_dtype=jnp.bfloat16, unpacked_dtype=F32).astype(BF16)
             for i in range(2)], axis=1)
        g = gcol_scr[pl.ds(r0, tile_rows), :]
        a = jnp.dot(x, wg_ref[k], preferred_element_type=F32)
        u = jnp.dot(x, wu_ref[k], preferred_element_type=F32)
        act = (a * _sigmoid(a)) * u * jnp.concatenate([g] * (D_EXPERT // LANES), axis=1)
        out = jnp.dot(act.astype(BF16), wd_ref[k], preferred_element_type=F32)
        for s in range(n_lt):
            otile_scr[pl.ds(slot * slot_rows + s * pitch, tile_rows), :] = out[:, s * LANES:(s + 1) * LANES]

    def scatter_add(r0, valid, slot):
        def add8(j8, c3):
            base = pl.multiple_of(j8 * 8, 8)
            dst = [pl.ds(pl.multiple_of(tok_scr[r0 + base + v] * n_lt, n_lt), n_lt) for v in range(8)]
            rows = [yacc_scr[dst[v], :]
                    + otile_scr[pl.ds(slot * slot_rows + base + v, n_lt, stride=pitch), :]
                    for v in range(8)]
            for v in range(8):
                yacc_scr[dst[v], :] = rows[v]
            return c3

        lax.fori_loop(0, (valid + 7) // 8, add8, 0)

    offs = [offs_ref[blk, step * MOE_EXPERTS_PER_STEP + k] for k in range(MOE_EXPERTS_PER_STEP)]
    cnts = [cnts_ref[blk, step * MOE_EXPERTS_PER_STEP + k] for k in range(MOE_EXPERTS_PER_STEP)]
    for k in range(MOE_EXPERTS_PER_STEP):
        compute(k, pl.multiple_of(offs[k], MOE_ALIGN), k)
    for k in range(MOE_EXPERTS_PER_STEP):
        scatter_add(offs[k], jnp.minimum(tile_rows, cnts[k]), k)
    for k in range(MOE_EXPERTS_PER_STEP):
        def more(c, cc, k=k):
            r0 = pl.multiple_of(offs[k] + c * tile_rows, MOE_ALIGN)
            compute(k, r0, 0)
            scatter_add(r0, jnp.minimum(tile_rows, cnts[k] - c * tile_rows), 0)
            return cc

        lax.fori_loop(1, (cnts[k] + tile_rows - 1) // tile_rows, more, 0)

    @pl.when(step == pl.num_programs(1) - 1)
    def _fin():
        moe = jnp.concatenate([yacc_scr[pl.ds(s, tb, stride=n_lt), :] for s in range(n_lt)], axis=1)
        y_ref[...] = _rms(x1_ref[...] + moe, gfin_ref[...])


def _moe(hn_packed, gates, x1, wg, wu, wd, g_final, tb):
    n, d = x1.shape
    nb = n // tb
    plan, meta, gate_rows = _moe_plan(gates, tb)
    pos = plan[:, 0:2, :].astype(jnp.int32)
    offs = meta[:, 0, :N_EXPERTS].astype(jnp.int32)
    cnts = meta[:, 1, :N_EXPERTS].astype(jnp.int32)
    p_rows = _moe_rows(tb)
    tile_rows, pitch = _moe_tile(tb)
    per = MOE_EXPERTS_PER_STEP
    n_lt = d // LANES
    smem = lambda: pl.BlockSpec((None, 2, tb), lambda i, e, *_: (i, 0, 0), memory_space=pltpu.SMEM)
    once = dict(pipeline_mode=pl.Buffered(1))
    grid_spec = pltpu.PrefetchScalarGridSpec(
        num_scalar_prefetch=2,
        grid=(nb, N_EXPERTS // per),
        in_specs=[smem(),
                  pl.BlockSpec((tb, d // 2), lambda i, e, *_: (i, 0), **once),
                  pl.BlockSpec((tb, EXPERT_TOP_K * LANES), lambda i, e, *_: (i, 0), **once),
                  pl.BlockSpec((tb, d), lambda i, e, *_: (i, 0), **once),
                  pl.BlockSpec((per, d, D_EXPERT), lambda i, e, *_: (e, 0, 0)),
                  pl.BlockSpec((per, d, D_EXPERT), lambda i, e, *_: (e, 0, 0)),
                  pl.BlockSpec((per, D_EXPERT, d), lambda i, e, *_: (e, 0, 0)),
                  pl.BlockSpec((1, d), lambda i, e, *_: (0, 0))],
        out_specs=pl.BlockSpec((tb, d), lambda i, e, *_: (i, 0)),
        scratch_shapes=[pltpu.VMEM((p_rows, d // 2), jnp.uint32),
                        pltpu.VMEM((p_rows, LANES), F32),
                        pltpu.VMEM((per * n_lt * pitch, LANES), F32),
                        pltpu.VMEM(((tb + 1) * n_lt, LANES), F32),
                        pltpu.SMEM((p_rows,), jnp.int32)])
    return pl.pallas_call(
        functools.partial(_moe_kernel, tile_rows, pitch),
        grid_spec=grid_spec,
        out_shape=jax.ShapeDtypeStruct((n, d), F32),
        compiler_params=pltpu.CompilerParams(dimension_semantics=("arbitrary", "arbitrary"),
                                             vmem_limit_bytes=MOE_VMEM_LIMIT),
        name="moe",
    )(offs, cnts, pos, hn_packed, gate_rows, x1, wg, wu, wd, g_final)


def kernel(x_prompt, x_sample, cache_k, cache_v, state_ssm_re, state_ssm_im, rel_bias, g_mix, w_in, g_out_a, g_out_b, w_out, ssm_a_re, ssm_a_im, ssm_log_dt, ssm_b_re, ssm_b_im, ssm_c_re, ssm_c_im, ssm_d, w_glu, b_glu, g_ffn, w_router_group, w_router_expert, w_expert_gate, w_expert_up, w_expert_down, g_final):
    depth = g_mix.shape[0]
    assert depth == 1, "kernel is written for the single-layer configuration of the problem"
    l = 0
    b, t, d = x_prompt.shape
    bd, ts, _ = x_sample.shape
    assert ts == DEC_T and t % ATT_BLOCK == 0 and cache_k.shape[2] == MAX_WINDOW
    keep = min(MAX_WINDOW, t)

    w_in_b = w_in[l].astype(BF16)
    gm = g_mix[l][None, :]
    w_router = jnp.concatenate(
        [jnp.transpose(w_router_expert[l], (1, 0, 2)).reshape(d, N_EXPERTS), w_router_group[l],
         jnp.zeros((d, ROUTER_LANES - N_EXPERTS - N_EXPERT_GROUPS), F32)], axis=1)
    w_router_hi = w_router.astype(BF16)
    w_router_lo = (w_router - w_router_hi.astype(F32)).astype(BF16)
    post_w = dict(
        g_out_a=g_out_a[l][None, :], g_out_b=g_out_b[l][None, :], w_glu=w_glu[l].astype(BF16),
        b_glu=b_glu[l][None, :], w_out=w_out[l].astype(BF16), g_ffn=g_ffn[l][None, :],
        w_router=jnp.concatenate([w_router_hi, w_router_lo], axis=1))
    wg = w_expert_gate[l].astype(BF16)
    wu = w_expert_up[l].astype(BF16)
    wd = w_expert_down[l].astype(BF16)
    gfin = g_final[None, :]
    ssm_ops = _ssm_prep(ssm_a_re[l], ssm_a_im[l], ssm_log_dt[l], ssm_b_re[l], ssm_b_im[l],
                        ssm_c_re[l], ssm_c_im[l])
    d_tile = jnp.tile(ssm_d[l].reshape(N_GROUPS, 1, SSM_GROUP), (1, 1, CHUNK))

    qp, kp, vp, kt_win, vt_win, u_tok = _inproj_pair(x_prompt, gm, w_in_b, tm=1024, keep=keep)
    oa_p = _attn_prompt(qp, kp, vp, rel_bias)
    ys_p, rp, ip = _ssm_prompt(u_tok, ssm_ops, ssm_d[l])
    x1_p, hn_p, gates_p = _post_mix(oa_p, ys_p, x_prompt, post_w, tm=512, pair=True)
    y_p = _moe(hn_p.reshape(b * t, d // 2), gates_p.reshape(b * t, ROUTER_LANES), x1_p.reshape(b * t, d),
               wg, wu, wd, gfin, tb=min(MOE_BLOCK, b * t))

    n_s = bd * ts
    qs, ks, vs, us = _inproj_tok(x_sample.reshape(n_s, d), gm, w_in_b)
    ckt = jnp.transpose(cache_k[l], (0, 2, 3, 1)).reshape(bd, MIX_A, MAX_WINDOW)
    cvt = jnp.transpose(cache_v[l], (0, 2, 3, 1)).reshape(bd, MIX_A, MAX_WINDOW)
    oa_s = _attn_decode(qs.reshape(bd, ts, MIX_A), ks.reshape(bd, ts, MIX_A), vs.reshape(bd, ts, MIX_A),
                        ckt, cvt, rel_bias)
    ugs = jnp.transpose(us.reshape(bd, ts, N_GROUPS, SSM_GROUP), (2, 0, 1, 3)).reshape(N_GROUPS, bd, ts * SSM_GROUP)
    s0r = jnp.transpose(state_ssm_re[l], (1, 0, 2))
    s0i = jnp.transpose(state_ssm_im[l], (1, 0, 2))
    ygs, rs, is_ = _ssm_decode(ugs, s0r, s0i, ssm_ops, d_tile)
    ys_s = jnp.transpose(ygs.reshape(N_GROUPS, bd, ts, SSM_GROUP), (1, 2, 0, 3)).reshape(n_s, MIX_B)
    x1_s, hn_s, gates_s = _post_mix(oa_s.reshape(1, n_s, MIX_A), ys_s.reshape(1, n_s, MIX_B),
                                    x_sample.reshape(1, n_s, d), post_w, tm=n_s, pair=False)
    y_s = _moe(hn_s.reshape(n_s, d // 2), gates_s.reshape(n_s, ROUTER_LANES), x1_s.reshape(n_s, d),
               wg, wu, wd, gfin, tb=n_s)

    y_prompt = y_p.reshape(b, t, d)
    y_sample = y_s.reshape(bd, ts, d)
    k_win = jnp.transpose(kt_win.reshape(b, N_HEADS, HEAD_DIM, keep), (0, 3, 1, 2))[None]
    v_win = jnp.transpose(vt_win.reshape(b, N_HEADS, HEAD_DIM, keep), (0, 3, 1, 2))[None]
    k_new = ks.reshape(1, bd, ts, N_HEADS, HEAD_DIM)
    v_new = vs.reshape(1, bd, ts, N_HEADS, HEAD_DIM)
    return (y_prompt, y_sample, k_win, v_win, k_new, v_new,
            rp.reshape(1, b, N_GROUPS, SSM_STATE), ip.reshape(1, b, N_GROUPS, SSM_STATE),
            jnp.transpose(rs, (1, 0, 2))[None], jnp.transpose(is_, (1, 0, 2))[None])
```

```python
import functools
import math

import jax
import jax.numpy as jnp
from jax import lax
from jax.experimental import pallas as pl
from jax.experimental.pallas import tpu as pltpu

F32 = jnp.float32
BF16 = jnp.bfloat16

D_MODEL = 1024
HEAD_DIM = 64
MIX_A = 512
N_HEADS = 8
MIX_B = 512
SSM_GROUP = 16
N_GROUPS = 32
SSM_STATE = 64
PROJ_COLS = 3 * MIX_A + MIX_B
DILATIONS = (1, 4, 16)
TAPS = 128
MAX_WINDOW = 2048
N_BUCKETS = 32
N_EXPERT_GROUPS = 4
EXPERTS_PER_GROUP = 8
N_EXPERTS = 32
EXPERT_TOP_K = 2
D_EXPERT = 256
EPS = 1e-6
NEG_INF = -1e30
SCALE = HEAD_DIM ** -0.5

LANES = 128
ROUTER_LANES = 128
CHUNK = 16
ATT_BLOCK = 2048
ATT_PITCH = 136
VMEM_LIMIT = 56 * 1024 * 1024


def _cparams(n_axes):
    return pltpu.CompilerParams(dimension_semantics=("arbitrary",) * n_axes,
                                vmem_limit_bytes=VMEM_LIMIT)


def _t5_bucket(distance):
    max_exact = N_BUCKETS // 2
    nf = jnp.maximum(distance, 1).astype(F32)
    large = max_exact + jnp.floor(jnp.log(nf / max_exact) / math.log(MAX_WINDOW / max_exact)
                                  * (N_BUCKETS - max_exact)).astype(jnp.int32)
    large = jnp.minimum(large, N_BUCKETS - 1)
    return jnp.where(distance < max_exact, distance, large)


def _rms(x, g):
    return x * lax.rsqrt(jnp.mean(x * x, axis=-1, keepdims=True) + EPS) * g


def _inproj_pair_kernel(first_win_tile, x_ref, g_ref, w_ref, q_ref, k_ref, v_ref, kt_ref, vt_ref, u_ref):
    h = _rms(x_ref[...], g_ref[...])
    p = jnp.dot(h.astype(BF16), w_ref[...], preferred_element_type=F32)
    for j in range(MIX_A // LANES):
        q_ref[j] = p[:, LANES * j:LANES * (j + 1)] * SCALE
        k_ref[j] = p[:, MIX_A + LANES * j:MIX_A + LANES * (j + 1)]
        v_ref[j] = p[:, 2 * MIX_A + LANES * j:2 * MIX_A + LANES * (j + 1)]
    u_ref[...] = p[:, 3 * MIX_A:]

    @pl.when(pl.program_id(1) >= first_win_tile)
    def _window():
        kt_ref[...] = p[:, MIX_A:2 * MIX_A].T
        vt_ref[...] = p[:, 2 * MIX_A:3 * MIX_A].T


def _inproj_pair(x, g, w_bf16, tm, keep):
    b, t, d = x.shape
    npair = MIX_A // LANES
    first = (t - keep) // tm
    pair = jax.ShapeDtypeStruct((b, npair, t, LANES), F32)
    win = jax.ShapeDtypeStruct((b, MIX_A, keep), F32)
    tok = jax.ShapeDtypeStruct((b, t, MIX_A), F32)
    pair_spec = pl.BlockSpec((None, npair, tm, LANES), lambda bi, i: (bi, 0, i, 0))
    win_spec = pl.BlockSpec((None, MIX_A, tm), lambda bi, i: (bi, 0, jnp.maximum(i - first, 0)))
    tok_spec = pl.BlockSpec((None, tm, MIX_A), lambda bi, i: (bi, i, 0))
    return pl.pallas_call(
        functools.partial(_inproj_pair_kernel, first),
        grid=(b, t // tm),
        in_specs=[pl.BlockSpec((None, tm, d), lambda bi, i: (bi, i, 0)),
                  pl.BlockSpec((1, d), lambda bi, i: (0, 0)),
                  pl.BlockSpec((d, PROJ_COLS), lambda bi, i: (0, 0))],
        out_specs=[pair_spec, pair_spec, pair_spec, win_spec, win_spec, tok_spec],
        out_shape=[pair, pair, pair, win, win, tok],
        compiler_params=_cparams(2),
        name="inproj_prompt",
    )(x, g, w_bf16)


def _inproj_tok_kernel(x_ref, g_ref, w_ref, q_ref, k_ref, v_ref, u_ref):
    h = _rms(x_ref[...], g_ref[...])
    p = jnp.dot(h.astype(BF16), w_ref[...], preferred_element_type=F32)
    q_ref[...] = p[:, :MIX_A] * SCALE
    k_ref[...] = p[:, MIX_A:2 * MIX_A]
    v_ref[...] = p[:, 2 * MIX_A:3 * MIX_A]
    u_ref[...] = p[:, 3 * MIX_A:]


def _inproj_tok(x, g, w_bf16):
    n, d = x.shape
    out = jax.ShapeDtypeStruct((n, MIX_A), F32)
    return pl.pallas_call(
        _inproj_tok_kernel,
        out_shape=[out, out, out, out],
        compiler_params=pltpu.CompilerParams(vmem_limit_bytes=VMEM_LIMIT),
        name="inproj_sample",
    )(x, g, w_bf16)


def _prompt_bias_ids():
    r = jnp.arange(TAPS, dtype=jnp.int32)[:, None]
    kpos = jnp.arange(2 * TAPS, dtype=jnp.int32)[None, :] - TAPS
    rel = r - kpos
    valid = (rel >= 0) & (rel <= TAPS)
    ids = [jnp.where(valid, _t5_bucket(d * rel), -1) for d in DILATIONS]
    return jnp.stack(ids).astype(jnp.int32)


def _attn_prompt_kernel(rb_ref, ids_ref, q_ref, kc_ref, kp_ref, vc_ref, vp_ref, o_ref,
                        bias_scr, fm_scr, m0_scr, l0_scr, o0_scr, m1_scr, l1_scr, o1_scr,
                        m2_scr, l2_scr, o2_scr):
    bi = pl.program_id(0)
    p = pl.program_id(1)
    i = pl.program_id(2)
    blk = ATT_BLOCK
    npair = N_HEADS // 2

    @pl.when((bi == 0) & (p == 0) & (i == 0))
    def _build_bias():
        col = lax.broadcasted_iota(jnp.int32, (2 * TAPS, 2 * TAPS), 1)
        fm_scr[0] = jnp.zeros((2 * TAPS, 2 * TAPS), F32)
        fm_scr[1] = jnp.where(col < TAPS, NEG_INF, 0.0).astype(F32)
        for c in range(len(DILATIONS)):
            ids = ids_ref[c]

            for h in range(N_HEADS):
                tile = jnp.where(ids < 0, NEG_INF, 0.0).astype(F32)
                for bkt in range(N_BUCKETS):
                    tile = tile + jnp.where(ids == bkt, rb_ref[bkt:bkt + 1, h:h + 1], 0.0)
                bias_scr[c * npair + h // 2, (h % 2) * TAPS:(h % 2 + 1) * TAPS, :] = tile

    lane = lax.broadcasted_iota(jnp.int32, (1, LANES), 1)
    low = lane < HEAD_DIM
    nt = (((1,), (1,)), ((), ()))
    ones = jnp.ones((2 * TAPS, LANES), BF16)
    stats = ((m0_scr, l0_scr, o0_scr), (m1_scr, l1_scr, o1_scr), (m2_scr, l2_scr, o2_scr))

    def rows(ref, start, d):
        if d == 1:
            return ref[pl.ds(start, TAPS), :]
        return ref[pl.ds(start, TAPS, stride=d), :]

    def sub_block(c, d, qs, first, dst_rows):
        q = rows(q_ref, qs, d)
        q2 = jnp.concatenate([jnp.where(low, q, 0.0), jnp.where(low, 0.0, q)], axis=0).astype(BF16)
        back = qs - d * TAPS
        k_back = rows(kp_ref, blk + back, d) if back < 0 else rows(kc_ref, back, d)
        v_back = rows(vp_ref, blk + back, d) if back < 0 else rows(vc_ref, back, d)
        k = jnp.concatenate([k_back, rows(kc_ref, qs, d)], axis=0).astype(BF16)
        v = jnp.concatenate([v_back, rows(vc_ref, qs, d)], axis=0).astype(BF16)
        s_ = lax.dot_general(q2, k, nt, preferred_element_type=F32) + bias_scr[c * npair + p]
        if first is not None:
            s_ = s_ + fm_scr[first]
        m = jnp.max(s_, axis=1, keepdims=True)
        pe = jnp.exp(s_ - m).astype(BF16)
        ol = jnp.dot(pe, jnp.concatenate([v, ones], axis=1), preferred_element_type=F32)
        o, l = ol[:, :LANES], ol[:, LANES:]
        m_ref, l_ref, a_ref = stats[c]
        m_ref[dst_rows, :] = jnp.where(low, m[:TAPS], m[TAPS:])
        l_ref[dst_rows, :] = jnp.where(low, l[:TAPS], l[TAPS:])
        a_ref[dst_rows, :] = jnp.where(low, o[:TAPS], o[TAPS:])

    at_start = jnp.where(i == 0, 1, 0)
    for c, d in enumerate(DILATIONS):
        n_grp = blk // (d * TAPS)
        for r in range(d):
            for g in range(n_grp):
                qs = r + d * TAPS * g
                first = at_start if g == 0 else None
                if d == 1:
                    dst_rows = pl.ds(qs, TAPS)
                elif n_grp > 1:
                    dst_rows = pl.ds(qs, TAPS, stride=d)
                else:
                    dst_rows = pl.ds(r * ATT_PITCH, TAPS)
                sub_block(c, d, qs, first, dst_rows)

    d_last = DILATIONS[-1]

    def merge_body(j, carry):
        tok = pl.ds(pl.multiple_of(j * d_last, d_last), d_last)
        dil = pl.ds(j, d_last, stride=ATT_PITCH)
        m0, m1, m2 = m0_scr[tok, :], m1_scr[tok, :], m2_scr[dil, :]
        m = jnp.maximum(jnp.maximum(m0, m1), m2)
        a0, a1, a2 = jnp.exp(m0 - m), jnp.exp(m1 - m), jnp.exp(m2 - m)
        num = o0_scr[tok, :] * a0 + o1_scr[tok, :] * a1 + o2_scr[dil, :] * a2
        den = l0_scr[tok, :] * a0 + l1_scr[tok, :] * a1 + l2_scr[dil, :] * a2
        o_ref[tok, :] = num / den
        return carry

    lax.fori_loop(0, blk // d_last, merge_body, 0, unroll=4)


def _attn_prompt(q, k, v, rel_bias):
    b, npair, t, _ = q.shape
    blk = ATT_BLOCK
    cur = pl.BlockSpec((None, None, blk, LANES), lambda bi, p, i: (bi, p, i, 0))
    prev = pl.BlockSpec((None, None, blk, LANES), lambda bi, p, i: (bi, p, jnp.maximum(i - 1, 0), 0))
    n_tiles = len(DILATIONS) * npair
    stats = ([pltpu.VMEM((blk, LANES), F32)] * 6
             + [pltpu.VMEM((DILATIONS[-1] * ATT_PITCH, LANES), F32)] * 3)
    return pl.pallas_call(
        _attn_prompt_kernel,
        grid=(b, npair, t // blk),
        in_specs=[pl.BlockSpec((N_BUCKETS, N_HEADS), lambda bi, p, i: (0, 0)),
                  pl.BlockSpec((len(DILATIONS), TAPS, 2 * TAPS), lambda bi, p, i: (0, 0, 0)),
                  cur, cur, prev, cur, prev],
        out_specs=cur,
        out_shape=jax.ShapeDtypeStruct(q.shape, F32),
        scratch_shapes=[pltpu.VMEM((n_tiles, 2 * TAPS, 2 * TAPS), F32),
                        pltpu.VMEM((2, 2 * TAPS, 2 * TAPS), F32)] + stats,
        compiler_params=_cparams(3),
        name="attn_prompt",
    )(rel_bias, _prompt_bias_ids(), q, k, k, v, v)


DEC_T = 4
DEC_NEW_PAD = 128
DEC_SEQ_PER_STEP = 2
DEC_KEYS = MAX_WINDOW + DEC_NEW_PAD


def _decode_tables():
    qi = jnp.arange(DEC_T, dtype=jnp.int32)[:, None]
    rho = jnp.arange(MAX_WINDOW, dtype=jnp.int32)[None, :]
    dist_b = MAX_WINDOW + qi - rho
    mult_b = jnp.zeros_like(dist_b)
    for d in DILATIONS:
        mult_b = mult_b + (((dist_b % d) == 0) & (dist_b <= d * TAPS)).astype(jnp.int32)
    jj = jnp.arange(DEC_NEW_PAD, dtype=jnp.int32)[None, :]
    dist_n = qi - jj
    mult_n = jnp.where(dist_n == 0, len(DILATIONS), jnp.where((dist_n > 0) & (jj < DEC_T), 1, 0))
    dist = jnp.concatenate([dist_b, jnp.maximum(dist_n, 0)], axis=1)
    mult = jnp.concatenate([mult_b, mult_n.astype(jnp.int32)], axis=1)
    ids = jnp.where(mult > 0, _t5_bucket(dist), -1).astype(jnp.int32)
    ids = jnp.repeat(ids, N_HEADS, axis=0)
    mult = jnp.repeat(mult, N_HEADS, axis=0).astype(F32)
    return ids, mult


def _attn_decode_kernel(rbt_ref, ids_ref, mult_ref, q_ref, kn_ref, vn_ref, kt_ref, vt_ref, o_ref, bias_scr):
    n_rows = DEC_T * N_HEADS

    @pl.when(pl.program_id(0) == 0)
    def _build_bias():
        ids = ids_ref[...]
        tile = jnp.where(ids < 0, NEG_INF, 0.0).astype(F32)
        for bkt in range(N_BUCKETS):
            tile = tile + jnp.where(ids == bkt, rbt_ref[:, bkt:bkt + 1], 0.0)
        bias_scr[...] = tile

    lane = lax.broadcasted_iota(jnp.int32, (N_HEADS, MIX_A), 1)
    head = lax.broadcasted_iota(jnp.int32, (N_HEADS, MIX_A), 0)
    hmask = (lane // HEAD_DIM) == head

    zpad = jnp.zeros((DEC_NEW_PAD - DEC_T, MIX_A), F32)
    nt = (((1,), (1,)), ((), ()))
    for j in range(q_ref.shape[0]):
        q = q_ref[j]
        qm = jnp.where(hmask[None], q[:, None, :], 0.0).reshape(n_rows, MIX_A).astype(BF16)
        kn = jnp.concatenate([kn_ref[j], zpad], axis=0).astype(BF16)
        vn = jnp.concatenate([vn_ref[j], zpad], axis=0).astype(BF16)
        kt = kt_ref[j].astype(BF16)
        vt = vt_ref[j].astype(BF16)
        s = jnp.concatenate([jnp.dot(qm, kt, preferred_element_type=F32),
                             lax.dot_general(qm, kn, nt, preferred_element_type=F32)], axis=1)
        s = s + bias_scr[...]
        m = jnp.max(s, axis=1, keepdims=True)
        pe = jnp.exp(s - m) * mult_ref[...]
        l = jnp.sum(pe, axis=1, keepdims=True)
        pb = pe.astype(BF16)
        o = (lax.dot_general(pb[:, :MAX_WINDOW], vt, nt, preferred_element_type=F32)
             + jnp.dot(pb[:, MAX_WINDOW:], vn, preferred_element_type=F32)) / l
        o3 = o.reshape(DEC_T, N_HEADS, MIX_A)
        o_ref[j] = jnp.sum(jnp.where(hmask[None], o3, 0.0), axis=1)


def _attn_decode(q, k_new, v_new, cache_kt, cache_vt, rel_bias):
    bd = q.shape[0]
    ids, mult = _decode_tables()
    rbt = jnp.tile(rel_bias.T, (DEC_T, 1))
    per = DEC_SEQ_PER_STEP if bd % DEC_SEQ_PER_STEP == 0 else 1
    new_spec = pl.BlockSpec((per, DEC_T, MIX_A), lambda b: (b, 0, 0))
    cache_spec = pl.BlockSpec((per, MIX_A, MAX_WINDOW), lambda b: (b, 0, 0))
    n_rows = DEC_T * N_HEADS
    const = lambda shape: pl.BlockSpec(shape, lambda b: (0, 0))
    return pl.pallas_call(
        _attn_decode_kernel,
        grid=(bd // per,),
        in_specs=[const((n_rows, N_BUCKETS)), const((n_rows, DEC_KEYS)), const((n_rows, DEC_KEYS)),
                  new_spec, new_spec, new_spec, cache_spec, cache_spec],
        out_specs=new_spec,
        out_shape=jax.ShapeDtypeStruct((bd, DEC_T, MIX_A), F32),
        scratch_shapes=[pltpu.VMEM((n_rows, DEC_KEYS), F32)],
        compiler_params=_cparams(1),
        name="attn_decode",
    )(rbt, ids, mult, q, k_new, v_new, cache_kt, cache_vt)


def _ssm_prep_kernel(are_ref, aim_ref, ldt_ref, btr_ref, bti_ref, cre_ref, cim_ref,
                     tit_ref, tinr_ref, tini_ref, toutr_ref, touti_ref, pw_ref, tintr_ref, tinti_ref):
    hi = lax.Precision.HIGHEST
    nt = (((1,), (1,)), ((), ()))
    a_re, a_im = are_ref[...], aim_ref[...]
    dt = jnp.exp(ldt_ref[...])
    decay = jnp.exp(a_re * dt)
    ab_re = decay * jnp.cos(a_im * dt)
    ab_im = decay * jnp.sin(a_im * dt)
    inv = 1.0 / (a_re * a_re + a_im * a_im)
    coef_re = ((ab_re - 1.0) * a_re + ab_im * a_im) * inv
    coef_im = (ab_im * a_re - (ab_re - 1.0) * a_im) * inv
    bt_re, bt_im = btr_ref[...], bti_ref[...]
    bb_re = coef_re * bt_re - coef_im * bt_im
    bb_im = coef_re * bt_im + coef_im * bt_re
    pw = [(jnp.ones_like(ab_re), jnp.zeros_like(ab_im))]
    for _ in range(CHUNK):
        pr, pi = pw[-1]
        pw.append((pr * ab_re - pi * ab_im, pr * ab_im + pi * ab_re))
    tin_re = jnp.concatenate(
        [pw[CHUNK - 1 - ti][0] * bb_re - pw[CHUNK - 1 - ti][1] * bb_im for ti in range(CHUNK)], axis=0).astype(BF16)
    tin_im = jnp.concatenate(
        [pw[CHUNK - 1 - ti][0] * bb_im + pw[CHUNK - 1 - ti][1] * bb_re for ti in range(CHUNK)], axis=0).astype(BF16)
    tinr_ref[...] = tin_re
    tini_ref[...] = tin_im
    p_i = lax.broadcasted_iota(jnp.int32, (SSM_STATE, SSM_STATE), 0)
    p_j = lax.broadcasted_iota(jnp.int32, (SSM_STATE, SSM_STATE), 1)
    eye = jnp.where(p_i == p_j, 1.0, 0.0).astype(BF16)
    tintr_ref[...] = lax.dot_general(eye, tin_re, nt, preferred_element_type=F32).astype(BF16)
    tinti_ref[...] = lax.dot_general(eye, tin_im, nt, preferred_element_type=F32).astype(BF16)
    c_re, c_im = cre_ref[...], cim_ref[...]
    cp_re = [c_re * pr - c_im * pi for pr, pi in pw]
    cp_im = [c_re * pi + c_im * pr for pr, pi in pw]
    toutr_ref[...] = jnp.concatenate(cp_re[1:], axis=0).astype(BF16)
    touti_ref[...] = (-jnp.concatenate(cp_im[1:], axis=0)).astype(BF16)
    kall = (lax.dot_general(jnp.concatenate(cp_re[:CHUNK], axis=0), bb_re, nt, precision=hi,
                            preferred_element_type=F32)
            - lax.dot_general(jnp.concatenate(cp_im[:CHUNK], axis=0), bb_im, nt, precision=hi,
                              preferred_element_type=F32))
    w = CHUNK * SSM_GROUP
    kb = kall.astype(BF16)
    ci_idx = lax.broadcasted_iota(jnp.int32, (SSM_GROUP, w), 0)
    col_idx = lax.broadcasted_iota(jnp.int32, (SSM_GROUP, w), 1)
    acc = jnp.zeros((w, w), F32)
    for ti in range(CHUNK):
        rows = ti * SSM_GROUP
        shifted = kb if ti == 0 else jnp.concatenate(
            [jnp.zeros((rows, SSM_GROUP), BF16), kb[:w - rows]], axis=0)
        place = jnp.where(col_idx == ci_idx + rows, 1.0, 0.0).astype(BF16)
        acc = acc + jnp.dot(shifted, place, preferred_element_type=F32)
    tit_ref[...] = acc.astype(BF16)
    zero = jnp.zeros_like(ab_re)
    pw_ref[...] = jnp.concatenate([pw[CHUNK][0], pw[CHUNK][1], pw[DEC_T][0], pw[DEC_T][1],
                                   zero, zero, zero, zero], axis=0)


def _ssm_prep(a_re, a_im, log_dt, b_re, b_im, c_re, c_im):
    g, p = a_re.shape
    w = CHUNK * SSM_GROUP
    row = lambda a: a[:, None, :]
    bt_re = jnp.transpose(b_re, (0, 2, 1))
    bt_im = jnp.transpose(b_im, (0, 2, 1))
    ldt = jnp.broadcast_to(log_dt[:, None, None], (g, 1, p))
    gspec = lambda shape: pl.BlockSpec((None,) + shape, lambda gi: (gi, 0, 0))
    return pl.pallas_call(
        _ssm_prep_kernel,
        grid=(g,),
        in_specs=[gspec((1, p)), gspec((1, p)), gspec((1, p)), gspec((SSM_GROUP, p)), gspec((SSM_GROUP, p)),
                  gspec((SSM_GROUP, p)), gspec((SSM_GROUP, p))],
        out_specs=[gspec((w, w)), gspec((w, p)), gspec((w, p)), gspec((w, p)), gspec((w, p)), gspec((8, p)),
                   gspec((p, w)), gspec((p, w))],
        out_shape=[jax.ShapeDtypeStruct((g, w, w), BF16)] + [jax.ShapeDtypeStruct((g, w, p), BF16)] * 4
                  + [jax.ShapeDtypeStruct((g, 8, p), F32)] + [jax.ShapeDtypeStruct((g, p, w), BF16)] * 2,
        compiler_params=_cparams(1),
        name="ssm_prep",
    )(row(a_re), row(a_im), ldt, bt_re, bt_im, c_re, c_im)


def _ssm_prompt_kernel(u_ref, tit_ref, tintr_ref, tinti_ref, toutr_ref, touti_ref, d_ref, pw_ref,
                       y_ref, sre_ref, sim_ref, ut_scr, yt_scr, pr_scr, pi_scr, sr_scr, si_scr):
    ngrp = LANES // SSM_GROUP
    nc = u_ref.shape[0] // CHUNK
    xt = [u_ref[pl.ds(ti, nc, stride=CHUNK), :].T for ti in range(CHUNK)]
    lane_pad = jnp.zeros((nc, LANES - SSM_STATE), F32)
    for j in range(ngrp):
        ut = jnp.concatenate([x[j * SSM_GROUP:(j + 1) * SSM_GROUP, :] for x in xt], axis=0).astype(BF16)
        ut_scr[j] = ut
        rows_j = pl.ds(j, nc, stride=ngrp)
        pr_scr[rows_j, :] = jnp.concatenate(
            [jnp.dot(tintr_ref[j], ut, preferred_element_type=F32).T, lane_pad], axis=1)
        pi_scr[rows_j, :] = jnp.concatenate(
            [jnp.dot(tinti_ref[j], ut, preferred_element_type=F32).T, lane_pad], axis=1)
    pad1 = jnp.zeros((ngrp, LANES - SSM_STATE), F32)
    ar = jnp.concatenate([jnp.concatenate([pw_ref[j, 0:1, :] for j in range(ngrp)], axis=0), pad1], axis=1)
    ai = jnp.concatenate([jnp.concatenate([pw_ref[j, 1:2, :] for j in range(ngrp)], axis=0), pad1], axis=1)

    def step(c, carry):
        sr, si = carry
        rows_c = pl.ds(pl.multiple_of(c * ngrp, ngrp), ngrp)
        sr_scr[rows_c, :] = sr
        si_scr[rows_c, :] = si
        return (ar * sr - ai * si + pr_scr[rows_c, :], ar * si + ai * sr + pi_scr[rows_c, :])

    zero = jnp.zeros((ngrp, LANES), F32)
    fin_r, fin_i = lax.fori_loop(0, nc, step, (zero, zero), unroll=8)
    for j in range(ngrp):
        sre_ref[j] = fin_r[j:j + 1, :SSM_STATE]
        sim_ref[j] = fin_i[j:j + 1, :SSM_STATE]
        rows_j = pl.ds(j, nc, stride=ngrp)
        yt = jnp.dot(tit_ref[j], ut_scr[j], preferred_element_type=F32)
        yt = yt + jnp.dot(toutr_ref[j], sr_scr[rows_j, :][:, :SSM_STATE].T.astype(BF16),
                          preferred_element_type=F32)
        yt = yt + jnp.dot(touti_ref[j], si_scr[rows_j, :][:, :SSM_STATE].T.astype(BF16),
                          preferred_element_type=F32)
        yt_scr[j] = yt
    d = d_ref[...]
    for to in range(CHUNK):
        v = jnp.concatenate([yt_scr[j, to * SSM_GROUP:(to + 1) * SSM_GROUP, :] for j in range(ngrp)], axis=0)
        rows = pl.ds(to, nc, stride=CHUNK)
        y_ref[rows, :] = v.T + d * u_ref[rows, :]


def _ssm_prompt(u, ops, d_skip):
    b, t, _ = u.shape
    p = SSM_STATE
    w = CHUNK * SSM_GROUP
    nc = t // CHUNK
    ngrp = LANES // SSM_GROUP
    g = N_GROUPS
    gspec = lambda shape: pl.BlockSpec((ngrp,) + shape, lambda bi, qi: (qi, 0, 0))
    io = pl.BlockSpec((None, t, LANES), lambda bi, qi: (bi, 0, qi))
    st = pl.BlockSpec((None, ngrp, 1, p), lambda bi, qi: (bi, qi, 0, 0))
    tit, _, _, toutr, touti, pw, tintr, tinti = ops
    return pl.pallas_call(
        _ssm_prompt_kernel,
        grid=(b, g // ngrp),
        in_specs=[io, gspec((w, w)), gspec((p, w)), gspec((p, w)), gspec((w, p)), gspec((w, p)),
                  pl.BlockSpec((None, 1, LANES), lambda bi, qi: (qi, 0, 0)), gspec((8, p))],
        out_specs=[io, st, st],
        out_shape=[jax.ShapeDtypeStruct(u.shape, F32),
                   jax.ShapeDtypeStruct((b, g, 1, p), F32), jax.ShapeDtypeStruct((b, g, 1, p), F32)],
        scratch_shapes=[pltpu.VMEM((ngrp, w, nc), BF16), pltpu.VMEM((ngrp, w, nc), F32)]
                       + [pltpu.VMEM((nc * ngrp, LANES), F32)] * 4,
        compiler_params=_cparams(2),
        name="ssm_prompt",
    )(u, tit, tintr, tinti, toutr, touti, d_skip.reshape(g // ngrp, 1, LANES), pw)


def _ssm_decode_kernel(u_ref, s0r_ref, s0i_ref, tit_ref, tinr_ref, tini_ref, toutr_ref, touti_ref, d_ref,
                       pw_ref, y_ref, sre_ref, sim_ref):
    nt = (((1,), (1,)), ((), ()))
    w = DEC_T * SSM_GROUP
    lo = (CHUNK - DEC_T) * SSM_GROUP
    u = u_ref[...]
    ub = u.astype(BF16)
    s0r = s0r_ref[...]
    s0i = s0i_ref[...]
    y = lax.dot_general(ub, tit_ref[0:w, 0:w], nt, preferred_element_type=F32)
    y = y + lax.dot_general(s0r.astype(BF16), toutr_ref[0:w, :], nt, preferred_element_type=F32)
    y = y + lax.dot_general(s0i.astype(BF16), touti_ref[0:w, :], nt, preferred_element_type=F32)
    y_ref[...] = y + d_ref[:, 0:w] * u
    ar = pw_ref[2:3, :]
    ai = pw_ref[3:4, :]
    sre_ref[...] = ar * s0r - ai * s0i + jnp.dot(ub, tinr_ref[lo:, :], preferred_element_type=F32)
    sim_ref[...] = ar * s0i + ai * s0r + jnp.dot(ub, tini_ref[lo:, :], preferred_element_type=F32)


def _ssm_decode(ug, s0_re, s0_im, ops, d_tile):
    g, bd, w = ug.shape
    p = SSM_STATE
    wc = CHUNK * SSM_GROUP
    gspec = lambda shape: pl.BlockSpec((None,) + shape, lambda gi: (gi, 0, 0))
    tit, tinr, tini, toutr, touti, pw = ops[:6]
    return pl.pallas_call(
        _ssm_decode_kernel,
        grid=(g,),
        in_specs=[gspec((bd, w)), gspec((bd, p)), gspec((bd, p)), gspec((wc, wc)), gspec((wc, p)),
                  gspec((wc, p)), gspec((wc, p)), gspec((wc, p)), gspec((1, wc)), gspec((8, p))],
        out_specs=[gspec((bd, w)), gspec((bd, p)), gspec((bd, p))],
        out_shape=[jax.ShapeDtypeStruct((g, bd, w), F32), jax.ShapeDtypeStruct((g, bd, p), F32),
                   jax.ShapeDtypeStruct((g, bd, p), F32)],
        compiler_params=_cparams(1),
        name="ssm_decode",
    )(ug, s0_re, s0_im, tit, tinr, tini, toutr, touti, d_tile, pw)


def _gelu_tanh(x):
    return 0.5 * x * (1.0 + jnp.tanh(math.sqrt(2.0 / math.pi) * (x + 0.044715 * (x * x * x))))


def _sigmoid(x):
    return 1.0 / (1.0 + jnp.exp(-x))


def _route(hn, wr_ref):
    n = hn.shape[0]
    h_hi = hn.astype(BF16)
    h_lo = (hn - h_hi.astype(F32)).astype(BF16)
    prod = jnp.dot(jnp.concatenate([h_hi, h_lo], axis=0), wr_ref[...], preferred_element_type=F32)
    logits = (prod[:n, :ROUTER_LANES] + prod[:n, ROUTER_LANES:]
              + prod[n:, :ROUTER_LANES] + prod[n:, ROUTER_LANES:])
    lidx = lax.broadcasted_iota(jnp.int32, (n, ROUTER_LANES), 1)
    is_e = lidx < N_EXPERTS
    is_g = jnp.logical_and(lidx >= N_EXPERTS, lidx < N_EXPERTS + N_EXPERT_GROUPS)
    gmax = jnp.max(jnp.where(is_g, logits, -jnp.inf), axis=1, keepdims=True)
    g_prob = 1.0 / jnp.sum(jnp.where(is_g, jnp.exp(logits - gmax), 0.0), axis=1, keepdims=True)
    g_sel = jnp.min(jnp.where(jnp.logical_and(is_g, logits == gmax), lidx - N_EXPERTS, N_EXPERT_GROUPS),
                    axis=1, keepdims=True)
    in_grp = jnp.logical_and(is_e, (lidx // EXPERTS_PER_GROUP) == g_sel)
    l1 = jnp.max(jnp.where(in_grp, logits, -jnp.inf), axis=1, keepdims=True)
    i1 = jnp.min(jnp.where(jnp.logical_and(in_grp, logits == l1), lidx, ROUTER_LANES), axis=1, keepdims=True)
    rest = jnp.logical_and(in_grp, lidx != i1)
    l2 = jnp.max(jnp.where(rest, logits, -jnp.inf), axis=1, keepdims=True)
    i2 = jnp.min(jnp.where(jnp.logical_and(rest, logits == l2), lidx, ROUTER_LANES), axis=1, keepdims=True)
    e2 = jnp.exp(l2 - l1)
    w1 = g_prob / (1.0 + e2)
    w2 = g_prob * e2 / (1.0 + e2)
    return jnp.where(lidx == i1, w1, 0.0) + jnp.where(lidx == i2, w2, 0.0)


def _post_body(oa, ys, x, ga_ref, gb_ref, wglu_ref, bglu_ref, wout_ref, gf_ref, wr_ref,
               x1_ref, hn_ref, gates_ref):
    z = _gelu_tanh(ys)
    gate = _sigmoid(jnp.dot(z.astype(BF16), wglu_ref[...], preferred_element_type=F32) + bglu_ref[...])
    ob = z * gate
    mixed = jnp.concatenate([_rms(oa, ga_ref[...]), _rms(ob, gb_ref[...])], axis=1).astype(BF16)
    x1 = x + jnp.dot(mixed, wout_ref[...], preferred_element_type=F32)
    x1_ref[...] = x1
    hn = _rms(x1, gf_ref[...])
    half = hn.shape[1] // 2
    hn_ref[...] = pltpu.pack_elementwise([hn[:, :half], hn[:, half:]], packed_dtype=jnp.bfloat16)
    gates_ref[...] = _route(hn, wr_ref)


def _post_pair_kernel(oa_ref, ys_ref, x_ref, *rest):
    oa = jnp.concatenate([oa_ref[j] for j in range(MIX_A // LANES)], axis=1)
    _post_body(oa, ys_ref[...], x_ref[...], *rest)


def _post_tok_kernel(oa_ref, ys_ref, x_ref, *rest):
    _post_body(oa_ref[...], ys_ref[...], x_ref[...], *rest)


def _post_mix(oa, ys, x, w, tm, pair):
    b, t, d = x.shape
    row = lambda width: pl.BlockSpec((None, tm, width), lambda bi, i: (bi, i, 0))
    const = lambda a: pl.BlockSpec(a.shape, lambda bi, i: (0,) * a.ndim)
    oa_spec = (pl.BlockSpec((None, MIX_A // LANES, tm, LANES), lambda bi, i: (bi, 0, i, 0)) if pair
               else row(MIX_A))
    weights = [w['g_out_a'], w['g_out_b'], w['w_glu'], w['b_glu'], w['w_out'], w['g_ffn'], w['w_router']]
    return pl.pallas_call(
        _post_pair_kernel if pair else _post_tok_kernel,
        grid=(b, t // tm),
        in_specs=[oa_spec, row(MIX_B), row(d)] + [const(a) for a in weights],
        out_specs=[row(d), row(d // 2), row(ROUTER_LANES)],
        out_shape=[jax.ShapeDtypeStruct((b, t, d), F32), jax.ShapeDtypeStruct((b, t, d // 2), jnp.uint32),
                   jax.ShapeDtypeStruct((b, t, ROUTER_LANES), F32)],
        compiler_params=_cparams(2),
        name="post_mix_prompt" if pair else "post_mix_sample",
    )(oa, ys, x, *weights)


MOE_BLOCK = 2048
MOE_TILE_SLACK = 1.25
MOE_ALIGN = 8
MOE_EXPERTS_PER_STEP = 2
MOE_VMEM_LIMIT = 60 * 1024 * 1024
PLAN_CHUNK = 256
BF16_ROWS = 16


def _moe_tile(tb):
    mean = EXPERT_TOP_K * tb / N_EXPERTS
    rows = -(-int(MOE_TILE_SLACK * mean) // BF16_ROWS) * BF16_ROWS
    pitch = rows // 8 + (1 - (rows // 8) % 2)
    return rows, 8 * pitch


def _moe_rows(tb):
    rows = EXPERT_TOP_K * tb + N_EXPERTS * (MOE_ALIGN - 1) + _moe_tile(tb)[0]
    return -(-rows // MOE_ALIGN) * MOE_ALIGN


def _moe_plan_kernel(gates_ref, plan_ref, meta_ref, grow_ref):
    tb = gates_ref.shape[0]
    gates = gates_ref[...]
    hot = gates > 0.0
    onehot = jnp.where(hot, 1.0, 0.0).astype(BF16)
    ch = min(PLAN_CHUNK, tb)
    r_i = lax.broadcasted_iota(jnp.int32, (ch, ch), 0)
    c_i = lax.broadcasted_iota(jnp.int32, (ch, ch), 1)
    earlier = jnp.where(c_i < r_i, 1.0, 0.0).astype(BF16)
    carry = jnp.zeros((1, ROUTER_LANES), F32)
    ranks = []
    for k in range(tb // ch):
        oh = onehot[k * ch:(k + 1) * ch]
        ranks.append(jnp.dot(earlier, oh, preferred_element_type=F32) + carry)
        carry = carry + jnp.sum(oh.astype(F32), axis=0, keepdims=True)
    rank = jnp.concatenate(ranks, axis=0)
    seg = jnp.floor((carry + (MOE_ALIGN - 1.0)) * (1.0 / MOE_ALIGN)) * MOE_ALIGN
    l_i = lax.broadcasted_iota(jnp.int32, (ROUTER_LANES, ROUTER_LANES), 0)
    l_j = lax.broadcasted_iota(jnp.int32, (ROUTER_LANES, ROUTER_LANES), 1)
    before = jnp.where(l_i < l_j, 1.0, 0.0).astype(F32)
    offs = jnp.dot(jnp.broadcast_to(seg, (8, ROUTER_LANES)), before, precision=lax.Precision.HIGHEST,
                   preferred_element_type=F32)[0:1]
    pos = rank + offs
    lane = lax.broadcasted_iota(jnp.int32, (tb, ROUTER_LANES), 1)
    lane_a = jnp.min(jnp.where(hot, lane, ROUTER_LANES), axis=1, keepdims=True)
    lane_b = jnp.max(jnp.where(hot, lane, -1), axis=1, keepdims=True)
    pick = lambda sel, val: jnp.sum(jnp.where(sel, val, 0.0), axis=1, keepdims=True)
    sel_a, sel_b = lane == lane_a, lane == lane_b
    z = jnp.where(lane == 0, pick(sel_a, pos), 0.0) + jnp.where(lane == 1, pick(sel_b, pos), 0.0)
    plan_ref[...] = z.T[0:8, :]
    meta_ref[...] = jnp.concatenate([offs, carry, jnp.zeros((6, ROUTER_LANES), F32)], axis=0)
    grow_ref[...] = jnp.concatenate([jnp.broadcast_to(pick(sel_a, gates), (tb, LANES)),
                                     jnp.broadcast_to(pick(sel_b, gates), (tb, LANES))], axis=1)


def _moe_plan(gates, tb):
    n = gates.shape[0]
    nb = n // tb
    return pl.pallas_call(
        _moe_plan_kernel,
        grid=(nb,),
        in_specs=[pl.BlockSpec((tb, ROUTER_LANES), lambda i: (i, 0))],
        out_specs=[pl.BlockSpec((None, 8, tb), lambda i: (i, 0, 0)),
                   pl.BlockSpec((None, 8, ROUTER_LANES), lambda i: (i, 0, 0)),
                   pl.BlockSpec((tb, EXPERT_TOP_K * LANES), lambda i: (i, 0))],
        out_shape=[jax.ShapeDtypeStruct((nb, 8, tb), F32), jax.ShapeDtypeStruct((nb, 8, ROUTER_LANES), F32),
                   jax.ShapeDtypeStruct((n, EXPERT_TOP_K * LANES), F32)],
        compiler_params=_cparams(1),
        name="moe_plan",
    )(gates)


def _moe_kernel(tile_rows, pitch, offs_ref, cnts_ref, pos_ref, hn_ref, grow_ref, x1_ref, wg_ref, wu_ref, wd_ref,
                gfin_ref, y_ref, xs_scr, gcol_scr, otile_scr, yacc_scr, tok_scr):
    blk = pl.program_id(0)
    step = pl.program_id(1)
    tb = hn_ref.shape[0]
    n_lt = D_MODEL // LANES

    @pl.when(step == 0)
    def _group_rows():
        xs_scr[...] = jnp.zeros_like(xs_scr)
        gcol_scr[...] = jnp.zeros_like(gcol_scr)
        yacc_scr[...] = jnp.zeros_like(yacc_scr)

        def clear_pad(ex, c):
            end = offs_ref[blk, ex] + cnts_ref[blk, ex]
            for k in range(MOE_ALIGN - 1):
                tok_scr[end + k] = tb
            return c

        lax.fori_loop(0, N_EXPERTS, clear_pad, 0)

        def place(t8, c):
            for u in range(8):
                t = t8 * 8 + u
                row = hn_ref[pl.ds(t, 1), :]
                gate = grow_ref[pl.ds(t, 1), :]
                for s in range(EXPERT_TOP_K):
                    p = pos_ref[s, t]
                    xs_scr[pl.ds(p, 1), :] = row
                    gcol_scr[pl.ds(p, 1), :] = gate[:, s * LANES:(s + 1) * LANES]
                    tok_scr[p] = t
            return c

        lax.fori_loop(0, tb // 8, place, 0)

    slot_rows = n_lt * pitch

    def compute(k, r0, slot):
        xw = xs_scr[pl.ds(r0, tile_rows), :]
        x = jnp.concatenate(
            [pltpu.unpack_elementwise(xw, index=i, packed_dtype=jnp.bfloat16, unpacked_dtype=F32).astype(BF16)
             for i in range(2)], axis=1)
        g = gcol_scr[pl.ds(r0, tile_rows), :]
        a = jnp.dot(x, wg_ref[k], preferred_element_type=F32)
        u = jnp.dot(x, wu_ref[k], preferred_element_type=F32)
        act = (a * _sigmoid(a)) * u * jnp.concatenate([g] * (D_EXPERT // LANES), axis=1)
        out = jnp.dot(act.astype(BF16), wd_ref[k], preferred_element_type=F32)
        for s in range(n_lt):
            otile_scr[pl.ds(slot * slot_rows + s * pitch, tile_rows), :] = out[:, s * LANES:(s + 1) * LANES]

    def scatter_add(r0, valid, slot):
        def add8(j8, c3):
            base = pl.multiple_of(j8 * 8, 8)
            dst = [pl.ds(pl.multiple_of(tok_scr[r0 + base + v] * n_lt, n_lt), n_lt) for v in range(8)]
            rows = [yacc_scr[dst[v], :]
                    + otile_scr[pl.ds(slot * slot_rows + base + v, n_lt, stride=pitch), :]
                    for v in range(8)]
            for v in range(8):
                yacc_scr[dst[v], :] = rows[v]
            return c3

        lax.fori_loop(0, (valid + 7) // 8, add8, 0)

    offs = [offs_ref[blk, step * MOE_EXPERTS_PER_STEP + k] for k in range(MOE_EXPERTS_PER_STEP)]
    cnts = [cnts_ref[blk, step * MOE_EXPERTS_PER_STEP + k] for k in range(MOE_EXPERTS_PER_STEP)]
    for k in range(MOE_EXPERTS_PER_STEP):
        compute(k, pl.multiple_of(offs[k], MOE_ALIGN), k)
    for k in range(MOE_EXPERTS_PER_STEP):
        scatter_add(offs[k], jnp.minimum(tile_rows, cnts[k]), k)
    for k in range(MOE_EXPERTS_PER_STEP):
        def more(c, cc, k=k):
            r0 = pl.multiple_of(offs[k] + c * tile_rows, MOE_ALIGN)
            compute(k, r0, 0)
            scatter_add(r0, jnp.minimum(tile_rows, cnts[k] - c * tile_rows), 0)
            return cc

        lax.fori_loop(1, (cnts[k] + tile_rows - 1) // tile_rows, more, 0)

    @pl.when(step == pl.num_programs(1) - 1)
    def _fin():
        moe = jnp.concatenate([yacc_scr[pl.ds(s, tb, stride=n_lt), :] for s in range(n_lt)], axis=1)
        y_ref[...] = _rms(x1_ref[...] + moe, gfin_ref[...])


def _moe(hn_packed, gates, x1, wg, wu, wd, g_final, tb):
    n, d = x1.shape
    nb = n // tb
    plan, meta, gate_rows = _moe_plan(gates, tb)
    pos = plan[:, 0:2, :].astype(jnp.int32)
    offs = meta[:, 0, :N_EXPERTS].astype(jnp.int32)
    cnts = meta[:, 1, :N_EXPERTS].astype(jnp.int32)
    p_rows = _moe_rows(tb)
    tile_rows, pitch = _moe_tile(tb)
    per = MOE_EXPERTS_PER_STEP
    n_lt = d // LANES
    smem = lambda: pl.BlockSpec((None, 2, tb), lambda i, e, *_: (i, 0, 0), memory_space=pltpu.SMEM)
    once = dict(pipeline_mode=pl.Buffered(1))
    grid_spec = pltpu.PrefetchScalarGridSpec(
        num_scalar_prefetch=2,
        grid=(nb, N_EXPERTS // per),
        in_specs=[smem(),
                  pl.BlockSpec((tb, d // 2), lambda i, e, *_: (i, 0), **once),
                  pl.BlockSpec((tb, EXPERT_TOP_K * LANES), lambda i, e, *_: (i, 0), **once),
                  pl.BlockSpec((tb, d), lambda i, e, *_: (i, 0), **once),
                  pl.BlockSpec((per, d, D_EXPERT), lambda i, e, *_: (e, 0, 0)),
                  pl.BlockSpec((per, d, D_EXPERT), lambda i, e, *_: (e, 0, 0)),
                  pl.BlockSpec((per, D_EXPERT, d), lambda i, e, *_: (e, 0, 0)),
                  pl.BlockSpec((1, d), lambda i, e, *_: (0, 0))],
        out_specs=pl.BlockSpec((tb, d), lambda i, e, *_: (i, 0)),
        scratch_shapes=[pltpu.VMEM((p_rows, d // 2), jnp.uint32),
                        pltpu.VMEM((p_rows, LANES), F32),
                        pltpu.VMEM((per * n_lt * pitch, LANES), F32),
                        pltpu.VMEM(((tb + 1) * n_lt, LANES), F32),
                        pltpu.SMEM((p_rows,), jnp.int32)])
    return pl.pallas_call(
        functools.partial(_moe_kernel, tile_rows, pitch),
        grid_spec=grid_spec,
        out_shape=jax.ShapeDtypeStruct((n, d), F32),
        compiler_params=pltpu.CompilerParams(dimension_semantics=("arbitrary", "arbitrary"),
                                             vmem_limit_bytes=MOE_VMEM_LIMIT),
        name="moe",
    )(offs, cnts, pos, hn_packed, gate_rows, x1, wg, wu, wd, g_final)


def kernel(x_prompt, x_sample, cache_k, cache_v, state_ssm_re, state_ssm_im, rel_bias, g_mix, w_in, g_out_a, g_out_b, w_out, ssm_a_re, ssm_a_im, ssm_log_dt, ssm_b_re, ssm_b_im, ssm_c_re, ssm_c_im, ssm_d, w_glu, b_glu, g_ffn, w_router_group, w_router_expert, w_expert_gate, w_expert_up, w_expert_down, g_final):
    depth = g_mix.shape[0]
    assert depth == 1, "kernel is written for the single-layer configuration of the problem"
    l = 0
    b, t, d = x_prompt.shape
    bd, ts, _ = x_sample.shape
    assert ts == DEC_T and t % ATT_BLOCK == 0 and cache_k.shape[2] == MAX_WINDOW
    keep = min(MAX_WINDOW, t)

    w_in_b = w_in[l].astype(BF16)
    gm = g_mix[l][None, :]
    w_router = jnp.concatenate(
        [jnp.transpose(w_router_expert[l], (1, 0, 2)).reshape(d, N_EXPERTS), w_router_group[l],
         jnp.zeros((d, ROUTER_LANES - N_EXPERTS - N_EXPERT_GROUPS), F32)], axis=1)
    w_router_hi = w_router.astype(BF16)
    w_router_lo = (w_router - w_router_hi.astype(F32)).astype(BF16)
    post_w = dict(
        g_out_a=g_out_a[l][None, :], g_out_b=g_out_b[l][None, :], w_glu=w_glu[l].astype(BF16),
        b_glu=b_glu[l][None, :], w_out=w_out[l].astype(BF16), g_ffn=g_ffn[l][None, :],
        w_router=jnp.concatenate([w_router_hi, w_router_lo], axis=1))
    wg = w_expert_gate[l].astype(BF16)
    wu = w_expert_up[l].astype(BF16)
    wd = w_expert_down[l].astype(BF16)
    gfin = g_final[None, :]
    ssm_ops = _ssm_prep(ssm_a_re[l], ssm_a_im[l], ssm_log_dt[l], ssm_b_re[l], ssm_b_im[l],
                        ssm_c_re[l], ssm_c_im[l])
    d_tile = jnp.tile(ssm_d[l].reshape(N_GROUPS, 1, SSM_GROUP), (1, 1, CHUNK))

    qp, kp, vp, kt_win, vt_win, u_tok = _inproj_pair(x_prompt, gm, w_in_b, tm=1024, keep=keep)
    oa_p = _attn_prompt(qp, kp, vp, rel_bias)
    ys_p, rp, ip = _ssm_prompt(u_tok, ssm_ops, ssm_d[l])
    x1_p, hn_p, gates_p = _post_mix(oa_p, ys_p, x_prompt, post_w, tm=1024, pair=True)
    y_p = _moe(hn_p.reshape(b * t, d // 2), gates_p.reshape(b * t, ROUTER_LANES), x1_p.reshape(b * t, d),
               wg, wu, wd, gfin, tb=min(MOE_BLOCK, b * t))

    n_s = bd * ts
    qs, ks, vs, us = _inproj_tok(x_sample.reshape(n_s, d), gm, w_in_b)
    ckt = jnp.transpose(cache_k[l], (0, 2, 3, 1)).reshape(bd, MIX_A, MAX_WINDOW)
    cvt = jnp.transpose(cache_v[l], (0, 2, 3, 1)).reshape(bd, MIX_A, MAX_WINDOW)
    oa_s = _attn_decode(qs.reshape(bd, ts, MIX_A), ks.reshape(bd, ts, MIX_A), vs.reshape(bd, ts, MIX_A),
                        ckt, cvt, rel_bias)
    ugs = jnp.transpose(us.reshape(bd, ts, N_GROUPS, SSM_GROUP), (2, 0, 1, 3)).reshape(N_GROUPS, bd, ts * SSM_GROUP)
    s0r = jnp.transpose(state_ssm_re[l], (1, 0, 2))
    s0i = jnp.transpose(state_ssm_im[l], (1, 0, 2))
    ygs, rs, is_ = _ssm_decode(ugs, s0r, s0i, ssm_ops, d_tile)
    ys_s = jnp.transpose(ygs.reshape(N_GROUPS, bd, ts, SSM_GROUP), (1, 2, 0, 3)).reshape(n_s, MIX_B)
    x1_s, hn_s, gates_s = _post_mix(oa_s.reshape(1, n_s, MIX_A), ys_s.reshape(1, n_s, MIX_B),
                                    x_sample.reshape(1, n_s, d), post_w, tm=n_s, pair=False)
    y_s = _moe(hn_s.reshape(n_s, d // 2), gates_s.reshape(n_s, ROUTER_LANES), x1_s.reshape(n_s, d),
               wg, wu, wd, gfin, tb=n_s)

    y_prompt = y_p.reshape(b, t, d)
    y_sample = y_s.reshape(bd, ts, d)
    k_win = jnp.transpose(kt_win.reshape(b, N_HEADS, HEAD_DIM, keep), (0, 3, 1, 2))[None]
    v_win = jnp.transpose(vt_win.reshape(b, N_HEADS, HEAD_DIM, keep), (0, 3, 1, 2))[None]
    k_new = ks.reshape(1, bd, ts, N_HEADS, HEAD_DIM)
    v_new = vs.reshape(1, bd, ts, N_HEADS, HEAD_DIM)
    return (y_prompt, y_sample, k_win, v_win, k_new, v_new,
            rp.reshape(1, b, N_GROUPS, SSM_STATE), ip.reshape(1, b, N_GROUPS, SSM_STATE),
            jnp.transpose(rs, (1, 0, 2))[None], jnp.transpose(is_, (1, 0, 2))[None])
```
